```python
import jax, jax.numpy as jnp
from jax import lax
import numpy as np

D_MODEL = 1024
BATCH = 2
SEQ = 8192
DEPTH = 2

GRID_W = 64
CTX_LEN = 256
MIX = D_MODEL
D_A = MIX // 2
D_B = MIX // 2
D_C = MIX // 2
D_D = MIX // 2
CONV_A = 31
CONV_B = 4
CONV_C = 3
PAD_A = (CONV_A // 2, CONV_A // 2)
PAD_B = (2, 1)
PAD_C = (1, 1)
LRU_BLOCK = 128
N_LRU_BLOCKS = D_B // LRU_BLOCK
LRU_C = 8.0
HEAD_DIM = 64
N_HEADS_D = D_D // HEAD_DIM
WIN_H = 8
WIN_W = 16
N_EXPERTS = 16
N_GROUPS = 4
EXPERTS_PER_GROUP = N_EXPERTS // N_GROUPS
TOP_K = 2
D_FF = 512
EPS = 1e-6
N_EVEN = (DEPTH + 1) // 2
N_ODD = DEPTH // 2

kernel_name = 'hybrid_conv_lru_shortconv_natten_moe_dit'


def rms_norm(x, g):
    xf = x.astype(jnp.float32)
    y = xf * lax.rsqrt(jnp.mean(xf * xf, -1, keepdims=True) + EPS)
    return (y * g.astype(jnp.float32)).astype(x.dtype)


def layer_norm(x, g, b):
    xf = x.astype(jnp.float32)
    mu = jnp.mean(xf, -1, keepdims=True)
    var = jnp.mean(jnp.square(xf - mu), -1, keepdims=True)
    y = (xf - mu) * lax.rsqrt(var + EPS) * g.astype(jnp.float32) + b.astype(jnp.float32)
    return y.astype(x.dtype)


def depthwise_conv(x, w, pad):
    return lax.conv_general_dilated(x, w[:, None, :].astype(x.dtype), window_strides=(1,), padding=[pad],
                                    dimension_numbers=('NWC', 'WIO', 'NWC'), feature_group_count=x.shape[-1])


def conformer_conv(p, dw_w, dw_b, ln_g, ln_b):
    v, g = jnp.split(p, 2, -1)
    u = v * jax.nn.sigmoid(g)
    u = depthwise_conv(u, dw_w, PAD_A) + dw_b
    return jax.nn.silu(layer_norm(u, ln_g, ln_b))


def rglru_coeffs(v, w_gate, b_gate, lam):
    bsz, length, _ = v.shape
    vb = v.reshape(bsz, length, N_LRU_BLOCKS, LRU_BLOCK)
    gates = jnp.einsum('blnk,gnkj->gblnj', vb, w_gate).reshape(2, bsz, length, D_B) + b_gate[:, None, None, :]
    r = jax.nn.sigmoid(gates[0].astype(jnp.float32))
    i = jax.nn.sigmoid(gates[1].astype(jnp.float32))
    log_a = -LRU_C * r * jax.nn.softplus(-lam.astype(jnp.float32))
    a = jnp.exp(log_a)
    b = jnp.sqrt(-jnp.expm1(2.0 * log_a)) * (i * v.astype(jnp.float32))
    return a, b


def linear_scan(a, b, h0, reverse):
    def combine(left, right):
        a_l, b_l = left
        a_r, b_r = right
        return a_l * a_r, a_r * b_l + b_r
    a_cum, b_cum = lax.associative_scan(combine, (a, b), reverse=reverse, axis=1)
    return a_cum * h0[:, None, :] + b_cum


def rglru_bidir(u_lat, u_ctx, conv_w, conv_b, gate_w, gate_b, lam, with_ctx):
    v_lat = depthwise_conv(u_lat, conv_w, PAD_B) + conv_b
    v_ctx = depthwise_conv(u_ctx, conv_w, PAD_B) + conv_b
    y_lat = jnp.zeros(v_lat.shape, jnp.float32)
    y_ctx = jnp.zeros(v_ctx.shape, jnp.float32)
    for d, rev in enumerate((False, True)):
        a_c, b_c = rglru_coeffs(v_ctx, gate_w[d], gate_b[d], lam[d])
        h_c = linear_scan(a_c, b_c, jnp.zeros_like(b_c[:, 0]), rev)
        h_end = h_c[:, 0] if rev else h_c[:, -1]
        a_l, b_l = rglru_coeffs(v_lat, gate_w[d], gate_b[d], lam[d])
        y_lat = y_lat + linear_scan(a_l, b_l, h_end, rev)
        if with_ctx:
            y_ctx = y_ctx + h_c
    return y_lat.astype(u_lat.dtype), (y_ctx.astype(u_ctx.dtype) if with_ctx else None)


def short_gated_conv(p, conv_w):
    xin, bg, cg = jnp.split(p, 3, -1)
    return bg * depthwise_conv(cg * xin, conv_w, PAD_C)


def context_attention(q, k, v):
    s = jnp.einsum('bqhd,bkhd->bhqk', q, k).astype(jnp.float32) * (HEAD_DIM ** -0.5)
    p = jax.nn.softmax(s, -1).astype(v.dtype)
    o = jnp.einsum('bhqk,bkhd->bqhd', p, v)
    return o.reshape(o.shape[0], o.shape[1], D_D)


def neighbourhood_attention(q, k, v, k_ctx, v_ctx, rpb):
    bsz, seq = q.shape[0], q.shape[1]
    rows = seq // GRID_W
    kh = min(WIN_H, rows)
    kw = WIN_W
    scale = HEAD_DIM ** -0.5
    qg = q.reshape(bsz, rows, GRID_W, N_HEADS_D, HEAD_DIM)
    kg = k.reshape(bsz, rows, GRID_W, N_HEADS_D, HEAD_DIM)
    vg = v.reshape(bsz, rows, GRID_W, N_HEADS_D, HEAD_DIM)
    col = jnp.arange(GRID_W)
    col_start = jnp.clip(col - kw // 2, 0, GRID_W - kw)
    key_cols = col_start[:, None] + jnp.arange(kw)
    col_idx = key_cols - col[:, None] + (WIN_W - 1)
    rpb_f = rpb.astype(jnp.float32)

    def row_block(r):
        r_start = jnp.clip(r - kh // 2, 0, rows - kh)
        q_r = lax.dynamic_index_in_dim(qg, r, axis=1, keepdims=False)
        k_rows = lax.dynamic_slice_in_dim(kg, r_start, kh, axis=1)
        v_rows = lax.dynamic_slice_in_dim(vg, r_start, kh, axis=1)
        k_win = k_rows[:, :, key_cols]
        v_win = v_rows[:, :, key_cols]
        row_idx = r_start + jnp.arange(kh) - r + (WIN_H - 1)
        bias = rpb_f[:, row_idx[None, :, None], col_idx[:, None, :]]
        s_loc = jnp.einsum('bqhd,brqkhd->bhqrk', q_r, k_win).astype(jnp.float32) * scale + bias[None]
        s_ctx = jnp.einsum('bqhd,bchd->bhqc', q_r, k_ctx).astype(jnp.float32) * scale
        s = jnp.concatenate([s_loc.reshape(bsz, N_HEADS_D, GRID_W, kh * kw), s_ctx], -1)
        p = jax.nn.softmax(s, -1).astype(v.dtype)
        p_loc = p[..., :kh * kw].reshape(bsz, N_HEADS_D, GRID_W, kh, kw)
        p_ctx = p[..., kh * kw:]
        return (jnp.einsum('bhqrk,brqkhd->bqhd', p_loc, v_win)
                + jnp.einsum('bhqc,bchd->bqhd', p_ctx, v_ctx))

    out = lax.map(row_block, jnp.arange(rows))
    return out.transpose(1, 0, 2, 3, 4).reshape(bsz, seq, D_D)


def mixer_conv_lru(n_lat, n_ctx, w_in, dw_w, dw_b, ln_g, ln_b, conv_w, conv_b, gate_w, gate_b, lam, with_ctx):
    splits = [2 * D_A, 2 * D_A + D_B]
    a_l, u_l, g_l = jnp.split(n_lat @ w_in, splits, -1)
    a_c, u_c, g_c = jnp.split(n_ctx @ w_in, splits, -1)
    y_l, y_c = rglru_bidir(u_l, u_c, conv_w, conv_b, gate_w, gate_b, lam, with_ctx)
    out_lat = jnp.concatenate([conformer_conv(a_l, dw_w, dw_b, ln_g, ln_b), y_l * jax.nn.gelu(g_l)], -1)
    if not with_ctx:
        return out_lat, None
    out_ctx = jnp.concatenate([conformer_conv(a_c, dw_w, dw_b, ln_g, ln_b), y_c * jax.nn.gelu(g_c)], -1)
    return out_lat, out_ctx


def mixer_conv_natten(n_lat, n_ctx, w_in, conv_w, rpb, with_ctx):
    def heads(t):
        return t.reshape(t.shape[0], t.shape[1], N_HEADS_D, HEAD_DIM)
    p_l = n_lat @ w_in
    p_c = n_ctx @ w_in
    q_l, k_l, v_l = [heads(t) for t in jnp.split(p_l[..., 3 * D_C:], 3, -1)]
    q_c, k_c, v_c = [heads(t) for t in jnp.split(p_c[..., 3 * D_C:], 3, -1)]
    out_lat = jnp.concatenate([short_gated_conv(p_l[..., :3 * D_C], conv_w),
                               neighbourhood_attention(q_l, k_l, v_l, k_c, v_c, rpb)], -1)
    if not with_ctx:
        return out_lat, None
    out_ctx = jnp.concatenate([short_gated_conv(p_c[..., :3 * D_C], conv_w), context_attention(q_c, k_c, v_c)], -1)
    return out_lat, out_ctx


def moe(h, router_w, router_bias, w_gate, w_up, w_down):
    shp = h.shape
    t = h.reshape(-1, D_MODEL)
    score = jax.nn.sigmoid((t @ router_w).astype(jnp.float32))
    sel = (score + router_bias.astype(jnp.float32)).reshape(-1, N_GROUPS, EXPERTS_PER_GROUP)
    group_score = jnp.sum(lax.top_k(sel, TOP_K)[0], -1)
    in_group = jax.nn.one_hot(jnp.argmax(group_score, -1), N_GROUPS, dtype=jnp.bool_)
    masked = jnp.where(in_group[:, :, None], sel, -jnp.inf).reshape(-1, N_EXPERTS)
    _, idx = lax.top_k(masked, TOP_K)
    w = jnp.take_along_axis(score, idx, -1)
    w = w / jnp.sum(w, -1, keepdims=True)
    combine = jnp.sum(jax.nn.one_hot(idx, N_EXPERTS, dtype=jnp.float32) * w[..., None], 1).astype(t.dtype)
    out = jnp.zeros_like(t)
    for e in range(N_EXPERTS):
        hid = jax.nn.silu(t @ w_gate[e]) * (t @ w_up[e])
        out = out + combine[:, e:e + 1] * (hid @ w_down[e])
    return out.reshape(shp)


def setup_inputs(seed: int = 0) -> dict:
    key = jax.random.key(seed)
    ks = jax.random.split(key, 32)
    D = D_MODEL

    def nrm(k, shape, s):
        return jax.random.normal(k, shape, jnp.float32) * s

    a_base = jax.random.uniform(ks[17], (N_EVEN, 2, D_B), jnp.float32, 0.9, 0.999) ** (1.0 / LRU_C)
    b_lambda = jnp.log(a_base) - jnp.log1p(-a_base)
    return {
        'x': nrm(ks[0], (BATCH, SEQ, D), 1.0),
        'c': nrm(ks[1], (BATCH, D), 1.0),
        'ctx': nrm(ks[2], (BATCH, CTX_LEN, D), 1.0),
        'c_ctx': nrm(ks[3], (D,), 1.0),
        'ada_w': nrm(ks[4], (DEPTH, D, 6 * D), 0.5 * D ** -0.5),
        'ada_b': nrm(ks[5], (DEPTH, 6 * D), 0.01),
        'norm_mix_g': 1.0 + nrm(ks[6], (DEPTH, D), 0.05),
        'norm_ffn_g': 1.0 + nrm(ks[7], (DEPTH, D), 0.05),
        'w_out': nrm(ks[8], (DEPTH, MIX, D), MIX ** -0.5),
        'ab_w_in': nrm(ks[9], (N_EVEN, D, 2 * D_A + 2 * D_B), D ** -0.5),
        'a_dw_w': nrm(ks[10], (N_EVEN, CONV_A, D_A), CONV_A ** -0.5),
        'a_dw_b': nrm(ks[11], (N_EVEN, D_A), 0.01),
        'a_ln_g': 1.0 + nrm(ks[12], (N_EVEN, D_A), 0.05),
        'a_ln_b': nrm(ks[13], (N_EVEN, D_A), 0.01),
        'b_conv_w': nrm(ks[14], (N_EVEN, CONV_B, D_B), CONV_B ** -0.5),
        'b_conv_b': nrm(ks[15], (N_EVEN, D_B), 0.01),
        'b_gate_w': nrm(ks[16], (N_EVEN, 2, 2, N_LRU_BLOCKS, LRU_BLOCK, LRU_BLOCK), LRU_BLOCK ** -0.5),
        'b_gate_b': nrm(ks[18], (N_EVEN, 2, 2, D_B), 0.01),
        'b_lambda': b_lambda,
        'cd_w_in': nrm(ks[19], (N_ODD, D, 3 * D_C + 3 * D_D), D ** -0.5),
        'c_conv_w': nrm(ks[20], (N_ODD, CONV_C, D_C), CONV_C ** -0.5),
        'd_rpb': nrm(ks[21], (N_ODD, N_HEADS_D, 2 * WIN_H - 1, 2 * WIN_W - 1), 0.1),
        'router_w': nrm(ks[22], (D, N_EXPERTS), D ** -0.5),
        'router_bias': nrm(ks[23], (N_EXPERTS,), 0.01),
        'moe_w_gate': nrm(ks[24], (DEPTH, N_EXPERTS, D, D_FF), D ** -0.5),
        'moe_w_up': nrm(ks[25], (DEPTH, N_EXPERTS, D, D_FF), D ** -0.5),
        'moe_w_down': nrm(ks[26], (DEPTH, N_EXPERTS, D_FF, D), D_FF ** -0.5),
        'final_g': 1.0 + nrm(ks[27], (D,), 0.05),
    }


def reference(x, c, ctx, c_ctx, ada_w, ada_b, norm_mix_g, norm_ffn_g, w_out,
              ab_w_in, a_dw_w, a_dw_b, a_ln_g, a_ln_b, b_conv_w, b_conv_b, b_gate_w, b_gate_b, b_lambda,
              cd_w_in, c_conv_w, d_rpb, router_w, router_bias, moe_w_gate, moe_w_up, moe_w_down, final_g):
    h_lat, h_ctx = x, ctx
    cond_lat = jax.nn.silu(c)
    cond_ctx = jax.nn.silu(c_ctx)[None]
    for layer in range(DEPTH):
        with_ctx = layer < DEPTH - 1
        j = layer // 2
        mod_l = (cond_lat @ ada_w[layer] + ada_b[layer])[:, None, :]
        mod_c = (cond_ctx @ ada_w[layer] + ada_b[layer])[:, None, :]
        sh1_l, sc1_l, g1_l, sh2_l, sc2_l, g2_l = jnp.split(mod_l, 6, -1)
        sh1_c, sc1_c, g1_c, sh2_c, sc2_c, g2_c = jnp.split(mod_c, 6, -1)
        n_lat = rms_norm(h_lat, norm_mix_g[layer]) * (1.0 + sc1_l) + sh1_l
        n_ctx = rms_norm(h_ctx, norm_mix_g[layer]) * (1.0 + sc1_c) + sh1_c
        if layer % 2 == 0:
            m_lat, m_ctx = mixer_conv_lru(n_lat, n_ctx, ab_w_in[j], a_dw_w[j], a_dw_b[j], a_ln_g[j], a_ln_b[j],
                                          b_conv_w[j], b_conv_b[j], b_gate_w[j], b_gate_b[j], b_lambda[j], with_ctx)
        else:
            m_lat, m_ctx = mixer_conv_natten(n_lat, n_ctx, cd_w_in[j], c_conv_w[j], d_rpb[j], with_ctx)
        h_lat = h_lat + g1_l * (m_lat @ w_out[layer])
        f_lat = moe(rms_norm(h_lat, norm_ffn_g[layer]) * (1.0 + sc2_l) + sh2_l,
                    router_w, router_bias, moe_w_gate[layer], moe_w_up[layer], moe_w_down[layer])
        h_lat = h_lat + g2_l * f_lat
        if with_ctx:
            h_ctx = h_ctx + g1_c * (m_ctx @ w_out[layer])
            f_ctx = moe(rms_norm(h_ctx, norm_ffn_g[layer]) * (1.0 + sc2_c) + sh2_c,
                        router_w, router_bias, moe_w_gate[layer], moe_w_up[layer], moe_w_down[layer])
            h_ctx = h_ctx + g2_c * f_ctx
    return rms_norm(h_lat, final_g)
```

```python
import functools

import jax
import jax.numpy as jnp
from jax import lax
from jax.experimental import pallas as pl
from jax.experimental.pallas import tpu as pltpu

F32 = jnp.float32
BF16 = jnp.bfloat16

D = 1024
BATCH = 2
SEQ = 8192
CTX = 256
GRID_W = 64
N_LAT = BATCH * SEQ
N_TOK = N_LAT + BATCH * CTX
HALF = 512
CONV_A = 31
CONV_B = 4
CONV_C = 3
LRU_BLOCK = 128
N_LRU_BLOCKS = HALF // LRU_BLOCK
LRU_C = 8.0
HEAD_DIM = 64
N_HEADS = HALF // HEAD_DIM
WIN_H = 8
WIN_W = 16
N_EXPERTS = 16
N_GROUPS = 4
PER_GROUP = N_EXPERTS // N_GROUPS
D_FF = 512
EPS = 1e-6
NEG = -1e30

TM = 256
TILES_PER_SEQ = SEQ // TM
N_LAT_TILES = N_LAT // TM
N_TILES = N_TOK // TM
TMOE = 512
HALO_A = 16
HALO_S = 8
ROWS_Q = TM // GRID_W
VMEM_LIMIT = 48 * 1024 * 1024


def _params(*sem):
    return pltpu.CompilerParams(dimension_semantics=sem, vmem_limit_bytes=VMEM_LIMIT)


def _sigmoid(x):
    return 1.0 / (1.0 + jnp.exp(-x))


def _silu(x):
    return x * _sigmoid(x)


def _gelu_tanh(x):
    return 0.5 * x * (1.0 + jnp.tanh(0.7978845608028654 * (x + 0.044715 * (x * x * x))))


def _rms_mod(x, g, scale, shift):
    y = x * lax.rsqrt(jnp.mean(x * x, axis=-1, keepdims=True) + EPS) * g
    return y * (1.0 + scale) + shift


def _nt_dot(a, b):
    return lax.dot_general(a, b, (((1,), (1,)), ((), ())), preferred_element_type=F32)


def _mod_row(tile_rows):
    per_seq = SEQ // tile_rows
    return lambda i: (jnp.minimum(i // per_seq, BATCH), 0, 0)


def _mod_kernel(c_ref, w_ref, b_ref, o_ref):
    c = c_ref[...]
    s = _silu(c).astype(BF16)
    o_ref[0] = jnp.dot(s, w_ref[0].astype(BF16), preferred_element_type=F32) + b_ref[0]


def _modulation(cond, ada_w, ada_b):
    depth = ada_w.shape[0]
    nb = 1536
    return pl.pallas_call(
        _mod_kernel,
        grid=(depth, 6 * D // nb),
        in_specs=[pl.BlockSpec((8, D), lambda l, j: (0, 0)),
                  pl.BlockSpec((1, D, nb), lambda l, j: (l, 0, j)),
                  pl.BlockSpec((1, 1, nb), lambda l, j: (l, 0, j))],
        out_specs=pl.BlockSpec((1, 8, nb), lambda l, j: (l, 0, j)),
        out_shape=jax.ShapeDtypeStruct((depth, 8, 6 * D), F32),
        compiler_params=_params("parallel", "parallel"),
        name="modulation",
    )(cond, ada_w, ada_b.reshape(depth, 1, 6 * D))


def _inproj0_kernel(x_ref, mod_ref, g_ref, w_ref, ua_ref, ub_ref, gg_ref):
    mod = mod_ref[0]
    n = _rms_mod(x_ref[...], g_ref[...], mod[:, D:2 * D], mod[:, 0:D])
    p = jnp.dot(n.astype(BF16), w_ref[...], preferred_element_type=F32)
    ua_ref[...] = p[:, 0:HALF] * _sigmoid(p[:, HALF:2 * HALF])
    ub_ref[...] = p[:, 2 * HALF:3 * HALF]
    gg_ref[...] = _gelu_tanh(p[:, 3 * HALF:4 * HALF])


def _inproj0(x, mod, g, w):
    tok = pl.BlockSpec((TM, HALF), lambda i: (i, 0))
    shp = jax.ShapeDtypeStruct((N_TOK, HALF), F32)
    return pl.pallas_call(
        _inproj0_kernel,
        grid=(N_TILES,),
        in_specs=[pl.BlockSpec((TM, D), lambda i: (i, 0)),
                  pl.BlockSpec((1, 1, 6 * D), _mod_row(TM)),
                  pl.BlockSpec((1, D), lambda i: (0, 0)),
                  pl.BlockSpec((D, 4 * HALF), lambda i: (0, 0))],
        out_specs=[tok, tok, tok],
        out_shape=[shp, shp, shp],
        compiler_params=_params("parallel"),
        name="inproj0",
    )(x, mod, g, w)


def _halo_specs(halo):
    per_tile = TM // halo
    last = N_TOK // halo - 1
    prev = pl.BlockSpec((halo, HALF), lambda i: (jnp.maximum(i * per_tile - 1, 0), 0))
    nxt = pl.BlockSpec((halo, HALF), lambda i: (jnp.minimum((i + 1) * per_tile, last), 0))
    return prev, nxt


def _seq_edges(i):
    is_ctx = i >= N_LAT_TILES
    first = jnp.logical_or(is_ctx, i % TILES_PER_SEQ == 0)
    last = jnp.logical_or(is_ctx, i % TILES_PER_SEQ == TILES_PER_SEQ - 1)
    return first, last


def _fill_padded(buf_ref, prev_ref, cur_ref, next_ref, halo):
    first, last = _seq_edges(pl.program_id(0))
    buf_ref[0:halo, :] = jnp.where(first, 0.0, prev_ref[...])
    buf_ref[halo:halo + TM, :] = cur_ref[...]
    buf_ref[halo + TM:halo + TM + halo, :] = jnp.where(last, 0.0, next_ref[...])


def _depthwise(buf_ref, w_ref, taps, first_off, rows, row0):
    acc = None
    for k in range(taps):
        term = w_ref[k:k + 1, :] * buf_ref[row0 + first_off + k:row0 + first_off + k + rows, :]
        acc = term if acc is None else acc + term
    return acc


CONV_ROWS = 32


def _conv0_kernel(ua_ref, uap_ref, uan_ref, ub_ref, ubp_ref, ubn_ref,
                  dww_ref, dwb_ref, lng_ref, lnb_ref, cw_ref, cb_ref,
                  a_ref, v_ref, bufa_ref, bufb_ref):
    _fill_padded(bufa_ref, uap_ref, ua_ref, uan_ref, HALO_A)
    _fill_padded(bufb_ref, ubp_ref, ub_ref, ubn_ref, HALO_S)
    for r in range(TM // CONV_ROWS):
        row0 = r * CONV_ROWS
        u = _depthwise(bufa_ref, dww_ref, CONV_A, HALO_A - CONV_A // 2, CONV_ROWS, row0) + dwb_ref[...]
        mu = jnp.mean(u, axis=-1, keepdims=True)
        uc = u - mu
        var = jnp.mean(uc * uc, axis=-1, keepdims=True)
        y = uc * lax.rsqrt(var + EPS) * lng_ref[...] + lnb_ref[...]
        a_ref[row0:row0 + CONV_ROWS, :] = _silu(y).astype(a_ref.dtype)
        v = _depthwise(bufb_ref, cw_ref, CONV_B, HALO_S - 2, CONV_ROWS, row0) + cb_ref[...]
        v_ref[row0:row0 + CONV_ROWS, :] = v


def _conv0(ua, ub, dw_w, dw_b, ln_g, ln_b, conv_w, conv_b):
    tok = pl.BlockSpec((TM, HALF), lambda i: (i, 0))
    pa, na = _halo_specs(HALO_A)
    ps, ns = _halo_specs(HALO_S)
    vec = pl.BlockSpec((1, HALF), lambda i: (0, 0))
    return pl.pallas_call(
        _conv0_kernel,
        grid=(N_TILES,),
        in_specs=[tok, pa, na, tok, ps, ns,
                  pl.BlockSpec((CONV_A, HALF), lambda i: (0, 0)), vec, vec, vec,
                  pl.BlockSpec((CONV_B, HALF), lambda i: (0, 0)), vec],
        out_specs=[tok, tok],
        out_shape=[jax.ShapeDtypeStruct((N_TOK, HALF), BF16), jax.ShapeDtypeStruct((N_TOK, HALF), F32)],
        scratch_shapes=[pltpu.VMEM((TM + 2 * HALO_A, HALF), F32), pltpu.VMEM((TM + 2 * HALO_S, HALF), F32)],
        compiler_params=_params("parallel"),
        name="conv0",
    )(ua, ua, ua, ub, ub, ub, dw_w, dw_b.reshape(1, HALF), ln_g.reshape(1, HALF), ln_b.reshape(1, HALF),
      conv_w, conv_b.reshape(1, HALF))


SCAN_UNROLL = 8


def _scan_kernel(v_ref, w_ref, gb_ref, lam_ref, y_ref, h_ref, a_ref, b_ref):
    d = pl.program_id(0)
    j = pl.program_id(2)

    @pl.when(j == 0)
    def _():
        h_ref[...] = jnp.zeros_like(h_ref)

    v = v_ref[...]
    vb = v.astype(BF16)
    lam = lam_ref[0]
    neg = -lam
    softplus = jnp.maximum(neg, 0.0) + jnp.log(1.0 + jnp.exp(-jnp.abs(neg)))
    for n in range(N_LRU_BLOCKS):
        sl = slice(n * LRU_BLOCK, (n + 1) * LRU_BLOCK)
        g = jnp.dot(vb[:, sl], w_ref[0, n], preferred_element_type=F32)
        r = _sigmoid(g[:, 0:LRU_BLOCK] + gb_ref[0, 0:1, sl])
        i = _sigmoid(g[:, LRU_BLOCK:2 * LRU_BLOCK] + gb_ref[0, 1:2, sl])
        log_a = -LRU_C * r * softplus[:, sl]
        a = jnp.exp(log_a)
        a_ref[:, sl] = a
        b_ref[:, sl] = jnp.sqrt(1.0 - jnp.exp(2.0 * log_a)) * (i * v[:, sl])

    def body(s, h):
        for u in range(SCAN_UNROLL):
            t = s * SCAN_UNROLL + u
            idx = jnp.where(d == 0, t, TM - 1 - t)
            h = a_ref[pl.ds(idx, 1), :] * h + b_ref[pl.ds(idx, 1), :]
            y_ref[0, pl.ds(idx, 1), :] = h
        return h

    h_ref[...] = lax.fori_loop(0, TM // SCAN_UNROLL, body, h_ref[...])


def _scan_tile(d, b, j):
    lat = b * TILES_PER_SEQ + jnp.where(d == 0, j - 1, TILES_PER_SEQ - j)
    return jnp.where(j == 0, N_LAT_TILES + b, lat)


def _scan(v, gate_w, gate_b, lam):
    return pl.pallas_call(
        _scan_kernel,
        grid=(2, BATCH, TILES_PER_SEQ + 1),
        in_specs=[pl.BlockSpec((TM, HALF), lambda d, b, j: (_scan_tile(d, b, j), 0)),
                  pl.BlockSpec((1, N_LRU_BLOCKS, LRU_BLOCK, 2 * LRU_BLOCK), lambda d, b, j: (d, 0, 0, 0)),
                  pl.BlockSpec((1, 2, HALF), lambda d, b, j: (d, 0, 0)),
                  pl.BlockSpec((1, 1, HALF), lambda d, b, j: (d, 0, 0))],
        out_specs=pl.BlockSpec((1, TM, HALF), lambda d, b, j: (d, _scan_tile(d, b, j), 0)),
        out_shape=jax.ShapeDtypeStruct((2, N_TOK, HALF), F32),
        scratch_shapes=[pltpu.VMEM((1, HALF), F32), pltpu.VMEM((TM, HALF), F32), pltpu.VMEM((TM, HALF), F32)],
        compiler_params=_params("arbitrary", "arbitrary", "arbitrary"),
        name="lru_scan",
    )(v, gate_w, gate_b, lam)


def _route(score, sel):
    rows = [sel[e:e + 1, :] for e in range(N_EXPERTS)]
    gbest = None
    gidx = None
    for g in range(N_GROUPS):
        top2 = None
        for p in range(PER_GROUP):
            for q in range(p + 1, PER_GROUP):
                s = rows[g * PER_GROUP + p] + rows[g * PER_GROUP + q]
                top2 = s if top2 is None else jnp.maximum(top2, s)
        if g == 0:
            gbest = top2
            gidx = jnp.zeros(top2.shape, jnp.int32)
        else:
            better = top2 > gbest
            gidx = jnp.where(better, g, gidx)
            gbest = jnp.where(better, top2, gbest)
    eint = lax.broadcasted_iota(jnp.int32, sel.shape, 0)
    eidx = eint.astype(F32)
    masked = jnp.where(jnp.right_shift(eint, 2) == gidx, sel, -jnp.inf)
    v1 = jnp.max(masked, axis=0, keepdims=True)
    i1 = jnp.min(jnp.where(masked == v1, eidx, float(N_EXPERTS)), axis=0, keepdims=True)
    masked2 = jnp.where(eidx == i1, -jnp.inf, masked)
    v2 = jnp.max(masked2, axis=0, keepdims=True)
    i2 = jnp.min(jnp.where(masked2 == v2, eidx, float(N_EXPERTS)), axis=0, keepdims=True)
    s1 = jnp.sum(jnp.where(eidx == i1, score, 0.0), axis=0, keepdims=True)
    s2 = jnp.sum(jnp.where(eidx == i2, score, 0.0), axis=0, keepdims=True)
    inv = 1.0 / (s1 + s2)
    return jnp.where(eidx == i1, s1 * inv, 0.0) + jnp.where(eidx == i2, s2 * inv, 0.0)


def _finish_outproj(m1, m2, x_ref, mod_ref, wo_ref, g_ref, rw_ref, rb_ref, h_ref, n_ref, comb_ref):
    mod = mod_ref[0]
    mix = (jnp.dot(m1, wo_ref[0:HALF, :], preferred_element_type=F32)
           + jnp.dot(m2, wo_ref[HALF:2 * HALF, :], preferred_element_type=F32))
    h = x_ref[...] + mod[:, 2 * D:3 * D] * mix
    h_ref[...] = h
    n = _rms_mod(h, g_ref[...], mod[:, 4 * D:5 * D], mod[:, 3 * D:4 * D])
    n_ref[...] = n.astype(n_ref.dtype)
    logits = lax.dot_general(rw_ref[...], n, (((1,), (1,)), ((), ())), preferred_element_type=F32,
                             precision=lax.Precision.HIGHEST)
    score = _sigmoid(logits)
    comb_ref[...] = _route(score, score + rb_ref[...])


def _outproj0_kernel(a_ref, y_ref, gg_ref, x_ref, mod_ref, wo_ref, g_ref, rw_ref, rb_ref, h_ref, n_ref, comb_ref):
    m2 = ((y_ref[0] + y_ref[1]) * gg_ref[...]).astype(BF16)
    _finish_outproj(a_ref[...], m2, x_ref, mod_ref, wo_ref, g_ref, rw_ref, rb_ref, h_ref, n_ref, comb_ref)


def _outproj1_kernel(cx_ref, cxp_ref, cxn_ref, bg_ref, att_ref, cw_ref, x_ref, mod_ref, wo_ref, g_ref, rw_ref, rb_ref,
                     h_ref, n_ref, comb_ref, buf_ref):
    _fill_padded(buf_ref, cxp_ref, cx_ref, cxn_ref, HALO_S)
    conv = _depthwise(buf_ref, cw_ref, CONV_C, HALO_S - 1, TM, 0)
    m1 = (bg_ref[...] * conv).astype(BF16)
    _finish_outproj(m1, att_ref[...], x_ref, mod_ref, wo_ref, g_ref, rw_ref, rb_ref, h_ref, n_ref, comb_ref)


def _outproj_common(n_tiles):
    in_specs = [pl.BlockSpec((TM, D), lambda i: (i, 0)),
                pl.BlockSpec((1, 1, 6 * D), _mod_row(TM)),
                pl.BlockSpec((D, D), lambda i: (0, 0)),
                pl.BlockSpec((1, D), lambda i: (0, 0)),
                pl.BlockSpec((N_EXPERTS, D), lambda i: (0, 0)),
                pl.BlockSpec((N_EXPERTS, 1), lambda i: (0, 0))]
    out_specs = [pl.BlockSpec((TM, D), lambda i: (i, 0)),
                 pl.BlockSpec((TM, D), lambda i: (i, 0)),
                 pl.BlockSpec((N_EXPERTS, TM), lambda i: (0, i))]
    rows = n_tiles * TM
    out_shape = [jax.ShapeDtypeStruct((rows, D), F32), jax.ShapeDtypeStruct((rows, D), BF16),
                 jax.ShapeDtypeStruct((N_EXPERTS, rows), F32)]
    return in_specs, out_specs, out_shape


def _outproj0(a, y, gg, x, mod, wo, g, rw_t, rb):
    tok = pl.BlockSpec((TM, HALF), lambda i: (i, 0))
    common_in, out_specs, out_shape = _outproj_common(N_TILES)
    return pl.pallas_call(
        _outproj0_kernel,
        grid=(N_TILES,),
        in_specs=[tok, pl.BlockSpec((2, TM, HALF), lambda i: (0, i, 0)), tok] + common_in,
        out_specs=out_specs,
        out_shape=out_shape,
        compiler_params=_params("parallel"),
        name="outproj0",
    )(a, y, gg, x, mod, wo, g, rw_t, rb)


def _outproj1(cx, bg, att, conv_w, x, mod, wo, g, rw_t, rb):
    tok = pl.BlockSpec((TM, HALF), lambda i: (i, 0))
    ps, ns = _halo_specs(HALO_S)
    common_in, out_specs, out_shape = _outproj_common(N_LAT_TILES)
    return pl.pallas_call(
        _outproj1_kernel,
        grid=(N_LAT_TILES,),
        in_specs=[tok, ps, ns, tok, tok, pl.BlockSpec((CONV_C, HALF), lambda i: (0, 0))] + common_in,
        out_specs=out_specs,
        out_shape=out_shape,
        scratch_shapes=[pltpu.VMEM((TM + 2 * HALO_S, HALF), F32)],
        compiler_params=_params("parallel"),
        name="outproj1",
    )(cx, cx, cx, bg, att, conv_w, x, mod, wo, g, rw_t, rb)


def _moe_kernel(n_ref, comb_ref, wg_ref, wu_ref, wd_ref, h_ref, mod_ref, fg_ref, o_ref, acc_ref, *, final_norm):
    e = pl.program_id(1)

    @pl.when(e == 0)
    def _():
        acc_ref[...] = jnp.zeros_like(acc_ref)

    x = n_ref[...]
    hid = _silu(jnp.dot(x, wg_ref[0], preferred_element_type=F32)) * jnp.dot(x, wu_ref[0], preferred_element_type=F32)
    comb = comb_ref[...]
    lane = lax.broadcasted_iota(jnp.int32, comb.shape, 1)
    cw = jnp.sum(jnp.where(lane == e, comb, 0.0), axis=1, keepdims=True)
    acc_ref[...] += jnp.dot((hid * cw).astype(BF16), wd_ref[0], preferred_element_type=F32)

    @pl.when(e == N_EXPERTS - 1)
    def _():
        out = h_ref[...] + mod_ref[0][:, 5 * D:6 * D] * acc_ref[...]
        if final_norm:
            out = out * lax.rsqrt(jnp.mean(out * out, axis=-1, keepdims=True) + EPS) * fg_ref[...]
        o_ref[...] = out


def _moe(n, comb, wg, wu, wd, h, mod, final_g, final_norm):
    rows = n.shape[0]
    tok = pl.BlockSpec((TMOE, D), lambda i, e: (i, 0))
    return pl.pallas_call(
        functools.partial(_moe_kernel, final_norm=final_norm),
        grid=(rows // TMOE, N_EXPERTS),
        in_specs=[tok,
                  pl.BlockSpec((TMOE, N_EXPERTS), lambda i, e: (i, 0)),
                  pl.BlockSpec((1, D, D_FF), lambda i, e: (e, 0, 0)),
                  pl.BlockSpec((1, D, D_FF), lambda i, e: (e, 0, 0)),
                  pl.BlockSpec((1, D_FF, D), lambda i, e: (e, 0, 0)),
                  tok,
                  pl.BlockSpec((1, 1, 6 * D), lambda i, e: _mod_row(TMOE)(i)),
                  pl.BlockSpec((1, D), lambda i, e: (0, 0))],
        out_specs=tok,
        out_shape=jax.ShapeDtypeStruct((rows, D), F32),
        scratch_shapes=[pltpu.VMEM((TMOE, D), F32)],
        compiler_params=_params("parallel", "arbitrary"),
        name="moe_final" if final_norm else "moe",
    )(n, comb, wg, wu, wd, h, mod, final_g)


def _inproj1_kernel(x_ref, mod_ref, g_ref, w_ref, cx_ref, bg_ref, q_ref, k_ref, v_ref):
    mod = mod_ref[0]
    n = _rms_mod(x_ref[...], g_ref[...], mod[:, D:2 * D], mod[:, 0:D])
    p = jnp.dot(n.astype(BF16), w_ref[...], preferred_element_type=F32)
    cx_ref[...] = p[:, 2 * HALF:3 * HALF] * p[:, 0:HALF]
    bg_ref[...] = p[:, HALF:2 * HALF]
    q_ref[...] = (p[:, 3 * HALF:4 * HALF] * (HEAD_DIM ** -0.5)).astype(BF16)
    k_ref[...] = p[:, 4 * HALF:5 * HALF].astype(BF16)
    v_ref[...] = p[:, 5 * HALF:6 * HALF].astype(BF16)


def _inproj1(x, mod, g, w):
    tok = pl.BlockSpec((TM, HALF), lambda i: (i, 0))
    f = jax.ShapeDtypeStruct((N_TOK, HALF), F32)
    h = jax.ShapeDtypeStruct((N_TOK, HALF), BF16)
    return pl.pallas_call(
        _inproj1_kernel,
        grid=(N_TILES,),
        in_specs=[pl.BlockSpec((TM, D), lambda i: (i, 0)),
                  pl.BlockSpec((1, 1, 6 * D), _mod_row(TM)),
                  pl.BlockSpec((1, D), lambda i: (0, 0)),
                  pl.BlockSpec((D, 6 * HALF), lambda i: (0, 0))],
        out_specs=[tok] * 5,
        out_shape=[f, f, h, h, h],
        compiler_params=_params("parallel"),
        name="inproj1",
    )(x, mod, g, w)


def _natten_kernel(q_ref, kp_ref, kc_ref, kn_ref, vp_ref, vc_ref, vn_ref, kx_ref, vx_ref, bias_ref, o_ref):
    lane = lax.broadcasted_iota(jnp.int32, (TM, 2 * HEAD_DIM), 1)
    low = lane < HEAD_DIM
    head_mask = [low.astype(F32).astype(BF16), jnp.logical_not(low).astype(F32).astype(BF16)]
    for g in range(N_HEADS // 2):
        sl = slice(2 * HEAD_DIM * g, 2 * HEAD_DIM * (g + 1))
        q2 = q_ref[:, sl]
        keys = [kp_ref[:, sl], kc_ref[:, sl], kn_ref[:, sl], kx_ref[:, sl]]
        vals = [vp_ref[:, sl], vc_ref[:, sl], vn_ref[:, sl], vx_ref[:, sl]]
        outs = []
        for hh in range(2):
            qm = q2 * head_mask[hh]
            s = [_nt_dot(qm, k) for k in keys]
            for t in range(3):
                s[t] = s[t] + bias_ref[0, 2 * g + hh, :, t * TM:(t + 1) * TM]
            m = jnp.max(jnp.maximum(jnp.maximum(s[0], s[1]), jnp.maximum(s[2], s[3])), axis=-1, keepdims=True)
            p = [jnp.exp(t - m) for t in s]
            den = jnp.sum(p[0] + p[1] + p[2] + p[3], axis=-1, keepdims=True)
            o = None
            for pt, vt in zip(p, vals):
                term = jnp.dot(pt.astype(BF16), vt, preferred_element_type=F32)
                o = term if o is None else o + term
            outs.append(o * (1.0 / den))
        o_ref[:, sl] = jnp.where(low, outs[0], outs[1]).astype(o_ref.dtype)


def _natten_bias(rpb):
    n_rows = SEQ // GRID_W
    i = jnp.arange(ROWS_Q)
    j = jnp.arange(3 * ROWS_Q)
    col = jnp.arange(GRID_W)
    col_start = jnp.clip(col - WIN_W // 2, 0, GRID_W - WIN_W)
    col_ok = (col[None, :] >= col_start[:, None]) & (col[None, :] < col_start[:, None] + WIN_W)
    col_idx = jnp.clip(col[None, :] - col[:, None] + (WIN_W - 1), 0, 2 * WIN_W - 2)
    kinds = []
    for r0 in (0, ROWS_Q, n_rows - ROWS_Q):
        r = r0 + i
        kr = r0 - ROWS_Q + j
        r_start = jnp.clip(r - WIN_H // 2, 0, n_rows - WIN_H)
        row_ok = ((kr[None, :] >= r_start[:, None]) & (kr[None, :] < r_start[:, None] + WIN_H)
                  & (kr[None, :] >= 0) & (kr[None, :] < n_rows))
        row_idx = jnp.clip(kr[None, :] - r[:, None] + (WIN_H - 1), 0, 2 * WIN_H - 2)
        b = rpb[:, row_idx[:, None, :, None], col_idx[None, :, None, :]]
        ok = row_ok[:, None, :, None] & col_ok[None, :, None, :]
        kinds.append(jnp.where(ok[None], b, NEG).reshape(N_HEADS, TM, 3 * TM))
    return jnp.stack(kinds).astype(F32)


def _natten(q, k, v, bias):
    def lat(off):
        def index(b, i):
            return (b * TILES_PER_SEQ + jnp.clip(i + off, 0, TILES_PER_SEQ - 1), 0)
        return pl.BlockSpec((TM, HALF), index)

    ctx = pl.BlockSpec((TM, HALF), lambda b, i: (N_LAT_TILES + b, 0))

    def kind(b, i):
        return (jnp.where(i == 0, 0, jnp.where(i == TILES_PER_SEQ - 1, 2, 1)), 0, 0, 0)

    return pl.pallas_call(
        _natten_kernel,
        grid=(BATCH, TILES_PER_SEQ),
        in_specs=[lat(0), lat(-1), lat(0), lat(1), lat(-1), lat(0), lat(1), ctx, ctx,
                  pl.BlockSpec((1, N_HEADS, TM, 3 * TM), kind)],
        out_specs=lat(0),
        out_shape=jax.ShapeDtypeStruct((N_LAT, HALF), BF16),
        compiler_params=_params("parallel", "arbitrary"),
        name="natten",
    )(q, k, k, k, v, v, v, k, v, bias)


def kernel(x, c, ctx, c_ctx, ada_w, ada_b, norm_mix_g, norm_ffn_g, w_out, ab_w_in, a_dw_w, a_dw_b, a_ln_g, a_ln_b,
           b_conv_w, b_conv_b, b_gate_w, b_gate_b, b_lambda, cd_w_in, c_conv_w, d_rpb, router_w, router_bias,
           moe_w_gate, moe_w_up, moe_w_down, final_g):
    tokens = jnp.concatenate([x.reshape(N_LAT, D), ctx.reshape(BATCH * CTX, D)], axis=0)
    cond = jnp.concatenate([c, c_ctx[None], jnp.zeros((8 - BATCH - 1, D), F32)], axis=0)
    mod = _modulation(cond, ada_w, ada_b)
    mod0 = mod[0].reshape(8, 1, 6 * D)
    mod1 = mod[1].reshape(8, 1, 6 * D)

    wg = moe_w_gate.astype(BF16)
    wu = moe_w_up.astype(BF16)
    wd = moe_w_down.astype(BF16)
    wo = w_out.astype(BF16)
    rw_t = router_w.T
    rb = router_bias.reshape(N_EXPERTS, 1)
    fg = final_g.reshape(1, D)

    ua, ub, gg = _inproj0(tokens, mod0, norm_mix_g[0].reshape(1, D), ab_w_in[0].astype(BF16))
    a_out, v = _conv0(ua, ub, a_dw_w[0], a_dw_b[0], a_ln_g[0], a_ln_b[0], b_conv_w[0], b_conv_b[0])
    gw = b_gate_w[0]
    gate_w = jnp.concatenate([gw[:, 0], gw[:, 1]], axis=-1).astype(BF16)
    y = _scan(v, gate_w, b_gate_b[0], b_lambda[0].reshape(2, 1, HALF))
    h1, n2, comb_t = _outproj0(a_out, y, gg, tokens, mod0, wo[0], norm_ffn_g[0].reshape(1, D), rw_t, rb)
    h2 = _moe(n2, comb_t.T, wg[0], wu[0], wd[0], h1, mod0, fg, False)

    cx, bg, q, k, vv = _inproj1(h2, mod1, norm_mix_g[1].reshape(1, D), cd_w_in[0].astype(BF16))
    att = _natten(q, k, vv, _natten_bias(d_rpb[0]))
    h3, n4, comb_t1 = _outproj1(cx, bg, att, c_conv_w[0], h2, mod1, wo[1], norm_ffn_g[1].reshape(1, D), rw_t, rb)
    out = _moe(n4, comb_t1.T, wg[1], wu[1], wd[1], h3, mod1, fg, True)
    return out.reshape(BATCH, SEQ, D)
```

```python
import functools

import jax
import jax.numpy as jnp
import numpy as np
from jax import lax
from jax.experimental import pallas as pl
from jax.experimental.pallas import tpu as pltpu

F32 = jnp.float32
BF16 = jnp.bfloat16

D = 1024
BATCH = 2
SEQ = 8192
CTX = 256
GRID_W = 64
N_LAT = BATCH * SEQ
N_TOK = N_LAT + BATCH * CTX
HALF = 512
CONV_A = 31
CONV_B = 4
CONV_C = 3
LRU_BLOCK = 128
N_LRU_BLOCKS = HALF // LRU_BLOCK
LRU_C = 8.0
HEAD_DIM = 64
N_HEADS = HALF // HEAD_DIM
WIN_H = 8
WIN_W = 16
N_EXPERTS = 16
N_GROUPS = 4
PER_GROUP = N_EXPERTS // N_GROUPS
D_FF = 512
EPS = 1e-6
NEG = -1e30

TM = 256
TILES_PER_SEQ = SEQ // TM
N_LAT_TILES = N_LAT // TM
N_TILES = N_TOK // TM
TMOE = 512
HALO_A = 16
HALO_S = 8
ROWS_Q = TM // GRID_W
VMEM_LIMIT = 48 * 1024 * 1024


def _params(*sem):
    return pltpu.CompilerParams(dimension_semantics=sem, vmem_limit_bytes=VMEM_LIMIT)


def _sigmoid(x):
    return 1.0 / (1.0 + jnp.exp(-x))


def _silu(x):
    return x * _sigmoid(x)


def _gelu_tanh(x):
    return 0.5 * x * (1.0 + jnp.tanh(0.7978845608028654 * (x + 0.044715 * (x * x * x))))


def _rms_mod(x, g, scale, shift):
    y = x * lax.rsqrt(jnp.mean(x * x, axis=-1, keepdims=True) + EPS) * g
    return y * (1.0 + scale) + shift


def _nt_dot(a, b):
    return lax.dot_general(a, b, (((1,), (1,)), ((), ())), preferred_element_type=F32)


def _mod_row(tile_rows):
    per_seq = SEQ // tile_rows
    return lambda i: (jnp.minimum(i // per_seq, BATCH), 0, 0)


def _mod_kernel(c_ref, w_ref, b_ref, o_ref):
    c = c_ref[...]
    s = _silu(c).astype(BF16)
    o_ref[0] = jnp.dot(s, w_ref[0].astype(BF16), preferred_element_type=F32) + b_ref[0]


def _modulation(cond, ada_w, ada_b):
    depth = ada_w.shape[0]
    nb = 1536
    return pl.pallas_call(
        _mod_kernel,
        grid=(depth, 6 * D // nb),
        in_specs=[pl.BlockSpec((8, D), lambda l, j: (0, 0)),
                  pl.BlockSpec((1, D, nb), lambda l, j: (l, 0, j)),
                  pl.BlockSpec((1, 1, nb), lambda l, j: (l, 0, j))],
        out_specs=pl.BlockSpec((1, 8, nb), lambda l, j: (l, 0, j)),
        out_shape=jax.ShapeDtypeStruct((depth, 8, 6 * D), F32),
        compiler_params=_params("parallel", "parallel"),
        name="modulation",
    )(cond, ada_w, ada_b.reshape(depth, 1, 6 * D))


def _inproj0_kernel(x_ref, mod_ref, g_ref, w_ref, ua_ref, ub_ref, gg_ref):
    mod = mod_ref[0]
    n = _rms_mod(x_ref[...], g_ref[...], mod[:, D:2 * D], mod[:, 0:D])
    p = jnp.dot(n.astype(BF16), w_ref[...], preferred_element_type=F32)
    ua_ref[...] = p[:, 0:HALF] * _sigmoid(p[:, HALF:2 * HALF])
    ub_ref[...] = p[:, 2 * HALF:3 * HALF]
    gg_ref[...] = _gelu_tanh(p[:, 3 * HALF:4 * HALF])


def _inproj0(x, mod, g, w):
    tok = pl.BlockSpec((TM, HALF), lambda i: (i, 0))
    shp = jax.ShapeDtypeStruct((N_TOK, HALF), F32)
    return pl.pallas_call(
        _inproj0_kernel,
        grid=(N_TILES,),
        in_specs=[pl.BlockSpec((TM, D), lambda i: (i, 0)),
                  pl.BlockSpec((1, 1, 6 * D), _mod_row(TM)),
                  pl.BlockSpec((1, D), lambda i: (0, 0)),
                  pl.BlockSpec((D, 4 * HALF), lambda i: (0, 0))],
        out_specs=[tok, tok, tok],
        out_shape=[shp, shp, shp],
        compiler_params=_params("parallel"),
        name="inproj0",
    )(x, mod, g, w)


def _halo_specs(halo):
    per_tile = TM // halo
    last = N_TOK // halo - 1
    prev = pl.BlockSpec((halo, HALF), lambda i: (jnp.maximum(i * per_tile - 1, 0), 0))
    nxt = pl.BlockSpec((halo, HALF), lambda i: (jnp.minimum((i + 1) * per_tile, last), 0))
    return prev, nxt


def _seq_edges(i):
    is_ctx = i >= N_LAT_TILES
    first = jnp.logical_or(is_ctx, i % TILES_PER_SEQ == 0)
    last = jnp.logical_or(is_ctx, i % TILES_PER_SEQ == TILES_PER_SEQ - 1)
    return first, last


def _fill_padded(buf_ref, prev_ref, cur_ref, next_ref, halo):
    first, last = _seq_edges(pl.program_id(0))
    buf_ref[0:halo, :] = jnp.where(first, 0.0, prev_ref[...])
    buf_ref[halo:halo + TM, :] = cur_ref[...]
    buf_ref[halo + TM:halo + TM + halo, :] = jnp.where(last, 0.0, next_ref[...])


def _depthwise(buf_ref, w_ref, taps, first_off, rows, row0):
    acc = None
    for k in range(taps):
        term = w_ref[k:k + 1, :] * buf_ref[row0 + first_off + k:row0 + first_off + k + rows, :]
        acc = term if acc is None else acc + term
    return acc


CONV_ROWS = 32


def _conv0_kernel(ua_ref, uap_ref, uan_ref, ub_ref, ubp_ref, ubn_ref,
                  dww_ref, dwb_ref, lng_ref, lnb_ref, cw_ref, cb_ref,
                  a_ref, v_ref, bufa_ref, bufb_ref):
    _fill_padded(bufa_ref, uap_ref, ua_ref, uan_ref, HALO_A)
    _fill_padded(bufb_ref, ubp_ref, ub_ref, ubn_ref, HALO_S)
    for r in range(TM // CONV_ROWS):
        row0 = r * CONV_ROWS
        u = _depthwise(bufa_ref, dww_ref, CONV_A, HALO_A - CONV_A // 2, CONV_ROWS, row0) + dwb_ref[...]
        mu = jnp.mean(u, axis=-1, keepdims=True)
        uc = u - mu
        var = jnp.mean(uc * uc, axis=-1, keepdims=True)
        y = uc * lax.rsqrt(var + EPS) * lng_ref[...] + lnb_ref[...]
        a_ref[row0:row0 + CONV_ROWS, :] = _silu(y).astype(a_ref.dtype)
        v = _depthwise(bufb_ref, cw_ref, CONV_B, HALO_S - 2, CONV_ROWS, row0) + cb_ref[...]
        v_ref[row0:row0 + CONV_ROWS, :] = v


def _conv0(ua, ub, dw_w, dw_b, ln_g, ln_b, conv_w, conv_b):
    tok = pl.BlockSpec((TM, HALF), lambda i: (i, 0))
    pa, na = _halo_specs(HALO_A)
    ps, ns = _halo_specs(HALO_S)
    vec = pl.BlockSpec((1, HALF), lambda i: (0, 0))
    return pl.pallas_call(
        _conv0_kernel,
        grid=(N_TILES,),
        in_specs=[tok, pa, na, tok, ps, ns,
                  pl.BlockSpec((CONV_A, HALF), lambda i: (0, 0)), vec, vec, vec,
                  pl.BlockSpec((CONV_B, HALF), lambda i: (0, 0)), vec],
        out_specs=[tok, tok],
        out_shape=[jax.ShapeDtypeStruct((N_TOK, HALF), BF16), jax.ShapeDtypeStruct((N_TOK, HALF), F32)],
        scratch_shapes=[pltpu.VMEM((TM + 2 * HALO_A, HALF), F32), pltpu.VMEM((TM + 2 * HALO_S, HALF), F32)],
        compiler_params=_params("parallel"),
        name="conv0",
    )(ua, ua, ua, ub, ub, ub, dw_w, dw_b.reshape(1, HALF), ln_g.reshape(1, HALF), ln_b.reshape(1, HALF),
      conv_w, conv_b.reshape(1, HALF))


SCAN_UNROLL = 8


def _scan_kernel(v_ref, w_ref, gb_ref, lam_ref, y_ref, h_ref, a_ref, b_ref):
    d = pl.program_id(0)
    j = pl.program_id(2)

    @pl.when(j == 0)
    def _():
        h_ref[...] = jnp.zeros_like(h_ref)

    v = v_ref[...]
    vb = v.astype(BF16)
    lam = lam_ref[0]
    neg = -lam
    softplus = jnp.maximum(neg, 0.0) + jnp.log(1.0 + jnp.exp(-jnp.abs(neg)))
    for n in range(N_LRU_BLOCKS):
        sl = slice(n * LRU_BLOCK, (n + 1) * LRU_BLOCK)
        g = jnp.dot(vb[:, sl], w_ref[0, n], preferred_element_type=F32)
        r = _sigmoid(g[:, 0:LRU_BLOCK] + gb_ref[0, 0:1, sl])
        i = _sigmoid(g[:, LRU_BLOCK:2 * LRU_BLOCK] + gb_ref[0, 1:2, sl])
        log_a = -LRU_C * r * softplus[:, sl]
        a = jnp.exp(log_a)
        a_ref[:, sl] = a
        b_ref[:, sl] = jnp.sqrt(1.0 - jnp.exp(2.0 * log_a)) * (i * v[:, sl])

    def body(s, h):
        for u in range(SCAN_UNROLL):
            t = s * SCAN_UNROLL + u
            idx = jnp.where(d == 0, t, TM - 1 - t)
            h = a_ref[pl.ds(idx, 1), :] * h + b_ref[pl.ds(idx, 1), :]
            y_ref[0, pl.ds(idx, 1), :] = h
        return h

    h_ref[...] = lax.fori_loop(0, TM // SCAN_UNROLL, body, h_ref[...])


def _scan_tile(d, b, j):
    lat = b * TILES_PER_SEQ + jnp.where(d == 0, j - 1, TILES_PER_SEQ - j)
    return jnp.where(j == 0, N_LAT_TILES + b, lat)


def _scan(v, gate_w, gate_b, lam):
    return pl.pallas_call(
        _scan_kernel,
        grid=(2, BATCH, TILES_PER_SEQ + 1),
        in_specs=[pl.BlockSpec((TM, HALF), lambda d, b, j: (_scan_tile(d, b, j), 0)),
                  pl.BlockSpec((1, N_LRU_BLOCKS, LRU_BLOCK, 2 * LRU_BLOCK), lambda d, b, j: (d, 0, 0, 0)),
                  pl.BlockSpec((1, 2, HALF), lambda d, b, j: (d, 0, 0)),
                  pl.BlockSpec((1, 1, HALF), lambda d, b, j: (d, 0, 0))],
        out_specs=pl.BlockSpec((1, TM, HALF), lambda d, b, j: (d, _scan_tile(d, b, j), 0)),
        out_shape=jax.ShapeDtypeStruct((2, N_TOK, HALF), F32),
        scratch_shapes=[pltpu.VMEM((1, HALF), F32), pltpu.VMEM((TM, HALF), F32), pltpu.VMEM((TM, HALF), F32)],
        compiler_params=_params("arbitrary", "arbitrary", "arbitrary"),
        name="lru_scan",
    )(v, gate_w, gate_b, lam)


def _route(score, sel):
    rows = [sel[e:e + 1, :] for e in range(N_EXPERTS)]
    gbest = None
    gidx = None
    for g in range(N_GROUPS):
        top2 = None
        for p in range(PER_GROUP):
            for q in range(p + 1, PER_GROUP):
                s = rows[g * PER_GROUP + p] + rows[g * PER_GROUP + q]
                top2 = s if top2 is None else jnp.maximum(top2, s)
        if g == 0:
            gbest = top2
            gidx = jnp.zeros(top2.shape, jnp.int32)
        else:
            better = top2 > gbest
            gidx = jnp.where(better, g, gidx)
            gbest = jnp.where(better, top2, gbest)
    eint = lax.broadcasted_iota(jnp.int32, sel.shape, 0)
    eidx = eint.astype(F32)
    masked = jnp.where(jnp.right_shift(eint, 2) == gidx, sel, -jnp.inf)
    v1 = jnp.max(masked, axis=0, keepdims=True)
    i1 = jnp.min(jnp.where(masked == v1, eidx, float(N_EXPERTS)), axis=0, keepdims=True)
    masked2 = jnp.where(eidx == i1, -jnp.inf, masked)
    v2 = jnp.max(masked2, axis=0, keepdims=True)
    i2 = jnp.min(jnp.where(masked2 == v2, eidx, float(N_EXPERTS)), axis=0, keepdims=True)
    s1 = jnp.sum(jnp.where(eidx == i1, score, 0.0), axis=0, keepdims=True)
    s2 = jnp.sum(jnp.where(eidx == i2, score, 0.0), axis=0, keepdims=True)
    inv = 1.0 / (s1 + s2)
    return jnp.where(eidx == i1, s1 * inv, 0.0) + jnp.where(eidx == i2, s2 * inv, 0.0)


def _finish_outproj(m1, m2, x_ref, mod_ref, wo_ref, g_ref, rw_ref, rb_ref, h_ref, n_ref, comb_ref):
    mod = mod_ref[0]
    mix = (jnp.dot(m1, wo_ref[0:HALF, :], preferred_element_type=F32)
           + jnp.dot(m2, wo_ref[HALF:2 * HALF, :], preferred_element_type=F32))
    h = x_ref[...] + mod[:, 2 * D:3 * D] * mix
    h_ref[...] = h
    n = _rms_mod(h, g_ref[...], mod[:, 4 * D:5 * D], mod[:, 3 * D:4 * D])
    n_ref[...] = n.astype(n_ref.dtype)
    logits = lax.dot_general(rw_ref[...], n, (((1,), (1,)), ((), ())), preferred_element_type=F32,
                             precision=lax.Precision.HIGHEST)
    score = _sigmoid(logits)
    comb_ref[...] = _route(score, score + rb_ref[...])


def _outproj0_kernel(a_ref, y_ref, gg_ref, x_ref, mod_ref, wo_ref, g_ref, rw_ref, rb_ref, h_ref, n_ref, comb_ref):
    m2 = ((y_ref[0] + y_ref[1]) * gg_ref[...]).astype(BF16)
    _finish_outproj(a_ref[...], m2, x_ref, mod_ref, wo_ref, g_ref, rw_ref, rb_ref, h_ref, n_ref, comb_ref)


def _outproj1_kernel(cx_ref, cxp_ref, cxn_ref, bg_ref, att_ref, cw_ref, x_ref, mod_ref, wo_ref, g_ref, rw_ref, rb_ref,
                     h_ref, n_ref, comb_ref, buf_ref):
    _fill_padded(buf_ref, cxp_ref, cx_ref, cxn_ref, HALO_S)
    conv = _depthwise(buf_ref, cw_ref, CONV_C, HALO_S - 1, TM, 0)
    m1 = (bg_ref[...] * conv).astype(BF16)
    _finish_outproj(m1, att_ref[...], x_ref, mod_ref, wo_ref, g_ref, rw_ref, rb_ref, h_ref, n_ref, comb_ref)


def _outproj_common(n_tiles):
    in_specs = [pl.BlockSpec((TM, D), lambda i: (i, 0)),
                pl.BlockSpec((1, 1, 6 * D), _mod_row(TM)),
                pl.BlockSpec((D, D), lambda i: (0, 0)),
                pl.BlockSpec((1, D), lambda i: (0, 0)),
                pl.BlockSpec((N_EXPERTS, D), lambda i: (0, 0)),
                pl.BlockSpec((N_EXPERTS, 1), lambda i: (0, 0))]
    out_specs = [pl.BlockSpec((TM, D), lambda i: (i, 0)),
                 pl.BlockSpec((TM, D), lambda i: (i, 0)),
                 pl.BlockSpec((N_EXPERTS, TM), lambda i: (0, i))]
    rows = n_tiles * TM
    out_shape = [jax.ShapeDtypeStruct((rows, D), F32), jax.ShapeDtypeStruct((rows, D), BF16),
                 jax.ShapeDtypeStruct((N_EXPERTS, rows), F32)]
    return in_specs, out_specs, out_shape


def _outproj0(a, y, gg, x, mod, wo, g, rw_t, rb):
    tok = pl.BlockSpec((TM, HALF), lambda i: (i, 0))
    common_in, out_specs, out_shape = _outproj_common(N_TILES)
    return pl.pallas_call(
        _outproj0_kernel,
        grid=(N_TILES,),
        in_specs=[tok, pl.BlockSpec((2, TM, HALF), lambda i: (0, i, 0)), tok] + common_in,
        out_specs=out_specs,
        out_shape=out_shape,
        compiler_params=_params("parallel"),
        name="outproj0",
    )(a, y, gg, x, mod, wo, g, rw_t, rb)


def _outproj1(cx, bg, att, conv_w, x, mod, wo, g, rw_t, rb):
    tok = pl.BlockSpec((TM, HALF), lambda i: (i, 0))
    ps, ns = _halo_specs(HALO_S)
    common_in, out_specs, out_shape = _outproj_common(N_LAT_TILES)
    return pl.pallas_call(
        _outproj1_kernel,
        grid=(N_LAT_TILES,),
        in_specs=[tok, ps, ns, tok, tok, pl.BlockSpec((CONV_C, HALF), lambda i: (0, 0))] + common_in,
        out_specs=out_specs,
        out_shape=out_shape,
        scratch_shapes=[pltpu.VMEM((TM + 2 * HALO_S, HALF), F32)],
        compiler_params=_params("parallel"),
        name="outproj1",
    )(cx, cx, cx, bg, att, conv_w, x, mod, wo, g, rw_t, rb)


def _moe_kernel(n_ref, comb_ref, wg_ref, wu_ref, wd_ref, h_ref, mod_ref, fg_ref, o_ref, acc_ref, *, final_norm):
    e = pl.program_id(1)

    @pl.when(e == 0)
    def _():
        acc_ref[...] = jnp.zeros_like(acc_ref)

    x = n_ref[...]
    hid = _silu(jnp.dot(x, wg_ref[0], preferred_element_type=F32)) * jnp.dot(x, wu_ref[0], preferred_element_type=F32)
    comb = comb_ref[...]
    lane = lax.broadcasted_iota(jnp.int32, comb.shape, 1)
    cw = jnp.sum(jnp.where(lane == e, comb, 0.0), axis=1, keepdims=True)
    acc_ref[...] += jnp.dot((hid * cw).astype(BF16), wd_ref[0], preferred_element_type=F32)

    @pl.when(e == N_EXPERTS - 1)
    def _():
        out = h_ref[...] + mod_ref[0][:, 5 * D:6 * D] * acc_ref[...]
        if final_norm:
            out = out * lax.rsqrt(jnp.mean(out * out, axis=-1, keepdims=True) + EPS) * fg_ref[...]
        o_ref[...] = out


def _moe(n, comb, wg, wu, wd, h, mod, final_g, final_norm):
    rows = n.shape[0]
    tok = pl.BlockSpec((TMOE, D), lambda i, e: (i, 0))
    return pl.pallas_call(
        functools.partial(_moe_kernel, final_norm=final_norm),
        grid=(rows // TMOE, N_EXPERTS),
        in_specs=[tok,
                  pl.BlockSpec((TMOE, N_EXPERTS), lambda i, e: (i, 0)),
                  pl.BlockSpec((1, D, D_FF), lambda i, e: (e, 0, 0)),
                  pl.BlockSpec((1, D, D_FF), lambda i, e: (e, 0, 0)),
                  pl.BlockSpec((1, D_FF, D), lambda i, e: (e, 0, 0)),
                  tok,
                  pl.BlockSpec((1, 1, 6 * D), lambda i, e: _mod_row(TMOE)(i)),
                  pl.BlockSpec((1, D), lambda i, e: (0, 0))],
        out_specs=tok,
        out_shape=jax.ShapeDtypeStruct((rows, D), F32),
        scratch_shapes=[pltpu.VMEM((TMOE, D), F32)],
        compiler_params=_params("parallel", "arbitrary"),
        name="moe_final" if final_norm else "moe",
    )(n, comb, wg, wu, wd, h, mod, final_g)


def _inproj1_kernel(x_ref, mod_ref, g_ref, w_ref, cx_ref, bg_ref, q_ref, k_ref, v_ref):
    mod = mod_ref[0]
    n = _rms_mod(x_ref[...], g_ref[...], mod[:, D:2 * D], mod[:, 0:D])
    p = jnp.dot(n.astype(BF16), w_ref[...], preferred_element_type=F32)
    cx_ref[...] = p[:, 2 * HALF:3 * HALF] * p[:, 0:HALF]
    bg_ref[...] = p[:, HALF:2 * HALF]
    q_ref[...] = (p[:, 3 * HALF:4 * HALF] * (HEAD_DIM ** -0.5)).astype(BF16)
    k_ref[...] = p[:, 4 * HALF:5 * HALF].astype(BF16)
    v_ref[...] = p[:, 5 * HALF:6 * HALF].astype(BF16)


def _inproj1(x, mod, g, w):
    tok = pl.BlockSpec((TM, HALF), lambda i: (i, 0))
    f = jax.ShapeDtypeStruct((N_TOK, HALF), F32)
    h = jax.ShapeDtypeStruct((N_TOK, HALF), BF16)
    return pl.pallas_call(
        _inproj1_kernel,
        grid=(N_TILES,),
        in_specs=[pl.BlockSpec((TM, D), lambda i: (i, 0)),
                  pl.BlockSpec((1, 1, 6 * D), _mod_row(TM)),
                  pl.BlockSpec((1, D), lambda i: (0, 0)),
                  pl.BlockSpec((D, 6 * HALF), lambda i: (0, 0))],
        out_specs=[tok] * 5,
        out_shape=[f, f, h, h, h],
        compiler_params=_params("parallel"),
        name="inproj1",
    )(x, mod, g, w)


def _natten_kernel(q_ref, kp_ref, kc_ref, kn_ref, vp_ref, vc_ref, vn_ref, kx_ref, vx_ref, bias_ref, o_ref):
    lane = lax.broadcasted_iota(jnp.int32, (TM, 2 * HEAD_DIM), 1)
    low = lane < HEAD_DIM
    head_mask = [low.astype(F32).astype(BF16), jnp.logical_not(low).astype(F32).astype(BF16)]
    for g in range(N_HEADS // 2):
        sl = slice(2 * HEAD_DIM * g, 2 * HEAD_DIM * (g + 1))
        q2 = q_ref[:, sl]
        keys = [kp_ref[:, sl], kc_ref[:, sl], kn_ref[:, sl], kx_ref[:, sl]]
        vals = [vp_ref[:, sl], vc_ref[:, sl], vn_ref[:, sl], vx_ref[:, sl]]
        outs = []
        for hh in range(2):
            qm = q2 * head_mask[hh]
            s = [_nt_dot(qm, k) for k in keys]
            for t in range(3):
                s[t] = s[t] + bias_ref[0, 2 * g + hh, :, t * TM:(t + 1) * TM]
            m = jnp.max(jnp.maximum(jnp.maximum(s[0], s[1]), jnp.maximum(s[2], s[3])), axis=-1, keepdims=True)
            p = [jnp.exp(t - m) for t in s]
            den = jnp.sum(p[0] + p[1] + p[2] + p[3], axis=-1, keepdims=True)
            o = None
            for pt, vt in zip(p, vals):
                term = jnp.dot(pt.astype(BF16), vt, preferred_element_type=F32)
                o = term if o is None else o + term
            outs.append(o * (1.0 / den))
        o_ref[:, sl] = jnp.where(low, outs[0], outs[1]).astype(o_ref.dtype)


def _natten_bias(rpb):
    n_rows = SEQ // GRID_W
    n_dr, n_dc = 2 * WIN_H - 1, 2 * WIN_W - 1
    i = np.arange(ROWS_Q)
    j = np.arange(3 * ROWS_Q)
    col = np.arange(GRID_W)
    col_start = np.clip(col - WIN_W // 2, 0, GRID_W - WIN_W)
    col_ok = (col[None, :] >= col_start[:, None]) & (col[None, :] < col_start[:, None] + WIN_W)
    col_idx = col[None, :] - col[:, None] + (WIN_W - 1)
    onehot = ((col_idx[None] == np.arange(n_dc)[:, None, None]) & col_ok[None]).astype(np.float32)
    col_exp = jnp.dot(rpb.reshape(N_HEADS * n_dr, n_dc).astype(F32), onehot.reshape(n_dc, GRID_W * GRID_W),
                      precision=lax.Precision.HIGHEST).reshape(N_HEADS, n_dr, GRID_W, GRID_W)
    kinds = []
    for r0 in (0, ROWS_Q, n_rows - ROWS_Q):
        r = r0 + i
        kr = r0 - ROWS_Q + j
        r_start = np.clip(r - WIN_H // 2, 0, n_rows - WIN_H)
        row_ok = ((kr[None, :] >= r_start[:, None]) & (kr[None, :] < r_start[:, None] + WIN_H)
                  & (kr[None, :] >= 0) & (kr[None, :] < n_rows))
        row_idx = np.clip(kr[None, :] - r[:, None] + (WIN_H - 1), 0, n_dr - 1)
        b = col_exp[:, row_idx]
        b = b.transpose(0, 1, 3, 2, 4)
        ok = row_ok[:, None, :, None] & col_ok[None, :, None, :]
        kinds.append(jnp.where(ok[None], b, NEG).reshape(N_HEADS, TM, 3 * TM))
    return jnp.stack(kinds).astype(F32)


def _natten(q, k, v, bias):
    def lat(off):
        def index(b, i):
            return (b * TILES_PER_SEQ + jnp.clip(i + off, 0, TILES_PER_SEQ - 1), 0)
        return pl.BlockSpec((TM, HALF), index)

    ctx = pl.BlockSpec((TM, HALF), lambda b, i: (N_LAT_TILES + b, 0))

    def kind(b, i):
        return (jnp.where(i == 0, 0, jnp.where(i == TILES_PER_SEQ - 1, 2, 1)), 0, 0, 0)

    return pl.pallas_call(
        _natten_kernel,
        grid=(BATCH, TILES_PER_SEQ),
        in_specs=[lat(0), lat(-1), lat(0), lat(1), lat(-1), lat(0), lat(1), ctx, ctx,
                  pl.BlockSpec((1, N_HEADS, TM, 3 * TM), kind)],
        out_specs=lat(0),
        out_shape=jax.ShapeDtypeStruct((N_LAT, HALF), BF16),
        compiler_params=_params("parallel", "arbitrary"),
        name="natten",
    )(q, k, k, k, v, v, v, k, v, bias)


def kernel(x, c, ctx, c_ctx, ada_w, ada_b, norm_mix_g, norm_ffn_g, w_out, ab_w_in, a_dw_w, a_dw_b, a_ln_g, a_ln_b,
           b_conv_w, b_conv_b, b_gate_w, b_gate_b, b_lambda, cd_w_in, c_conv_w, d_rpb, router_w, router_bias,
           moe_w_gate, moe_w_up, moe_w_down, final_g):
    tokens = jnp.concatenate([x.reshape(N_LAT, D), ctx.reshape(BATCH * CTX, D)], axis=0)
    cond = jnp.concatenate([c, c_ctx[None], jnp.zeros((8 - BATCH - 1, D), F32)], axis=0)
    mod = _modulation(cond, ada_w, ada_b)
    mod0 = mod[0].reshape(8, 1, 6 * D)
    mod1 = mod[1].reshape(8, 1, 6 * D)

    wg = moe_w_gate.astype(BF16)
    wu = moe_w_up.astype(BF16)
    wd = moe_w_down.astype(BF16)
    wo = w_out.astype(BF16)
    rw_t = router_w.T
    rb = router_bias.reshape(N_EXPERTS, 1)
    fg = final_g.reshape(1, D)

    ua, ub, gg = _inproj0(tokens, mod0, norm_mix_g[0].reshape(1, D), ab_w_in[0].astype(BF16))
    a_out, v = _conv0(ua, ub, a_dw_w[0], a_dw_b[0], a_ln_g[0], a_ln_b[0], b_conv_w[0], b_conv_b[0])
    gw = b_gate_w[0]
    gate_w = jnp.concatenate([gw[:, 0], gw[:, 1]], axis=-1).astype(BF16)
    y = _scan(v, gate_w, b_gate_b[0], b_lambda[0].reshape(2, 1, HALF))
    h1, n2, comb_t = _outproj0(a_out, y, gg, tokens, mod0, wo[0], norm_ffn_g[0].reshape(1, D), rw_t, rb)
    h2 = _moe(n2, comb_t.T, wg[0], wu[0], wd[0], h1, mod0, fg, False)

    cx, bg, q, k, vv = _inproj1(h2, mod1, norm_mix_g[1].reshape(1, D), cd_w_in[0].astype(BF16))
    att = _natten(q, k, vv, _natten_bias(d_rpb[0]))
    h3, n4, comb_t1 = _outproj1(cx, bg, att, c_conv_w[0], h2, mod1, wo[1], norm_ffn_g[1].reshape(1, D), rw_t, rb)
    out = _moe(n4, comb_t1.T, wg[1], wu[1], wd[1], h3, mod1, fg, True)
    return out.reshape(BATCH, SEQ, D)
```

```python
import functools

import jax
import jax.numpy as jnp
import numpy as np
from jax import lax
from jax.experimental import pallas as pl
from jax.experimental.pallas import tpu as pltpu

F32 = jnp.float32
BF16 = jnp.bfloat16

D = 1024
BATCH = 2
SEQ = 8192
CTX = 256
GRID_W = 64
N_LAT = BATCH * SEQ
N_TOK = N_LAT + BATCH * CTX
HALF = 512
CONV_A = 31
CONV_B = 4
CONV_C = 3
LRU_BLOCK = 128
N_LRU_BLOCKS = HALF // LRU_BLOCK
LRU_C = 8.0
HEAD_DIM = 64
N_HEADS = HALF // HEAD_DIM
WIN_H = 8
WIN_W = 16
N_EXPERTS = 16
N_GROUPS = 4
PER_GROUP = N_EXPERTS // N_GROUPS
D_FF = 512
EPS = 1e-6
NEG = -1e30

TM = 256
TILES_PER_SEQ = SEQ // TM
N_LAT_TILES = N_LAT // TM
N_TILES = N_TOK // TM
TMOE = 512
CAP = 160
CAP_PAD = 256
ROUTE_ROWS = 24
ROUTE_GROUP = 16
ROUTE_RANK = 17
HALO_A = 16
HALO_S = 8
ROWS_Q = TM // GRID_W
VMEM_LIMIT = 48 * 1024 * 1024


def _params(*sem):
    return pltpu.CompilerParams(dimension_semantics=sem, vmem_limit_bytes=VMEM_LIMIT)


def _sigmoid(x):
    return 1.0 / (1.0 + jnp.exp(-x))


def _silu(x):
    return x * _sigmoid(x)


def _gelu_tanh(x):
    return 0.5 * x * (1.0 + jnp.tanh(0.7978845608028654 * (x + 0.044715 * (x * x * x))))


def _rms_mod(x, g, scale, shift):
    y = x * lax.rsqrt(jnp.mean(x * x, axis=-1, keepdims=True) + EPS) * g
    return y * (1.0 + scale) + shift


def _nt_dot(a, b):
    return lax.dot_general(a, b, (((1,), (1,)), ((), ())), preferred_element_type=F32)


def _mod_row(tile_rows):
    per_seq = SEQ // tile_rows
    return lambda i: (jnp.minimum(i // per_seq, BATCH), 0, 0)


def _mod_kernel(c_ref, w_ref, b_ref, o_ref):
    c = c_ref[...]
    s = _silu(c).astype(BF16)
    o_ref[0] = jnp.dot(s, w_ref[0].astype(BF16), preferred_element_type=F32) + b_ref[0]


def _modulation(cond, ada_w, ada_b):
    depth = ada_w.shape[0]
    nb = 1536
    return pl.pallas_call(
        _mod_kernel,
        grid=(depth, 6 * D // nb),
        in_specs=[pl.BlockSpec((8, D), lambda l, j: (0, 0)),
                  pl.BlockSpec((1, D, nb), lambda l, j: (l, 0, j)),
                  pl.BlockSpec((1, 1, nb), lambda l, j: (l, 0, j))],
        out_specs=pl.BlockSpec((1, 8, nb), lambda l, j: (l, 0, j)),
        out_shape=jax.ShapeDtypeStruct((depth, 8, 6 * D), F32),
        compiler_params=_params("parallel", "parallel"),
        name="modulation",
    )(cond, ada_w, ada_b.reshape(depth, 1, 6 * D))


def _inproj0_kernel(x_ref, mod_ref, g_ref, w_ref, ua_ref, ub_ref, gg_ref):
    mod = mod_ref[0]
    n = _rms_mod(x_ref[...], g_ref[...], mod[:, D:2 * D], mod[:, 0:D])
    p = jnp.dot(n.astype(BF16), w_ref[...], preferred_element_type=F32)
    ua_ref[...] = p[:, 0:HALF] * _sigmoid(p[:, HALF:2 * HALF])
    ub_ref[...] = p[:, 2 * HALF:3 * HALF]
    gg_ref[...] = _gelu_tanh(p[:, 3 * HALF:4 * HALF])


def _inproj0(x, mod, g, w):
    tok = pl.BlockSpec((TM, HALF), lambda i: (i, 0))
    shp = jax.ShapeDtypeStruct((N_TOK, HALF), F32)
    return pl.pallas_call(
        _inproj0_kernel,
        grid=(N_TILES,),
        in_specs=[pl.BlockSpec((TM, D), lambda i: (i, 0)),
                  pl.BlockSpec((1, 1, 6 * D), _mod_row(TM)),
                  pl.BlockSpec((1, D), lambda i: (0, 0)),
                  pl.BlockSpec((D, 4 * HALF), lambda i: (0, 0))],
        out_specs=[tok, tok, tok],
        out_shape=[shp, shp, shp],
        compiler_params=_params("parallel"),
        name="inproj0",
    )(x, mod, g, w)


def _halo_specs(halo):
    per_tile = TM // halo
    last = N_TOK // halo - 1
    prev = pl.BlockSpec((halo, HALF), lambda i: (jnp.maximum(i * per_tile - 1, 0), 0))
    nxt = pl.BlockSpec((halo, HALF), lambda i: (jnp.minimum((i + 1) * per_tile, last), 0))
    return prev, nxt


def _seq_edges(i):
    is_ctx = i >= N_LAT_TILES
    first = jnp.logical_or(is_ctx, i % TILES_PER_SEQ == 0)
    last = jnp.logical_or(is_ctx, i % TILES_PER_SEQ == TILES_PER_SEQ - 1)
    return first, last


def _fill_padded(buf_ref, prev_ref, cur_ref, next_ref, halo):
    first, last = _seq_edges(pl.program_id(0))
    buf_ref[0:halo, :] = jnp.where(first, 0.0, prev_ref[...])
    buf_ref[halo:halo + TM, :] = cur_ref[...]
    buf_ref[halo + TM:halo + TM + halo, :] = jnp.where(last, 0.0, next_ref[...])


def _depthwise(buf_ref, w_ref, taps, first_off, rows, row0):
    acc = None
    for k in range(taps):
        term = w_ref[k:k + 1, :] * buf_ref[row0 + first_off + k:row0 + first_off + k + rows, :]
        acc = term if acc is None else acc + term
    return acc


CONV_ROWS = 32


def _conv0_kernel(ua_ref, uap_ref, uan_ref, ub_ref, ubp_ref, ubn_ref,
                  dww_ref, dwb_ref, lng_ref, lnb_ref, cw_ref, cb_ref,
                  a_ref, v_ref, bufa_ref, bufb_ref):
    _fill_padded(bufa_ref, uap_ref, ua_ref, uan_ref, HALO_A)
    _fill_padded(bufb_ref, ubp_ref, ub_ref, ubn_ref, HALO_S)
    for r in range(TM // CONV_ROWS):
        row0 = r * CONV_ROWS
        u = _depthwise(bufa_ref, dww_ref, CONV_A, HALO_A - CONV_A // 2, CONV_ROWS, row0) + dwb_ref[...]
        mu = jnp.mean(u, axis=-1, keepdims=True)
        uc = u - mu
        var = jnp.mean(uc * uc, axis=-1, keepdims=True)
        y = uc * lax.rsqrt(var + EPS) * lng_ref[...] + lnb_ref[...]
        a_ref[row0:row0 + CONV_ROWS, :] = _silu(y).astype(a_ref.dtype)
        v = _depthwise(bufb_ref, cw_ref, CONV_B, HALO_S - 2, CONV_ROWS, row0) + cb_ref[...]
        v_ref[row0:row0 + CONV_ROWS, :] = v


def _conv0(ua, ub, dw_w, dw_b, ln_g, ln_b, conv_w, conv_b):
    tok = pl.BlockSpec((TM, HALF), lambda i: (i, 0))
    pa, na = _halo_specs(HALO_A)
    ps, ns = _halo_specs(HALO_S)
    vec = pl.BlockSpec((1, HALF), lambda i: (0, 0))
    return pl.pallas_call(
        _conv0_kernel,
        grid=(N_TILES,),
        in_specs=[tok, pa, na, tok, ps, ns,
                  pl.BlockSpec((CONV_A, HALF), lambda i: (0, 0)), vec, vec, vec,
                  pl.BlockSpec((CONV_B, HALF), lambda i: (0, 0)), vec],
        out_specs=[tok, tok],
        out_shape=[jax.ShapeDtypeStruct((N_TOK, HALF), BF16), jax.ShapeDtypeStruct((N_TOK, HALF), F32)],
        scratch_shapes=[pltpu.VMEM((TM + 2 * HALO_A, HALF), F32), pltpu.VMEM((TM + 2 * HALO_S, HALF), F32)],
        compiler_params=_params("parallel"),
        name="conv0",
    )(ua, ua, ua, ub, ub, ub, dw_w, dw_b.reshape(1, HALF), ln_g.reshape(1, HALF), ln_b.reshape(1, HALF),
      conv_w, conv_b.reshape(1, HALF))


SCAN_UNROLL = 8


def _scan_kernel(v_ref, w_ref, gb_ref, lam_ref, y_ref, h_ref, a_ref, b_ref):
    d = pl.program_id(0)
    j = pl.program_id(2)

    @pl.when(j == 0)
    def _():
        h_ref[...] = jnp.zeros_like(h_ref)

    v = v_ref[...]
    vb = v.astype(BF16)
    lam = lam_ref[0]
    neg = -lam
    softplus = jnp.maximum(neg, 0.0) + jnp.log(1.0 + jnp.exp(-jnp.abs(neg)))
    for n in range(N_LRU_BLOCKS):
        sl = slice(n * LRU_BLOCK, (n + 1) * LRU_BLOCK)
        g = jnp.dot(vb[:, sl], w_ref[0, n], preferred_element_type=F32)
        r = _sigmoid(g[:, 0:LRU_BLOCK] + gb_ref[0, 0:1, sl])
        i = _sigmoid(g[:, LRU_BLOCK:2 * LRU_BLOCK] + gb_ref[0, 1:2, sl])
        log_a = -LRU_C * r * softplus[:, sl]
        a = jnp.exp(log_a)
        a_ref[:, sl] = a
        b_ref[:, sl] = jnp.sqrt(1.0 - jnp.exp(2.0 * log_a)) * (i * v[:, sl])

    def body(s, h):
        for u in range(SCAN_UNROLL):
            t = s * SCAN_UNROLL + u
            idx = jnp.where(d == 0, t, TM - 1 - t)
            h = a_ref[pl.ds(idx, 1), :] * h + b_ref[pl.ds(idx, 1), :]
            y_ref[0, pl.ds(idx, 1), :] = h
        return h

    h_ref[...] = lax.fori_loop(0, TM // SCAN_UNROLL, body, h_ref[...])


def _scan_tile(d, b, j):
    lat = b * TILES_PER_SEQ + jnp.where(d == 0, j - 1, TILES_PER_SEQ - j)
    return jnp.where(j == 0, N_LAT_TILES + b, lat)


def _scan(v, gate_w, gate_b, lam):
    return pl.pallas_call(
        _scan_kernel,
        grid=(2, BATCH, TILES_PER_SEQ + 1),
        in_specs=[pl.BlockSpec((TM, HALF), lambda d, b, j: (_scan_tile(d, b, j), 0)),
                  pl.BlockSpec((1, N_LRU_BLOCKS, LRU_BLOCK, 2 * LRU_BLOCK), lambda d, b, j: (d, 0, 0, 0)),
                  pl.BlockSpec((1, 2, HALF), lambda d, b, j: (d, 0, 0)),
                  pl.BlockSpec((1, 1, HALF), lambda d, b, j: (d, 0, 0))],
        out_specs=pl.BlockSpec((1, TM, HALF), lambda d, b, j: (d, _scan_tile(d, b, j), 0)),
        out_shape=jax.ShapeDtypeStruct((2, N_TOK, HALF), F32),
        scratch_shapes=[pltpu.VMEM((1, HALF), F32), pltpu.VMEM((TM, HALF), F32), pltpu.VMEM((TM, HALF), F32)],
        compiler_params=_params("arbitrary", "arbitrary", "arbitrary"),
        name="lru_scan",
    )(v, gate_w, gate_b, lam)


def _route(score, sel):
    rows = [sel[e:e + 1, :] for e in range(N_EXPERTS)]
    gbest = None
    gidx = None
    for g in range(N_GROUPS):
        top2 = None
        for p in range(PER_GROUP):
            for q in range(p + 1, PER_GROUP):
                s = rows[g * PER_GROUP + p] + rows[g * PER_GROUP + q]
                top2 = s if top2 is None else jnp.maximum(top2, s)
        if g == 0:
            gbest = top2
            gidx = jnp.zeros(top2.shape, jnp.int32)
        else:
            better = top2 > gbest
            gidx = jnp.where(better, g, gidx)
            gbest = jnp.where(better, top2, gbest)
    eint = lax.broadcasted_iota(jnp.int32, sel.shape, 0)
    eidx = eint.astype(F32)
    masked = jnp.where(jnp.right_shift(eint, 2) == gidx, sel, -jnp.inf)
    v1 = jnp.max(masked, axis=0, keepdims=True)
    i1 = jnp.min(jnp.where(masked == v1, eidx, float(N_EXPERTS)), axis=0, keepdims=True)
    masked2 = jnp.where(eidx == i1, -jnp.inf, masked)
    v2 = jnp.max(masked2, axis=0, keepdims=True)
    i2 = jnp.min(jnp.where(masked2 == v2, eidx, float(N_EXPERTS)), axis=0, keepdims=True)
    s1 = jnp.sum(jnp.where(eidx == i1, score, 0.0), axis=0, keepdims=True)
    s2 = jnp.sum(jnp.where(eidx == i2, score, 0.0), axis=0, keepdims=True)
    inv = 1.0 / (s1 + s2)
    return jnp.where(eidx == i1, s1 * inv, 0.0) + jnp.where(eidx == i2, s2 * inv, 0.0), gidx


def _group_ranks(gidx, carry_ref):
    @pl.when(pl.program_id(0) % (TMOE // TM) == 0)
    def _():
        carry_ref[...] = jnp.zeros_like(carry_ref)

    onehot = lax.broadcasted_iota(jnp.int32, (8, TM), 0) == gidx
    oh = jnp.where(onehot, 1.0, 0.0)
    before = lax.broadcasted_iota(jnp.int32, (TM, TM), 0) < lax.broadcasted_iota(jnp.int32, (TM, TM), 1)
    prefix = jnp.dot(oh.astype(BF16), jnp.where(before, 1.0, 0.0).astype(BF16), preferred_element_type=F32)
    carry = carry_ref[...]
    rank = jnp.sum(jnp.where(onehot, prefix + carry[:, 0:1], 0.0), axis=0, keepdims=True)
    carry = carry + jnp.sum(oh, axis=1, keepdims=True)
    carry_ref[...] = carry
    return rank, carry


def _finish_outproj(m1, m2, x_ref, mod_ref, wo_ref, g_ref, rw_ref, rb_ref, h_ref, n_ref, route_ref, cnt_ref, carry_ref):
    mod = mod_ref[0]
    mix = (jnp.dot(m1, wo_ref[0:HALF, :], preferred_element_type=F32)
           + jnp.dot(m2, wo_ref[HALF:2 * HALF, :], preferred_element_type=F32))
    h = x_ref[...] + mod[:, 2 * D:3 * D] * mix
    h_ref[...] = h
    n = _rms_mod(h, g_ref[...], mod[:, 4 * D:5 * D], mod[:, 3 * D:4 * D])
    n_ref[...] = n.astype(n_ref.dtype)
    logits = lax.dot_general(rw_ref[...], n, (((1,), (1,)), ((), ())), preferred_element_type=F32,
                             precision=lax.Precision.HIGHEST)
    score = _sigmoid(logits)
    comb, gidx = _route(score, score + rb_ref[...])
    rank, counts = _group_ranks(gidx, carry_ref)
    route_ref[0:N_EXPERTS, :] = comb
    route_ref[ROUTE_GROUP:ROUTE_GROUP + 1, :] = gidx.astype(F32)
    route_ref[ROUTE_RANK:ROUTE_RANK + 1, :] = rank
    route_ref[ROUTE_RANK + 1:ROUTE_ROWS, :] = jnp.zeros((ROUTE_ROWS - ROUTE_RANK - 1, TM), F32)
    cnt_ref[0] = counts


def _outproj0_kernel(a_ref, y_ref, gg_ref, x_ref, mod_ref, wo_ref, g_ref, rw_ref, rb_ref,
                     h_ref, n_ref, route_ref, cnt_ref, carry_ref):
    m2 = ((y_ref[0] + y_ref[1]) * gg_ref[...]).astype(BF16)
    _finish_outproj(a_ref[...], m2, x_ref, mod_ref, wo_ref, g_ref, rw_ref, rb_ref, h_ref, n_ref, route_ref, cnt_ref,
                    carry_ref)


def _outproj1_kernel(cx_ref, cxp_ref, cxn_ref, bg_ref, att_ref, cw_ref, x_ref, mod_ref, wo_ref, g_ref, rw_ref, rb_ref,
                     h_ref, n_ref, route_ref, cnt_ref, buf_ref, carry_ref):
    _fill_padded(buf_ref, cxp_ref, cx_ref, cxn_ref, HALO_S)
    conv = _depthwise(buf_ref, cw_ref, CONV_C, HALO_S - 1, TM, 0)
    m1 = (bg_ref[...] * conv).astype(BF16)
    _finish_outproj(m1, att_ref[...], x_ref, mod_ref, wo_ref, g_ref, rw_ref, rb_ref, h_ref, n_ref, route_ref, cnt_ref,
                    carry_ref)


def _outproj_common(n_tiles):
    in_specs = [pl.BlockSpec((TM, D), lambda i: (i, 0)),
                pl.BlockSpec((1, 1, 6 * D), _mod_row(TM)),
                pl.BlockSpec((D, D), lambda i: (0, 0)),
                pl.BlockSpec((1, D), lambda i: (0, 0)),
                pl.BlockSpec((N_EXPERTS, D), lambda i: (0, 0)),
                pl.BlockSpec((N_EXPERTS, 1), lambda i: (0, 0))]
    out_specs = [pl.BlockSpec((TM, D), lambda i: (i, 0)),
                 pl.BlockSpec((TM, D), lambda i: (i, 0)),
                 pl.BlockSpec((ROUTE_ROWS, TM), lambda i: (0, i)),
                 pl.BlockSpec((1, 8, 128), lambda i: (i, 0, 0))]
    rows = n_tiles * TM
    out_shape = [jax.ShapeDtypeStruct((rows, D), F32), jax.ShapeDtypeStruct((rows, D), BF16),
                 jax.ShapeDtypeStruct((ROUTE_ROWS, rows), F32), jax.ShapeDtypeStruct((n_tiles, 8, 128), F32)]
    return in_specs, out_specs, out_shape


def _outproj0(a, y, gg, x, mod, wo, g, rw_t, rb):
    tok = pl.BlockSpec((TM, HALF), lambda i: (i, 0))
    common_in, out_specs, out_shape = _outproj_common(N_TILES)
    return pl.pallas_call(
        _outproj0_kernel,
        grid=(N_TILES,),
        in_specs=[tok, pl.BlockSpec((2, TM, HALF), lambda i: (0, i, 0)), tok] + common_in,
        out_specs=out_specs,
        out_shape=out_shape,
        scratch_shapes=[pltpu.VMEM((8, 128), F32)],
        compiler_params=_params("arbitrary"),
        name="outproj0",
    )(a, y, gg, x, mod, wo, g, rw_t, rb)


def _outproj1(cx, bg, att, conv_w, x, mod, wo, g, rw_t, rb):
    tok = pl.BlockSpec((TM, HALF), lambda i: (i, 0))
    ps, ns = _halo_specs(HALO_S)
    common_in, out_specs, out_shape = _outproj_common(N_LAT_TILES)
    return pl.pallas_call(
        _outproj1_kernel,
        grid=(N_LAT_TILES,),
        in_specs=[tok, ps, ns, tok, tok, pl.BlockSpec((CONV_C, HALF), lambda i: (0, 0))] + common_in,
        out_specs=out_specs,
        out_shape=out_shape,
        scratch_shapes=[pltpu.VMEM((TM + 2 * HALO_S, HALF), F32), pltpu.VMEM((8, 128), F32)],
        compiler_params=_params("arbitrary"),
        name="outproj1",
    )(cx, cx, cx, bg, att, conv_w, x, mod, wo, g, rw_t, rb)


def _moe_kernel(nch_ref, n_ref, rt_ref, r_ref, wg_ref, wu_ref, wd_ref, h_ref, mod_ref, fg_ref, o_ref,
                acc_ref, yg_ref, *, final_norm):
    i = pl.program_id(0)
    g = pl.program_id(1)

    @pl.when(g == 0)
    def _():
        acc_ref[...] = jnp.zeros_like(acc_ref)
        yg_ref[...] = jnp.zeros_like(yg_ref)

    gf = g.astype(F32)
    in_group_row = rt_ref[ROUTE_GROUP:ROUTE_GROUP + 1, :] == gf
    rank_row = rt_ref[ROUTE_RANK:ROUTE_RANK + 1, :]
    in_group_col = r_ref[:, ROUTE_GROUP:ROUTE_GROUP + 1] == gf
    rank_col = r_ref[:, ROUTE_RANK:ROUTE_RANK + 1]
    slot_row = lax.broadcasted_iota(jnp.int32, (CAP, TMOE), 0).astype(F32)
    slot_col = lax.broadcasted_iota(jnp.int32, (TMOE, CAP_PAD), 1).astype(F32)

    def chunk(k, carry):
        base = (k * CAP).astype(F32)
        sel = slot_row == jnp.where(in_group_row, rank_row - base, -1.0)
        xg = jnp.dot(jnp.where(sel, 1.0, 0.0).astype(BF16), n_ref[...], preferred_element_type=F32).astype(BF16)
        y = None
        for j in range(PER_GROUP):
            comb_row = rt_ref[pl.ds(g * PER_GROUP + j, 1), :]
            cw = jnp.sum(jnp.where(sel, comb_row, 0.0), axis=1, keepdims=True)
            hid = (_silu(jnp.dot(xg, wg_ref[j], preferred_element_type=F32))
                   * jnp.dot(xg, wu_ref[j], preferred_element_type=F32) * cw)
            t = jnp.dot(hid.astype(BF16), wd_ref[j], preferred_element_type=F32)
            y = t if y is None else y + t
        yg_ref[0:CAP, :] = y.astype(BF16)
        local = rank_col - base
        key = jnp.where(jnp.logical_and(in_group_col, local < float(CAP)), local, -1.0)
        back = jnp.where(slot_col == key, 1.0, 0.0).astype(BF16)
        acc_ref[...] += jnp.dot(back, yg_ref[...], preferred_element_type=F32)
        return carry

    lax.fori_loop(0, nch_ref[i * N_GROUPS + g], chunk, 0)

    @pl.when(g == N_GROUPS - 1)
    def _():
        out = h_ref[...] + mod_ref[0][:, 5 * D:6 * D] * acc_ref[...]
        if final_norm:
            out = out * lax.rsqrt(jnp.mean(out * out, axis=-1, keepdims=True) + EPS) * fg_ref[...]
        o_ref[...] = out


def _moe(n, route_t, counts, wg, wu, wd, h, mod, final_g, final_norm):
    rows = n.shape[0]
    sub = TMOE // TM
    per_tile = counts[sub - 1::sub, :N_GROUPS, 0].astype(jnp.int32)
    n_chunks = ((per_tile + (CAP - 1)) // CAP).reshape(-1)
    tok = pl.BlockSpec((TMOE, D), lambda i, g, nch: (i, 0))
    grid_spec = pltpu.PrefetchScalarGridSpec(
        num_scalar_prefetch=1,
        grid=(rows // TMOE, N_GROUPS),
        in_specs=[tok,
                  pl.BlockSpec((ROUTE_ROWS, TMOE), lambda i, g, nch: (0, i)),
                  pl.BlockSpec((TMOE, ROUTE_ROWS), lambda i, g, nch: (i, 0)),
                  pl.BlockSpec((PER_GROUP, D, D_FF), lambda i, g, nch: (g, 0, 0)),
                  pl.BlockSpec((PER_GROUP, D, D_FF), lambda i, g, nch: (g, 0, 0)),
                  pl.BlockSpec((PER_GROUP, D_FF, D), lambda i, g, nch: (g, 0, 0)),
                  tok,
                  pl.BlockSpec((1, 1, 6 * D), lambda i, g, nch: _mod_row(TMOE)(i)),
                  pl.BlockSpec((1, D), lambda i, g, nch: (0, 0))],
        out_specs=tok,
        scratch_shapes=[pltpu.VMEM((TMOE, D), F32), pltpu.VMEM((CAP_PAD, D), BF16)])
    return pl.pallas_call(
        functools.partial(_moe_kernel, final_norm=final_norm),
        grid_spec=grid_spec,
        out_shape=jax.ShapeDtypeStruct((rows, D), F32),
        compiler_params=_params("parallel", "arbitrary"),
        name="moe_final" if final_norm else "moe",
    )(n_chunks, n, route_t, route_t.T, wg, wu, wd, h, mod, final_g)


def _inproj1_kernel(x_ref, mod_ref, g_ref, w_ref, cx_ref, bg_ref, q_ref, k_ref, v_ref):
    mod = mod_ref[0]
    n = _rms_mod(x_ref[...], g_ref[...], mod[:, D:2 * D], mod[:, 0:D])
    p = jnp.dot(n.astype(BF16), w_ref[...], preferred_element_type=F32)
    cx_ref[...] = p[:, 2 * HALF:3 * HALF] * p[:, 0:HALF]
    bg_ref[...] = p[:, HALF:2 * HALF]
    q_ref[...] = (p[:, 3 * HALF:4 * HALF] * (HEAD_DIM ** -0.5)).astype(BF16)
    k_ref[...] = p[:, 4 * HALF:5 * HALF].astype(BF16)
    v_ref[...] = p[:, 5 * HALF:6 * HALF].astype(BF16)


def _inproj1(x, mod, g, w):
    tok = pl.BlockSpec((TM, HALF), lambda i: (i, 0))
    f = jax.ShapeDtypeStruct((N_TOK, HALF), F32)
    h = jax.ShapeDtypeStruct((N_TOK, HALF), BF16)
    return pl.pallas_call(
        _inproj1_kernel,
        grid=(N_TILES,),
        in_specs=[pl.BlockSpec((TM, D), lambda i: (i, 0)),
                  pl.BlockSpec((1, 1, 6 * D), _mod_row(TM)),
                  pl.BlockSpec((1, D), lambda i: (0, 0)),
                  pl.BlockSpec((D, 6 * HALF), lambda i: (0, 0))],
        out_specs=[tok] * 5,
        out_shape=[f, f, h, h, h],
        compiler_params=_params("parallel"),
        name="inproj1",
    )(x, mod, g, w)


def _natten_kernel(q_ref, kp_ref, kc_ref, kn_ref, vp_ref, vc_ref, vn_ref, kx_ref, vx_ref, bias_ref, o_ref):
    lane = lax.broadcasted_iota(jnp.int32, (TM, 2 * HEAD_DIM), 1)
    low = lane < HEAD_DIM
    head_mask = [low.astype(F32).astype(BF16), jnp.logical_not(low).astype(F32).astype(BF16)]
    for g in range(N_HEADS // 2):
        sl = slice(2 * HEAD_DIM * g, 2 * HEAD_DIM * (g + 1))
        q2 = q_ref[:, sl]
        keys = [kp_ref[:, sl], kc_ref[:, sl], kn_ref[:, sl], kx_ref[:, sl]]
        vals = [vp_ref[:, sl], vc_ref[:, sl], vn_ref[:, sl], vx_ref[:, sl]]
        outs = []
        for hh in range(2):
            qm = q2 * head_mask[hh]
            s = [_nt_dot(qm, k) for k in keys]
            for t in range(3):
                s[t] = s[t] + bias_ref[0, 2 * g + hh, :, t * TM:(t + 1) * TM]
            m = jnp.max(jnp.maximum(jnp.maximum(s[0], s[1]), jnp.maximum(s[2], s[3])), axis=-1, keepdims=True)
            p = [jnp.exp(t - m) for t in s]
            den = jnp.sum(p[0] + p[1] + p[2] + p[3], axis=-1, keepdims=True)
            o = None
            for pt, vt in zip(p, vals):
                term = jnp.dot(pt.astype(BF16), vt, preferred_element_type=F32)
                o = term if o is None else o + term
            outs.append(o * (1.0 / den))
        o_ref[:, sl] = jnp.where(low, outs[0], outs[1]).astype(o_ref.dtype)


def _natten_bias(rpb):
    n_rows = SEQ // GRID_W
    n_dr, n_dc = 2 * WIN_H - 1, 2 * WIN_W - 1
    i = np.arange(ROWS_Q)
    j = np.arange(3 * ROWS_Q)
    col = np.arange(GRID_W)
    col_start = np.clip(col - WIN_W // 2, 0, GRID_W - WIN_W)
    col_ok = (col[None, :] >= col_start[:, None]) & (col[None, :] < col_start[:, None] + WIN_W)
    col_idx = col[None, :] - col[:, None] + (WIN_W - 1)
    onehot = ((col_idx[None] == np.arange(n_dc)[:, None, None]) & col_ok[None]).astype(np.float32)
    col_exp = jnp.dot(rpb.reshape(N_HEADS * n_dr, n_dc).astype(F32), onehot.reshape(n_dc, GRID_W * GRID_W),
                      precision=lax.Precision.HIGHEST).reshape(N_HEADS, n_dr, GRID_W, GRID_W)
    kinds = []
    for r0 in (0, ROWS_Q, n_rows - ROWS_Q):
        r = r0 + i
        kr = r0 - ROWS_Q + j
        r_start = np.clip(r - WIN_H // 2, 0, n_rows - WIN_H)
        row_ok = ((kr[None, :] >= r_start[:, None]) & (kr[None, :] < r_start[:, None] + WIN_H)
                  & (kr[None, :] >= 0) & (kr[None, :] < n_rows))
        row_idx = np.clip(kr[None, :] - r[:, None] + (WIN_H - 1), 0, n_dr - 1)
        b = col_exp[:, row_idx]
        b = b.transpose(0, 1, 3, 2, 4)
        ok = row_ok[:, None, :, None] & col_ok[None, :, None, :]
        kinds.append(jnp.where(ok[None], b, NEG).reshape(N_HEADS, TM, 3 * TM))
    return jnp.stack(kinds).astype(F32)


def _natten(q, k, v, bias):
    def lat(off):
        def index(b, i):
            return (b * TILES_PER_SEQ + jnp.clip(i + off, 0, TILES_PER_SEQ - 1), 0)
        return pl.BlockSpec((TM, HALF), index)

    ctx = pl.BlockSpec((TM, HALF), lambda b, i: (N_LAT_TILES + b, 0))

    def kind(b, i):
        return (jnp.where(i == 0, 0, jnp.where(i == TILES_PER_SEQ - 1, 2, 1)), 0, 0, 0)

    return pl.pallas_call(
        _natten_kernel,
        grid=(BATCH, TILES_PER_SEQ),
        in_specs=[lat(0), lat(-1), lat(0), lat(1), lat(-1), lat(0), lat(1), ctx, ctx,
                  pl.BlockSpec((1, N_HEADS, TM, 3 * TM), kind)],
        out_specs=lat(0),
        out_shape=jax.ShapeDtypeStruct((N_LAT, HALF), BF16),
        compiler_params=_params("parallel", "arbitrary"),
        name="natten",
    )(q, k, k, k, v, v, v, k, v, bias)


def kernel(x, c, ctx, c_ctx, ada_w, ada_b, norm_mix_g, norm_ffn_g, w_out, ab_w_in, a_dw_w, a_dw_b, a_ln_g, a_ln_b,
           b_conv_w, b_conv_b, b_gate_w, b_gate_b, b_lambda, cd_w_in, c_conv_w, d_rpb, router_w, router_bias,
           moe_w_gate, moe_w_up, moe_w_down, final_g):
    tokens = jnp.concatenate([x.reshape(N_LAT, D), ctx.reshape(BATCH * CTX, D)], axis=0)
    cond = jnp.concatenate([c, c_ctx[None], jnp.zeros((8 - BATCH - 1, D), F32)], axis=0)
    mod = _modulation(cond, ada_w, ada_b)
    mod0 = mod[0].reshape(8, 1, 6 * D)
    mod1 = mod[1].reshape(8, 1, 6 * D)

    wg = moe_w_gate.astype(BF16)
    wu = moe_w_up.astype(BF16)
    wd = moe_w_down.astype(BF16)
    wo = w_out.astype(BF16)
    rw_t = router_w.T
    rb = router_bias.reshape(N_EXPERTS, 1)
    fg = final_g.reshape(1, D)

    ua, ub, gg = _inproj0(tokens, mod0, norm_mix_g[0].reshape(1, D), ab_w_in[0].astype(BF16))
    a_out, v = _conv0(ua, ub, a_dw_w[0], a_dw_b[0], a_ln_g[0], a_ln_b[0], b_conv_w[0], b_conv_b[0])
    gw = b_gate_w[0]
    gate_w = jnp.concatenate([gw[:, 0], gw[:, 1]], axis=-1).astype(BF16)
    y = _scan(v, gate_w, b_gate_b[0], b_lambda[0].reshape(2, 1, HALF))
    h1, n2, route0, cnt0 = _outproj0(a_out, y, gg, tokens, mod0, wo[0], norm_ffn_g[0].reshape(1, D), rw_t, rb)
    h2 = _moe(n2, route0, cnt0, wg[0], wu[0], wd[0], h1, mod0, fg, False)

    cx, bg, q, k, vv = _inproj1(h2, mod1, norm_mix_g[1].reshape(1, D), cd_w_in[0].astype(BF16))
    att = _natten(q, k, vv, _natten_bias(d_rpb[0]))
    h3, n4, route1, cnt1 = _outproj1(cx, bg, att, c_conv_w[0], h2, mod1, wo[1], norm_ffn_g[1].reshape(1, D), rw_t, rb)
    out = _moe(n4, route1, cnt1, wg[1], wu[1], wd[1], h3, mod1, fg, True)
    return out.reshape(BATCH, SEQ, D)
```

```python
import functools

import jax
import jax.numpy as jnp
import numpy as np
from jax import lax
from jax.experimental import pallas as pl
from jax.experimental.pallas import tpu as pltpu

F32 = jnp.float32
BF16 = jnp.bfloat16

D = 1024
BATCH = 2
SEQ = 8192
CTX = 256
GRID_W = 64
N_LAT = BATCH * SEQ
N_TOK = N_LAT + BATCH * CTX
HALF = 512
CONV_A = 31
CONV_B = 4
CONV_C = 3
LRU_BLOCK = 128
N_LRU_BLOCKS = HALF // LRU_BLOCK
LRU_C = 8.0
HEAD_DIM = 64
N_HEADS = HALF // HEAD_DIM
WIN_H = 8
WIN_W = 16
N_EXPERTS = 16
N_GROUPS = 4
PER_GROUP = N_EXPERTS // N_GROUPS
D_FF = 512
EPS = 1e-6
NEG = -1e30

TM = 256
TILES_PER_SEQ = SEQ // TM
N_LAT_TILES = N_LAT // TM
N_TILES = N_TOK // TM
TMOE = 512
CAP = 128
ROUTE_ROWS = 24
ROUTE_GROUP = 16
ROUTE_RANK = 17
HALO_A = 16
HALO_S = 8
ROWS_Q = TM // GRID_W
VMEM_LIMIT = 48 * 1024 * 1024
VMEM_LIMIT_MOE = 56 * 1024 * 1024


def _params(*sem, vmem=VMEM_LIMIT):
    return pltpu.CompilerParams(dimension_semantics=sem, vmem_limit_bytes=vmem)


def _sigmoid(x):
    return 1.0 / (1.0 + jnp.exp(-x))


def _silu(x):
    return x * _sigmoid(x)


def _gelu_tanh(x):
    return 0.5 * x * (1.0 + jnp.tanh(0.7978845608028654 * (x + 0.044715 * (x * x * x))))


def _rms_mod(x, g, scale, shift):
    y = x * lax.rsqrt(jnp.mean(x * x, axis=-1, keepdims=True) + EPS) * g
    return y * (1.0 + scale) + shift


def _nt_dot(a, b):
    return lax.dot_general(a, b, (((1,), (1,)), ((), ())), preferred_element_type=F32)


def _mod_row(tile_rows):
    per_seq = SEQ // tile_rows
    return lambda i: (jnp.minimum(i // per_seq, BATCH), 0, 0)


def _mod_kernel(c_ref, w_ref, b_ref, o_ref):
    c = c_ref[...]
    s = _silu(c).astype(BF16)
    o_ref[0] = jnp.dot(s, w_ref[0].astype(BF16), preferred_element_type=F32) + b_ref[0]


def _modulation(cond, ada_w, ada_b):
    depth = ada_w.shape[0]
    nb = 1536
    return pl.pallas_call(
        _mod_kernel,
        grid=(depth, 6 * D // nb),
        in_specs=[pl.BlockSpec((8, D), lambda l, j: (0, 0)),
                  pl.BlockSpec((1, D, nb), lambda l, j: (l, 0, j)),
                  pl.BlockSpec((1, 1, nb), lambda l, j: (l, 0, j))],
        out_specs=pl.BlockSpec((1, 8, nb), lambda l, j: (l, 0, j)),
        out_shape=jax.ShapeDtypeStruct((depth, 8, 6 * D), F32),
        compiler_params=_params("parallel", "parallel"),
        name="modulation",
    )(cond, ada_w, ada_b.reshape(depth, 1, 6 * D))


def _token_specs():
    lat = pl.BlockSpec((TM, D), lambda i: (jnp.minimum(i, N_LAT_TILES - 1), 0))
    ctx = pl.BlockSpec((TM, D), lambda i: (jnp.maximum(i - N_LAT_TILES, 0), 0))
    return [lat, ctx]


def _token_tile(lat_ref, ctx_ref):
    return jnp.where(pl.program_id(0) < N_LAT_TILES, lat_ref[...], ctx_ref[...])


def _inproj0_kernel(x_ref, c_ref, mod_ref, g_ref, w_ref, ua_ref, ub_ref, gg_ref):
    mod = mod_ref[0]
    n = _rms_mod(_token_tile(x_ref, c_ref), g_ref[...], mod[:, D:2 * D], mod[:, 0:D])
    p = jnp.dot(n.astype(BF16), w_ref[...], preferred_element_type=F32)
    ua_ref[...] = p[:, 0:HALF] * _sigmoid(p[:, HALF:2 * HALF])
    ub_ref[...] = p[:, 2 * HALF:3 * HALF]
    gg_ref[...] = _gelu_tanh(p[:, 3 * HALF:4 * HALF])


def _inproj0(x, c, mod, g, w):
    tok = pl.BlockSpec((TM, HALF), lambda i: (i, 0))
    shp = jax.ShapeDtypeStruct((N_TOK, HALF), F32)
    return pl.pallas_call(
        _inproj0_kernel,
        grid=(N_TILES,),
        in_specs=_token_specs() + [
                  pl.BlockSpec((1, 1, 6 * D), _mod_row(TM)),
                  pl.BlockSpec((1, D), lambda i: (0, 0)),
                  pl.BlockSpec((D, 4 * HALF), lambda i: (0, 0))],
        out_specs=[tok, tok, tok],
        out_shape=[shp, shp, shp],
        compiler_params=_params("parallel"),
        name="inproj0",
    )(x, c, mod, g, w)


def _halo_specs(halo):
    per_tile = TM // halo
    last = N_TOK // halo - 1
    prev = pl.BlockSpec((halo, HALF), lambda i: (jnp.maximum(i * per_tile - 1, 0), 0))
    nxt = pl.BlockSpec((halo, HALF), lambda i: (jnp.minimum((i + 1) * per_tile, last), 0))
    return prev, nxt


def _seq_edges(i):
    is_ctx = i >= N_LAT_TILES
    first = jnp.logical_or(is_ctx, i % TILES_PER_SEQ == 0)
    last = jnp.logical_or(is_ctx, i % TILES_PER_SEQ == TILES_PER_SEQ - 1)
    return first, last


def _fill_padded(buf_ref, prev_ref, cur_ref, next_ref, halo):
    first, last = _seq_edges(pl.program_id(0))
    buf_ref[0:halo, :] = jnp.where(first, 0.0, prev_ref[...])
    buf_ref[halo:halo + TM, :] = cur_ref[...]
    buf_ref[halo + TM:halo + TM + halo, :] = jnp.where(last, 0.0, next_ref[...])


def _depthwise(buf_ref, w_ref, taps, first_off, rows, row0):
    acc = None
    for k in range(taps):
        term = w_ref[k:k + 1, :] * buf_ref[row0 + first_off + k:row0 + first_off + k + rows, :]
        acc = term if acc is None else acc + term
    return acc


CONV_ROWS = 32


def _conv0_kernel(ua_ref, uap_ref, uan_ref, ub_ref, ubp_ref, ubn_ref,
                  dww_ref, dwb_ref, lng_ref, lnb_ref, cw_ref, cb_ref,
                  a_ref, v_ref, bufa_ref, bufb_ref):
    _fill_padded(bufa_ref, uap_ref, ua_ref, uan_ref, HALO_A)
    _fill_padded(bufb_ref, ubp_ref, ub_ref, ubn_ref, HALO_S)
    for r in range(TM // CONV_ROWS):
        row0 = r * CONV_ROWS
        u = _depthwise(bufa_ref, dww_ref, CONV_A, HALO_A - CONV_A // 2, CONV_ROWS, row0) + dwb_ref[...]
        mu = jnp.mean(u, axis=-1, keepdims=True)
        uc = u - mu
        var = jnp.mean(uc * uc, axis=-1, keepdims=True)
        y = uc * lax.rsqrt(var + EPS) * lng_ref[...] + lnb_ref[...]
        a_ref[row0:row0 + CONV_ROWS, :] = _silu(y).astype(a_ref.dtype)
        v = _depthwise(bufb_ref, cw_ref, CONV_B, HALO_S - 2, CONV_ROWS, row0) + cb_ref[...]
        v_ref[row0:row0 + CONV_ROWS, :] = v


def _conv0(ua, ub, dw_w, dw_b, ln_g, ln_b, conv_w, conv_b):
    tok = pl.BlockSpec((TM, HALF), lambda i: (i, 0))
    pa, na = _halo_specs(HALO_A)
    ps, ns = _halo_specs(HALO_S)
    vec = pl.BlockSpec((1, HALF), lambda i: (0, 0))
    return pl.pallas_call(
        _conv0_kernel,
        grid=(N_TILES,),
        in_specs=[tok, pa, na, tok, ps, ns,
                  pl.BlockSpec((CONV_A, HALF), lambda i: (0, 0)), vec, vec, vec,
                  pl.BlockSpec((CONV_B, HALF), lambda i: (0, 0)), vec],
        out_specs=[tok, tok],
        out_shape=[jax.ShapeDtypeStruct((N_TOK, HALF), BF16), jax.ShapeDtypeStruct((N_TOK, HALF), F32)],
        scratch_shapes=[pltpu.VMEM((TM + 2 * HALO_A, HALF), F32), pltpu.VMEM((TM + 2 * HALO_S, HALF), F32)],
        compiler_params=_params("parallel"),
        name="conv0",
    )(ua, ua, ua, ub, ub, ub, dw_w, dw_b.reshape(1, HALF), ln_g.reshape(1, HALF), ln_b.reshape(1, HALF),
      conv_w, conv_b.reshape(1, HALF))


SCAN_UNROLL = 8


def _scan_kernel(v_ref, w_ref, gb_ref, lam_ref, y_ref, h_ref, a_ref, b_ref):
    d = pl.program_id(0)
    j = pl.program_id(2)

    @pl.when(j == 0)
    def _():
        h_ref[...] = jnp.zeros_like(h_ref)

    v = v_ref[...]
    vb = v.astype(BF16)
    lam = lam_ref[0]
    neg = -lam
    softplus = jnp.maximum(neg, 0.0) + jnp.log(1.0 + jnp.exp(-jnp.abs(neg)))
    for n in range(N_LRU_BLOCKS):
        sl = slice(n * LRU_BLOCK, (n + 1) * LRU_BLOCK)
        g = jnp.dot(vb[:, sl], w_ref[0, n], preferred_element_type=F32)
        r = _sigmoid(g[:, 0:LRU_BLOCK] + gb_ref[0, 0:1, sl])
        i = _sigmoid(g[:, LRU_BLOCK:2 * LRU_BLOCK] + gb_ref[0, 1:2, sl])
        log_a = -LRU_C * r * softplus[:, sl]
        a = jnp.exp(log_a)
        a_ref[:, sl] = a
        b_ref[:, sl] = jnp.sqrt(1.0 - jnp.exp(2.0 * log_a)) * (i * v[:, sl])

    def body(s, h):
        for u in range(SCAN_UNROLL):
            t = s * SCAN_UNROLL + u
            idx = jnp.where(d == 0, t, TM - 1 - t)
            h = a_ref[pl.ds(idx, 1), :] * h + b_ref[pl.ds(idx, 1), :]
            y_ref[0, pl.ds(idx, 1), :] = h
        return h

    h_ref[...] = lax.fori_loop(0, TM // SCAN_UNROLL, body, h_ref[...])


def _scan_tile(d, b, j):
    lat = b * TILES_PER_SEQ + jnp.where(d == 0, j - 1, TILES_PER_SEQ - j)
    return jnp.where(j == 0, N_LAT_TILES + b, lat)


def _scan(v, gate_w, gate_b, lam):
    return pl.pallas_call(
        _scan_kernel,
        grid=(2, BATCH, TILES_PER_SEQ + 1),
        in_specs=[pl.BlockSpec((TM, HALF), lambda d, b, j: (_scan_tile(d, b, j), 0)),
                  pl.BlockSpec((1, N_LRU_BLOCKS, LRU_BLOCK, 2 * LRU_BLOCK), lambda d, b, j: (d, 0, 0, 0)),
                  pl.BlockSpec((1, 2, HALF), lambda d, b, j: (d, 0, 0)),
                  pl.BlockSpec((1, 1, HALF), lambda d, b, j: (d, 0, 0))],
        out_specs=pl.BlockSpec((1, TM, HALF), lambda d, b, j: (d, _scan_tile(d, b, j), 0)),
        out_shape=jax.ShapeDtypeStruct((2, N_TOK, HALF), F32),
        scratch_shapes=[pltpu.VMEM((1, HALF), F32), pltpu.VMEM((TM, HALF), F32), pltpu.VMEM((TM, HALF), F32)],
        compiler_params=_params("arbitrary", "arbitrary", "arbitrary"),
        name="lru_scan",
    )(v, gate_w, gate_b, lam)


def _route(score, sel):
    rows = [sel[e:e + 1, :] for e in range(N_EXPERTS)]
    gbest = None
    gidx = None
    for g in range(N_GROUPS):
        top2 = None
        for p in range(PER_GROUP):
            for q in range(p + 1, PER_GROUP):
                s = rows[g * PER_GROUP + p] + rows[g * PER_GROUP + q]
                top2 = s if top2 is None else jnp.maximum(top2, s)
        if g == 0:
            gbest = top2
            gidx = jnp.zeros(top2.shape, jnp.int32)
        else:
            better = top2 > gbest
            gidx = jnp.where(better, g, gidx)
            gbest = jnp.where(better, top2, gbest)
    eint = lax.broadcasted_iota(jnp.int32, sel.shape, 0)
    eidx = eint.astype(F32)
    masked = jnp.where(jnp.right_shift(eint, 2) == gidx, sel, -jnp.inf)
    v1 = jnp.max(masked, axis=0, keepdims=True)
    i1 = jnp.min(jnp.where(masked == v1, eidx, float(N_EXPERTS)), axis=0, keepdims=True)
    masked2 = jnp.where(eidx == i1, -jnp.inf, masked)
    v2 = jnp.max(masked2, axis=0, keepdims=True)
    i2 = jnp.min(jnp.where(masked2 == v2, eidx, float(N_EXPERTS)), axis=0, keepdims=True)
    s1 = jnp.sum(jnp.where(eidx == i1, score, 0.0), axis=0, keepdims=True)
    s2 = jnp.sum(jnp.where(eidx == i2, score, 0.0), axis=0, keepdims=True)
    inv = 1.0 / (s1 + s2)
    return jnp.where(eidx == i1, s1 * inv, 0.0) + jnp.where(eidx == i2, s2 * inv, 0.0), gidx


def _group_ranks(gidx, carry_ref):
    @pl.when(pl.program_id(0) % (TMOE // TM) == 0)
    def _():
        carry_ref[...] = jnp.zeros_like(carry_ref)

    onehot = lax.broadcasted_iota(jnp.int32, (8, TM), 0) == gidx
    oh = jnp.where(onehot, 1.0, 0.0)
    before = lax.broadcasted_iota(jnp.int32, (TM, TM), 0) < lax.broadcasted_iota(jnp.int32, (TM, TM), 1)
    prefix = jnp.dot(oh.astype(BF16), jnp.where(before, 1.0, 0.0).astype(BF16), preferred_element_type=F32)
    carry = carry_ref[...]
    rank = jnp.sum(jnp.where(onehot, prefix + carry[:, 0:1], 0.0), axis=0, keepdims=True)
    carry = carry + jnp.sum(oh, axis=1, keepdims=True)
    carry_ref[...] = carry
    return rank, carry


def _split_bf16(v):
    hi = v.astype(BF16)
    return hi, (v - hi.astype(F32)).astype(BF16)


def _finish_outproj(m1, m2, x, mod_ref, wo_ref, g_ref, rw_ref, rb_ref, h_ref, n_ref, route_ref, cnt_ref, carry_ref):
    mod = mod_ref[0]
    mix = (jnp.dot(m1, wo_ref[0:HALF, :], preferred_element_type=F32)
           + jnp.dot(m2, wo_ref[HALF:2 * HALF, :], preferred_element_type=F32))
    h = x + mod[:, 2 * D:3 * D] * mix
    h_ref[...] = h
    n = _rms_mod(h, g_ref[...], mod[:, 4 * D:5 * D], mod[:, 3 * D:4 * D])
    n_hi, n_lo = _split_bf16(n)
    n_ref[...] = n_hi
    w_hi, w_lo = _split_bf16(rw_ref[...])
    logits = _nt_dot(w_hi, n_hi) + _nt_dot(w_hi, n_lo) + _nt_dot(w_lo, n_hi)
    score = _sigmoid(logits)
    comb, gidx = _route(score, score + rb_ref[...])
    rank, counts = _group_ranks(gidx, carry_ref)
    route_ref[0:N_EXPERTS, :] = comb
    route_ref[ROUTE_GROUP:ROUTE_GROUP + 1, :] = gidx.astype(F32)
    route_ref[ROUTE_RANK:ROUTE_RANK + 1, :] = rank
    route_ref[ROUTE_RANK + 1:ROUTE_ROWS, :] = jnp.zeros((ROUTE_ROWS - ROUTE_RANK - 1, TM), F32)
    cnt_ref[0] = counts


def _outproj0_kernel(a_ref, y_ref, gg_ref, x_ref, c_ref, mod_ref, wo_ref, g_ref, rw_ref, rb_ref,
                     h_ref, n_ref, route_ref, cnt_ref, carry_ref):
    m2 = ((y_ref[0] + y_ref[1]) * gg_ref[...]).astype(BF16)
    _finish_outproj(a_ref[...], m2, _token_tile(x_ref, c_ref), mod_ref, wo_ref, g_ref, rw_ref, rb_ref, h_ref, n_ref,
                    route_ref, cnt_ref, carry_ref)


def _outproj1_kernel(cx_ref, cxp_ref, cxn_ref, bg_ref, att_ref, cw_ref, x_ref, mod_ref, wo_ref, g_ref, rw_ref, rb_ref,
                     h_ref, n_ref, route_ref, cnt_ref, buf_ref, carry_ref):
    _fill_padded(buf_ref, cxp_ref, cx_ref, cxn_ref, HALO_S)
    conv = _depthwise(buf_ref, cw_ref, CONV_C, HALO_S - 1, TM, 0)
    m1 = (bg_ref[...] * conv).astype(BF16)
    _finish_outproj(m1, att_ref[...], x_ref[...], mod_ref, wo_ref, g_ref, rw_ref, rb_ref, h_ref, n_ref, route_ref,
                    cnt_ref, carry_ref)


def _outproj_common(n_tiles):
    in_specs = [pl.BlockSpec((1, 1, 6 * D), _mod_row(TM)),
                pl.BlockSpec((D, D), lambda i: (0, 0)),
                pl.BlockSpec((1, D), lambda i: (0, 0)),
                pl.BlockSpec((N_EXPERTS, D), lambda i: (0, 0)),
                pl.BlockSpec((N_EXPERTS, 1), lambda i: (0, 0))]
    out_specs = [pl.BlockSpec((TM, D), lambda i: (i, 0)),
                 pl.BlockSpec((TM, D), lambda i: (i, 0)),
                 pl.BlockSpec((ROUTE_ROWS, TM), lambda i: (0, i)),
                 pl.BlockSpec((1, 8, 128), lambda i: (i, 0, 0))]
    rows = n_tiles * TM
    out_shape = [jax.ShapeDtypeStruct((rows, D), F32), jax.ShapeDtypeStruct((rows, D), BF16),
                 jax.ShapeDtypeStruct((ROUTE_ROWS, rows), F32), jax.ShapeDtypeStruct((n_tiles, 8, 128), F32)]
    return in_specs, out_specs, out_shape


def _outproj0(a, y, gg, x, c, mod, wo, g, rw_t, rb):
    tok = pl.BlockSpec((TM, HALF), lambda i: (i, 0))
    common_in, out_specs, out_shape = _outproj_common(N_TILES)
    return pl.pallas_call(
        _outproj0_kernel,
        grid=(N_TILES,),
        in_specs=[tok, pl.BlockSpec((2, TM, HALF), lambda i: (0, i, 0)), tok] + _token_specs() + common_in,
        out_specs=out_specs,
        out_shape=out_shape,
        scratch_shapes=[pltpu.VMEM((8, 128), F32)],
        compiler_params=_params("arbitrary"),
        name="outproj0",
    )(a, y, gg, x, c, mod, wo, g, rw_t, rb)


def _outproj1(cx, bg, att, conv_w, x, mod, wo, g, rw_t, rb):
    tok = pl.BlockSpec((TM, HALF), lambda i: (i, 0))
    ps, ns = _halo_specs(HALO_S)
    common_in, out_specs, out_shape = _outproj_common(N_LAT_TILES)
    return pl.pallas_call(
        _outproj1_kernel,
        grid=(N_LAT_TILES,),
        in_specs=[tok, ps, ns, tok, tok, pl.BlockSpec((CONV_C, HALF), lambda i: (0, 0)),
                  pl.BlockSpec((TM, D), lambda i: (i, 0))] + common_in,
        out_specs=out_specs,
        out_shape=out_shape,
        scratch_shapes=[pltpu.VMEM((TM + 2 * HALO_S, HALF), F32), pltpu.VMEM((8, 128), F32)],
        compiler_params=_params("arbitrary"),
        name="outproj1",
    )(cx, cx, cx, bg, att, conv_w, x, mod, wo, g, rw_t, rb)


def _moe_kernel(nch_ref, n_ref, rt_ref, r_ref, wg_ref, wu_ref, wd_ref, h_ref, mod_ref, fg_ref, o_ref,
                hid_ref, *, final_norm, first_tile, subtiles):
    i = pl.program_id(0)
    g = pl.program_id(1)

    @pl.when(g == 0)
    def _():
        o_ref[...] = jnp.zeros_like(o_ref)

    gf = g.astype(F32)
    slot_row = lax.broadcasted_iota(jnp.int32, (CAP, TMOE), 0).astype(F32)
    slot_col = lax.broadcasted_iota(jnp.int32, (TMOE, CAP), 1).astype(F32)

    for s in range(subtiles):
        rows = slice(s * TMOE, (s + 1) * TMOE)
        in_group_row = rt_ref[ROUTE_GROUP:ROUTE_GROUP + 1, rows] == gf
        rank_row = rt_ref[ROUTE_RANK:ROUTE_RANK + 1, rows]
        in_group_col = r_ref[rows, ROUTE_GROUP:ROUTE_GROUP + 1] == gf
        rank_col = r_ref[rows, ROUTE_RANK:ROUTE_RANK + 1]

        def chunk(k, carry, rows=rows, in_group_row=in_group_row, rank_row=rank_row,
                  in_group_col=in_group_col, rank_col=rank_col):
            base = (k * CAP).astype(F32)
            sel = slot_row == jnp.where(in_group_row, rank_row - base, -1.0)
            xg = jnp.dot(jnp.where(sel, 1.0, 0.0).astype(BF16), n_ref[rows, :],
                         preferred_element_type=F32).astype(BF16)
            for j in range(PER_GROUP):
                comb_row = rt_ref[pl.ds(g * PER_GROUP + j, 1), rows]
                cw = jnp.sum(jnp.where(sel, comb_row, 0.0), axis=1, keepdims=True)
                hid = (_silu(jnp.dot(xg, wg_ref[j], preferred_element_type=F32))
                       * jnp.dot(xg, wu_ref[j], preferred_element_type=F32) * cw)
                hid_ref[:, j * D_FF:(j + 1) * D_FF] = hid.astype(BF16)
            y = jnp.dot(hid_ref[...], wd_ref[0], preferred_element_type=F32).astype(BF16)
            local = rank_col - base
            back = jnp.where(jnp.logical_and(in_group_col, slot_col == local), 1.0, 0.0).astype(BF16)
            o_ref[rows, :] += jnp.dot(back, y, preferred_element_type=F32)
            return carry

        lax.fori_loop(0, nch_ref[(first_tile + i * subtiles + s) * N_GROUPS + g], chunk, 0)

    @pl.when(g == N_GROUPS - 1)
    def _():
        out = h_ref[...] + mod_ref[0][:, 5 * D:6 * D] * o_ref[...]
        if final_norm:
            out = out * lax.rsqrt(jnp.mean(out * out, axis=-1, keepdims=True) + EPS) * fg_ref[...]
        o_ref[...] = out


def _moe_chunks(counts):
    sub = TMOE // TM
    per_tile = counts[sub - 1::sub, :N_GROUPS, 0].astype(jnp.int32)
    return ((per_tile + (CAP - 1)) // CAP).reshape(-1)


def _moe(n_chunks, n, route_t, route, wg, wu, wd, layer, h, mod, final_g, *, first_tile, n_tiles, subtiles, final_norm,
         name):
    step = subtiles * TMOE
    first = first_tile // subtiles
    mod_row = lambda i, g, nch: (jnp.minimum(((first + i) * step) // SEQ, BATCH), 0, 0)
    tok = pl.BlockSpec((step, D), lambda i, g, nch: (first + i, 0))
    grid_spec = pltpu.PrefetchScalarGridSpec(
        num_scalar_prefetch=1,
        grid=(n_tiles // subtiles, N_GROUPS),
        in_specs=[tok,
                  pl.BlockSpec((ROUTE_ROWS, step), lambda i, g, nch: (0, first + i)),
                  pl.BlockSpec((step, ROUTE_ROWS), lambda i, g, nch: (first + i, 0)),
                  pl.BlockSpec((PER_GROUP, D, D_FF), lambda i, g, nch: (layer * N_GROUPS + g, 0, 0)),
                  pl.BlockSpec((PER_GROUP, D, D_FF), lambda i, g, nch: (layer * N_GROUPS + g, 0, 0)),
                  pl.BlockSpec((1, PER_GROUP * D_FF, D), lambda i, g, nch: (layer * N_GROUPS + g, 0, 0)),
                  tok,
                  pl.BlockSpec((1, 1, 6 * D), mod_row),
                  pl.BlockSpec((1, D), lambda i, g, nch: (0, 0))],
        out_specs=pl.BlockSpec((step, D), lambda i, g, nch: (i, 0)),
        scratch_shapes=[pltpu.VMEM((CAP, PER_GROUP * D_FF), BF16)])
    return pl.pallas_call(
        functools.partial(_moe_kernel, final_norm=final_norm, first_tile=first_tile, subtiles=subtiles),
        grid_spec=grid_spec,
        out_shape=jax.ShapeDtypeStruct((n_tiles * TMOE, D), F32),
        compiler_params=_params("parallel", "arbitrary", vmem=VMEM_LIMIT_MOE),
        name=name,
    )(n_chunks, n, route_t, route, wg, wu, wd, h, mod, final_g)


def _inproj1_kernel(x_ref, c_ref, mod_ref, g_ref, w_ref, cx_ref, bg_ref, q_ref, k_ref, v_ref):
    mod = mod_ref[0]
    n = _rms_mod(_token_tile(x_ref, c_ref), g_ref[...], mod[:, D:2 * D], mod[:, 0:D])
    p = jnp.dot(n.astype(BF16), w_ref[...], preferred_element_type=F32)
    cx_ref[...] = p[:, 2 * HALF:3 * HALF] * p[:, 0:HALF]
    bg_ref[...] = p[:, HALF:2 * HALF]
    q_ref[...] = (p[:, 3 * HALF:4 * HALF] * (HEAD_DIM ** -0.5)).astype(BF16)
    k_ref[...] = p[:, 4 * HALF:5 * HALF].astype(BF16)
    v_ref[...] = p[:, 5 * HALF:6 * HALF].astype(BF16)


def _inproj1(x, c, mod, g, w):
    tok = pl.BlockSpec((TM, HALF), lambda i: (i, 0))
    f = jax.ShapeDtypeStruct((N_TOK, HALF), F32)
    h = jax.ShapeDtypeStruct((N_TOK, HALF), BF16)
    return pl.pallas_call(
        _inproj1_kernel,
        grid=(N_TILES,),
        in_specs=_token_specs() + [
                  pl.BlockSpec((1, 1, 6 * D), _mod_row(TM)),
                  pl.BlockSpec((1, D), lambda i: (0, 0)),
                  pl.BlockSpec((D, 6 * HALF), lambda i: (0, 0))],
        out_specs=[tok] * 5,
        out_shape=[f, f, h, h, h],
        compiler_params=_params("parallel"),
        name="inproj1",
    )(x, c, mod, g, w)


def _natten_kernel(q_ref, kp_ref, kc_ref, kn_ref, vp_ref, vc_ref, vn_ref, kx_ref, vx_ref, bias_ref, o_ref):
    lane = lax.broadcasted_iota(jnp.int32, (TM, 2 * HEAD_DIM), 1)
    low = lane < HEAD_DIM
    head_mask = [low.astype(F32).astype(BF16), jnp.logical_not(low).astype(F32).astype(BF16)]
    for g in range(N_HEADS // 2):
        sl = slice(2 * HEAD_DIM * g, 2 * HEAD_DIM * (g + 1))
        q2 = q_ref[:, sl]
        keys = [kp_ref[:, sl], kc_ref[:, sl], kn_ref[:, sl], kx_ref[:, sl]]
        vals = [vp_ref[:, sl], vc_ref[:, sl], vn_ref[:, sl], vx_ref[:, sl]]
        outs = []
        for hh in range(2):
            qm = q2 * head_mask[hh]
            s = [_nt_dot(qm, k) for k in keys]
            for t in range(3):
                s[t] = s[t] + bias_ref[0, 2 * g + hh, :, t * TM:(t + 1) * TM]
            m = jnp.max(jnp.maximum(jnp.maximum(s[0], s[1]), jnp.maximum(s[2], s[3])), axis=-1, keepdims=True)
            p = [jnp.exp(t - m) for t in s]
            den = jnp.sum(p[0] + p[1] + p[2] + p[3], axis=-1, keepdims=True)
            o = None
            for pt, vt in zip(p, vals):
                term = jnp.dot(pt.astype(BF16), vt, preferred_element_type=F32)
                o = term if o is None else o + term
            outs.append(o * (1.0 / den))
        o_ref[:, sl] = jnp.where(low, outs[0], outs[1]).astype(o_ref.dtype)


def _natten_bias(rpb):
    n_rows = SEQ // GRID_W
    n_dr, n_dc = 2 * WIN_H - 1, 2 * WIN_W - 1
    i = np.arange(ROWS_Q)
    j = np.arange(3 * ROWS_Q)
    col = np.arange(GRID_W)
    col_start = np.clip(col - WIN_W // 2, 0, GRID_W - WIN_W)
    col_ok = (col[None, :] >= col_start[:, None]) & (col[None, :] < col_start[:, None] + WIN_W)
    col_idx = col[None, :] - col[:, None] + (WIN_W - 1)
    onehot = ((col_idx[None] == np.arange(n_dc)[:, None, None]) & col_ok[None]).astype(np.float32)
    col_exp = jnp.dot(rpb.reshape(N_HEADS * n_dr, n_dc).astype(F32), onehot.reshape(n_dc, GRID_W * GRID_W),
                      precision=lax.Precision.HIGHEST).reshape(N_HEADS, n_dr, GRID_W, GRID_W)
    kinds = []
    for r0 in (0, ROWS_Q, n_rows - ROWS_Q):
        r = r0 + i
        kr = r0 - ROWS_Q + j
        r_start = np.clip(r - WIN_H // 2, 0, n_rows - WIN_H)
        row_ok = ((kr[None, :] >= r_start[:, None]) & (kr[None, :] < r_start[:, None] + WIN_H)
                  & (kr[None, :] >= 0) & (kr[None, :] < n_rows))
        row_idx = np.clip(kr[None, :] - r[:, None] + (WIN_H - 1), 0, n_dr - 1)
        b = col_exp[:, row_idx]
        b = b.transpose(0, 1, 3, 2, 4)
        ok = row_ok[:, None, :, None] & col_ok[None, :, None, :]
        kinds.append(jnp.where(ok[None], b, NEG).reshape(N_HEADS, TM, 3 * TM))
    return jnp.stack(kinds).astype(F32)


def _natten(q, k, v, bias):
    def lat(off):
        def index(b, i):
            return (b * TILES_PER_SEQ + jnp.clip(i + off, 0, TILES_PER_SEQ - 1), 0)
        return pl.BlockSpec((TM, HALF), index)

    ctx = pl.BlockSpec((TM, HALF), lambda b, i: (N_LAT_TILES + b, 0))

    def kind(b, i):
        return (jnp.where(i == 0, 0, jnp.where(i == TILES_PER_SEQ - 1, 2, 1)), 0, 0, 0)

    return pl.pallas_call(
        _natten_kernel,
        grid=(BATCH, TILES_PER_SEQ),
        in_specs=[lat(0), lat(-1), lat(0), lat(1), lat(-1), lat(0), lat(1), ctx, ctx,
                  pl.BlockSpec((1, N_HEADS, TM, 3 * TM), kind)],
        out_specs=lat(0),
        out_shape=jax.ShapeDtypeStruct((N_LAT, HALF), BF16),
        compiler_params=_params("parallel", "arbitrary"),
        name="natten",
    )(q, k, k, k, v, v, v, k, v, bias)


def kernel(x, c, ctx, c_ctx, ada_w, ada_b, norm_mix_g, norm_ffn_g, w_out, ab_w_in, a_dw_w, a_dw_b, a_ln_g, a_ln_b,
           b_conv_w, b_conv_b, b_gate_w, b_gate_b, b_lambda, cd_w_in, c_conv_w, d_rpb, router_w, router_bias,
           moe_w_gate, moe_w_up, moe_w_down, final_g):
    x_lat = x.reshape(N_LAT, D)
    x_ctx = ctx.reshape(BATCH * CTX, D)
    cond = jnp.concatenate([c, c_ctx[None], jnp.zeros((8 - BATCH - 1, D), F32)], axis=0)
    mod = _modulation(cond, ada_w, ada_b)
    mod0 = mod[0].reshape(8, 1, 6 * D)
    mod1 = mod[1].reshape(8, 1, 6 * D)

    depth = moe_w_gate.shape[0]
    wg = moe_w_gate.astype(BF16).reshape(depth * N_EXPERTS, D, D_FF)
    wu = moe_w_up.astype(BF16).reshape(depth * N_EXPERTS, D, D_FF)
    wd = moe_w_down.astype(BF16).reshape(depth * N_GROUPS, PER_GROUP * D_FF, D)
    wo = w_out.astype(BF16)
    lat_tiles = N_LAT // TMOE
    ctx_tiles = BATCH * CTX // TMOE
    rw_t = router_w.T
    rb = router_bias.reshape(N_EXPERTS, 1)
    fg = final_g.reshape(1, D)

    ua, ub, gg = _inproj0(x_lat, x_ctx, mod0, norm_mix_g[0].reshape(1, D), ab_w_in[0].astype(BF16))
    a_out, v = _conv0(ua, ub, a_dw_w[0], a_dw_b[0], a_ln_g[0], a_ln_b[0], b_conv_w[0], b_conv_b[0])
    gw = b_gate_w[0]
    gate_w = jnp.concatenate([gw[:, 0], gw[:, 1]], axis=-1).astype(BF16)
    y = _scan(v, gate_w, b_gate_b[0], b_lambda[0].reshape(2, 1, HALF))
    h1, n2, route0, cnt0 = _outproj0(a_out, y, gg, x_lat, x_ctx, mod0, wo[0], norm_ffn_g[0].reshape(1, D), rw_t, rb)
    moe0 = functools.partial(_moe, _moe_chunks(cnt0), n2, route0, route0.T, wg, wu, wd, 0, h1, mod0, fg,
                             final_norm=False)
    h2_lat = moe0(first_tile=0, n_tiles=lat_tiles, subtiles=2, name="moe_lat")
    h2_ctx = moe0(first_tile=lat_tiles, n_tiles=ctx_tiles, subtiles=1, name="moe_ctx")

    cx, bg, q, k, vv = _inproj1(h2_lat, h2_ctx, mod1, norm_mix_g[1].reshape(1, D), cd_w_in[0].astype(BF16))
    att = _natten(q, k, vv, _natten_bias(d_rpb[0]))
    h3, n4, route1, cnt1 = _outproj1(cx, bg, att, c_conv_w[0], h2_lat, mod1, wo[1], norm_ffn_g[1].reshape(1, D),
                                     rw_t, rb)
    out = _moe(_moe_chunks(cnt1), n4, route1, route1.T, wg, wu, wd, 1, h3, mod1, fg,
               first_tile=0, n_tiles=lat_tiles, subtiles=2, final_norm=True, name="moe_final")
    return out.reshape(BATCH, SEQ, D)
```

```python
import functools

import jax
import jax.numpy as jnp
import numpy as np
from jax import lax
from jax.experimental import pallas as pl
from jax.experimental.pallas import tpu as pltpu

F32 = jnp.float32
BF16 = jnp.bfloat16

D = 1024
BATCH = 2
SEQ = 8192
CTX = 256
GRID_W = 64
N_LAT = BATCH * SEQ
N_TOK = N_LAT + BATCH * CTX
HALF = 512
CONV_A = 31
CONV_B = 4
CONV_C = 3
LRU_BLOCK = 128
N_LRU_BLOCKS = HALF // LRU_BLOCK
LRU_C = 8.0
HEAD_DIM = 64
N_HEADS = HALF // HEAD_DIM
WIN_H = 8
WIN_W = 16
N_EXPERTS = 16
N_GROUPS = 4
PER_GROUP = N_EXPERTS // N_GROUPS
D_FF = 512
EPS = 1e-6
NEG = -1e30

TM = 256
TILES_PER_SEQ = SEQ // TM
N_LAT_TILES = N_LAT // TM
N_TILES = N_TOK // TM
TMOE = 512
CAP = 128
ROUTE_ROWS = 24
ROUTE_GROUP = 16
ROUTE_RANK = 17
HALO_A = 16
HALO_S = 8
ROWS_Q = TM // GRID_W
VMEM_LIMIT = 48 * 1024 * 1024
VMEM_LIMIT_MOE = 56 * 1024 * 1024


def _params(*sem, vmem=VMEM_LIMIT):
    return pltpu.CompilerParams(dimension_semantics=sem, vmem_limit_bytes=vmem)


def _sigmoid(x):
    return 0.5 * jnp.tanh(0.5 * x) + 0.5


def _silu(x):
    return x * _sigmoid(x)


def _gelu_tanh(x):
    return 0.5 * x * (1.0 + jnp.tanh(0.7978845608028654 * (x + 0.044715 * (x * x * x))))


def _rms_mod(x, g, scale, shift):
    y = x * lax.rsqrt(jnp.mean(x * x, axis=-1, keepdims=True) + EPS) * g
    return y * (1.0 + scale) + shift


def _nt_dot(a, b):
    return lax.dot_general(a, b, (((1,), (1,)), ((), ())), preferred_element_type=F32)


def _mod_row(tile_rows):
    per_seq = SEQ // tile_rows
    return lambda i: (jnp.minimum(i // per_seq, BATCH), 0, 0)


def _mod_kernel(c_ref, w_ref, b_ref, o_ref):
    c = c_ref[...]
    s = _silu(c).astype(BF16)
    o_ref[0] = jnp.dot(s, w_ref[0].astype(BF16), preferred_element_type=F32) + b_ref[0]


def _modulation(cond, ada_w, ada_b):
    depth = ada_w.shape[0]
    nb = 1536
    return pl.pallas_call(
        _mod_kernel,
        grid=(depth, 6 * D // nb),
        in_specs=[pl.BlockSpec((8, D), lambda l, j: (0, 0)),
                  pl.BlockSpec((1, D, nb), lambda l, j: (l, 0, j)),
                  pl.BlockSpec((1, 1, nb), lambda l, j: (l, 0, j))],
        out_specs=pl.BlockSpec((1, 8, nb), lambda l, j: (l, 0, j)),
        out_shape=jax.ShapeDtypeStruct((depth, 8, 6 * D), F32),
        compiler_params=_params("parallel", "parallel"),
        name="modulation",
    )(cond, ada_w, ada_b.reshape(depth, 1, 6 * D))


def _token_specs():
    lat = pl.BlockSpec((TM, D), lambda i: (jnp.minimum(i, N_LAT_TILES - 1), 0))
    ctx = pl.BlockSpec((TM, D), lambda i: (jnp.maximum(i - N_LAT_TILES, 0), 0))
    return [lat, ctx]


def _token_tile(lat_ref, ctx_ref):
    return jnp.where(pl.program_id(0) < N_LAT_TILES, lat_ref[...], ctx_ref[...])


def _inproj0_kernel(x_ref, c_ref, mod_ref, g_ref, w_ref, ua_ref, ub_ref, gg_ref):
    mod = mod_ref[0]
    n = _rms_mod(_token_tile(x_ref, c_ref), g_ref[...], mod[:, D:2 * D], mod[:, 0:D])
    p = jnp.dot(n.astype(BF16), w_ref[...], preferred_element_type=F32)
    ua_ref[...] = p[:, 0:HALF] * _sigmoid(p[:, HALF:2 * HALF])
    ub_ref[...] = p[:, 2 * HALF:3 * HALF]
    gg_ref[...] = _gelu_tanh(p[:, 3 * HALF:4 * HALF])


def _inproj0(x, c, mod, g, w):
    tok = pl.BlockSpec((TM, HALF), lambda i: (i, 0))
    shp = jax.ShapeDtypeStruct((N_TOK, HALF), F32)
    return pl.pallas_call(
        _inproj0_kernel,
        grid=(N_TILES,),
        in_specs=_token_specs() + [
                  pl.BlockSpec((1, 1, 6 * D), _mod_row(TM)),
                  pl.BlockSpec((1, D), lambda i: (0, 0)),
                  pl.BlockSpec((D, 4 * HALF), lambda i: (0, 0))],
        out_specs=[tok, tok, tok],
        out_shape=[shp, shp, shp],
        compiler_params=_params("parallel"),
        name="inproj0",
    )(x, c, mod, g, w)


def _halo_specs(halo):
    per_tile = TM // halo
    last = N_TOK // halo - 1
    prev = pl.BlockSpec((halo, HALF), lambda i: (jnp.maximum(i * per_tile - 1, 0), 0))
    nxt = pl.BlockSpec((halo, HALF), lambda i: (jnp.minimum((i + 1) * per_tile, last), 0))
    return prev, nxt


def _seq_edges(i):
    is_ctx = i >= N_LAT_TILES
    first = jnp.logical_or(is_ctx, i % TILES_PER_SEQ == 0)
    last = jnp.logical_or(is_ctx, i % TILES_PER_SEQ == TILES_PER_SEQ - 1)
    return first, last


def _fill_padded(buf_ref, prev_ref, cur_ref, next_ref, halo):
    first, last = _seq_edges(pl.program_id(0))
    buf_ref[0:halo, :] = jnp.where(first, 0.0, prev_ref[...])
    buf_ref[halo:halo + TM, :] = cur_ref[...]
    buf_ref[halo + TM:halo + TM + halo, :] = jnp.where(last, 0.0, next_ref[...])


def _tap_weight(w_ref, k, rows):
    return jnp.concatenate([w_ref[k]] * (rows // 8), axis=0)


def _sublane_replicated(w):
    return jnp.broadcast_to(w[:, None, :], (w.shape[0], 8, w.shape[1]))


def _depthwise(buf_ref, w_ref, taps, first_off, rows, row0):
    acc = None
    for k in range(taps):
        term = _tap_weight(w_ref, k, rows) * buf_ref[row0 + first_off + k:row0 + first_off + k + rows, :]
        acc = term if acc is None else acc + term
    return acc


CONV_ROWS = 32


SHIFT_ROWS = TM + 2 * HALO_A - 8


def _conv0_kernel(ua_ref, uap_ref, uan_ref, ub_ref, ubp_ref, ubn_ref,
                  dww_ref, dwb_ref, lng_ref, lnb_ref, cw_ref, cb_ref,
                  a_ref, v_ref, bufa_ref, bufb_ref, shift_ref):
    _fill_padded(bufa_ref, uap_ref, ua_ref, uan_ref, HALO_A)
    _fill_padded(bufb_ref, ubp_ref, ub_ref, ubn_ref, HALO_S)
    for s in range(1, 8):
        shift_ref[s - 1] = bufa_ref[s:s + SHIFT_ROWS, :]
    first = HALO_A - CONV_A // 2
    for r in range(TM // CONV_ROWS):
        row0 = r * CONV_ROWS
        u = None
        for k in range(CONV_A):
            off = first + k
            base = row0 + off - off % 8
            rows = bufa_ref[base:base + CONV_ROWS, :] if off % 8 == 0 else shift_ref[off % 8 - 1, base:base + CONV_ROWS, :]
            term = _tap_weight(dww_ref, k, CONV_ROWS) * rows
            u = term if u is None else u + term
        u = u + dwb_ref[...]
        mu = jnp.mean(u, axis=-1, keepdims=True)
        uc = u - mu
        var = jnp.mean(uc * uc, axis=-1, keepdims=True)
        y = uc * lax.rsqrt(var + EPS) * lng_ref[...] + lnb_ref[...]
        a_ref[row0:row0 + CONV_ROWS, :] = _silu(y).astype(a_ref.dtype)
        v = _depthwise(bufb_ref, cw_ref, CONV_B, HALO_S - 2, CONV_ROWS, row0) + cb_ref[...]
        v_ref[0, row0:row0 + CONV_ROWS, :] = v


def _seq_major(i):
    is_lat = i < N_LAT_TILES
    return (jnp.where(is_lat, i // TILES_PER_SEQ, i - N_LAT_TILES), jnp.where(is_lat, 1 + i % TILES_PER_SEQ, 0), 0)


def _conv0(ua, ub, dw_w, dw_b, ln_g, ln_b, conv_w, conv_b):
    tok = pl.BlockSpec((TM, HALF), lambda i: (i, 0))
    pa, na = _halo_specs(HALO_A)
    ps, ns = _halo_specs(HALO_S)
    vec = pl.BlockSpec((1, HALF), lambda i: (0, 0))
    return pl.pallas_call(
        _conv0_kernel,
        grid=(N_TILES,),
        in_specs=[tok, pa, na, tok, ps, ns,
                  pl.BlockSpec((CONV_A, 8, HALF), lambda i: (0, 0, 0)), vec, vec, vec,
                  pl.BlockSpec((CONV_B, 8, HALF), lambda i: (0, 0, 0)), vec],
        out_specs=[tok, pl.BlockSpec((1, TM, HALF), _seq_major)],
        out_shape=[jax.ShapeDtypeStruct((N_TOK, HALF), BF16),
                   jax.ShapeDtypeStruct((BATCH, SEQ + CTX, HALF), F32)],
        scratch_shapes=[pltpu.VMEM((TM + 2 * HALO_A, HALF), F32), pltpu.VMEM((TM + 2 * HALO_S, HALF), F32),
                        pltpu.VMEM((7, SHIFT_ROWS, HALF), F32)],
        compiler_params=_params("parallel"),
        name="conv0",
    )(ua, ua, ua, ub, ub, ub, _sublane_replicated(dw_w), dw_b.reshape(1, HALF), ln_g.reshape(1, HALF),
      ln_b.reshape(1, HALF), _sublane_replicated(conv_w), conv_b.reshape(1, HALF))


SCAN_UNROLL = 8


def _scan_kernel(vf_ref, vb_ref, w_ref, gb_ref, lam_ref, yf_ref, yb_ref, h_ref, a_ref, b_ref):
    @pl.when(pl.program_id(0) == 0)
    def _():
        h_ref[...] = jnp.zeros_like(h_ref)

    for d, v_ref in enumerate((vf_ref, vb_ref)):
        v = v_ref[...].reshape(BATCH * TM, HALF)
        vb = v.astype(BF16)
        neg = -lam_ref[d]
        softplus = jnp.maximum(neg, 0.0) + jnp.log(1.0 + jnp.exp(-jnp.abs(neg)))
        for n in range(N_LRU_BLOCKS):
            sl = slice(n * LRU_BLOCK, (n + 1) * LRU_BLOCK)
            g = jnp.dot(vb[:, sl], w_ref[d, n], preferred_element_type=F32)
            r = _sigmoid(g[:, 0:LRU_BLOCK] + gb_ref[d, 0:1, sl])
            i = _sigmoid(g[:, LRU_BLOCK:2 * LRU_BLOCK] + gb_ref[d, 1:2, sl])
            a = jnp.exp(-LRU_C * r * softplus[:, sl])
            a_ref[d, :, sl] = a
            b_ref[d, :, sl] = jnp.sqrt(1.0 - a * a) * (i * v[:, sl])

    def body(s, hs):
        hs = list(hs)
        for u in range(SCAN_UNROLL):
            t = s * SCAN_UNROLL + u
            for d, y_ref in enumerate((yf_ref, yb_ref)):
                row = t if d == 0 else TM - 1 - t
                for bt in range(BATCH):
                    c = d * BATCH + bt
                    src = bt * TM + row
                    hs[c] = a_ref[d, pl.ds(src, 1), :] * hs[c] + b_ref[d, pl.ds(src, 1), :]
                    y_ref[bt, pl.ds(row, 1), :] = hs[c]
        return tuple(hs)

    init = tuple(h_ref[c:c + 1, :] for c in range(2 * BATCH))
    final = lax.fori_loop(0, TM // SCAN_UNROLL, body, init)
    for c in range(2 * BATCH):
        h_ref[c:c + 1, :] = final[c]


def _scan(v, gate_w, gate_b, lam):
    fwd = pl.BlockSpec((BATCH, TM, HALF), lambda j: (0, j, 0))
    bwd = pl.BlockSpec((BATCH, TM, HALF), lambda j: (0, jnp.where(j == 0, 0, TILES_PER_SEQ + 1 - j), 0))
    shp = jax.ShapeDtypeStruct((BATCH, SEQ + CTX, HALF), F32)
    return pl.pallas_call(
        _scan_kernel,
        grid=(TILES_PER_SEQ + 1,),
        in_specs=[fwd, bwd,
                  pl.BlockSpec((2, N_LRU_BLOCKS, LRU_BLOCK, 2 * LRU_BLOCK), lambda j: (0, 0, 0, 0)),
                  pl.BlockSpec((2, 2, HALF), lambda j: (0, 0, 0)),
                  pl.BlockSpec((2, 1, HALF), lambda j: (0, 0, 0))],
        out_specs=[fwd, bwd],
        out_shape=[shp, shp],
        scratch_shapes=[pltpu.VMEM((8, HALF), F32), pltpu.VMEM((2, BATCH * TM, HALF), F32),
                        pltpu.VMEM((2, BATCH * TM, HALF), F32)],
        compiler_params=_params("arbitrary"),
        name="lru_scan",
    )(v, v, gate_w, gate_b, lam)


def _route(score, sel):
    rows = [sel[e:e + 1, :] for e in range(N_EXPERTS)]
    gbest = None
    gidx = None
    for g in range(N_GROUPS):
        top2 = None
        for p in range(PER_GROUP):
            for q in range(p + 1, PER_GROUP):
                s = rows[g * PER_GROUP + p] + rows[g * PER_GROUP + q]
                top2 = s if top2 is None else jnp.maximum(top2, s)
        if g == 0:
            gbest = top2
            gidx = jnp.zeros(top2.shape, jnp.int32)
        else:
            better = top2 > gbest
            gidx = jnp.where(better, g, gidx)
            gbest = jnp.where(better, top2, gbest)
    eint = lax.broadcasted_iota(jnp.int32, sel.shape, 0)
    eidx = eint.astype(F32)
    masked = jnp.where(jnp.right_shift(eint, 2) == gidx, sel, -jnp.inf)
    v1 = jnp.max(masked, axis=0, keepdims=True)
    i1 = jnp.min(jnp.where(masked == v1, eidx, float(N_EXPERTS)), axis=0, keepdims=True)
    masked2 = jnp.where(eidx == i1, -jnp.inf, masked)
    v2 = jnp.max(masked2, axis=0, keepdims=True)
    i2 = jnp.min(jnp.where(masked2 == v2, eidx, float(N_EXPERTS)), axis=0, keepdims=True)
    s1 = jnp.sum(jnp.where(eidx == i1, score, 0.0), axis=0, keepdims=True)
    s2 = jnp.sum(jnp.where(eidx == i2, score, 0.0), axis=0, keepdims=True)
    inv = 1.0 / (s1 + s2)
    return jnp.where(eidx == i1, s1 * inv, 0.0) + jnp.where(eidx == i2, s2 * inv, 0.0), gidx


def _group_ranks(gidx, carry_ref):
    @pl.when(pl.program_id(0) % (TMOE // TM) == 0)
    def _():
        carry_ref[...] = jnp.zeros_like(carry_ref)

    onehot = lax.broadcasted_iota(jnp.int32, (8, TM), 0) == gidx
    oh = jnp.where(onehot, 1.0, 0.0)
    before = lax.broadcasted_iota(jnp.int32, (TM, TM), 0) < lax.broadcasted_iota(jnp.int32, (TM, TM), 1)
    prefix = jnp.dot(oh.astype(BF16), jnp.where(before, 1.0, 0.0).astype(BF16), preferred_element_type=F32)
    carry = carry_ref[...]
    rank = jnp.sum(jnp.where(onehot, prefix + carry[:, 0:1], 0.0), axis=0, keepdims=True)
    carry = carry + jnp.sum(oh, axis=1, keepdims=True)
    carry_ref[...] = carry
    return rank, carry


def _split_bf16(v):
    hi = v.astype(BF16)
    return hi, (v - hi.astype(F32)).astype(BF16)


def _finish_outproj(m1, m2, x, mod_ref, wo_ref, g_ref, rw_ref, rb_ref, h_ref, n_ref, route_ref, cnt_ref, carry_ref):
    mod = mod_ref[0]
    mix = (jnp.dot(m1, wo_ref[0:HALF, :], preferred_element_type=F32)
           + jnp.dot(m2, wo_ref[HALF:2 * HALF, :], preferred_element_type=F32))
    h = x + mod[:, 2 * D:3 * D] * mix
    h_ref[...] = h
    n = _rms_mod(h, g_ref[...], mod[:, 4 * D:5 * D], mod[:, 3 * D:4 * D])
    n_hi, n_lo = _split_bf16(n)
    n_ref[...] = n_hi
    w_hi, w_lo = _split_bf16(rw_ref[...])
    logits = _nt_dot(w_hi, n_hi) + _nt_dot(w_hi, n_lo) + _nt_dot(w_lo, n_hi)
    score = _sigmoid(logits)
    comb, gidx = _route(score, score + rb_ref[...])
    rank, counts = _group_ranks(gidx, carry_ref)
    route_ref[0:N_EXPERTS, :] = comb
    route_ref[ROUTE_GROUP:ROUTE_GROUP + 1, :] = gidx.astype(F32)
    route_ref[ROUTE_RANK:ROUTE_RANK + 1, :] = rank
    route_ref[ROUTE_RANK + 1:ROUTE_ROWS, :] = jnp.zeros((ROUTE_ROWS - ROUTE_RANK - 1, TM), F32)
    cnt_ref[0] = counts


def _outproj0_kernel(a_ref, yf_ref, yb_ref, gg_ref, x_ref, c_ref, mod_ref, wo_ref, g_ref, rw_ref, rb_ref,
                     h_ref, n_ref, route_ref, cnt_ref, carry_ref):
    m2 = ((yf_ref[0] + yb_ref[0]) * gg_ref[...]).astype(BF16)
    _finish_outproj(a_ref[...], m2, _token_tile(x_ref, c_ref), mod_ref, wo_ref, g_ref, rw_ref, rb_ref, h_ref, n_ref,
                    route_ref, cnt_ref, carry_ref)


def _outproj1_kernel(cx_ref, cxp_ref, cxn_ref, bg_ref, att_ref, cw_ref, x_ref, mod_ref, wo_ref, g_ref, rw_ref, rb_ref,
                     h_ref, n_ref, route_ref, cnt_ref, buf_ref, carry_ref):
    _fill_padded(buf_ref, cxp_ref, cx_ref, cxn_ref, HALO_S)
    conv = _depthwise(buf_ref, cw_ref, CONV_C, HALO_S - 1, TM, 0)
    m1 = (bg_ref[...] * conv).astype(BF16)
    _finish_outproj(m1, att_ref[...], x_ref[...], mod_ref, wo_ref, g_ref, rw_ref, rb_ref, h_ref, n_ref, route_ref,
                    cnt_ref, carry_ref)


def _outproj_common(n_tiles):
    in_specs = [pl.BlockSpec((1, 1, 6 * D), _mod_row(TM)),
                pl.BlockSpec((D, D), lambda i: (0, 0)),
                pl.BlockSpec((1, D), lambda i: (0, 0)),
                pl.BlockSpec((N_EXPERTS, D), lambda i: (0, 0)),
                pl.BlockSpec((N_EXPERTS, 1), lambda i: (0, 0))]
    out_specs = [pl.BlockSpec((TM, D), lambda i: (i, 0)),
                 pl.BlockSpec((TM, D), lambda i: (i, 0)),
                 pl.BlockSpec((ROUTE_ROWS, TM), lambda i: (0, i)),
                 pl.BlockSpec((1, 8, 128), lambda i: (i, 0, 0))]
    rows = n_tiles * TM
    out_shape = [jax.ShapeDtypeStruct((rows, D), F32), jax.ShapeDtypeStruct((rows, D), BF16),
                 jax.ShapeDtypeStruct((ROUTE_ROWS, rows), F32), jax.ShapeDtypeStruct((n_tiles, 8, 128), F32)]
    return in_specs, out_specs, out_shape


def _outproj0(a, yf, yb, gg, x, c, mod, wo, g, rw_t, rb):
    tok = pl.BlockSpec((TM, HALF), lambda i: (i, 0))
    scan_tok = pl.BlockSpec((1, TM, HALF), _seq_major)
    common_in, out_specs, out_shape = _outproj_common(N_TILES)
    return pl.pallas_call(
        _outproj0_kernel,
        grid=(N_TILES,),
        in_specs=[tok, scan_tok, scan_tok, tok] + _token_specs() + common_in,
        out_specs=out_specs,
        out_shape=out_shape,
        scratch_shapes=[pltpu.VMEM((8, 128), F32)],
        compiler_params=_params("arbitrary"),
        name="outproj0",
    )(a, yf, yb, gg, x, c, mod, wo, g, rw_t, rb)


def _outproj1(cx, bg, att, conv_w, x, mod, wo, g, rw_t, rb):
    tok = pl.BlockSpec((TM, HALF), lambda i: (i, 0))
    ps, ns = _halo_specs(HALO_S)
    common_in, out_specs, out_shape = _outproj_common(N_LAT_TILES)
    return pl.pallas_call(
        _outproj1_kernel,
        grid=(N_LAT_TILES,),
        in_specs=[tok, ps, ns, tok, tok, pl.BlockSpec((CONV_C, 8, HALF), lambda i: (0, 0, 0)),
                  pl.BlockSpec((TM, D), lambda i: (i, 0))] + common_in,
        out_specs=out_specs,
        out_shape=out_shape,
        scratch_shapes=[pltpu.VMEM((TM + 2 * HALO_S, HALF), F32), pltpu.VMEM((8, 128), F32)],
        compiler_params=_params("arbitrary"),
        name="outproj1",
    )(cx, cx, cx, bg, att, _sublane_replicated(conv_w), x, mod, wo, g, rw_t, rb)


def _moe_kernel(nch_ref, n_ref, rt_ref, r_ref, wg_ref, wu_ref, wd_ref, h_ref, mod_ref, fg_ref, o_ref,
                hid_ref, *, final_norm, first_tile, subtiles):
    i = pl.program_id(0)
    g = pl.program_id(1)

    @pl.when(g == 0)
    def _():
        o_ref[...] = jnp.zeros_like(o_ref)

    gf = g.astype(F32)
    slot_row = lax.broadcasted_iota(jnp.int32, (CAP, TMOE), 0).astype(F32)
    slot_col = lax.broadcasted_iota(jnp.int32, (TMOE, CAP), 1).astype(F32)

    for s in range(subtiles):
        rows = slice(s * TMOE, (s + 1) * TMOE)
        in_group_row = rt_ref[ROUTE_GROUP:ROUTE_GROUP + 1, rows] == gf
        rank_row = rt_ref[ROUTE_RANK:ROUTE_RANK + 1, rows]
        in_group_col = r_ref[rows, ROUTE_GROUP:ROUTE_GROUP + 1] == gf
        rank_col = r_ref[rows, ROUTE_RANK:ROUTE_RANK + 1]

        def chunk(k, carry, rows=rows, in_group_row=in_group_row, rank_row=rank_row,
                  in_group_col=in_group_col, rank_col=rank_col):
            base = (k * CAP).astype(F32)
            sel = slot_row == jnp.where(in_group_row, rank_row - base, -1.0)
            xg = jnp.dot(jnp.where(sel, 1.0, 0.0).astype(BF16), n_ref[rows, :],
                         preferred_element_type=F32).astype(BF16)
            for j in range(PER_GROUP):
                comb_row = rt_ref[pl.ds(g * PER_GROUP + j, 1), rows]
                cw = jnp.sum(jnp.where(sel, comb_row, 0.0), axis=1, keepdims=True)
                hid = (_silu(jnp.dot(xg, wg_ref[j], preferred_element_type=F32))
                       * jnp.dot(xg, wu_ref[j], preferred_element_type=F32) * cw)
                hid_ref[:, j * D_FF:(j + 1) * D_FF] = hid.astype(BF16)
            y = jnp.dot(hid_ref[...], wd_ref[0], preferred_element_type=F32).astype(BF16)
            local = rank_col - base
            back = jnp.where(jnp.logical_and(in_group_col, slot_col == local), 1.0, 0.0).astype(BF16)
            o_ref[rows, :] += jnp.dot(back, y, preferred_element_type=F32)
            return carry

        lax.fori_loop(0, nch_ref[(first_tile + i * subtiles + s) * N_GROUPS + g], chunk, 0)

    @pl.when(g == N_GROUPS - 1)
    def _():
        out = h_ref[...] + mod_ref[0][:, 5 * D:6 * D] * o_ref[...]
        if final_norm:
            out = out * lax.rsqrt(jnp.mean(out * out, axis=-1, keepdims=True) + EPS) * fg_ref[...]
        o_ref[...] = out


def _moe_chunks(counts):
    sub = TMOE // TM
    per_tile = counts[sub - 1::sub, :N_GROUPS, 0].astype(jnp.int32)
    return ((per_tile + (CAP - 1)) // CAP).reshape(-1)


def _moe(n_chunks, n, route_t, route, wg, wu, wd, layer, h, mod, final_g, *, first_tile, n_tiles, subtiles, final_norm,
         name):
    step = subtiles * TMOE
    first = first_tile // subtiles
    mod_row = lambda i, g, nch: (jnp.minimum(((first + i) * step) // SEQ, BATCH), 0, 0)
    tok = pl.BlockSpec((step, D), lambda i, g, nch: (first + i, 0))
    grid_spec = pltpu.PrefetchScalarGridSpec(
        num_scalar_prefetch=1,
        grid=(n_tiles // subtiles, N_GROUPS),
        in_specs=[tok,
                  pl.BlockSpec((ROUTE_ROWS, step), lambda i, g, nch: (0, first + i)),
                  pl.BlockSpec((step, ROUTE_ROWS), lambda i, g, nch: (first + i, 0)),
                  pl.BlockSpec((PER_GROUP, D, D_FF), lambda i, g, nch: (layer * N_GROUPS + g, 0, 0)),
                  pl.BlockSpec((PER_GROUP, D, D_FF), lambda i, g, nch: (layer * N_GROUPS + g, 0, 0)),
                  pl.BlockSpec((1, PER_GROUP * D_FF, D), lambda i, g, nch: (layer * N_GROUPS + g, 0, 0)),
                  tok,
                  pl.BlockSpec((1, 1, 6 * D), mod_row),
                  pl.BlockSpec((1, D), lambda i, g, nch: (0, 0))],
        out_specs=pl.BlockSpec((step, D), lambda i, g, nch: (i, 0)),
        scratch_shapes=[pltpu.VMEM((CAP, PER_GROUP * D_FF), BF16)])
    return pl.pallas_call(
        functools.partial(_moe_kernel, final_norm=final_norm, first_tile=first_tile, subtiles=subtiles),
        grid_spec=grid_spec,
        out_shape=jax.ShapeDtypeStruct((n_tiles * TMOE, D), F32),
        compiler_params=_params("parallel", "arbitrary", vmem=VMEM_LIMIT_MOE),
        name=name,
    )(n_chunks, n, route_t, route, wg, wu, wd, h, mod, final_g)


def _inproj1_kernel(x_ref, c_ref, mod_ref, g_ref, w_ref, cx_ref, bg_ref, q_ref, k_ref, v_ref):
    mod = mod_ref[0]
    n = _rms_mod(_token_tile(x_ref, c_ref), g_ref[...], mod[:, D:2 * D], mod[:, 0:D])
    p = jnp.dot(n.astype(BF16), w_ref[...], preferred_element_type=F32)
    cx_ref[...] = p[:, 2 * HALF:3 * HALF] * p[:, 0:HALF]
    bg_ref[...] = p[:, HALF:2 * HALF]
    q_ref[...] = (p[:, 3 * HALF:4 * HALF] * (HEAD_DIM ** -0.5)).astype(BF16)
    k_ref[...] = p[:, 4 * HALF:5 * HALF].astype(BF16)
    v_ref[...] = p[:, 5 * HALF:6 * HALF].astype(BF16)


def _inproj1(x, c, mod, g, w):
    tok = pl.BlockSpec((TM, HALF), lambda i: (i, 0))
    f = jax.ShapeDtypeStruct((N_TOK, HALF), F32)
    h = jax.ShapeDtypeStruct((N_TOK, HALF), BF16)
    return pl.pallas_call(
        _inproj1_kernel,
        grid=(N_TILES,),
        in_specs=_token_specs() + [
                  pl.BlockSpec((1, 1, 6 * D), _mod_row(TM)),
                  pl.BlockSpec((1, D), lambda i: (0, 0)),
                  pl.BlockSpec((D, 6 * HALF), lambda i: (0, 0))],
        out_specs=[tok] * 5,
        out_shape=[f, f, h, h, h],
        compiler_params=_params("parallel"),
        name="inproj1",
    )(x, c, mod, g, w)


def _natten_kernel(q_ref, kp_ref, kc_ref, kn_ref, vp_ref, vc_ref, vn_ref, kx_ref, vx_ref, bias_ref, o_ref):
    lane = lax.broadcasted_iota(jnp.int32, (TM, 2 * HEAD_DIM), 1)
    low = lane < HEAD_DIM
    head_mask = [low.astype(F32).astype(BF16), jnp.logical_not(low).astype(F32).astype(BF16)]
    for g in range(N_HEADS // 2):
        sl = slice(2 * HEAD_DIM * g, 2 * HEAD_DIM * (g + 1))
        q2 = q_ref[:, sl]
        keys = [kp_ref[:, sl], kc_ref[:, sl], kn_ref[:, sl], kx_ref[:, sl]]
        vals = [vp_ref[:, sl], vc_ref[:, sl], vn_ref[:, sl], vx_ref[:, sl]]
        outs = []
        for hh in range(2):
            qm = q2 * head_mask[hh]
            s = [_nt_dot(qm, k) for k in keys]
            for t in range(3):
                s[t] = s[t] + bias_ref[0, 2 * g + hh, :, t * TM:(t + 1) * TM]
            m = jnp.max(jnp.maximum(jnp.maximum(s[0], s[1]), jnp.maximum(s[2], s[3])), axis=-1, keepdims=True)
            p = [jnp.exp(t - m) for t in s]
            den = jnp.sum(p[0] + p[1] + p[2] + p[3], axis=-1, keepdims=True)
            o = None
            for pt, vt in zip(p, vals):
                term = jnp.dot(pt.astype(BF16), vt, preferred_element_type=F32)
                o = term if o is None else o + term
            outs.append(o * (1.0 / den))
        o_ref[:, sl] = jnp.where(low, outs[0], outs[1]).astype(o_ref.dtype)


def _natten_bias(rpb):
    n_rows = SEQ // GRID_W
    n_dr, n_dc = 2 * WIN_H - 1, 2 * WIN_W - 1
    i = np.arange(ROWS_Q)
    j = np.arange(3 * ROWS_Q)
    col = np.arange(GRID_W)
    col_start = np.clip(col - WIN_W // 2, 0, GRID_W - WIN_W)
    col_ok = (col[None, :] >= col_start[:, None]) & (col[None, :] < col_start[:, None] + WIN_W)
    col_idx = col[None, :] - col[:, None] + (WIN_W - 1)
    onehot = ((col_idx[None] == np.arange(n_dc)[:, None, None]) & col_ok[None]).astype(np.float32)
    col_exp = jnp.dot(rpb.reshape(N_HEADS * n_dr, n_dc).astype(F32), onehot.reshape(n_dc, GRID_W * GRID_W),
                      precision=lax.Precision.HIGHEST).reshape(N_HEADS, n_dr, GRID_W, GRID_W)
    kinds = []
    for r0 in (0, ROWS_Q, n_rows - ROWS_Q):
        r = r0 + i
        kr = r0 - ROWS_Q + j
        r_start = np.clip(r - WIN_H // 2, 0, n_rows - WIN_H)
        row_ok = ((kr[None, :] >= r_start[:, None]) & (kr[None, :] < r_start[:, None] + WIN_H)
                  & (kr[None, :] >= 0) & (kr[None, :] < n_rows))
        row_idx = np.clip(kr[None, :] - r[:, None] + (WIN_H - 1), 0, n_dr - 1)
        b = col_exp[:, row_idx]
        b = b.transpose(0, 1, 3, 2, 4)
        ok = row_ok[:, None, :, None] & col_ok[None, :, None, :]
        kinds.append(jnp.where(ok[None], b, NEG).reshape(N_HEADS, TM, 3 * TM))
    return jnp.stack(kinds).astype(F32)


def _natten(q, k, v, bias):
    def lat(off):
        def index(b, i):
            return (b * TILES_PER_SEQ + jnp.clip(i + off, 0, TILES_PER_SEQ - 1), 0)
        return pl.BlockSpec((TM, HALF), index)

    ctx = pl.BlockSpec((TM, HALF), lambda b, i: (N_LAT_TILES + b, 0))

    def kind(b, i):
        return (jnp.where(i == 0, 0, jnp.where(i == TILES_PER_SEQ - 1, 2, 1)), 0, 0, 0)

    return pl.pallas_call(
        _natten_kernel,
        grid=(BATCH, TILES_PER_SEQ),
        in_specs=[lat(0), lat(-1), lat(0), lat(1), lat(-1), lat(0), lat(1), ctx, ctx,
                  pl.BlockSpec((1, N_HEADS, TM, 3 * TM), kind)],
        out_specs=lat(0),
        out_shape=jax.ShapeDtypeStruct((N_LAT, HALF), BF16),
        compiler_params=_params("parallel", "arbitrary"),
        name="natten",
    )(q, k, k, k, v, v, v, k, v, bias)


def kernel(x, c, ctx, c_ctx, ada_w, ada_b, norm_mix_g, norm_ffn_g, w_out, ab_w_in, a_dw_w, a_dw_b, a_ln_g, a_ln_b,
           b_conv_w, b_conv_b, b_gate_w, b_gate_b, b_lambda, cd_w_in, c_conv_w, d_rpb, router_w, router_bias,
           moe_w_gate, moe_w_up, moe_w_down, final_g):
    x_lat = x.reshape(N_LAT, D)
    x_ctx = ctx.reshape(BATCH * CTX, D)
    cond = jnp.concatenate([c, c_ctx[None], jnp.zeros((8 - BATCH - 1, D), F32)], axis=0)
    mod = _modulation(cond, ada_w, ada_b)
    mod0 = mod[0].reshape(8, 1, 6 * D)
    mod1 = mod[1].reshape(8, 1, 6 * D)

    depth = moe_w_gate.shape[0]
    wg = moe_w_gate.astype(BF16).reshape(depth * N_EXPERTS, D, D_FF)
    wu = moe_w_up.astype(BF16).reshape(depth * N_EXPERTS, D, D_FF)
    wd = moe_w_down.astype(BF16).reshape(depth * N_GROUPS, PER_GROUP * D_FF, D)
    wo = w_out.astype(BF16)
    lat_tiles = N_LAT // TMOE
    ctx_tiles = BATCH * CTX // TMOE
    rw_t = router_w.T
    rb = router_bias.reshape(N_EXPERTS, 1)
    fg = final_g.reshape(1, D)

    ua, ub, gg = _inproj0(x_lat, x_ctx, mod0, norm_mix_g[0].reshape(1, D), ab_w_in[0].astype(BF16))
    a_out, v = _conv0(ua, ub, a_dw_w[0], a_dw_b[0], a_ln_g[0], a_ln_b[0], b_conv_w[0], b_conv_b[0])
    gw = b_gate_w[0]
    gate_w = jnp.concatenate([gw[:, 0], gw[:, 1]], axis=-1).astype(BF16)
    yf, yb = _scan(v, gate_w, b_gate_b[0], b_lambda[0].reshape(2, 1, HALF))
    h1, n2, route0, cnt0 = _outproj0(a_out, yf, yb, gg, x_lat, x_ctx, mod0, wo[0], norm_ffn_g[0].reshape(1, D), rw_t, rb)
    moe0 = functools.partial(_moe, _moe_chunks(cnt0), n2, route0, route0.T, wg, wu, wd, 0, h1, mod0, fg,
                             final_norm=False)
    h2_lat = moe0(first_tile=0, n_tiles=lat_tiles, subtiles=2, name="moe_lat")
    h2_ctx = moe0(first_tile=lat_tiles, n_tiles=ctx_tiles, subtiles=1, name="moe_ctx")

    cx, bg, q, k, vv = _inproj1(h2_lat, h2_ctx, mod1, norm_mix_g[1].reshape(1, D), cd_w_in[0].astype(BF16))
    att = _natten(q, k, vv, _natten_bias(d_rpb[0]))
    h3, n4, route1, cnt1 = _outproj1(cx, bg, att, c_conv_w[0], h2_lat, mod1, wo[1], norm_ffn_g[1].reshape(1, D),
                                     rw_t, rb)
    out = _moe(_moe_chunks(cnt1), n4, route1, route1.T, wg, wu, wd, 1, h3, mod1, fg,
               first_tile=0, n_tiles=lat_tiles, subtiles=2, final_norm=True, name="moe_final")
    return out.reshape(BATCH, SEQ, D)
```

```python
import functools
import math

import jax
import jax.numpy as jnp
import numpy as np
from jax import lax
from jax.experimental import pallas as pl
from jax.experimental.pallas import tpu as pltpu

F32 = jnp.float32
BF16 = jnp.bfloat16

D = 1024
BATCH = 2
SEQ = 8192
CTX = 256
GRID_W = 64
N_LAT = BATCH * SEQ
N_TOK = N_LAT + BATCH * CTX
HALF = 512
CONV_A = 31
CONV_B = 4
CONV_C = 3
LRU_BLOCK = 128
N_LRU_BLOCKS = HALF // LRU_BLOCK
LRU_C = 8.0
HEAD_DIM = 64
N_HEADS = HALF // HEAD_DIM
WIN_H = 8
WIN_W = 16
N_EXPERTS = 16
N_GROUPS = 4
PER_GROUP = N_EXPERTS // N_GROUPS
D_FF = 512
EPS = 1e-6
NEG = -1e30
LOG2E = math.log2(math.e)

TM = 256
TILES_PER_SEQ = SEQ // TM
N_LAT_TILES = N_LAT // TM
N_TILES = N_TOK // TM
TMOE = 512
CAP = 128
ROUTE_ROWS = 24
ROUTE_GROUP = 16
ROUTE_RANK = 17
HALO_A = 16
HALO_S = 8
ROWS_Q = TM // GRID_W
VMEM_LIMIT = 48 * 1024 * 1024
VMEM_LIMIT_MOE = 56 * 1024 * 1024


def _params(*sem, vmem=VMEM_LIMIT):
    return pltpu.CompilerParams(dimension_semantics=sem, vmem_limit_bytes=vmem)


def _sigmoid(x):
    return 0.5 * jnp.tanh(0.5 * x) + 0.5


def _silu(x):
    return x * _sigmoid(x)


def _gelu_tanh(x):
    return 0.5 * x * (1.0 + jnp.tanh(0.7978845608028654 * (x + 0.044715 * (x * x * x))))


def _rms_mod(x, g, scale, shift):
    y = x * lax.rsqrt(jnp.mean(x * x, axis=-1, keepdims=True) + EPS) * g
    return y * (1.0 + scale) + shift


def _nt_dot(a, b):
    return lax.dot_general(a, b, (((1,), (1,)), ((), ())), preferred_element_type=F32)


def _mod_row(tile_rows):
    per_seq = SEQ // tile_rows
    return lambda i: (jnp.minimum(i // per_seq, BATCH), 0, 0)


def _mod_kernel(c_ref, w_ref, b_ref, o_ref):
    c = c_ref[...]
    s = _silu(c).astype(BF16)
    o_ref[0] = jnp.dot(s, w_ref[0].astype(BF16), preferred_element_type=F32) + b_ref[0]


def _modulation(cond, ada_w, ada_b):
    depth = ada_w.shape[0]
    nb = 1536
    return pl.pallas_call(
        _mod_kernel,
        grid=(depth, 6 * D // nb),
        in_specs=[pl.BlockSpec((8, D), lambda l, j: (0, 0)),
                  pl.BlockSpec((1, D, nb), lambda l, j: (l, 0, j)),
                  pl.BlockSpec((1, 1, nb), lambda l, j: (l, 0, j))],
        out_specs=pl.BlockSpec((1, 8, nb), lambda l, j: (l, 0, j)),
        out_shape=jax.ShapeDtypeStruct((depth, 8, 6 * D), F32),
        compiler_params=_params("parallel", "parallel"),
        name="modulation",
    )(cond, ada_w, ada_b.reshape(depth, 1, 6 * D))


def _token_specs():
    lat = pl.BlockSpec((TM, D), lambda i: (jnp.minimum(i, N_LAT_TILES - 1), 0))
    ctx = pl.BlockSpec((TM, D), lambda i: (jnp.maximum(i - N_LAT_TILES, 0), 0))
    return [lat, ctx]


def _token_tile(lat_ref, ctx_ref):
    return jnp.where(pl.program_id(0) < N_LAT_TILES, lat_ref[...], ctx_ref[...])


def _inproj0_kernel(x_ref, c_ref, mod_ref, g_ref, w_ref, ua_ref, ub_ref, gg_ref):
    mod = mod_ref[0]
    n = _rms_mod(_token_tile(x_ref, c_ref), g_ref[...], mod[:, D:2 * D], mod[:, 0:D])
    p = jnp.dot(n.astype(BF16), w_ref[...], preferred_element_type=F32)
    ua_ref[...] = p[:, 0:HALF] * _sigmoid(p[:, HALF:2 * HALF])
    ub_ref[...] = p[:, 2 * HALF:3 * HALF]
    gg_ref[...] = _gelu_tanh(p[:, 3 * HALF:4 * HALF])


def _inproj0(x, c, mod, g, w):
    tok = pl.BlockSpec((TM, HALF), lambda i: (i, 0))
    shp = jax.ShapeDtypeStruct((N_TOK, HALF), F32)
    return pl.pallas_call(
        _inproj0_kernel,
        grid=(N_TILES,),
        in_specs=_token_specs() + [
                  pl.BlockSpec((1, 1, 6 * D), _mod_row(TM)),
                  pl.BlockSpec((1, D), lambda i: (0, 0)),
                  pl.BlockSpec((D, 4 * HALF), lambda i: (0, 0))],
        out_specs=[tok, tok, tok],
        out_shape=[shp, shp, shp],
        compiler_params=_params("parallel"),
        name="inproj0",
    )(x, c, mod, g, w)


def _halo_specs(halo):
    per_tile = TM // halo
    last = N_TOK // halo - 1
    prev = pl.BlockSpec((halo, HALF), lambda i: (jnp.maximum(i * per_tile - 1, 0), 0))
    nxt = pl.BlockSpec((halo, HALF), lambda i: (jnp.minimum((i + 1) * per_tile, last), 0))
    return prev, nxt


def _seq_edges(i):
    is_ctx = i >= N_LAT_TILES
    first = jnp.logical_or(is_ctx, i % TILES_PER_SEQ == 0)
    last = jnp.logical_or(is_ctx, i % TILES_PER_SEQ == TILES_PER_SEQ - 1)
    return first, last


def _fill_padded(buf_ref, prev_ref, cur_ref, next_ref, halo):
    first, last = _seq_edges(pl.program_id(0))
    buf_ref[0:halo, :] = jnp.where(first, 0.0, prev_ref[...])
    buf_ref[halo:halo + TM, :] = cur_ref[...]
    buf_ref[halo + TM:halo + TM + halo, :] = jnp.where(last, 0.0, next_ref[...])


def _tap_weight(w_ref, k, rows):
    return jnp.concatenate([w_ref[k]] * (rows // 8), axis=0)


def _sublane_replicated(w):
    return jnp.broadcast_to(w[:, None, :], (w.shape[0], 8, w.shape[1]))


def _depthwise(buf_ref, w_ref, taps, first_off, rows, row0):
    acc = None
    for k in range(taps):
        term = _tap_weight(w_ref, k, rows) * buf_ref[row0 + first_off + k:row0 + first_off + k + rows, :]
        acc = term if acc is None else acc + term
    return acc


CONV_ROWS = 32


SHIFT_ROWS = TM + 2 * HALO_A - 8


def _conv0_kernel(ua_ref, uap_ref, uan_ref, ub_ref, ubp_ref, ubn_ref,
                  dww_ref, dwb_ref, lng_ref, lnb_ref, cw_ref, cb_ref,
                  a_ref, v_ref, bufa_ref, bufb_ref, shift_ref):
    _fill_padded(bufa_ref, uap_ref, ua_ref, uan_ref, HALO_A)
    _fill_padded(bufb_ref, ubp_ref, ub_ref, ubn_ref, HALO_S)
    for s in range(1, 8):
        shift_ref[s - 1] = bufa_ref[s:s + SHIFT_ROWS, :]
    first = HALO_A - CONV_A // 2
    for r in range(TM // CONV_ROWS):
        row0 = r * CONV_ROWS
        u = None
        for k in range(CONV_A):
            off = first + k
            base = row0 + off - off % 8
            rows = bufa_ref[base:base + CONV_ROWS, :] if off % 8 == 0 else shift_ref[off % 8 - 1, base:base + CONV_ROWS, :]
            term = _tap_weight(dww_ref, k, CONV_ROWS) * rows
            u = term if u is None else u + term
        u = u + dwb_ref[...]
        mu = jnp.mean(u, axis=-1, keepdims=True)
        uc = u - mu
        var = jnp.mean(uc * uc, axis=-1, keepdims=True)
        y = uc * lax.rsqrt(var + EPS) * lng_ref[...] + lnb_ref[...]
        a_ref[row0:row0 + CONV_ROWS, :] = _silu(y).astype(a_ref.dtype)
        v = _depthwise(bufb_ref, cw_ref, CONV_B, HALO_S - 2, CONV_ROWS, row0) + cb_ref[...]
        v_ref[0, row0:row0 + CONV_ROWS, :] = v


def _seq_major(i):
    is_lat = i < N_LAT_TILES
    return (jnp.where(is_lat, i // TILES_PER_SEQ, i - N_LAT_TILES), jnp.where(is_lat, 1 + i % TILES_PER_SEQ, 0), 0)


def _conv0(ua, ub, dw_w, dw_b, ln_g, ln_b, conv_w, conv_b):
    tok = pl.BlockSpec((TM, HALF), lambda i: (i, 0))
    pa, na = _halo_specs(HALO_A)
    ps, ns = _halo_specs(HALO_S)
    vec = pl.BlockSpec((1, HALF), lambda i: (0, 0))
    return pl.pallas_call(
        _conv0_kernel,
        grid=(N_TILES,),
        in_specs=[tok, pa, na, tok, ps, ns,
                  pl.BlockSpec((CONV_A, 8, HALF), lambda i: (0, 0, 0)), vec, vec, vec,
                  pl.BlockSpec((CONV_B, 8, HALF), lambda i: (0, 0, 0)), vec],
        out_specs=[tok, pl.BlockSpec((1, TM, HALF), _seq_major)],
        out_shape=[jax.ShapeDtypeStruct((N_TOK, HALF), BF16),
                   jax.ShapeDtypeStruct((BATCH, SEQ + CTX, HALF), F32)],
        scratch_shapes=[pltpu.VMEM((TM + 2 * HALO_A, HALF), F32), pltpu.VMEM((TM + 2 * HALO_S, HALF), F32),
                        pltpu.VMEM((7, SHIFT_ROWS, HALF), F32)],
        compiler_params=_params("parallel"),
        name="conv0",
    )(ua, ua, ua, ub, ub, ub, _sublane_replicated(dw_w), dw_b.reshape(1, HALF), ln_g.reshape(1, HALF),
      ln_b.reshape(1, HALF), _sublane_replicated(conv_w), conv_b.reshape(1, HALF))


SCAN_UNROLL = 8


def _scan_kernel(vf_ref, vb_ref, w_ref, gb_ref, lam_ref, yf_ref, yb_ref, h_ref, a_ref, b_ref):
    @pl.when(pl.program_id(0) == 0)
    def _():
        h_ref[...] = jnp.zeros_like(h_ref)

    for d, v_ref in enumerate((vf_ref, vb_ref)):
        v = v_ref[...].reshape(BATCH * TM, HALF)
        vb = v.astype(BF16)
        neg = -lam_ref[d]
        softplus = jnp.maximum(neg, 0.0) + jnp.log(1.0 + jnp.exp(-jnp.abs(neg)))
        rate = (-0.5 * LRU_C * LOG2E) * softplus
        for n in range(N_LRU_BLOCKS):
            sl = slice(n * LRU_BLOCK, (n + 1) * LRU_BLOCK)
            g = jnp.dot(vb[:, sl], w_ref[d, n], preferred_element_type=F32)
            tr = jnp.tanh(g[:, 0:LRU_BLOCK] + gb_ref[d, 0:1, sl])
            ti = jnp.tanh(g[:, LRU_BLOCK:2 * LRU_BLOCK] + gb_ref[d, 1:2, sl])
            a = jnp.exp2(rate[:, sl] * tr + rate[:, sl])
            a_ref[d, :, sl] = a
            b_ref[d, :, sl] = jnp.sqrt(1.0 - a * a) * ((0.5 * ti + 0.5) * v[:, sl])

    def body(s, hs):
        hs = list(hs)
        for u in range(SCAN_UNROLL):
            t = s * SCAN_UNROLL + u
            for d, y_ref in enumerate((yf_ref, yb_ref)):
                row = t if d == 0 else TM - 1 - t
                for bt in range(BATCH):
                    c = d * BATCH + bt
                    src = bt * TM + row
                    hs[c] = a_ref[d, pl.ds(src, 1), :] * hs[c] + b_ref[d, pl.ds(src, 1), :]
                    y_ref[bt, pl.ds(row, 1), :] = hs[c]
        return tuple(hs)

    init = tuple(h_ref[c:c + 1, :] for c in range(2 * BATCH))
    final = lax.fori_loop(0, TM // SCAN_UNROLL, body, init)
    for c in range(2 * BATCH):
        h_ref[c:c + 1, :] = final[c]


def _scan(v, gate_w, gate_b, lam):
    fwd = pl.BlockSpec((BATCH, TM, HALF), lambda j: (0, j, 0))
    bwd = pl.BlockSpec((BATCH, TM, HALF), lambda j: (0, jnp.where(j == 0, 0, TILES_PER_SEQ + 1 - j), 0))
    shp = jax.ShapeDtypeStruct((BATCH, SEQ + CTX, HALF), F32)
    return pl.pallas_call(
        _scan_kernel,
        grid=(TILES_PER_SEQ + 1,),
        in_specs=[fwd, bwd,
                  pl.BlockSpec((2, N_LRU_BLOCKS, LRU_BLOCK, 2 * LRU_BLOCK), lambda j: (0, 0, 0, 0)),
                  pl.BlockSpec((2, 2, HALF), lambda j: (0, 0, 0)),
                  pl.BlockSpec((2, 1, HALF), lambda j: (0, 0, 0))],
        out_specs=[fwd, bwd],
        out_shape=[shp, shp],
        scratch_shapes=[pltpu.VMEM((8, HALF), F32), pltpu.VMEM((2, BATCH * TM, HALF), F32),
                        pltpu.VMEM((2, BATCH * TM, HALF), F32)],
        compiler_params=_params("arbitrary"),
        name="lru_scan",
    )(v, v, gate_w, gate_b, lam)


def _route(score, sel):
    rows = [sel[e:e + 1, :] for e in range(N_EXPERTS)]
    gbest = None
    gidx = None
    for g in range(N_GROUPS):
        top2 = None
        for p in range(PER_GROUP):
            for q in range(p + 1, PER_GROUP):
                s = rows[g * PER_GROUP + p] + rows[g * PER_GROUP + q]
                top2 = s if top2 is None else jnp.maximum(top2, s)
        if g == 0:
            gbest = top2
            gidx = jnp.zeros(top2.shape, jnp.int32)
        else:
            better = top2 > gbest
            gidx = jnp.where(better, g, gidx)
            gbest = jnp.where(better, top2, gbest)
    eint = lax.broadcasted_iota(jnp.int32, sel.shape, 0)
    eidx = eint.astype(F32)
    masked = jnp.where(jnp.right_shift(eint, 2) == gidx, sel, -jnp.inf)
    v1 = jnp.max(masked, axis=0, keepdims=True)
    i1 = jnp.min(jnp.where(masked == v1, eidx, float(N_EXPERTS)), axis=0, keepdims=True)
    masked2 = jnp.where(eidx == i1, -jnp.inf, masked)
    v2 = jnp.max(masked2, axis=0, keepdims=True)
    i2 = jnp.min(jnp.where(masked2 == v2, eidx, float(N_EXPERTS)), axis=0, keepdims=True)
    s1 = jnp.sum(jnp.where(eidx == i1, score, 0.0), axis=0, keepdims=True)
    s2 = jnp.sum(jnp.where(eidx == i2, score, 0.0), axis=0, keepdims=True)
    inv = 1.0 / (s1 + s2)
    return jnp.where(eidx == i1, s1 * inv, 0.0) + jnp.where(eidx == i2, s2 * inv, 0.0), gidx


def _group_ranks(gidx, carry_ref):
    @pl.when(pl.program_id(0) % (TMOE // TM) == 0)
    def _():
        carry_ref[...] = jnp.zeros_like(carry_ref)

    onehot = lax.broadcasted_iota(jnp.int32, (8, TM), 0) == gidx
    oh = jnp.where(onehot, 1.0, 0.0)
    before = lax.broadcasted_iota(jnp.int32, (TM, TM), 0) < lax.broadcasted_iota(jnp.int32, (TM, TM), 1)
    prefix = jnp.dot(oh.astype(BF16), jnp.where(before, 1.0, 0.0).astype(BF16), preferred_element_type=F32)
    carry = carry_ref[...]
    rank = jnp.sum(jnp.where(onehot, prefix + carry[:, 0:1], 0.0), axis=0, keepdims=True)
    carry = carry + jnp.sum(oh, axis=1, keepdims=True)
    carry_ref[...] = carry
    return rank, carry


def _split_bf16(v):
    hi = v.astype(BF16)
    return hi, (v - hi.astype(F32)).astype(BF16)


def _finish_outproj(m1, m2, x, mod_ref, wo_ref, g_ref, rw_ref, rb_ref, h_ref, n_ref, route_ref, cnt_ref, carry_ref):
    mod = mod_ref[0]
    mix = (jnp.dot(m1, wo_ref[0:HALF, :], preferred_element_type=F32)
           + jnp.dot(m2, wo_ref[HALF:2 * HALF, :], preferred_element_type=F32))
    h = x + mod[:, 2 * D:3 * D] * mix
    h_ref[...] = h
    n = _rms_mod(h, g_ref[...], mod[:, 4 * D:5 * D], mod[:, 3 * D:4 * D])
    n_hi, n_lo = _split_bf16(n)
    n_ref[...] = n_hi
    w_hi, w_lo = _split_bf16(rw_ref[...])
    logits = _nt_dot(w_hi, n_hi) + _nt_dot(w_hi, n_lo) + _nt_dot(w_lo, n_hi)
    score = _sigmoid(logits)
    comb, gidx = _route(score, score + rb_ref[...])
    rank, counts = _group_ranks(gidx, carry_ref)
    route_ref[0:N_EXPERTS, :] = comb
    route_ref[ROUTE_GROUP:ROUTE_GROUP + 1, :] = gidx.astype(F32)
    route_ref[ROUTE_RANK:ROUTE_RANK + 1, :] = rank
    route_ref[ROUTE_RANK + 1:ROUTE_ROWS, :] = jnp.zeros((ROUTE_ROWS - ROUTE_RANK - 1, TM), F32)
    cnt_ref[0] = counts


def _outproj0_kernel(a_ref, yf_ref, yb_ref, gg_ref, x_ref, c_ref, mod_ref, wo_ref, g_ref, rw_ref, rb_ref,
                     h_ref, n_ref, route_ref, cnt_ref, carry_ref):
    m2 = ((yf_ref[0] + yb_ref[0]) * gg_ref[...]).astype(BF16)
    _finish_outproj(a_ref[...], m2, _token_tile(x_ref, c_ref), mod_ref, wo_ref, g_ref, rw_ref, rb_ref, h_ref, n_ref,
                    route_ref, cnt_ref, carry_ref)


def _outproj1_kernel(cx_ref, cxp_ref, cxn_ref, bg_ref, att_ref, cw_ref, x_ref, mod_ref, wo_ref, g_ref, rw_ref, rb_ref,
                     h_ref, n_ref, route_ref, cnt_ref, buf_ref, carry_ref):
    _fill_padded(buf_ref, cxp_ref, cx_ref, cxn_ref, HALO_S)
    conv = _depthwise(buf_ref, cw_ref, CONV_C, HALO_S - 1, TM, 0)
    m1 = (bg_ref[...] * conv).astype(BF16)
    _finish_outproj(m1, att_ref[...], x_ref[...], mod_ref, wo_ref, g_ref, rw_ref, rb_ref, h_ref, n_ref, route_ref,
                    cnt_ref, carry_ref)


def _outproj_common(n_tiles):
    in_specs = [pl.BlockSpec((1, 1, 6 * D), _mod_row(TM)),
                pl.BlockSpec((D, D), lambda i: (0, 0)),
                pl.BlockSpec((1, D), lambda i: (0, 0)),
                pl.BlockSpec((N_EXPERTS, D), lambda i: (0, 0)),
                pl.BlockSpec((N_EXPERTS, 1), lambda i: (0, 0))]
    out_specs = [pl.BlockSpec((TM, D), lambda i: (i, 0)),
                 pl.BlockSpec((TM, D), lambda i: (i, 0)),
                 pl.BlockSpec((ROUTE_ROWS, TM), lambda i: (0, i)),
                 pl.BlockSpec((1, 8, 128), lambda i: (i, 0, 0))]
    rows = n_tiles * TM
    out_shape = [jax.ShapeDtypeStruct((rows, D), F32), jax.ShapeDtypeStruct((rows, D), BF16),
                 jax.ShapeDtypeStruct((ROUTE_ROWS, rows), F32), jax.ShapeDtypeStruct((n_tiles, 8, 128), F32)]
    return in_specs, out_specs, out_shape


def _outproj0(a, yf, yb, gg, x, c, mod, wo, g, rw_t, rb):
    tok = pl.BlockSpec((TM, HALF), lambda i: (i, 0))
    scan_tok = pl.BlockSpec((1, TM, HALF), _seq_major)
    common_in, out_specs, out_shape = _outproj_common(N_TILES)
    return pl.pallas_call(
        _outproj0_kernel,
        grid=(N_TILES,),
        in_specs=[tok, scan_tok, scan_tok, tok] + _token_specs() + common_in,
        out_specs=out_specs,
        out_shape=out_shape,
        scratch_shapes=[pltpu.VMEM((8, 128), F32)],
        compiler_params=_params("arbitrary"),
        name="outproj0",
    )(a, yf, yb, gg, x, c, mod, wo, g, rw_t, rb)


def _outproj1(cx, bg, att, conv_w, x, mod, wo, g, rw_t, rb):
    tok = pl.BlockSpec((TM, HALF), lambda i: (i, 0))
    ps, ns = _halo_specs(HALO_S)
    common_in, out_specs, out_shape = _outproj_common(N_LAT_TILES)
    return pl.pallas_call(
        _outproj1_kernel,
        grid=(N_LAT_TILES,),
        in_specs=[tok, ps, ns, tok, tok, pl.BlockSpec((CONV_C, 8, HALF), lambda i: (0, 0, 0)),
                  pl.BlockSpec((TM, D), lambda i: (i, 0))] + common_in,
        out_specs=out_specs,
        out_shape=out_shape,
        scratch_shapes=[pltpu.VMEM((TM + 2 * HALO_S, HALF), F32), pltpu.VMEM((8, 128), F32)],
        compiler_params=_params("arbitrary"),
        name="outproj1",
    )(cx, cx, cx, bg, att, _sublane_replicated(conv_w), x, mod, wo, g, rw_t, rb)


def _moe_kernel(nch_ref, n_ref, rt_ref, r_ref, wg_ref, wu_ref, wd_ref, h_ref, mod_ref, fg_ref, o_ref,
                hid_ref, *, final_norm, first_tile, subtiles):
    i = pl.program_id(0)
    g = pl.program_id(1)

    @pl.when(g == 0)
    def _():
        o_ref[...] = jnp.zeros_like(o_ref)

    gf = g.astype(F32)
    slot_row = lax.broadcasted_iota(jnp.int32, (CAP, TMOE), 0).astype(F32)
    slot_col = lax.broadcasted_iota(jnp.int32, (TMOE, CAP), 1).astype(F32)

    for s in range(subtiles):
        rows = slice(s * TMOE, (s + 1) * TMOE)
        in_group_row = rt_ref[ROUTE_GROUP:ROUTE_GROUP + 1, rows] == gf
        rank_row = rt_ref[ROUTE_RANK:ROUTE_RANK + 1, rows]
        in_group_col = r_ref[rows, ROUTE_GROUP:ROUTE_GROUP + 1] == gf
        rank_col = r_ref[rows, ROUTE_RANK:ROUTE_RANK + 1]

        def chunk(k, carry, rows=rows, in_group_row=in_group_row, rank_row=rank_row,
                  in_group_col=in_group_col, rank_col=rank_col):
            base = (k * CAP).astype(F32)
            sel = slot_row == jnp.where(in_group_row, rank_row - base, -1.0)
            xg = jnp.dot(jnp.where(sel, 1.0, 0.0).astype(BF16), n_ref[rows, :],
                         preferred_element_type=F32).astype(BF16)
            for j in range(PER_GROUP):
                comb_row = rt_ref[pl.ds(g * PER_GROUP + j, 1), rows]
                cw = jnp.sum(jnp.where(sel, comb_row, 0.0), axis=1, keepdims=True)
                hid = (_silu(jnp.dot(xg, wg_ref[0, j], preferred_element_type=F32))
                       * jnp.dot(xg, wu_ref[0, j], preferred_element_type=F32) * cw)
                hid_ref[:, j * D_FF:(j + 1) * D_FF] = hid.astype(BF16)
            y = jnp.dot(hid_ref[...], wd_ref[0].reshape(PER_GROUP * D_FF, D),
                        preferred_element_type=F32).astype(BF16)
            local = rank_col - base
            back = jnp.where(jnp.logical_and(in_group_col, slot_col == local), 1.0, 0.0).astype(BF16)
            o_ref[rows, :] += jnp.dot(back, y, preferred_element_type=F32)
            return carry

        lax.fori_loop(0, nch_ref[(first_tile + i * subtiles + s) * N_GROUPS + g], chunk, 0)

    @pl.when(g == N_GROUPS - 1)
    def _():
        out = h_ref[...] + mod_ref[0][:, 5 * D:6 * D] * o_ref[...]
        if final_norm:
            out = out * lax.rsqrt(jnp.mean(out * out, axis=-1, keepdims=True) + EPS) * fg_ref[...]
        o_ref[...] = out


def _moe_chunks(counts):
    sub = TMOE // TM
    per_tile = counts[sub - 1::sub, :N_GROUPS, 0].astype(jnp.int32)
    return ((per_tile + (CAP - 1)) // CAP).reshape(-1)


def _moe(n_chunks, n, route_t, route, wg, wu, wd, layer, h, mod, final_g, *, first_tile, n_tiles, subtiles, final_norm,
         name):
    step = subtiles * TMOE
    first = first_tile // subtiles
    mod_row = lambda i, g, nch: (jnp.minimum(((first + i) * step) // SEQ, BATCH), 0, 0)
    tok = pl.BlockSpec((step, D), lambda i, g, nch: (first + i, 0))
    grid_spec = pltpu.PrefetchScalarGridSpec(
        num_scalar_prefetch=1,
        grid=(n_tiles // subtiles, N_GROUPS),
        in_specs=[tok,
                  pl.BlockSpec((ROUTE_ROWS, step), lambda i, g, nch: (0, first + i)),
                  pl.BlockSpec((step, ROUTE_ROWS), lambda i, g, nch: (first + i, 0)),
                  pl.BlockSpec((1, PER_GROUP, D, D_FF), lambda i, g, nch: (layer, g, 0, 0)),
                  pl.BlockSpec((1, PER_GROUP, D, D_FF), lambda i, g, nch: (layer, g, 0, 0)),
                  pl.BlockSpec((1, PER_GROUP, D_FF, D), lambda i, g, nch: (layer, g, 0, 0)),
                  tok,
                  pl.BlockSpec((1, 1, 6 * D), mod_row),
                  pl.BlockSpec((1, D), lambda i, g, nch: (0, 0))],
        out_specs=pl.BlockSpec((step, D), lambda i, g, nch: (i, 0)),
        scratch_shapes=[pltpu.VMEM((CAP, PER_GROUP * D_FF), BF16)])
    return pl.pallas_call(
        functools.partial(_moe_kernel, final_norm=final_norm, first_tile=first_tile, subtiles=subtiles),
        grid_spec=grid_spec,
        out_shape=jax.ShapeDtypeStruct((n_tiles * TMOE, D), F32),
        compiler_params=_params("parallel", "arbitrary", vmem=VMEM_LIMIT_MOE),
        name=name,
    )(n_chunks, n, route_t, route, wg, wu, wd, h, mod, final_g)


def _inproj1_kernel(x_ref, c_ref, mod_ref, g_ref, w_ref, wvt_ref, cx_ref, bg_ref, q_ref, k_ref, vt_ref):
    mod = mod_ref[0]
    n = _rms_mod(_token_tile(x_ref, c_ref), g_ref[...], mod[:, D:2 * D], mod[:, 0:D]).astype(BF16)
    p = jnp.dot(n, w_ref[...], preferred_element_type=F32)
    cx_ref[...] = p[:, 2 * HALF:3 * HALF] * p[:, 0:HALF]
    bg_ref[...] = p[:, HALF:2 * HALF]
    q_ref[...] = (p[:, 3 * HALF:4 * HALF] * (HEAD_DIM ** -0.5 * LOG2E)).astype(BF16)
    k_ref[...] = p[:, 4 * HALF:5 * HALF].astype(BF16)
    vt_ref[...] = _nt_dot(wvt_ref[...], n).astype(BF16)


def _inproj1(x, c, mod, g, w):
    tok = pl.BlockSpec((TM, HALF), lambda i: (i, 0))
    f = jax.ShapeDtypeStruct((N_TOK, HALF), F32)
    h = jax.ShapeDtypeStruct((N_TOK, HALF), BF16)
    return pl.pallas_call(
        _inproj1_kernel,
        grid=(N_TILES,),
        in_specs=_token_specs() + [
                  pl.BlockSpec((1, 1, 6 * D), _mod_row(TM)),
                  pl.BlockSpec((1, D), lambda i: (0, 0)),
                  pl.BlockSpec((D, 5 * HALF), lambda i: (0, 0)),
                  pl.BlockSpec((HALF, D), lambda i: (0, 0))],
        out_specs=[tok] * 4 + [pl.BlockSpec((HALF, TM), lambda i: (0, i))],
        out_shape=[f, f, h, h, jax.ShapeDtypeStruct((HALF, N_TOK), BF16)],
        compiler_params=_params("parallel"),
        name="inproj1",
    )(x, c, mod, g, w[:, :5 * HALF], w[:, 5 * HALF:].T)


def _natten_kernel(q_ref, kp_ref, kc_ref, kn_ref, vp_ref, vc_ref, vn_ref, kx_ref, vx_ref, bias_ref, o_ref):
    pair = 2 * HEAD_DIM
    low = lax.broadcasted_iota(jnp.int32, (TM, pair), 1) < HEAD_DIM
    head_mask = [low.astype(F32).astype(BF16), jnp.logical_not(low).astype(F32).astype(BF16)]
    top = lax.broadcasted_iota(jnp.int32, (pair, TM), 0) < HEAD_DIM
    own_rows = [top.astype(F32).astype(BF16), jnp.logical_not(top).astype(F32).astype(BF16)]
    for g in range(N_HEADS // 2):
        sl = slice(pair * g, pair * (g + 1))
        q2 = q_ref[:, sl]
        keys = [kp_ref[:, sl], kc_ref[:, sl], kn_ref[:, sl], kx_ref[:, sl]]
        vals_t = [vp_ref[sl, :], vc_ref[sl, :], vn_ref[sl, :], vx_ref[sl, :]]
        halves = []
        for hh in range(2):
            qm = q2 * head_mask[hh]
            s = [_nt_dot(qm, k) for k in keys]
            for t in range(3):
                s[t] = s[t] + bias_ref[0, 2 * g + hh, :, t * TM:(t + 1) * TM]
            m = jnp.max(jnp.maximum(jnp.maximum(s[0], s[1]), jnp.maximum(s[2], s[3])), axis=-1, keepdims=True)
            acc = None
            for st, vt in zip(s, vals_t):
                lhs = vt * own_rows[hh] + own_rows[1 - hh]
                term = _nt_dot(lhs, jnp.exp2(st - m).astype(BF16))
                acc = term if acc is None else acc + term
            if hh == 0:
                halves.append(acc[0:HEAD_DIM] * (1.0 / acc[HEAD_DIM:HEAD_DIM + 1]))
            else:
                halves.append(acc[HEAD_DIM:pair] * (1.0 / acc[0:1]))
        o_ref[:, sl] = jnp.concatenate(halves, axis=0).T.astype(o_ref.dtype)


def _natten_bias(rpb):
    n_rows = SEQ // GRID_W
    n_dr, n_dc = 2 * WIN_H - 1, 2 * WIN_W - 1
    i = np.arange(ROWS_Q)
    j = np.arange(3 * ROWS_Q)
    col = np.arange(GRID_W)
    col_start = np.clip(col - WIN_W // 2, 0, GRID_W - WIN_W)
    col_ok = (col[None, :] >= col_start[:, None]) & (col[None, :] < col_start[:, None] + WIN_W)
    col_idx = col[None, :] - col[:, None] + (WIN_W - 1)
    onehot = ((col_idx[None] == np.arange(n_dc)[:, None, None]) & col_ok[None]).astype(np.float32)
    col_exp = jnp.dot(rpb.reshape(N_HEADS * n_dr, n_dc).astype(F32), onehot.reshape(n_dc, GRID_W * GRID_W),
                      precision=lax.Precision.HIGHEST).reshape(N_HEADS, n_dr, GRID_W, GRID_W)
    kinds = []
    for r0 in (0, ROWS_Q, n_rows - ROWS_Q):
        r = r0 + i
        kr = r0 - ROWS_Q + j
        r_start = np.clip(r - WIN_H // 2, 0, n_rows - WIN_H)
        row_ok = ((kr[None, :] >= r_start[:, None]) & (kr[None, :] < r_start[:, None] + WIN_H)
                  & (kr[None, :] >= 0) & (kr[None, :] < n_rows))
        row_idx = np.clip(kr[None, :] - r[:, None] + (WIN_H - 1), 0, n_dr - 1)
        b = col_exp[:, row_idx]
        b = b.transpose(0, 1, 3, 2, 4)
        ok = row_ok[:, None, :, None] & col_ok[None, :, None, :]
        kinds.append(jnp.where(ok[None], b * LOG2E, NEG).reshape(N_HEADS, TM, 3 * TM))
    return jnp.stack(kinds).astype(F32)


def _natten(q, k, vt, bias):
    def tile(b, i, off):
        return b * TILES_PER_SEQ + jnp.clip(i + off, 0, TILES_PER_SEQ - 1)

    def lat(off):
        return pl.BlockSpec((TM, HALF), lambda b, i: (tile(b, i, off), 0))

    def lat_t(off):
        return pl.BlockSpec((HALF, TM), lambda b, i: (0, tile(b, i, off)))

    ctx = pl.BlockSpec((TM, HALF), lambda b, i: (N_LAT_TILES + b, 0))
    ctx_t = pl.BlockSpec((HALF, TM), lambda b, i: (0, N_LAT_TILES + b))

    def kind(b, i):
        return (jnp.where(i == 0, 0, jnp.where(i == TILES_PER_SEQ - 1, 2, 1)), 0, 0, 0)

    return pl.pallas_call(
        _natten_kernel,
        grid=(BATCH, TILES_PER_SEQ),
        in_specs=[lat(0), lat(-1), lat(0), lat(1), lat_t(-1), lat_t(0), lat_t(1), ctx, ctx_t,
                  pl.BlockSpec((1, N_HEADS, TM, 3 * TM), kind)],
        out_specs=lat(0),
        out_shape=jax.ShapeDtypeStruct((N_LAT, HALF), BF16),
        compiler_params=_params("parallel", "arbitrary"),
        name="natten",
    )(q, k, k, k, vt, vt, vt, k, vt, bias)


def kernel(x, c, ctx, c_ctx, ada_w, ada_b, norm_mix_g, norm_ffn_g, w_out, ab_w_in, a_dw_w, a_dw_b, a_ln_g, a_ln_b,
           b_conv_w, b_conv_b, b_gate_w, b_gate_b, b_lambda, cd_w_in, c_conv_w, d_rpb, router_w, router_bias,
           moe_w_gate, moe_w_up, moe_w_down, final_g):
    x_lat = x.reshape(N_LAT, D)
    x_ctx = ctx.reshape(BATCH * CTX, D)
    cond = jnp.concatenate([c, c_ctx[None], jnp.zeros((8 - BATCH - 1, D), F32)], axis=0)
    mod = _modulation(cond, ada_w, ada_b)
    mod0 = mod[0].reshape(8, 1, 6 * D)
    mod1 = mod[1].reshape(8, 1, 6 * D)

    depth = moe_w_gate.shape[0]
    wg = moe_w_gate.astype(BF16)
    wu = moe_w_up.astype(BF16)
    wd = moe_w_down.astype(BF16)
    wo = w_out.astype(BF16)
    lat_tiles = N_LAT // TMOE
    ctx_tiles = BATCH * CTX // TMOE
    rw_t = router_w.T
    rb = router_bias.reshape(N_EXPERTS, 1)
    fg = final_g.reshape(1, D)

    ua, ub, gg = _inproj0(x_lat, x_ctx, mod0, norm_mix_g[0].reshape(1, D), ab_w_in[0].astype(BF16))
    a_out, v = _conv0(ua, ub, a_dw_w[0], a_dw_b[0], a_ln_g[0], a_ln_b[0], b_conv_w[0], b_conv_b[0])
    gw = b_gate_w[0]
    gate_w = (0.5 * jnp.concatenate([gw[:, 0], gw[:, 1]], axis=-1)).astype(BF16)
    yf, yb = _scan(v, gate_w, 0.5 * b_gate_b[0], b_lambda[0].reshape(2, 1, HALF))
    h1, n2, route0, cnt0 = _outproj0(a_out, yf, yb, gg, x_lat, x_ctx, mod0, wo[0], norm_ffn_g[0].reshape(1, D), rw_t, rb)
    moe0 = functools.partial(_moe, _moe_chunks(cnt0), n2, route0, route0.T, wg, wu, wd, 0, h1, mod0, fg,
                             final_norm=False)
    h2_lat = moe0(first_tile=0, n_tiles=lat_tiles, subtiles=2, name="moe_lat")
    h2_ctx = moe0(first_tile=lat_tiles, n_tiles=ctx_tiles, subtiles=1, name="moe_ctx")

    cx, bg, q, k, vv = _inproj1(h2_lat, h2_ctx, mod1, norm_mix_g[1].reshape(1, D), cd_w_in[0].astype(BF16))
    att = _natten(q, k, vv, _natten_bias(d_rpb[0]))
    h3, n4, route1, cnt1 = _outproj1(cx, bg, att, c_conv_w[0], h2_lat, mod1, wo[1], norm_ffn_g[1].reshape(1, D),
                                     rw_t, rb)
    out = _moe(_moe_chunks(cnt1), n4, route1, route1.T, wg, wu, wd, 1, h3, mod1, fg,
               first_tile=0, n_tiles=lat_tiles, subtiles=2, final_norm=True, name="moe_final")
    return out.reshape(BATCH, SEQ, D)
```

```python
import functools
import math

import jax
import jax.numpy as jnp
import numpy as np
from jax import lax
from jax.experimental import pallas as pl
from jax.experimental.pallas import tpu as pltpu

F32 = jnp.float32
BF16 = jnp.bfloat16

D = 1024
BATCH = 2
SEQ = 8192
CTX = 256
GRID_W = 64
N_LAT = BATCH * SEQ
N_TOK = N_LAT + BATCH * CTX
HALF = 512
CONV_A = 31
CONV_B = 4
CONV_C = 3
LRU_BLOCK = 128
N_LRU_BLOCKS = HALF // LRU_BLOCK
LRU_C = 8.0
HEAD_DIM = 64
N_HEADS = HALF // HEAD_DIM
WIN_H = 8
WIN_W = 16
N_EXPERTS = 16
N_GROUPS = 4
PER_GROUP = N_EXPERTS // N_GROUPS
D_FF = 512
EPS = 1e-6
NEG = -1e30
LOG2E = math.log2(math.e)

TM = 256
TILES_PER_SEQ = SEQ // TM
N_LAT_TILES = N_LAT // TM
N_TILES = N_TOK // TM
TMOE = 512
CAP = 128
ROUTE_ROWS = 24
ROUTE_GROUP = 16
ROUTE_RANK = 17
HALO_A = 16
HALO_S = 8
ROWS_Q = TM // GRID_W
VMEM_LIMIT = 48 * 1024 * 1024
VMEM_LIMIT_MOE = 56 * 1024 * 1024


def _params(*sem, vmem=VMEM_LIMIT):
    return pltpu.CompilerParams(dimension_semantics=sem, vmem_limit_bytes=vmem)


def _sigmoid(x):
    return 0.5 * jnp.tanh(0.5 * x) + 0.5


def _silu(x):
    return x * _sigmoid(x)


def _gelu_tanh(x):
    return 0.5 * x * (1.0 + jnp.tanh(0.7978845608028654 * (x + 0.044715 * (x * x * x))))


def _rms_mod(x, g, scale, shift):
    y = x * lax.rsqrt(jnp.mean(x * x, axis=-1, keepdims=True) + EPS) * g
    return y * (1.0 + scale) + shift


def _nt_dot(a, b):
    return lax.dot_general(a, b, (((1,), (1,)), ((), ())), preferred_element_type=F32)


def _held(hold):
    return (lambda i: i) if hold is None else (lambda i: jnp.minimum(i, hold))


def _mod_row(tile_rows, hold=None):
    per_seq = SEQ // tile_rows
    tile = _held(hold)
    return lambda i: (jnp.minimum(tile(i) // per_seq, BATCH), 0, 0)


def _mod_kernel(c_ref, w_ref, b_ref, o_ref):
    c = c_ref[...]
    s = _silu(c).astype(BF16)
    o_ref[0] = jnp.dot(s, w_ref[0].astype(BF16), preferred_element_type=F32) + b_ref[0]


def _modulation(cond, ada_w, ada_b):
    depth = ada_w.shape[0]
    nb = 1536
    return pl.pallas_call(
        _mod_kernel,
        grid=(depth, 6 * D // nb),
        in_specs=[pl.BlockSpec((8, D), lambda l, j: (0, 0)),
                  pl.BlockSpec((1, D, nb), lambda l, j: (l, 0, j)),
                  pl.BlockSpec((1, 1, nb), lambda l, j: (l, 0, j))],
        out_specs=pl.BlockSpec((1, 8, nb), lambda l, j: (l, 0, j)),
        out_shape=jax.ShapeDtypeStruct((depth, 8, 6 * D), F32),
        compiler_params=_params("parallel", "parallel"),
        name="modulation",
    )(cond, ada_w, ada_b.reshape(depth, 1, 6 * D))


def _token_specs(hold=None):
    tile = _held(hold)
    lat = pl.BlockSpec((TM, D), lambda i: (jnp.minimum(tile(i), N_LAT_TILES - 1), 0))
    ctx = pl.BlockSpec((TM, D), lambda i: (jnp.maximum(tile(i) - N_LAT_TILES, 0), 0))
    return [lat, ctx]


def _token_tile(lat_ref, ctx_ref, tile=None):
    tile = pl.program_id(0) if tile is None else tile
    return jnp.where(tile < N_LAT_TILES, lat_ref[...], ctx_ref[...])


def _inproj0_kernel(x_ref, c_ref, mod_ref, g_ref, w_ref, ua_ref, ub_ref, gg_ref):
    mod = mod_ref[0]
    n = _rms_mod(_token_tile(x_ref, c_ref), g_ref[...], mod[:, D:2 * D], mod[:, 0:D])
    p = jnp.dot(n.astype(BF16), w_ref[...], preferred_element_type=F32)
    ua_ref[...] = p[:, 0:HALF] * _sigmoid(p[:, HALF:2 * HALF])
    ub_ref[...] = p[:, 2 * HALF:3 * HALF]
    gg_ref[...] = _gelu_tanh(p[:, 3 * HALF:4 * HALF])


def _inproj0(x, c, mod, g, w):
    tok = pl.BlockSpec((TM, HALF), lambda i: (i, 0))
    shp = jax.ShapeDtypeStruct((N_TOK, HALF), F32)
    return pl.pallas_call(
        _inproj0_kernel,
        grid=(N_TILES,),
        in_specs=_token_specs() + [
                  pl.BlockSpec((1, 1, 6 * D), _mod_row(TM)),
                  pl.BlockSpec((1, D), lambda i: (0, 0)),
                  pl.BlockSpec((D, 4 * HALF), lambda i: (0, 0))],
        out_specs=[tok, tok, tok],
        out_shape=[shp, shp, shp],
        compiler_params=_params("parallel"),
        name="inproj0",
    )(x, c, mod, g, w)


def _halo_specs(halo, hold=None):
    per_tile = TM // halo
    last = N_TOK // halo - 1
    tile = _held(hold)
    prev = pl.BlockSpec((halo, HALF), lambda i: (jnp.maximum(tile(i) * per_tile - 1, 0), 0))
    nxt = pl.BlockSpec((halo, HALF), lambda i: (jnp.minimum((tile(i) + 1) * per_tile, last), 0))
    return prev, nxt


def _seq_edges(i):
    is_ctx = i >= N_LAT_TILES
    first = jnp.logical_or(is_ctx, i % TILES_PER_SEQ == 0)
    last = jnp.logical_or(is_ctx, i % TILES_PER_SEQ == TILES_PER_SEQ - 1)
    return first, last


def _fill_padded(buf_ref, prev_ref, cur_ref, next_ref, halo, tile=None):
    first, last = _seq_edges(pl.program_id(0) if tile is None else tile)
    buf_ref[0:halo, :] = jnp.where(first, 0.0, prev_ref[...])
    buf_ref[halo:halo + TM, :] = cur_ref[...]
    buf_ref[halo + TM:halo + TM + halo, :] = jnp.where(last, 0.0, next_ref[...])


def _tap_weight(w_ref, k, rows):
    return jnp.concatenate([w_ref[k]] * (rows // 8), axis=0)


def _sublane_replicated(w):
    return jnp.broadcast_to(w[:, None, :], (w.shape[0], 8, w.shape[1]))


def _depthwise(buf_ref, w_ref, taps, first_off, rows, row0):
    acc = None
    for k in range(taps):
        term = _tap_weight(w_ref, k, rows) * buf_ref[row0 + first_off + k:row0 + first_off + k + rows, :]
        acc = term if acc is None else acc + term
    return acc


CONV_ROWS = 32


SHIFT_ROWS = TM + 2 * HALO_A - 8


def _conv0_kernel(ua_ref, uap_ref, uan_ref, ub_ref, ubp_ref, ubn_ref,
                  dww_ref, dwb_ref, lng_ref, lnb_ref, cw_ref, cb_ref,
                  a_ref, v_ref, bufa_ref, bufb_ref, shift_ref):
    _fill_padded(bufa_ref, uap_ref, ua_ref, uan_ref, HALO_A)
    _fill_padded(bufb_ref, ubp_ref, ub_ref, ubn_ref, HALO_S)
    for s in range(1, 8):
        shift_ref[s - 1] = bufa_ref[s:s + SHIFT_ROWS, :]
    first = HALO_A - CONV_A // 2
    for r in range(TM // CONV_ROWS):
        row0 = r * CONV_ROWS
        u = None
        for k in range(CONV_A):
            off = first + k
            base = row0 + off - off % 8
            rows = bufa_ref[base:base + CONV_ROWS, :] if off % 8 == 0 else shift_ref[off % 8 - 1, base:base + CONV_ROWS, :]
            term = _tap_weight(dww_ref, k, CONV_ROWS) * rows
            u = term if u is None else u + term
        u = u + dwb_ref[...]
        mu = jnp.mean(u, axis=-1, keepdims=True)
        uc = u - mu
        var = jnp.mean(uc * uc, axis=-1, keepdims=True)
        y = uc * lax.rsqrt(var + EPS) * lng_ref[...] + lnb_ref[...]
        a_ref[row0:row0 + CONV_ROWS, :] = _silu(y).astype(a_ref.dtype)
        v = _depthwise(bufb_ref, cw_ref, CONV_B, HALO_S - 2, CONV_ROWS, row0) + cb_ref[...]
        v_ref[0, row0:row0 + CONV_ROWS, :] = v


def _seq_major(i, hold=None):
    i = _held(hold)(i)
    is_lat = i < N_LAT_TILES
    return (jnp.where(is_lat, i // TILES_PER_SEQ, i - N_LAT_TILES), jnp.where(is_lat, 1 + i % TILES_PER_SEQ, 0), 0)


def _conv0(ua, ub, dw_w, dw_b, ln_g, ln_b, conv_w, conv_b):
    tok = pl.BlockSpec((TM, HALF), lambda i: (i, 0))
    pa, na = _halo_specs(HALO_A)
    ps, ns = _halo_specs(HALO_S)
    vec = pl.BlockSpec((1, HALF), lambda i: (0, 0))
    return pl.pallas_call(
        _conv0_kernel,
        grid=(N_TILES,),
        in_specs=[tok, pa, na, tok, ps, ns,
                  pl.BlockSpec((CONV_A, 8, HALF), lambda i: (0, 0, 0)), vec, vec, vec,
                  pl.BlockSpec((CONV_B, 8, HALF), lambda i: (0, 0, 0)), vec],
        out_specs=[tok, pl.BlockSpec((1, TM, HALF), _seq_major)],
        out_shape=[jax.ShapeDtypeStruct((N_TOK, HALF), BF16),
                   jax.ShapeDtypeStruct((BATCH, SEQ + CTX, HALF), F32)],
        scratch_shapes=[pltpu.VMEM((TM + 2 * HALO_A, HALF), F32), pltpu.VMEM((TM + 2 * HALO_S, HALF), F32),
                        pltpu.VMEM((7, SHIFT_ROWS, HALF), F32)],
        compiler_params=_params("parallel"),
        name="conv0",
    )(ua, ua, ua, ub, ub, ub, _sublane_replicated(dw_w), dw_b.reshape(1, HALF), ln_g.reshape(1, HALF),
      ln_b.reshape(1, HALF), _sublane_replicated(conv_w), conv_b.reshape(1, HALF))


SCAN_UNROLL = 8


def _scan_kernel(vf_ref, vb_ref, w_ref, gb_ref, lam_ref, wg_ref, wu_ref, wd_ref,
                 yf_ref, yb_ref, wg_out, wu_out, wd_out, h_ref, a_ref, b_ref):
    wg_out[...] = wg_ref[...].astype(BF16)
    wu_out[...] = wu_ref[...].astype(BF16)
    wd_out[...] = wd_ref[...].astype(BF16)

    @pl.when(pl.program_id(0) == 0)
    def _():
        h_ref[...] = jnp.zeros_like(h_ref)

    for d, v_ref in enumerate((vf_ref, vb_ref)):
        v = v_ref[...].reshape(BATCH * TM, HALF)
        vb = v.astype(BF16)
        neg = -lam_ref[d]
        softplus = jnp.maximum(neg, 0.0) + jnp.log(1.0 + jnp.exp(-jnp.abs(neg)))
        rate = (-0.5 * LRU_C * LOG2E) * softplus
        for n in range(N_LRU_BLOCKS):
            sl = slice(n * LRU_BLOCK, (n + 1) * LRU_BLOCK)
            g = jnp.dot(vb[:, sl], w_ref[d, n], preferred_element_type=F32)
            tr = jnp.tanh(g[:, 0:LRU_BLOCK] + gb_ref[d, 0:1, sl])
            ti = jnp.tanh(g[:, LRU_BLOCK:2 * LRU_BLOCK] + gb_ref[d, 1:2, sl])
            a = jnp.exp2(rate[:, sl] * tr + rate[:, sl])
            a_ref[d, :, sl] = a
            b_ref[d, :, sl] = jnp.sqrt(1.0 - a * a) * ((0.5 * ti + 0.5) * v[:, sl])

    def body(s, hs):
        hs = list(hs)
        for u in range(SCAN_UNROLL):
            t = s * SCAN_UNROLL + u
            for d, y_ref in enumerate((yf_ref, yb_ref)):
                row = t if d == 0 else TM - 1 - t
                for bt in range(BATCH):
                    c = d * BATCH + bt
                    src = bt * TM + row
                    hs[c] = a_ref[d, pl.ds(src, 1), :] * hs[c] + b_ref[d, pl.ds(src, 1), :]
                    y_ref[bt, pl.ds(row, 1), :] = hs[c]
        return tuple(hs)

    init = tuple(h_ref[c:c + 1, :] for c in range(2 * BATCH))
    final = lax.fori_loop(0, TM // SCAN_UNROLL, body, init)
    for c in range(2 * BATCH):
        h_ref[c:c + 1, :] = final[c]


def _scan(v, gate_w, gate_b, lam, moe_wg, moe_wu, moe_wd):
    steps = TILES_PER_SEQ + 1
    depth = moe_wg.shape[0]
    assert depth * N_EXPERTS <= steps
    fwd = pl.BlockSpec((BATCH, TM, HALF), lambda j: (0, j, 0))
    bwd = pl.BlockSpec((BATCH, TM, HALF), lambda j: (0, jnp.where(j == 0, 0, TILES_PER_SEQ + 1 - j), 0))
    shp = jax.ShapeDtypeStruct((BATCH, SEQ + CTX, HALF), F32)

    def expert(j):
        s = jnp.minimum(j, depth * N_EXPERTS - 1)
        return (s // N_EXPERTS, s % N_EXPERTS, 0, 0)

    up = pl.BlockSpec((1, 1, D, D_FF), expert)
    down = pl.BlockSpec((1, 1, D_FF, D), expert)
    return pl.pallas_call(
        _scan_kernel,
        grid=(steps,),
        in_specs=[fwd, bwd,
                  pl.BlockSpec((2, N_LRU_BLOCKS, LRU_BLOCK, 2 * LRU_BLOCK), lambda j: (0, 0, 0, 0)),
                  pl.BlockSpec((2, 2, HALF), lambda j: (0, 0, 0)),
                  pl.BlockSpec((2, 1, HALF), lambda j: (0, 0, 0)),
                  up, up, down],
        out_specs=[fwd, bwd, up, up, down],
        out_shape=[shp, shp, jax.ShapeDtypeStruct(moe_wg.shape, BF16), jax.ShapeDtypeStruct(moe_wu.shape, BF16),
                   jax.ShapeDtypeStruct(moe_wd.shape, BF16)],
        scratch_shapes=[pltpu.VMEM((8, HALF), F32), pltpu.VMEM((2, BATCH * TM, HALF), F32),
                        pltpu.VMEM((2, BATCH * TM, HALF), F32)],
        compiler_params=_params("arbitrary"),
        name="lru_scan",
    )(v, v, gate_w, gate_b, lam, moe_wg, moe_wu, moe_wd)


def _route(score, sel):
    rows = [sel[e:e + 1, :] for e in range(N_EXPERTS)]
    gbest = None
    gidx = None
    for g in range(N_GROUPS):
        top2 = None
        for p in range(PER_GROUP):
            for q in range(p + 1, PER_GROUP):
                s = rows[g * PER_GROUP + p] + rows[g * PER_GROUP + q]
                top2 = s if top2 is None else jnp.maximum(top2, s)
        if g == 0:
            gbest = top2
            gidx = jnp.zeros(top2.shape, jnp.int32)
        else:
            better = top2 > gbest
            gidx = jnp.where(better, g, gidx)
            gbest = jnp.where(better, top2, gbest)
    eint = lax.broadcasted_iota(jnp.int32, sel.shape, 0)
    eidx = eint.astype(F32)
    masked = jnp.where(jnp.right_shift(eint, 2) == gidx, sel, -jnp.inf)
    v1 = jnp.max(masked, axis=0, keepdims=True)
    i1 = jnp.min(jnp.where(masked == v1, eidx, float(N_EXPERTS)), axis=0, keepdims=True)
    masked2 = jnp.where(eidx == i1, -jnp.inf, masked)
    v2 = jnp.max(masked2, axis=0, keepdims=True)
    i2 = jnp.min(jnp.where(masked2 == v2, eidx, float(N_EXPERTS)), axis=0, keepdims=True)
    s1 = jnp.sum(jnp.where(eidx == i1, score, 0.0), axis=0, keepdims=True)
    s2 = jnp.sum(jnp.where(eidx == i2, score, 0.0), axis=0, keepdims=True)
    inv = 1.0 / (s1 + s2)
    return jnp.where(eidx == i1, s1 * inv, 0.0) + jnp.where(eidx == i2, s2 * inv, 0.0), gidx


def _group_ranks(gidx, carry_ref, tile):
    onehot = lax.broadcasted_iota(jnp.int32, (8, TM), 0) == gidx
    oh = jnp.where(onehot, 1.0, 0.0)
    before = lax.broadcasted_iota(jnp.int32, (TM, TM), 0) < lax.broadcasted_iota(jnp.int32, (TM, TM), 1)
    prefix = jnp.dot(oh.astype(BF16), jnp.where(before, 1.0, 0.0).astype(BF16), preferred_element_type=F32)
    carry = jnp.where(tile % (TMOE // TM) == 0, 0.0, carry_ref[...])
    rank = jnp.sum(jnp.where(onehot, prefix + carry[:, 0:1], 0.0), axis=0, keepdims=True)
    carry = carry + jnp.sum(oh, axis=1, keepdims=True)
    carry_ref[...] = carry
    return rank, carry


def _split_bf16(v):
    hi = v.astype(BF16)
    return hi, (v - hi.astype(F32)).astype(BF16)


def _outproj_step(m1, m2, x, mod_ref, wo_ref, g_ref, rw_ref, rb_ref, h_ref, n_ref, route_ref, cnt_ref, carry_ref,
                  nprev_ref):
    step = pl.program_id(0)
    n_hi, n_lo = _split_bf16(nprev_ref[...])
    w_hi, w_lo = _split_bf16(rw_ref[...])

    mod = mod_ref[0]
    mix = (jnp.dot(m1, wo_ref[0:HALF, :], preferred_element_type=F32)
           + jnp.dot(m2, wo_ref[HALF:2 * HALF, :], preferred_element_type=F32))

    logits = _nt_dot(w_hi, n_hi) + _nt_dot(w_hi, n_lo) + _nt_dot(w_lo, n_hi)

    h = x + mod[:, 2 * D:3 * D] * mix
    h_ref[...] = h
    n = _rms_mod(h, g_ref[...], mod[:, 4 * D:5 * D], mod[:, 3 * D:4 * D])
    n_ref[...] = n.astype(n_ref.dtype)
    nprev_ref[...] = n

    score = _sigmoid(logits)
    comb, gidx = _route(score, score + rb_ref[...])
    rank, counts = _group_ranks(gidx, carry_ref, step - 1)
    route_ref[0:N_EXPERTS, :] = comb
    route_ref[ROUTE_GROUP:ROUTE_GROUP + 1, :] = gidx.astype(F32)
    route_ref[ROUTE_RANK:ROUTE_RANK + 1, :] = rank
    route_ref[ROUTE_RANK + 1:ROUTE_ROWS, :] = jnp.zeros((ROUTE_ROWS - ROUTE_RANK - 1, TM), F32)
    cnt_ref[0] = counts


def _init_pipeline(carry_ref, nprev_ref):
    @pl.when(pl.program_id(0) == 0)
    def _():
        nprev_ref[...] = jnp.zeros_like(nprev_ref)
        carry_ref[...] = jnp.zeros_like(carry_ref)


def _outproj0_kernel(a_ref, yf_ref, yb_ref, gg_ref, x_ref, c_ref, mod_ref, wo_ref, g_ref, rw_ref, rb_ref,
                     h_ref, n_ref, route_ref, cnt_ref, carry_ref, nprev_ref):
    _init_pipeline(carry_ref, nprev_ref)
    tile = jnp.minimum(pl.program_id(0), N_TILES - 1)
    m2 = ((yf_ref[0] + yb_ref[0]) * gg_ref[...]).astype(BF16)
    _outproj_step(a_ref[...], m2, _token_tile(x_ref, c_ref, tile), mod_ref, wo_ref, g_ref, rw_ref, rb_ref,
                  h_ref, n_ref, route_ref, cnt_ref, carry_ref, nprev_ref)


def _outproj1_kernel(cx_ref, cxp_ref, cxn_ref, bg_ref, att_ref, cw_ref, x_ref, mod_ref, wo_ref, g_ref, rw_ref, rb_ref,
                     h_ref, n_ref, route_ref, cnt_ref, buf_ref, carry_ref, nprev_ref):
    _init_pipeline(carry_ref, nprev_ref)
    tile = jnp.minimum(pl.program_id(0), N_LAT_TILES - 1)
    _fill_padded(buf_ref, cxp_ref, cx_ref, cxn_ref, HALO_S, tile)
    conv = _depthwise(buf_ref, cw_ref, CONV_C, HALO_S - 1, TM, 0)
    m1 = (bg_ref[...] * conv).astype(BF16)
    _outproj_step(m1, att_ref[...], x_ref[...], mod_ref, wo_ref, g_ref, rw_ref, rb_ref,
                  h_ref, n_ref, route_ref, cnt_ref, carry_ref, nprev_ref)


def _outproj_common(n_tiles):
    hold = n_tiles - 1
    tile = _held(hold)
    routed = lambda i: jnp.maximum(i - 1, 0)
    in_specs = [pl.BlockSpec((1, 1, 6 * D), _mod_row(TM, hold)),
                pl.BlockSpec((D, D), lambda i: (0, 0)),
                pl.BlockSpec((1, D), lambda i: (0, 0)),
                pl.BlockSpec((N_EXPERTS, D), lambda i: (0, 0)),
                pl.BlockSpec((N_EXPERTS, 1), lambda i: (0, 0))]
    out_specs = [pl.BlockSpec((TM, D), lambda i: (tile(i), 0)),
                 pl.BlockSpec((TM, D), lambda i: (tile(i), 0)),
                 pl.BlockSpec((ROUTE_ROWS, TM), lambda i: (0, routed(i))),
                 pl.BlockSpec((1, 8, 128), lambda i: (routed(i), 0, 0))]
    rows = n_tiles * TM
    out_shape = [jax.ShapeDtypeStruct((rows, D), F32), jax.ShapeDtypeStruct((rows, D), BF16),
                 jax.ShapeDtypeStruct((ROUTE_ROWS, rows), F32), jax.ShapeDtypeStruct((n_tiles, 8, 128), F32)]
    scratch = [pltpu.VMEM((8, 128), F32), pltpu.VMEM((TM, D), F32)]
    return in_specs, out_specs, out_shape, scratch


def _outproj0(a, yf, yb, gg, x, c, mod, wo, g, rw_t, rb):
    hold = N_TILES - 1
    tok = pl.BlockSpec((TM, HALF), lambda i: (_held(hold)(i), 0))
    scan_tok = pl.BlockSpec((1, TM, HALF), functools.partial(_seq_major, hold=hold))
    common_in, out_specs, out_shape, scratch = _outproj_common(N_TILES)
    return pl.pallas_call(
        _outproj0_kernel,
        grid=(N_TILES + 1,),
        in_specs=[tok, scan_tok, scan_tok, tok] + _token_specs(hold) + common_in,
        out_specs=out_specs,
        out_shape=out_shape,
        scratch_shapes=scratch,
        compiler_params=_params("arbitrary"),
        name="outproj0",
    )(a, yf, yb, gg, x, c, mod, wo, g, rw_t, rb)


def _outproj1(cx, bg, att, conv_w, x, mod, wo, g, rw_t, rb):
    hold = N_LAT_TILES - 1
    tok = pl.BlockSpec((TM, HALF), lambda i: (_held(hold)(i), 0))
    ps, ns = _halo_specs(HALO_S, hold)
    common_in, out_specs, out_shape, scratch = _outproj_common(N_LAT_TILES)
    return pl.pallas_call(
        _outproj1_kernel,
        grid=(N_LAT_TILES + 1,),
        in_specs=[tok, ps, ns, tok, tok, pl.BlockSpec((CONV_C, 8, HALF), lambda i: (0, 0, 0)),
                  pl.BlockSpec((TM, D), lambda i: (_held(hold)(i), 0))] + common_in,
        out_specs=out_specs,
        out_shape=out_shape,
        scratch_shapes=[pltpu.VMEM((TM + 2 * HALO_S, HALF), F32)] + scratch,
        compiler_params=_params("arbitrary"),
        name="outproj1",
    )(cx, cx, cx, bg, att, _sublane_replicated(conv_w), x, mod, wo, g, rw_t, rb)


def _moe_kernel(nch_ref, n_ref, rt_ref, r_ref, wg_ref, wu_ref, wd_ref, h_ref, mod_ref, fg_ref, o_ref,
                hid_ref, *, final_norm, first_tile, subtiles):
    i = pl.program_id(0)
    g = pl.program_id(1)

    @pl.when(g == 0)
    def _():
        o_ref[...] = jnp.zeros_like(o_ref)

    gf = g.astype(F32)
    slot_row = lax.broadcasted_iota(jnp.int32, (CAP, TMOE), 0).astype(F32)
    slot_col = lax.broadcasted_iota(jnp.int32, (TMOE, CAP), 1).astype(F32)

    for s in range(subtiles):
        rows = slice(s * TMOE, (s + 1) * TMOE)
        in_group_row = rt_ref[ROUTE_GROUP:ROUTE_GROUP + 1, rows] == gf
        rank_row = rt_ref[ROUTE_RANK:ROUTE_RANK + 1, rows]
        in_group_col = r_ref[rows, ROUTE_GROUP:ROUTE_GROUP + 1] == gf
        rank_col = r_ref[rows, ROUTE_RANK:ROUTE_RANK + 1]

        def chunk(k, carry, rows=rows, in_group_row=in_group_row, rank_row=rank_row,
                  in_group_col=in_group_col, rank_col=rank_col):
            base = (k * CAP).astype(F32)
            sel = slot_row == jnp.where(in_group_row, rank_row - base, -1.0)
            xg = jnp.dot(jnp.where(sel, 1.0, 0.0).astype(BF16), n_ref[rows, :],
                         preferred_element_type=F32).astype(BF16)
            for j in range(PER_GROUP):
                comb_row = rt_ref[pl.ds(g * PER_GROUP + j, 1), rows]
                cw = jnp.sum(jnp.where(sel, comb_row, 0.0), axis=1, keepdims=True)
                hid = (_silu(jnp.dot(xg, wg_ref[0, j], preferred_element_type=F32))
                       * jnp.dot(xg, wu_ref[0, j], preferred_element_type=F32) * cw)
                hid_ref[:, j * D_FF:(j + 1) * D_FF] = hid.astype(BF16)
            y = jnp.dot(hid_ref[...], wd_ref[0].reshape(PER_GROUP * D_FF, D),
                        preferred_element_type=F32).astype(BF16)
            local = rank_col - base
            back = jnp.where(jnp.logical_and(in_group_col, slot_col == local), 1.0, 0.0).astype(BF16)
            o_ref[rows, :] += jnp.dot(back, y, preferred_element_type=F32)
            return carry

        lax.fori_loop(0, nch_ref[(first_tile + i * subtiles + s) * N_GROUPS + g], chunk, 0)

    @pl.when(g == N_GROUPS - 1)
    def _():
        out = h_ref[...] + mod_ref[0][:, 5 * D:6 * D] * o_ref[...]
        if final_norm:
            out = out * lax.rsqrt(jnp.mean(out * out, axis=-1, keepdims=True) + EPS) * fg_ref[...]
        o_ref[...] = out


def _moe_chunks(counts):
    sub = TMOE // TM
    per_tile = counts[sub - 1::sub, :N_GROUPS, 0].astype(jnp.int32)
    return ((per_tile + (CAP - 1)) // CAP).reshape(-1)


def _moe(n_chunks, n, route_t, route, wg, wu, wd, layer, h, mod, final_g, *, first_tile, n_tiles, subtiles, final_norm,
         name):
    step = subtiles * TMOE
    first = first_tile // subtiles
    mod_row = lambda i, g, nch: (jnp.minimum(((first + i) * step) // SEQ, BATCH), 0, 0)
    tok = pl.BlockSpec((step, D), lambda i, g, nch: (first + i, 0))
    grid_spec = pltpu.PrefetchScalarGridSpec(
        num_scalar_prefetch=1,
        grid=(n_tiles // subtiles, N_GROUPS),
        in_specs=[tok,
                  pl.BlockSpec((ROUTE_ROWS, step), lambda i, g, nch: (0, first + i)),
                  pl.BlockSpec((step, ROUTE_ROWS), lambda i, g, nch: (first + i, 0)),
                  pl.BlockSpec((1, PER_GROUP, D, D_FF), lambda i, g, nch: (layer, g, 0, 0)),
                  pl.BlockSpec((1, PER_GROUP, D, D_FF), lambda i, g, nch: (layer, g, 0, 0)),
                  pl.BlockSpec((1, PER_GROUP, D_FF, D), lambda i, g, nch: (layer, g, 0, 0)),
                  tok,
                  pl.BlockSpec((1, 1, 6 * D), mod_row),
                  pl.BlockSpec((1, D), lambda i, g, nch: (0, 0))],
        out_specs=pl.BlockSpec((step, D), lambda i, g, nch: (i, 0)),
        scratch_shapes=[pltpu.VMEM((CAP, PER_GROUP * D_FF), BF16)])
    return pl.pallas_call(
        functools.partial(_moe_kernel, final_norm=final_norm, first_tile=first_tile, subtiles=subtiles),
        grid_spec=grid_spec,
        out_shape=jax.ShapeDtypeStruct((n_tiles * TMOE, D), F32),
        compiler_params=_params("parallel", "arbitrary", vmem=VMEM_LIMIT_MOE),
        name=name,
    )(n_chunks, n, route_t, route, wg, wu, wd, h, mod, final_g)


def _inproj1_kernel(x_ref, c_ref, mod_ref, g_ref, w_ref, wvt_ref, cx_ref, bg_ref, q_ref, k_ref, vt_ref):
    mod = mod_ref[0]
    n = _rms_mod(_token_tile(x_ref, c_ref), g_ref[...], mod[:, D:2 * D], mod[:, 0:D]).astype(BF16)
    p = jnp.dot(n, w_ref[...], preferred_element_type=F32)
    cx_ref[...] = p[:, 2 * HALF:3 * HALF] * p[:, 0:HALF]
    bg_ref[...] = p[:, HALF:2 * HALF]
    q_ref[...] = (p[:, 3 * HALF:4 * HALF] * (HEAD_DIM ** -0.5 * LOG2E)).astype(BF16)
    k_ref[...] = p[:, 4 * HALF:5 * HALF].astype(BF16)
    vt_ref[...] = _nt_dot(wvt_ref[...], n).astype(BF16)


def _inproj1(x, c, mod, g, w):
    tok = pl.BlockSpec((TM, HALF), lambda i: (i, 0))
    f = jax.ShapeDtypeStruct((N_TOK, HALF), F32)
    h = jax.ShapeDtypeStruct((N_TOK, HALF), BF16)
    return pl.pallas_call(
        _inproj1_kernel,
        grid=(N_TILES,),
        in_specs=_token_specs() + [
                  pl.BlockSpec((1, 1, 6 * D), _mod_row(TM)),
                  pl.BlockSpec((1, D), lambda i: (0, 0)),
                  pl.BlockSpec((D, 5 * HALF), lambda i: (0, 0)),
                  pl.BlockSpec((HALF, D), lambda i: (0, 0))],
        out_specs=[tok] * 4 + [pl.BlockSpec((HALF, TM), lambda i: (0, i))],
        out_shape=[f, f, h, h, jax.ShapeDtypeStruct((HALF, N_TOK), BF16)],
        compiler_params=_params("parallel"),
        name="inproj1",
    )(x, c, mod, g, w[:, :5 * HALF], w[:, 5 * HALF:].T)


def _natten_kernel(q_ref, kp_ref, kc_ref, kn_ref, vp_ref, vc_ref, vn_ref, kx_ref, vx_ref, bias_ref, o_ref):
    pair = 2 * HEAD_DIM
    low = lax.broadcasted_iota(jnp.int32, (TM, pair), 1) < HEAD_DIM
    head_mask = [low.astype(F32).astype(BF16), jnp.logical_not(low).astype(F32).astype(BF16)]
    top = lax.broadcasted_iota(jnp.int32, (pair, TM), 0) < HEAD_DIM
    own_rows = [top.astype(F32).astype(BF16), jnp.logical_not(top).astype(F32).astype(BF16)]
    def scores(head):
        g, hh = divmod(head, 2)
        sl = slice(pair * g, pair * (g + 1))
        qm = q_ref[:, sl] * head_mask[hh]
        s = [_nt_dot(qm, k_ref[:, sl]) for k_ref in (kp_ref, kc_ref, kn_ref, kx_ref)]
        for t in range(3):
            s[t] = s[t] + bias_ref[0, head, :, t * TM:(t + 1) * TM]
        m = jnp.max(jnp.maximum(jnp.maximum(s[0], s[1]), jnp.maximum(s[2], s[3])), axis=-1, keepdims=True)
        return s, m

    def attend(head, s, m):
        g, hh = divmod(head, 2)
        sl = slice(pair * g, pair * (g + 1))
        acc = None
        for st, vt_ref in zip(s, (vp_ref, vc_ref, vn_ref, vx_ref)):
            lhs = vt_ref[sl, :] * own_rows[hh] + own_rows[1 - hh]
            term = _nt_dot(lhs, jnp.exp2(st - m).astype(BF16))
            acc = term if acc is None else acc + term
        if hh == 0:
            return acc[0:HEAD_DIM] * (1.0 / acc[HEAD_DIM:HEAD_DIM + 1])
        return acc[HEAD_DIM:pair] * (1.0 / acc[0:1])

    pending = scores(0)
    halves = []
    for head in range(N_HEADS):
        current = pending
        if head + 1 < N_HEADS:
            pending = scores(head + 1)
        halves.append(attend(head, *current))
        if head % 2 == 1:
            sl = slice(pair * (head // 2), pair * (head // 2 + 1))
            o_ref[:, sl] = jnp.concatenate(halves, axis=0).T.astype(o_ref.dtype)
            halves = []


def _natten_bias(rpb):
    n_rows = SEQ // GRID_W
    n_dr, n_dc = 2 * WIN_H - 1, 2 * WIN_W - 1
    i = np.arange(ROWS_Q)
    j = np.arange(3 * ROWS_Q)
    col = np.arange(GRID_W)
    col_start = np.clip(col - WIN_W // 2, 0, GRID_W - WIN_W)
    col_ok = (col[None, :] >= col_start[:, None]) & (col[None, :] < col_start[:, None] + WIN_W)
    col_idx = col[None, :] - col[:, None] + (WIN_W - 1)
    onehot = ((col_idx[None] == np.arange(n_dc)[:, None, None]) & col_ok[None]).astype(np.float32)
    col_exp = jnp.dot(rpb.reshape(N_HEADS * n_dr, n_dc).astype(F32), onehot.reshape(n_dc, GRID_W * GRID_W),
                      precision=lax.Precision.HIGHEST).reshape(N_HEADS, n_dr, GRID_W, GRID_W)
    col_exp = jnp.where(col_ok[None, None], col_exp * LOG2E, NEG)
    masked = jnp.full((N_HEADS, GRID_W, GRID_W), NEG, F32)
    kinds = []
    for r0 in (0, ROWS_Q, n_rows - ROWS_Q):
        r = r0 + i
        kr = r0 - ROWS_Q + j
        r_start = np.clip(r - WIN_H // 2, 0, n_rows - WIN_H)
        row_ok = ((kr[None, :] >= r_start[:, None]) & (kr[None, :] < r_start[:, None] + WIN_H)
                  & (kr[None, :] >= 0) & (kr[None, :] < n_rows))
        row_idx = kr[None, :] - r[:, None] + (WIN_H - 1)
        rows = [jnp.concatenate([col_exp[:, row_idx[qi, kj]] if row_ok[qi, kj] else masked for kj in range(3 * ROWS_Q)],
                                axis=-1) for qi in range(ROWS_Q)]
        kinds.append(jnp.concatenate(rows, axis=1))
    return jnp.stack(kinds)


def _natten(q, k, vt, bias):
    def tile(b, i, off):
        return b * TILES_PER_SEQ + jnp.clip(i + off, 0, TILES_PER_SEQ - 1)

    def lat(off):
        return pl.BlockSpec((TM, HALF), lambda b, i: (tile(b, i, off), 0))

    def lat_t(off):
        return pl.BlockSpec((HALF, TM), lambda b, i: (0, tile(b, i, off)))

    ctx = pl.BlockSpec((TM, HALF), lambda b, i: (N_LAT_TILES + b, 0))
    ctx_t = pl.BlockSpec((HALF, TM), lambda b, i: (0, N_LAT_TILES + b))

    def kind(b, i):
        return (jnp.where(i == 0, 0, jnp.where(i == TILES_PER_SEQ - 1, 2, 1)), 0, 0, 0)

    return pl.pallas_call(
        _natten_kernel,
        grid=(BATCH, TILES_PER_SEQ),
        in_specs=[lat(0), lat(-1), lat(0), lat(1), lat_t(-1), lat_t(0), lat_t(1), ctx, ctx_t,
                  pl.BlockSpec((1, N_HEADS, TM, 3 * TM), kind)],
        out_specs=lat(0),
        out_shape=jax.ShapeDtypeStruct((N_LAT, HALF), BF16),
        compiler_params=_params("parallel", "arbitrary"),
        name="natten",
    )(q, k, k, k, vt, vt, vt, k, vt, bias)


def kernel(x, c, ctx, c_ctx, ada_w, ada_b, norm_mix_g, norm_ffn_g, w_out, ab_w_in, a_dw_w, a_dw_b, a_ln_g, a_ln_b,
           b_conv_w, b_conv_b, b_gate_w, b_gate_b, b_lambda, cd_w_in, c_conv_w, d_rpb, router_w, router_bias,
           moe_w_gate, moe_w_up, moe_w_down, final_g):
    x_lat = x.reshape(N_LAT, D)
    x_ctx = ctx.reshape(BATCH * CTX, D)
    cond = jnp.concatenate([c, c_ctx[None], jnp.zeros((8 - BATCH - 1, D), F32)], axis=0)
    mod = _modulation(cond, ada_w, ada_b)
    mod0 = mod[0].reshape(8, 1, 6 * D)
    mod1 = mod[1].reshape(8, 1, 6 * D)

    wo = w_out.astype(BF16)
    lat_tiles = N_LAT // TMOE
    ctx_tiles = BATCH * CTX // TMOE
    rw_t = router_w.T
    rb = router_bias.reshape(N_EXPERTS, 1)
    fg = final_g.reshape(1, D)

    ua, ub, gg = _inproj0(x_lat, x_ctx, mod0, norm_mix_g[0].reshape(1, D), ab_w_in[0].astype(BF16))
    a_out, v = _conv0(ua, ub, a_dw_w[0], a_dw_b[0], a_ln_g[0], a_ln_b[0], b_conv_w[0], b_conv_b[0])
    gw = b_gate_w[0]
    gate_w = (0.5 * jnp.concatenate([gw[:, 0], gw[:, 1]], axis=-1)).astype(BF16)
    yf, yb, wg, wu, wd = _scan(v, gate_w, 0.5 * b_gate_b[0], b_lambda[0].reshape(2, 1, HALF),
                               moe_w_gate, moe_w_up, moe_w_down)
    h1, n2, route0, cnt0 = _outproj0(a_out, yf, yb, gg, x_lat, x_ctx, mod0, wo[0], norm_ffn_g[0].reshape(1, D), rw_t, rb)
    moe0 = functools.partial(_moe, _moe_chunks(cnt0), n2, route0, route0.T, wg, wu, wd, 0, h1, mod0, fg,
                             final_norm=False)
    h2_lat = moe0(first_tile=0, n_tiles=lat_tiles, subtiles=2, name="moe_lat")
    h2_ctx = moe0(first_tile=lat_tiles, n_tiles=ctx_tiles, subtiles=1, name="moe_ctx")

    cx, bg, q, k, vv = _inproj1(h2_lat, h2_ctx, mod1, norm_mix_g[1].reshape(1, D), cd_w_in[0].astype(BF16))
    att = _natten(q, k, vv, _natten_bias(d_rpb[0]))
    h3, n4, route1, cnt1 = _outproj1(cx, bg, att, c_conv_w[0], h2_lat, mod1, wo[1], norm_ffn_g[1].reshape(1, D),
                                     rw_t, rb)
    out = _moe(_moe_chunks(cnt1), n4, route1, route1.T, wg, wu, wd, 1, h3, mod1, fg,
               first_tile=0, n_tiles=lat_tiles, subtiles=2, final_norm=True, name="moe_final")
    return out.reshape(BATCH, SEQ, D)
```

```python
import functools
import math

import jax
import jax.numpy as jnp
import numpy as np
from jax import lax
from jax.experimental import pallas as pl
from jax.experimental.pallas import tpu as pltpu

F32 = jnp.float32
BF16 = jnp.bfloat16

D = 1024
BATCH = 2
SEQ = 8192
CTX = 256
GRID_W = 64
N_LAT = BATCH * SEQ
N_TOK = N_LAT + BATCH * CTX
HALF = 512
CONV_A = 31
CONV_B = 4
CONV_C = 3
LRU_BLOCK = 128
N_LRU_BLOCKS = HALF // LRU_BLOCK
LRU_C = 8.0
HEAD_DIM = 64
N_HEADS = HALF // HEAD_DIM
WIN_H = 8
WIN_W = 16
N_EXPERTS = 16
N_GROUPS = 4
PER_GROUP = N_EXPERTS // N_GROUPS
D_FF = 512
EPS = 1e-6
NEG = -1e30
LOG2E = math.log2(math.e)

TM = 256
TILES_PER_SEQ = SEQ // TM
N_LAT_TILES = N_LAT // TM
N_TILES = N_TOK // TM
TMOE = 512
CAP = 128
ROUTE_ROWS = 24
ROUTE_GROUP = 16
ROUTE_RANK = 17
HALO_A = 16
HALO_S = 8
ROWS_Q = TM // GRID_W
VMEM_LIMIT = 48 * 1024 * 1024
VMEM_LIMIT_MOE = 56 * 1024 * 1024


def _params(*sem, vmem=VMEM_LIMIT):
    return pltpu.CompilerParams(dimension_semantics=sem, vmem_limit_bytes=vmem)


def _sigmoid(x):
    return 0.5 * jnp.tanh(0.5 * x) + 0.5


def _silu(x):
    return x * _sigmoid(x)


def _gelu_tanh(x):
    return 0.5 * x * (1.0 + jnp.tanh(0.7978845608028654 * (x + 0.044715 * (x * x * x))))


def _rms_mod(x, g, scale, shift):
    y = x * lax.rsqrt(jnp.mean(x * x, axis=-1, keepdims=True) + EPS) * g
    return y * (1.0 + scale) + shift


def _nt_dot(a, b):
    return lax.dot_general(a, b, (((1,), (1,)), ((), ())), preferred_element_type=F32)


def _held(hold):
    return (lambda i: i) if hold is None else (lambda i: jnp.minimum(i, hold))


def _mod_row(tile_rows, hold=None):
    per_seq = SEQ // tile_rows
    tile = _held(hold)
    return lambda i: (jnp.minimum(tile(i) // per_seq, BATCH), 0, 0)


def _mod_kernel(c_ref, w_ref, b_ref, o_ref):
    c = c_ref[...]
    s = _silu(c).astype(BF16)
    o_ref[0] = jnp.dot(s, w_ref[0].astype(BF16), preferred_element_type=F32) + b_ref[0]


def _modulation(cond, ada_w, ada_b):
    depth = ada_w.shape[0]
    nb = 1536
    return pl.pallas_call(
        _mod_kernel,
        grid=(depth, 6 * D // nb),
        in_specs=[pl.BlockSpec((8, D), lambda l, j: (0, 0)),
                  pl.BlockSpec((1, D, nb), lambda l, j: (l, 0, j)),
                  pl.BlockSpec((1, 1, nb), lambda l, j: (l, 0, j))],
        out_specs=pl.BlockSpec((1, 8, nb), lambda l, j: (l, 0, j)),
        out_shape=jax.ShapeDtypeStruct((depth, 8, 6 * D), F32),
        compiler_params=_params("parallel", "parallel"),
        name="modulation",
    )(cond, ada_w, ada_b.reshape(depth, 1, 6 * D))


def _token_specs(hold=None):
    tile = _held(hold)
    lat = pl.BlockSpec((TM, D), lambda i: (jnp.minimum(tile(i), N_LAT_TILES - 1), 0))
    ctx = pl.BlockSpec((TM, D), lambda i: (jnp.maximum(tile(i) - N_LAT_TILES, 0), 0))
    return [lat, ctx]


def _token_tile(lat_ref, ctx_ref, tile=None):
    tile = pl.program_id(0) if tile is None else tile
    return jnp.where(tile < N_LAT_TILES, lat_ref[...], ctx_ref[...])


def _inproj0_kernel(x_ref, c_ref, mod_ref, g_ref, w_ref, ua_ref, ub_ref, gg_ref):
    mod = mod_ref[0]
    n = _rms_mod(_token_tile(x_ref, c_ref), g_ref[...], mod[:, D:2 * D], mod[:, 0:D]).astype(BF16)

    def proj(c):
        return jnp.dot(n, w_ref[:, c * HALF:(c + 1) * HALF], preferred_element_type=F32)

    value, gate = proj(0), proj(1)
    recur = proj(2)
    ua_ref[...] = value * _sigmoid(gate)
    gelu_in = proj(3)
    ub_ref[...] = recur
    gg_ref[...] = _gelu_tanh(gelu_in).astype(gg_ref.dtype)


def _inproj0(x, c, mod, g, w):
    tok = pl.BlockSpec((TM, HALF), lambda i: (i, 0))
    shp = jax.ShapeDtypeStruct((N_TOK, HALF), F32)
    return pl.pallas_call(
        _inproj0_kernel,
        grid=(N_TILES,),
        in_specs=_token_specs() + [
                  pl.BlockSpec((1, 1, 6 * D), _mod_row(TM)),
                  pl.BlockSpec((1, D), lambda i: (0, 0)),
                  pl.BlockSpec((D, 4 * HALF), lambda i: (0, 0))],
        out_specs=[tok, tok, tok],
        out_shape=[shp, shp, jax.ShapeDtypeStruct((N_TOK, HALF), BF16)],
        compiler_params=_params("parallel"),
        name="inproj0",
    )(x, c, mod, g, w)


def _halo_specs(halo, hold=None):
    per_tile = TM // halo
    last = N_TOK // halo - 1
    tile = _held(hold)
    prev = pl.BlockSpec((halo, HALF), lambda i: (jnp.maximum(tile(i) * per_tile - 1, 0), 0))
    nxt = pl.BlockSpec((halo, HALF), lambda i: (jnp.minimum((tile(i) + 1) * per_tile, last), 0))
    return prev, nxt


def _seq_edges(i):
    is_ctx = i >= N_LAT_TILES
    first = jnp.logical_or(is_ctx, i % TILES_PER_SEQ == 0)
    last = jnp.logical_or(is_ctx, i % TILES_PER_SEQ == TILES_PER_SEQ - 1)
    return first, last


def _fill_padded(buf_ref, prev_ref, cur_ref, next_ref, halo, tile=None):
    first, last = _seq_edges(pl.program_id(0) if tile is None else tile)
    buf_ref[0:halo, :] = jnp.where(first, 0.0, prev_ref[...])
    buf_ref[halo:halo + TM, :] = cur_ref[...]
    buf_ref[halo + TM:halo + TM + halo, :] = jnp.where(last, 0.0, next_ref[...])


def _tap_weight(w_ref, k, rows):
    return jnp.concatenate([w_ref[k]] * (rows // 8), axis=0)


def _sublane_replicated(w):
    return jnp.broadcast_to(w[:, None, :], (w.shape[0], 8, w.shape[1]))


def _depthwise(buf_ref, w_ref, taps, first_off, rows, row0):
    acc = None
    for k in range(taps):
        term = _tap_weight(w_ref, k, rows) * buf_ref[row0 + first_off + k:row0 + first_off + k + rows, :]
        acc = term if acc is None else acc + term
    return acc


CONV_ROWS = 32


SHIFT_ROWS = TM + 2 * HALO_A - 8


def _conv0_kernel(ua_ref, uap_ref, uan_ref, ub_ref, ubp_ref, ubn_ref,
                  dww_ref, dwb_ref, lng_ref, lnb_ref, cw_ref, cb_ref,
                  a_ref, v_ref, bufa_ref, bufb_ref, shift_ref):
    _fill_padded(bufa_ref, uap_ref, ua_ref, uan_ref, HALO_A)
    _fill_padded(bufb_ref, ubp_ref, ub_ref, ubn_ref, HALO_S)
    for s in range(1, 8):
        shift_ref[s - 1] = bufa_ref[s:s + SHIFT_ROWS, :]
    first = HALO_A - CONV_A // 2
    for r in range(TM // CONV_ROWS):
        row0 = r * CONV_ROWS
        u = None
        for k in range(CONV_A):
            off = first + k
            base = row0 + off - off % 8
            rows = bufa_ref[base:base + CONV_ROWS, :] if off % 8 == 0 else shift_ref[off % 8 - 1, base:base + CONV_ROWS, :]
            term = _tap_weight(dww_ref, k, CONV_ROWS) * rows
            u = term if u is None else u + term
        u = u + dwb_ref[...]
        mu = jnp.mean(u, axis=-1, keepdims=True)
        uc = u - mu
        var = jnp.mean(uc * uc, axis=-1, keepdims=True)
        y = uc * lax.rsqrt(var + EPS) * lng_ref[...] + lnb_ref[...]
        a_ref[row0:row0 + CONV_ROWS, :] = _silu(y).astype(a_ref.dtype)
        v = _depthwise(bufb_ref, cw_ref, CONV_B, HALO_S - 2, CONV_ROWS, row0) + cb_ref[...]
        v_ref[0, row0:row0 + CONV_ROWS, :] = v


def _seq_major(i, hold=None):
    i = _held(hold)(i)
    is_lat = i < N_LAT_TILES
    return (jnp.where(is_lat, i // TILES_PER_SEQ, i - N_LAT_TILES), jnp.where(is_lat, 1 + i % TILES_PER_SEQ, 0), 0)


def _conv0(ua, ub, dw_w, dw_b, ln_g, ln_b, conv_w, conv_b):
    tok = pl.BlockSpec((TM, HALF), lambda i: (i, 0))
    pa, na = _halo_specs(HALO_A)
    ps, ns = _halo_specs(HALO_S)
    vec = pl.BlockSpec((1, HALF), lambda i: (0, 0))
    return pl.pallas_call(
        _conv0_kernel,
        grid=(N_TILES,),
        in_specs=[tok, pa, na, tok, ps, ns,
                  pl.BlockSpec((CONV_A, 8, HALF), lambda i: (0, 0, 0)), vec, vec, vec,
                  pl.BlockSpec((CONV_B, 8, HALF), lambda i: (0, 0, 0)), vec],
        out_specs=[tok, pl.BlockSpec((1, TM, HALF), _seq_major)],
        out_shape=[jax.ShapeDtypeStruct((N_TOK, HALF), BF16),
                   jax.ShapeDtypeStruct((BATCH, SEQ + CTX, HALF), F32)],
        scratch_shapes=[pltpu.VMEM((TM + 2 * HALO_A, HALF), F32), pltpu.VMEM((TM + 2 * HALO_S, HALF), F32),
                        pltpu.VMEM((7, SHIFT_ROWS, HALF), F32)],
        compiler_params=_params("parallel"),
        name="conv0",
    )(ua, ua, ua, ub, ub, ub, _sublane_replicated(dw_w), dw_b.reshape(1, HALF), ln_g.reshape(1, HALF),
      ln_b.reshape(1, HALF), _sublane_replicated(conv_w), conv_b.reshape(1, HALF))


SCAN_UNROLL = 8


def _scan_kernel(vf_ref, vb_ref, w_ref, gb_ref, lam_ref, wg_ref, wu_ref, wd_ref,
                 yf_ref, yb_ref, wg_out, wu_out, wd_out, h_ref, a_ref, b_ref):
    wg_out[...] = wg_ref[...].astype(BF16)
    wu_out[...] = wu_ref[...].astype(BF16)
    wd_out[...] = wd_ref[...].astype(BF16)

    @pl.when(pl.program_id(0) == 0)
    def _():
        h_ref[...] = jnp.zeros_like(h_ref)

    for d, v_ref in enumerate((vf_ref, vb_ref)):
        v = v_ref[...].reshape(BATCH * TM, HALF)
        vb = v.astype(BF16)
        neg = -lam_ref[d]
        softplus = jnp.maximum(neg, 0.0) + jnp.log(1.0 + jnp.exp(-jnp.abs(neg)))
        rate = (-0.5 * LRU_C * LOG2E) * softplus
        for n in range(N_LRU_BLOCKS):
            sl = slice(n * LRU_BLOCK, (n + 1) * LRU_BLOCK)
            g = jnp.dot(vb[:, sl], w_ref[d, n], preferred_element_type=F32)
            tr = jnp.tanh(g[:, 0:LRU_BLOCK] + gb_ref[d, 0:1, sl])
            ti = jnp.tanh(g[:, LRU_BLOCK:2 * LRU_BLOCK] + gb_ref[d, 1:2, sl])
            a = jnp.exp2(rate[:, sl] * tr + rate[:, sl])
            a_ref[d, :, sl] = a
            b_ref[d, :, sl] = jnp.sqrt(1.0 - a * a) * ((0.5 * ti + 0.5) * v[:, sl])

    def body(s, hs):
        hs = list(hs)
        for u in range(SCAN_UNROLL):
            t = s * SCAN_UNROLL + u
            for d, y_ref in enumerate((yf_ref, yb_ref)):
                row = t if d == 0 else TM - 1 - t
                for bt in range(BATCH):
                    c = d * BATCH + bt
                    src = bt * TM + row
                    hs[c] = a_ref[d, pl.ds(src, 1), :] * hs[c] + b_ref[d, pl.ds(src, 1), :]
                    y_ref[bt, pl.ds(row, 1), :] = hs[c]
        return tuple(hs)

    init = tuple(h_ref[c:c + 1, :] for c in range(2 * BATCH))
    final = lax.fori_loop(0, TM // SCAN_UNROLL, body, init)
    for c in range(2 * BATCH):
        h_ref[c:c + 1, :] = final[c]


def _scan(v, gate_w, gate_b, lam, moe_wg, moe_wu, moe_wd):
    steps = TILES_PER_SEQ + 1
    depth = moe_wg.shape[0]
    assert depth * N_EXPERTS <= steps
    fwd = pl.BlockSpec((BATCH, TM, HALF), lambda j: (0, j, 0))
    bwd = pl.BlockSpec((BATCH, TM, HALF), lambda j: (0, jnp.where(j == 0, 0, TILES_PER_SEQ + 1 - j), 0))
    shp = jax.ShapeDtypeStruct((BATCH, SEQ + CTX, HALF), F32)

    def expert(j):
        s = jnp.minimum(j, depth * N_EXPERTS - 1)
        return (s // N_EXPERTS, s % N_EXPERTS, 0, 0)

    up = pl.BlockSpec((1, 1, D, D_FF), expert)
    down = pl.BlockSpec((1, 1, D_FF, D), expert)
    return pl.pallas_call(
        _scan_kernel,
        grid=(steps,),
        in_specs=[fwd, bwd,
                  pl.BlockSpec((2, N_LRU_BLOCKS, LRU_BLOCK, 2 * LRU_BLOCK), lambda j: (0, 0, 0, 0)),
                  pl.BlockSpec((2, 2, HALF), lambda j: (0, 0, 0)),
                  pl.BlockSpec((2, 1, HALF), lambda j: (0, 0, 0)),
                  up, up, down],
        out_specs=[fwd, bwd, up, up, down],
        out_shape=[shp, shp, jax.ShapeDtypeStruct(moe_wg.shape, BF16), jax.ShapeDtypeStruct(moe_wu.shape, BF16),
                   jax.ShapeDtypeStruct(moe_wd.shape, BF16)],
        scratch_shapes=[pltpu.VMEM((8, HALF), F32), pltpu.VMEM((2, BATCH * TM, HALF), F32),
                        pltpu.VMEM((2, BATCH * TM, HALF), F32)],
        compiler_params=_params("arbitrary"),
        name="lru_scan",
    )(v, v, gate_w, gate_b, lam, moe_wg, moe_wu, moe_wd)


def _route(score, sel):
    rows = [sel[e:e + 1, :] for e in range(N_EXPERTS)]
    gbest = None
    gidx = None
    for g in range(N_GROUPS):
        top2 = None
        for p in range(PER_GROUP):
            for q in range(p + 1, PER_GROUP):
                s = rows[g * PER_GROUP + p] + rows[g * PER_GROUP + q]
                top2 = s if top2 is None else jnp.maximum(top2, s)
        if g == 0:
            gbest = top2
            gidx = jnp.zeros(top2.shape, jnp.int32)
        else:
            better = top2 > gbest
            gidx = jnp.where(better, g, gidx)
            gbest = jnp.where(better, top2, gbest)
    eint = lax.broadcasted_iota(jnp.int32, sel.shape, 0)
    eidx = eint.astype(F32)
    masked = jnp.where(jnp.right_shift(eint, 2) == gidx, sel, -jnp.inf)
    v1 = jnp.max(masked, axis=0, keepdims=True)
    i1 = jnp.min(jnp.where(masked == v1, eidx, float(N_EXPERTS)), axis=0, keepdims=True)
    masked2 = jnp.where(eidx == i1, -jnp.inf, masked)
    v2 = jnp.max(masked2, axis=0, keepdims=True)
    i2 = jnp.min(jnp.where(masked2 == v2, eidx, float(N_EXPERTS)), axis=0, keepdims=True)
    s1 = jnp.sum(jnp.where(eidx == i1, score, 0.0), axis=0, keepdims=True)
    s2 = jnp.sum(jnp.where(eidx == i2, score, 0.0), axis=0, keepdims=True)
    inv = 1.0 / (s1 + s2)
    return jnp.where(eidx == i1, s1 * inv, 0.0) + jnp.where(eidx == i2, s2 * inv, 0.0), gidx


def _group_ranks(gidx, carry_ref, tile):
    onehot = lax.broadcasted_iota(jnp.int32, (8, TM), 0) == gidx
    oh = jnp.where(onehot, 1.0, 0.0)
    before = lax.broadcasted_iota(jnp.int32, (TM, TM), 0) < lax.broadcasted_iota(jnp.int32, (TM, TM), 1)
    prefix = jnp.dot(oh.astype(BF16), jnp.where(before, 1.0, 0.0).astype(BF16), preferred_element_type=F32)
    carry = jnp.where(tile % (TMOE // TM) == 0, 0.0, carry_ref[...])
    rank = jnp.sum(jnp.where(onehot, prefix + carry[:, 0:1], 0.0), axis=0, keepdims=True)
    carry = carry + jnp.sum(oh, axis=1, keepdims=True)
    carry_ref[...] = carry
    return rank, carry


def _split_bf16(v):
    hi = v.astype(BF16)
    return hi, (v - hi.astype(F32)).astype(BF16)


def _outproj_step(m1, m2, x, mod_ref, wo_ref, g_ref, rw_ref, rb_ref, h_ref, n_ref, route_ref, cnt_ref, carry_ref,
                  nprev_ref):
    step = pl.program_id(0)
    n_hi, n_lo = _split_bf16(nprev_ref[...])
    w_hi, w_lo = _split_bf16(rw_ref[...])

    mod = mod_ref[0]
    mix = (jnp.dot(m1, wo_ref[0:HALF, :], preferred_element_type=F32)
           + jnp.dot(m2, wo_ref[HALF:2 * HALF, :], preferred_element_type=F32))

    logits = _nt_dot(w_hi, n_hi) + _nt_dot(w_hi, n_lo) + _nt_dot(w_lo, n_hi)

    h = x + mod[:, 2 * D:3 * D] * mix
    h_ref[...] = h
    n = _rms_mod(h, g_ref[...], mod[:, 4 * D:5 * D], mod[:, 3 * D:4 * D])
    n_ref[...] = n.astype(n_ref.dtype)
    nprev_ref[...] = n

    score = _sigmoid(logits)
    comb, gidx = _route(score, score + rb_ref[...])
    rank, counts = _group_ranks(gidx, carry_ref, step - 1)
    route_ref[0:N_EXPERTS, :] = comb
    route_ref[ROUTE_GROUP:ROUTE_GROUP + 1, :] = gidx.astype(F32)
    route_ref[ROUTE_RANK:ROUTE_RANK + 1, :] = rank
    route_ref[ROUTE_RANK + 1:ROUTE_ROWS, :] = jnp.zeros((ROUTE_ROWS - ROUTE_RANK - 1, TM), F32)
    cnt_ref[0] = counts


def _init_pipeline(carry_ref, nprev_ref):
    @pl.when(pl.program_id(0) == 0)
    def _():
        nprev_ref[...] = jnp.zeros_like(nprev_ref)
        carry_ref[...] = jnp.zeros_like(carry_ref)


def _outproj0_kernel(a_ref, yf_ref, yb_ref, gg_ref, x_ref, c_ref, mod_ref, wo_ref, g_ref, rw_ref, rb_ref,
                     h_ref, n_ref, route_ref, cnt_ref, carry_ref, nprev_ref):
    _init_pipeline(carry_ref, nprev_ref)
    tile = jnp.minimum(pl.program_id(0), N_TILES - 1)
    m2 = ((yf_ref[0] + yb_ref[0]) * gg_ref[...]).astype(BF16)
    _outproj_step(a_ref[...], m2, _token_tile(x_ref, c_ref, tile), mod_ref, wo_ref, g_ref, rw_ref, rb_ref,
                  h_ref, n_ref, route_ref, cnt_ref, carry_ref, nprev_ref)


def _outproj1_kernel(cx_ref, cxp_ref, cxn_ref, bg_ref, att_ref, cw_ref, x_ref, mod_ref, wo_ref, g_ref, rw_ref, rb_ref,
                     h_ref, n_ref, route_ref, cnt_ref, buf_ref, carry_ref, nprev_ref):
    _init_pipeline(carry_ref, nprev_ref)
    tile = jnp.minimum(pl.program_id(0), N_LAT_TILES - 1)
    _fill_padded(buf_ref, cxp_ref, cx_ref, cxn_ref, HALO_S, tile)
    conv = _depthwise(buf_ref, cw_ref, CONV_C, HALO_S - 1, TM, 0)
    m1 = (bg_ref[...] * conv).astype(BF16)
    _outproj_step(m1, att_ref[...], x_ref[...], mod_ref, wo_ref, g_ref, rw_ref, rb_ref,
                  h_ref, n_ref, route_ref, cnt_ref, carry_ref, nprev_ref)


def _outproj_common(n_tiles):
    hold = n_tiles - 1
    tile = _held(hold)
    routed = lambda i: jnp.maximum(i - 1, 0)
    in_specs = [pl.BlockSpec((1, 1, 6 * D), _mod_row(TM, hold)),
                pl.BlockSpec((D, D), lambda i: (0, 0)),
                pl.BlockSpec((1, D), lambda i: (0, 0)),
                pl.BlockSpec((N_EXPERTS, D), lambda i: (0, 0)),
                pl.BlockSpec((N_EXPERTS, 1), lambda i: (0, 0))]
    out_specs = [pl.BlockSpec((TM, D), lambda i: (tile(i), 0)),
                 pl.BlockSpec((TM, D), lambda i: (tile(i), 0)),
                 pl.BlockSpec((ROUTE_ROWS, TM), lambda i: (0, routed(i))),
                 pl.BlockSpec((1, 8, 128), lambda i: (routed(i), 0, 0))]
    rows = n_tiles * TM
    out_shape = [jax.ShapeDtypeStruct((rows, D), F32), jax.ShapeDtypeStruct((rows, D), BF16),
                 jax.ShapeDtypeStruct((ROUTE_ROWS, rows), F32), jax.ShapeDtypeStruct((n_tiles, 8, 128), F32)]
    scratch = [pltpu.VMEM((8, 128), F32), pltpu.VMEM((TM, D), F32)]
    return in_specs, out_specs, out_shape, scratch


def _outproj0(a, yf, yb, gg, x, c, mod, wo, g, rw_t, rb):
    hold = N_TILES - 1
    tok = pl.BlockSpec((TM, HALF), lambda i: (_held(hold)(i), 0))
    scan_tok = pl.BlockSpec((1, TM, HALF), functools.partial(_seq_major, hold=hold))
    common_in, out_specs, out_shape, scratch = _outproj_common(N_TILES)
    return pl.pallas_call(
        _outproj0_kernel,
        grid=(N_TILES + 1,),
        in_specs=[tok, scan_tok, scan_tok, tok] + _token_specs(hold) + common_in,
        out_specs=out_specs,
        out_shape=out_shape,
        scratch_shapes=scratch,
        compiler_params=_params("arbitrary"),
        name="outproj0",
    )(a, yf, yb, gg, x, c, mod, wo, g, rw_t, rb)


def _outproj1(cx, bg, att, conv_w, x, mod, wo, g, rw_t, rb):
    hold = N_LAT_TILES - 1
    tok = pl.BlockSpec((TM, HALF), lambda i: (_held(hold)(i), 0))
    ps, ns = _halo_specs(HALO_S, hold)
    common_in, out_specs, out_shape, scratch = _outproj_common(N_LAT_TILES)
    return pl.pallas_call(
        _outproj1_kernel,
        grid=(N_LAT_TILES + 1,),
        in_specs=[tok, ps, ns, tok, tok, pl.BlockSpec((CONV_C, 8, HALF), lambda i: (0, 0, 0)),
                  pl.BlockSpec((TM, D), lambda i: (_held(hold)(i), 0))] + common_in,
        out_specs=out_specs,
        out_shape=out_shape,
        scratch_shapes=[pltpu.VMEM((TM + 2 * HALO_S, HALF), F32)] + scratch,
        compiler_params=_params("arbitrary"),
        name="outproj1",
    )(cx, cx, cx, bg, att, _sublane_replicated(conv_w), x, mod, wo, g, rw_t, rb)


def _moe_kernel(cnt_ref, n_ref, rt_ref, r_ref, wg_ref, wu_ref, wd_ref, h_ref, mod_ref, fg_ref, o_ref,
                hid_ref, *, final_norm, first_tile, subtiles):
    i = pl.program_id(0)
    g = pl.program_id(1)

    @pl.when(g == 0)
    def _():
        o_ref[...] = jnp.zeros_like(o_ref)

    gf = g.astype(F32)
    slot_row = lax.broadcasted_iota(jnp.int32, (CAP, TMOE), 0).astype(F32)
    slot_col = lax.broadcasted_iota(jnp.int32, (TMOE, CAP), 1).astype(F32)

    counts = [cnt_ref[(first_tile + i * subtiles + s) * N_GROUPS + g] for s in range(subtiles)]
    starts = [sum(counts[:s], jnp.int32(0)) for s in range(subtiles)]
    total = starts[-1] + counts[-1]
    tile_rows = [slice(s * TMOE, (s + 1) * TMOE) for s in range(subtiles)]

    def run_chunk(k, sources):
        base = k * CAP
        sels, shifts = [], []
        xg = None
        for s in sources:
            shift = (starts[s] - base).astype(F32)
            in_group = rt_ref[ROUTE_GROUP:ROUTE_GROUP + 1, tile_rows[s]] == gf
            slot = jnp.where(in_group, rt_ref[ROUTE_RANK:ROUTE_RANK + 1, tile_rows[s]] + shift, -1.0)
            sel = slot_row == slot
            part = jnp.dot(jnp.where(sel, 1.0, 0.0).astype(BF16), n_ref[tile_rows[s], :], preferred_element_type=F32)
            xg = part if xg is None else xg + part
            sels.append(sel)
            shifts.append(shift)
        xg = xg.astype(BF16)
        for j in range(PER_GROUP):
            cw = None
            for s, sel in zip(sources, sels):
                comb_row = rt_ref[pl.ds(g * PER_GROUP + j, 1), tile_rows[s]]
                part = jnp.sum(jnp.where(sel, comb_row, 0.0), axis=1, keepdims=True)
                cw = part if cw is None else cw + part
            hid = (_silu(jnp.dot(xg, wg_ref[0, j], preferred_element_type=F32))
                   * jnp.dot(xg, wu_ref[0, j], preferred_element_type=F32) * cw)
            hid_ref[:, j * D_FF:(j + 1) * D_FF] = hid.astype(BF16)
        y = jnp.dot(hid_ref[...], wd_ref[0].reshape(PER_GROUP * D_FF, D),
                    preferred_element_type=F32).astype(BF16)
        for s, shift in zip(sources, shifts):
            in_group = r_ref[tile_rows[s], ROUTE_GROUP:ROUTE_GROUP + 1] == gf
            slot = r_ref[tile_rows[s], ROUTE_RANK:ROUTE_RANK + 1] + shift
            back = jnp.where(jnp.logical_and(in_group, slot_col == slot), 1.0, 0.0).astype(BF16)
            o_ref[tile_rows[s], :] += jnp.dot(back, y, preferred_element_type=F32)

    def loop(lo, hi, sources):
        def body(k, carry):
            run_chunk(k, sources)
            return carry
        lax.fori_loop(lo, hi, body, 0)

    for s in range(subtiles):
        end = starts[s] + counts[s]
        first_inside = (starts[s] + (CAP - 1)) // CAP
        if s + 1 < subtiles:
            loop(first_inside, end // CAP, [s])

            @pl.when(end % CAP != 0)
            def _(s=s, end=end):
                run_chunk(end // CAP, [s, s + 1])
        else:
            loop(first_inside, (end + (CAP - 1)) // CAP, [s])

    @pl.when(g == N_GROUPS - 1)
    def _():
        out = h_ref[...] + mod_ref[0][:, 5 * D:6 * D] * o_ref[...]
        if final_norm:
            out = out * lax.rsqrt(jnp.mean(out * out, axis=-1, keepdims=True) + EPS) * fg_ref[...]
        o_ref[...] = out


def _moe_chunks(counts):
    sub = TMOE // TM
    return counts[sub - 1::sub, :N_GROUPS, 0].astype(jnp.int32).reshape(-1)


def _moe(n_chunks, n, route_t, route, wg, wu, wd, layer, h, mod, final_g, *, first_tile, n_tiles, subtiles, final_norm,
         name):
    assert subtiles in (1, 2) and first_tile % subtiles == 0 and n_tiles % subtiles == 0
    step = subtiles * TMOE
    first = first_tile // subtiles
    mod_row = lambda i, g, nch: (jnp.minimum(((first + i) * step) // SEQ, BATCH), 0, 0)
    tok = pl.BlockSpec((step, D), lambda i, g, nch: (first + i, 0))
    grid_spec = pltpu.PrefetchScalarGridSpec(
        num_scalar_prefetch=1,
        grid=(n_tiles // subtiles, N_GROUPS),
        in_specs=[tok,
                  pl.BlockSpec((ROUTE_ROWS, step), lambda i, g, nch: (0, first + i)),
                  pl.BlockSpec((step, ROUTE_ROWS), lambda i, g, nch: (first + i, 0)),
                  pl.BlockSpec((1, PER_GROUP, D, D_FF), lambda i, g, nch: (layer, g, 0, 0)),
                  pl.BlockSpec((1, PER_GROUP, D, D_FF), lambda i, g, nch: (layer, g, 0, 0)),
                  pl.BlockSpec((1, PER_GROUP, D_FF, D), lambda i, g, nch: (layer, g, 0, 0)),
                  tok,
                  pl.BlockSpec((1, 1, 6 * D), mod_row),
                  pl.BlockSpec((1, D), lambda i, g, nch: (0, 0))],
        out_specs=pl.BlockSpec((step, D), lambda i, g, nch: (i, 0)),
        scratch_shapes=[pltpu.VMEM((CAP, PER_GROUP * D_FF), BF16)])
    return pl.pallas_call(
        functools.partial(_moe_kernel, final_norm=final_norm, first_tile=first_tile, subtiles=subtiles),
        grid_spec=grid_spec,
        out_shape=jax.ShapeDtypeStruct((n_tiles * TMOE, D), F32),
        compiler_params=_params("parallel", "arbitrary", vmem=VMEM_LIMIT_MOE),
        name=name,
    )(n_chunks, n, route_t, route, wg, wu, wd, h, mod, final_g)


def _inproj1_kernel(x_ref, c_ref, mod_ref, g_ref, w_ref, wvt_ref, cx_ref, bg_ref, q_ref, k_ref, vt_ref):
    mod = mod_ref[0]
    n = _rms_mod(_token_tile(x_ref, c_ref), g_ref[...], mod[:, D:2 * D], mod[:, 0:D]).astype(BF16)

    def proj(c):
        return jnp.dot(n, w_ref[:, c * HALF:(c + 1) * HALF], preferred_element_type=F32)

    conv_in, conv_gate = proj(0), proj(2)
    out_gate = proj(1)
    cx_ref[...] = conv_gate * conv_in
    q = proj(3)
    bg_ref[...] = out_gate
    k = proj(4)
    q_ref[...] = (q * (HEAD_DIM ** -0.5 * LOG2E)).astype(BF16)
    vt = _nt_dot(wvt_ref[...], n)
    k_ref[...] = k.astype(BF16)
    vt_ref[...] = vt.astype(BF16)


def _inproj1(x, c, mod, g, w):
    tok = pl.BlockSpec((TM, HALF), lambda i: (i, 0))
    f = jax.ShapeDtypeStruct((N_TOK, HALF), F32)
    h = jax.ShapeDtypeStruct((N_TOK, HALF), BF16)
    return pl.pallas_call(
        _inproj1_kernel,
        grid=(N_TILES,),
        in_specs=_token_specs() + [
                  pl.BlockSpec((1, 1, 6 * D), _mod_row(TM)),
                  pl.BlockSpec((1, D), lambda i: (0, 0)),
                  pl.BlockSpec((D, 5 * HALF), lambda i: (0, 0)),
                  pl.BlockSpec((HALF, D), lambda i: (0, 0))],
        out_specs=[tok] * 4 + [pl.BlockSpec((HALF, TM), lambda i: (0, i))],
        out_shape=[f, f, h, h, jax.ShapeDtypeStruct((HALF, N_TOK), BF16)],
        compiler_params=_params("parallel"),
        name="inproj1",
    )(x, c, mod, g, w[:, :5 * HALF], w[:, 5 * HALF:].T)


def _natten_kernel(q_ref, kp_ref, kc_ref, kn_ref, vp_ref, vc_ref, vn_ref, kx_ref, vx_ref, bias_ref, o_ref):
    pair = 2 * HEAD_DIM
    low = lax.broadcasted_iota(jnp.int32, (TM, pair), 1) < HEAD_DIM
    head_mask = [low.astype(F32).astype(BF16), jnp.logical_not(low).astype(F32).astype(BF16)]
    top = lax.broadcasted_iota(jnp.int32, (pair, TM), 0) < HEAD_DIM
    own_rows = [top.astype(F32).astype(BF16), jnp.logical_not(top).astype(F32).astype(BF16)]
    def scores(head):
        g, hh = divmod(head, 2)
        sl = slice(pair * g, pair * (g + 1))
        qm = q_ref[:, sl] * head_mask[hh]
        s = [_nt_dot(qm, k_ref[:, sl]) for k_ref in (kp_ref, kc_ref, kn_ref, kx_ref)]
        for t in range(3):
            s[t] = s[t] + bias_ref[0, head, :, t * TM:(t + 1) * TM]
        m = jnp.max(jnp.maximum(jnp.maximum(s[0], s[1]), jnp.maximum(s[2], s[3])), axis=-1, keepdims=True)
        return s, m

    def attend(head, s, m):
        g, hh = divmod(head, 2)
        sl = slice(pair * g, pair * (g + 1))
        acc = None
        for st, vt_ref in zip(s, (vp_ref, vc_ref, vn_ref, vx_ref)):
            lhs = vt_ref[sl, :] * own_rows[hh] + own_rows[1 - hh]
            term = _nt_dot(lhs, jnp.exp2(st - m).astype(BF16))
            acc = term if acc is None else acc + term
        if hh == 0:
            return acc[0:HEAD_DIM] * (1.0 / acc[HEAD_DIM:HEAD_DIM + 1])
        return acc[HEAD_DIM:pair] * (1.0 / acc[0:1])

    pending = scores(0)
    halves = []
    for head in range(N_HEADS):
        current = pending
        if head + 1 < N_HEADS:
            pending = scores(head + 1)
        halves.append(attend(head, *current))
        if head % 2 == 1:
            sl = slice(pair * (head // 2), pair * (head // 2 + 1))
            o_ref[:, sl] = jnp.concatenate(halves, axis=0).T.astype(o_ref.dtype)
            halves = []


def _natten_bias(rpb):
    n_rows = SEQ // GRID_W
    n_dr, n_dc = 2 * WIN_H - 1, 2 * WIN_W - 1
    i = np.arange(ROWS_Q)
    j = np.arange(3 * ROWS_Q)
    col = np.arange(GRID_W)
    col_start = np.clip(col - WIN_W // 2, 0, GRID_W - WIN_W)
    col_ok = (col[None, :] >= col_start[:, None]) & (col[None, :] < col_start[:, None] + WIN_W)
    col_idx = col[None, :] - col[:, None] + (WIN_W - 1)
    onehot = ((col_idx[None] == np.arange(n_dc)[:, None, None]) & col_ok[None]).astype(np.float32)
    col_exp = jnp.dot(rpb.reshape(N_HEADS * n_dr, n_dc).astype(F32), onehot.reshape(n_dc, GRID_W * GRID_W),
                      precision=lax.Precision.HIGHEST).reshape(N_HEADS, n_dr, GRID_W, GRID_W)
    col_exp = jnp.where(col_ok[None, None], col_exp * LOG2E, NEG)
    masked = jnp.full((N_HEADS, GRID_W, GRID_W), NEG, F32)
    kinds = []
    for r0 in (0, ROWS_Q, n_rows - ROWS_Q):
        r = r0 + i
        kr = r0 - ROWS_Q + j
        r_start = np.clip(r - WIN_H // 2, 0, n_rows - WIN_H)
        row_ok = ((kr[None, :] >= r_start[:, None]) & (kr[None, :] < r_start[:, None] + WIN_H)
                  & (kr[None, :] >= 0) & (kr[None, :] < n_rows))
        row_idx = kr[None, :] - r[:, None] + (WIN_H - 1)
        rows = [jnp.concatenate([col_exp[:, row_idx[qi, kj]] if row_ok[qi, kj] else masked for kj in range(3 * ROWS_Q)],
                                axis=-1) for qi in range(ROWS_Q)]
        kinds.append(jnp.concatenate(rows, axis=1))
    return jnp.stack(kinds)


def _natten(q, k, vt, bias):
    def tile(b, i, off):
        return b * TILES_PER_SEQ + jnp.clip(i + off, 0, TILES_PER_SEQ - 1)

    def lat(off):
        return pl.BlockSpec((TM, HALF), lambda b, i: (tile(b, i, off), 0))

    def lat_t(off):
        return pl.BlockSpec((HALF, TM), lambda b, i: (0, tile(b, i, off)))

    ctx = pl.BlockSpec((TM, HALF), lambda b, i: (N_LAT_TILES + b, 0))
    ctx_t = pl.BlockSpec((HALF, TM), lambda b, i: (0, N_LAT_TILES + b))

    def kind(b, i):
        return (jnp.where(i == 0, 0, jnp.where(i == TILES_PER_SEQ - 1, 2, 1)), 0, 0, 0)

    return pl.pallas_call(
        _natten_kernel,
        grid=(BATCH, TILES_PER_SEQ),
        in_specs=[lat(0), lat(-1), lat(0), lat(1), lat_t(-1), lat_t(0), lat_t(1), ctx, ctx_t,
                  pl.BlockSpec((1, N_HEADS, TM, 3 * TM), kind)],
        out_specs=lat(0),
        out_shape=jax.ShapeDtypeStruct((N_LAT, HALF), BF16),
        compiler_params=_params("parallel", "arbitrary"),
        name="natten",
    )(q, k, k, k, vt, vt, vt, k, vt, bias)


def kernel(x, c, ctx, c_ctx, ada_w, ada_b, norm_mix_g, norm_ffn_g, w_out, ab_w_in, a_dw_w, a_dw_b, a_ln_g, a_ln_b,
           b_conv_w, b_conv_b, b_gate_w, b_gate_b, b_lambda, cd_w_in, c_conv_w, d_rpb, router_w, router_bias,
           moe_w_gate, moe_w_up, moe_w_down, final_g):
    x_lat = x.reshape(N_LAT, D)
    x_ctx = ctx.reshape(BATCH * CTX, D)
    cond = jnp.concatenate([c, c_ctx[None], jnp.zeros((8 - BATCH - 1, D), F32)], axis=0)
    mod = _modulation(cond, ada_w, ada_b)
    mod0 = mod[0].reshape(8, 1, 6 * D)
    mod1 = mod[1].reshape(8, 1, 6 * D)

    wo = w_out.astype(BF16)
    lat_tiles = N_LAT // TMOE
    ctx_tiles = BATCH * CTX // TMOE
    rw_t = router_w.T
    rb = router_bias.reshape(N_EXPERTS, 1)
    fg = final_g.reshape(1, D)

    ua, ub, gg = _inproj0(x_lat, x_ctx, mod0, norm_mix_g[0].reshape(1, D), ab_w_in[0].astype(BF16))
    a_out, v = _conv0(ua, ub, a_dw_w[0], a_dw_b[0], a_ln_g[0], a_ln_b[0], b_conv_w[0], b_conv_b[0])
    gw = b_gate_w[0]
    gate_w = (0.5 * jnp.concatenate([gw[:, 0], gw[:, 1]], axis=-1)).astype(BF16)
    yf, yb, wg, wu, wd = _scan(v, gate_w, 0.5 * b_gate_b[0], b_lambda[0].reshape(2, 1, HALF),
                               moe_w_gate, moe_w_up, moe_w_down)
    h1, n2, route0, cnt0 = _outproj0(a_out, yf, yb, gg, x_lat, x_ctx, mod0, wo[0], norm_ffn_g[0].reshape(1, D), rw_t, rb)
    moe0 = functools.partial(_moe, _moe_chunks(cnt0), n2, route0, route0.T, wg, wu, wd, 0, h1, mod0, fg,
                             final_norm=False)
    h2_lat = moe0(first_tile=0, n_tiles=lat_tiles, subtiles=2, name="moe_lat")
    h2_ctx = moe0(first_tile=lat_tiles, n_tiles=ctx_tiles, subtiles=1, name="moe_ctx")

    cx, bg, q, k, vv = _inproj1(h2_lat, h2_ctx, mod1, norm_mix_g[1].reshape(1, D), cd_w_in[0].astype(BF16))
    att = _natten(q, k, vv, _natten_bias(d_rpb[0]))
    h3, n4, route1, cnt1 = _outproj1(cx, bg, att, c_conv_w[0], h2_lat, mod1, wo[1], norm_ffn_g[1].reshape(1, D),
                                     rw_t, rb)
    out = _moe(_moe_chunks(cnt1), n4, route1, route1.T, wg, wu, wd, 1, h3, mod1, fg,
               first_tile=0, n_tiles=lat_tiles, subtiles=2, final_norm=True, name="moe_final")
    return out.reshape(BATCH, SEQ, D)
```

```python
import functools
import math

import jax
import jax.numpy as jnp
import numpy as np
from jax import lax
from jax.experimental import pallas as pl
from jax.experimental.pallas import tpu as pltpu

F32 = jnp.float32
BF16 = jnp.bfloat16

D = 1024
BATCH = 2
SEQ = 8192
CTX = 256
GRID_W = 64
N_LAT = BATCH * SEQ
N_TOK = N_LAT + BATCH * CTX
HALF = 512
CONV_A = 31
CONV_B = 4
CONV_C = 3
LRU_BLOCK = 128
N_LRU_BLOCKS = HALF // LRU_BLOCK
LRU_C = 8.0
HEAD_DIM = 64
N_HEADS = HALF // HEAD_DIM
WIN_H = 8
WIN_W = 16
N_EXPERTS = 16
N_GROUPS = 4
PER_GROUP = N_EXPERTS // N_GROUPS
D_FF = 512
EPS = 1e-6
NEG = -1e30
LOG2E = math.log2(math.e)

TM = 256
TILES_PER_SEQ = SEQ // TM
N_LAT_TILES = N_LAT // TM
N_TILES = N_TOK // TM
TMOE = 512
CAP = 128
ROUTE_ROWS = 24
ROUTE_LANES = 128
ROUTE_GROUP = 16
ROUTE_RANK = 17
HALO_A = 16
HALO_S = 8
ROWS_Q = TM // GRID_W
VMEM_LIMIT = 48 * 1024 * 1024
VMEM_LIMIT_MOE = 56 * 1024 * 1024


def _params(*sem, vmem=VMEM_LIMIT):
    return pltpu.CompilerParams(dimension_semantics=sem, vmem_limit_bytes=vmem)


def _sigmoid(x):
    return 0.5 * jnp.tanh(0.5 * x) + 0.5


def _silu(x):
    return x * _sigmoid(x)


def _gelu_tanh(x):
    return 0.5 * x * (1.0 + jnp.tanh(0.7978845608028654 * (x + 0.044715 * (x * x * x))))


def _rms_mod(x, g, scale, shift):
    y = x * lax.rsqrt(jnp.mean(x * x, axis=-1, keepdims=True) + EPS) * g
    return y * (1.0 + scale) + shift


def _nt_dot(a, b):
    return lax.dot_general(a, b, (((1,), (1,)), ((), ())), preferred_element_type=F32)


def _held(hold):
    return (lambda i: i) if hold is None else (lambda i: jnp.minimum(i, hold))


def _mod_row(tile_rows, hold=None):
    per_seq = SEQ // tile_rows
    tile = _held(hold)
    return lambda i: (jnp.minimum(tile(i) // per_seq, BATCH), 0, 0)


def _mod_kernel(c_ref, w_ref, b_ref, o_ref):
    c = c_ref[...]
    s = _silu(c).astype(BF16)
    o_ref[0] = jnp.dot(s, w_ref[0].astype(BF16), preferred_element_type=F32) + b_ref[0]


def _modulation(cond, ada_w, ada_b):
    depth = ada_w.shape[0]
    nb = 1536
    return pl.pallas_call(
        _mod_kernel,
        grid=(depth, 6 * D // nb),
        in_specs=[pl.BlockSpec((8, D), lambda l, j: (0, 0)),
                  pl.BlockSpec((1, D, nb), lambda l, j: (l, 0, j)),
                  pl.BlockSpec((1, 1, nb), lambda l, j: (l, 0, j))],
        out_specs=pl.BlockSpec((1, 8, nb), lambda l, j: (l, 0, j)),
        out_shape=jax.ShapeDtypeStruct((depth, 8, 6 * D), F32),
        compiler_params=_params("parallel", "parallel"),
        name="modulation",
    )(cond, ada_w, ada_b.reshape(depth, 1, 6 * D))


def _token_specs(hold=None):
    tile = _held(hold)
    lat = pl.BlockSpec((TM, D), lambda i: (jnp.minimum(tile(i), N_LAT_TILES - 1), 0))
    ctx = pl.BlockSpec((TM, D), lambda i: (jnp.maximum(tile(i) - N_LAT_TILES, 0), 0))
    return [lat, ctx]


def _token_tile(lat_ref, ctx_ref, tile=None):
    tile = pl.program_id(0) if tile is None else tile
    return jnp.where(tile < N_LAT_TILES, lat_ref[...], ctx_ref[...])


def _inproj0_kernel(x_ref, c_ref, mod_ref, g_ref, w_ref, ua_ref, ub_ref, gg_ref):
    mod = mod_ref[0]
    n = _rms_mod(_token_tile(x_ref, c_ref), g_ref[...], mod[:, D:2 * D], mod[:, 0:D]).astype(BF16)

    def proj(c):
        return jnp.dot(n, w_ref[:, c * HALF:(c + 1) * HALF], preferred_element_type=F32)

    value, gate = proj(0), proj(1)
    recur = proj(2)
    ua_ref[...] = value * _sigmoid(gate)
    gelu_in = proj(3)
    ub_ref[...] = recur
    gg_ref[...] = _gelu_tanh(gelu_in).astype(gg_ref.dtype)


def _inproj0(x, c, mod, g, w):
    tok = pl.BlockSpec((TM, HALF), lambda i: (i, 0))
    shp = jax.ShapeDtypeStruct((N_TOK, HALF), F32)
    return pl.pallas_call(
        _inproj0_kernel,
        grid=(N_TILES,),
        in_specs=_token_specs() + [
                  pl.BlockSpec((1, 1, 6 * D), _mod_row(TM)),
                  pl.BlockSpec((1, D), lambda i: (0, 0)),
                  pl.BlockSpec((D, 4 * HALF), lambda i: (0, 0))],
        out_specs=[tok, tok, tok],
        out_shape=[shp, shp, jax.ShapeDtypeStruct((N_TOK, HALF), BF16)],
        compiler_params=_params("parallel"),
        name="inproj0",
    )(x, c, mod, g, w)


def _halo_specs(halo, hold=None):
    per_tile = TM // halo
    last = N_TOK // halo - 1
    tile = _held(hold)
    prev = pl.BlockSpec((halo, HALF), lambda i: (jnp.maximum(tile(i) * per_tile - 1, 0), 0))
    nxt = pl.BlockSpec((halo, HALF), lambda i: (jnp.minimum((tile(i) + 1) * per_tile, last), 0))
    return prev, nxt


def _seq_edges(i):
    is_ctx = i >= N_LAT_TILES
    first = jnp.logical_or(is_ctx, i % TILES_PER_SEQ == 0)
    last = jnp.logical_or(is_ctx, i % TILES_PER_SEQ == TILES_PER_SEQ - 1)
    return first, last


def _fill_padded(buf_ref, prev_ref, cur_ref, next_ref, halo, tile=None):
    first, last = _seq_edges(pl.program_id(0) if tile is None else tile)
    buf_ref[0:halo, :] = jnp.where(first, 0.0, prev_ref[...])
    buf_ref[halo:halo + TM, :] = cur_ref[...]
    buf_ref[halo + TM:halo + TM + halo, :] = jnp.where(last, 0.0, next_ref[...])


def _tap_weight(w_ref, k, rows):
    return jnp.concatenate([w_ref[k]] * (rows // 8), axis=0)


def _sublane_replicated(w):
    return jnp.broadcast_to(w[:, None, :], (w.shape[0], 8, w.shape[1]))


def _depthwise(buf_ref, w_ref, taps, first_off, rows, row0):
    acc = None
    for k in range(taps):
        term = _tap_weight(w_ref, k, rows) * buf_ref[row0 + first_off + k:row0 + first_off + k + rows, :]
        acc = term if acc is None else acc + term
    return acc


CONV_ROWS = 32


SHIFT_ROWS = TM + 2 * HALO_A - 8


def _conv0_kernel(ua_ref, uap_ref, uan_ref, ub_ref, ubp_ref, ubn_ref,
                  dww_ref, dwb_ref, lng_ref, lnb_ref, cw_ref, cb_ref,
                  a_ref, v_ref, bufa_ref, bufb_ref, shift_ref):
    _fill_padded(bufa_ref, uap_ref, ua_ref, uan_ref, HALO_A)
    _fill_padded(bufb_ref, ubp_ref, ub_ref, ubn_ref, HALO_S)
    for s in range(1, 8):
        shift_ref[s - 1] = bufa_ref[s:s + SHIFT_ROWS, :]
    first = HALO_A - CONV_A // 2
    for r in range(TM // CONV_ROWS):
        row0 = r * CONV_ROWS
        u = None
        for k in range(CONV_A):
            off = first + k
            base = row0 + off - off % 8
            rows = bufa_ref[base:base + CONV_ROWS, :] if off % 8 == 0 else shift_ref[off % 8 - 1, base:base + CONV_ROWS, :]
            term = _tap_weight(dww_ref, k, CONV_ROWS) * rows
            u = term if u is None else u + term
        u = u + dwb_ref[...]
        mu = jnp.mean(u, axis=-1, keepdims=True)
        uc = u - mu
        var = jnp.mean(uc * uc, axis=-1, keepdims=True)
        y = uc * lax.rsqrt(var + EPS) * lng_ref[...] + lnb_ref[...]
        a_ref[row0:row0 + CONV_ROWS, :] = _silu(y).astype(a_ref.dtype)
        v = _depthwise(bufb_ref, cw_ref, CONV_B, HALO_S - 2, CONV_ROWS, row0) + cb_ref[...]
        v_ref[0, row0:row0 + CONV_ROWS, :] = v


def _seq_major(i, hold=None):
    i = _held(hold)(i)
    is_lat = i < N_LAT_TILES
    return (jnp.where(is_lat, i // TILES_PER_SEQ, i - N_LAT_TILES), jnp.where(is_lat, 1 + i % TILES_PER_SEQ, 0), 0)


def _conv0(ua, ub, dw_w, dw_b, ln_g, ln_b, conv_w, conv_b):
    tok = pl.BlockSpec((TM, HALF), lambda i: (i, 0))
    pa, na = _halo_specs(HALO_A)
    ps, ns = _halo_specs(HALO_S)
    vec = pl.BlockSpec((1, HALF), lambda i: (0, 0))
    return pl.pallas_call(
        _conv0_kernel,
        grid=(N_TILES,),
        in_specs=[tok, pa, na, tok, ps, ns,
                  pl.BlockSpec((CONV_A, 8, HALF), lambda i: (0, 0, 0)), vec, vec, vec,
                  pl.BlockSpec((CONV_B, 8, HALF), lambda i: (0, 0, 0)), vec],
        out_specs=[tok, pl.BlockSpec((1, TM, HALF), _seq_major)],
        out_shape=[jax.ShapeDtypeStruct((N_TOK, HALF), BF16),
                   jax.ShapeDtypeStruct((BATCH, SEQ + CTX, HALF), F32)],
        scratch_shapes=[pltpu.VMEM((TM + 2 * HALO_A, HALF), F32), pltpu.VMEM((TM + 2 * HALO_S, HALF), F32),
                        pltpu.VMEM((7, SHIFT_ROWS, HALF), F32)],
        compiler_params=_params("parallel"),
        name="conv0",
    )(ua, ua, ua, ub, ub, ub, _sublane_replicated(dw_w), dw_b.reshape(1, HALF), ln_g.reshape(1, HALF),
      ln_b.reshape(1, HALF), _sublane_replicated(conv_w), conv_b.reshape(1, HALF))


SCAN_UNROLL = 8


def _scan_kernel(vf_ref, vb_ref, w_ref, gb_ref, lam_ref, wg_ref, wu_ref, wd_ref,
                 yf_ref, yb_ref, wg_out, wu_out, wd_out, h_ref, a_ref, b_ref):
    wg_out[...] = wg_ref[...].astype(BF16)
    wu_out[...] = wu_ref[...].astype(BF16)
    wd_out[...] = wd_ref[...].astype(BF16)

    @pl.when(pl.program_id(0) == 0)
    def _():
        h_ref[...] = jnp.zeros_like(h_ref)

    for d, v_ref in enumerate((vf_ref, vb_ref)):
        v = v_ref[...].reshape(BATCH * TM, HALF)
        vb = v.astype(BF16)
        neg = -lam_ref[d]
        softplus = jnp.maximum(neg, 0.0) + jnp.log(1.0 + jnp.exp(-jnp.abs(neg)))
        rate = (-0.5 * LRU_C * LOG2E) * softplus
        for n in range(N_LRU_BLOCKS):
            sl = slice(n * LRU_BLOCK, (n + 1) * LRU_BLOCK)
            g = jnp.dot(vb[:, sl], w_ref[d, n], preferred_element_type=F32)
            tr = jnp.tanh(g[:, 0:LRU_BLOCK] + gb_ref[d, 0:1, sl])
            ti = jnp.tanh(g[:, LRU_BLOCK:2 * LRU_BLOCK] + gb_ref[d, 1:2, sl])
            a = jnp.exp2(rate[:, sl] * tr + rate[:, sl])
            a_ref[d, :, sl] = a
            b_ref[d, :, sl] = jnp.sqrt(1.0 - a * a) * ((0.5 * ti + 0.5) * v[:, sl])

    def body(s, hs):
        hs = list(hs)
        for u in range(SCAN_UNROLL):
            t = s * SCAN_UNROLL + u
            for d, y_ref in enumerate((yf_ref, yb_ref)):
                row = t if d == 0 else TM - 1 - t
                for bt in range(BATCH):
                    c = d * BATCH + bt
                    src = bt * TM + row
                    hs[c] = a_ref[d, pl.ds(src, 1), :] * hs[c] + b_ref[d, pl.ds(src, 1), :]
                    y_ref[bt, pl.ds(row, 1), :] = hs[c]
        return tuple(hs)

    init = tuple(h_ref[c:c + 1, :] for c in range(2 * BATCH))
    final = init
    for s in range(TM // SCAN_UNROLL):
        final = body(s, final)
    for c in range(2 * BATCH):
        h_ref[c:c + 1, :] = final[c]


def _scan(v, gate_w, gate_b, lam, moe_wg, moe_wu, moe_wd):
    steps = TILES_PER_SEQ + 1
    depth = moe_wg.shape[0]
    assert depth * N_EXPERTS <= steps
    fwd = pl.BlockSpec((BATCH, TM, HALF), lambda j: (0, j, 0))
    bwd = pl.BlockSpec((BATCH, TM, HALF), lambda j: (0, jnp.where(j == 0, 0, TILES_PER_SEQ + 1 - j), 0))
    shp = jax.ShapeDtypeStruct((BATCH, SEQ + CTX, HALF), F32)

    def expert(j):
        s = jnp.minimum(j, depth * N_EXPERTS - 1)
        return (s // N_EXPERTS, s % N_EXPERTS, 0, 0)

    up = pl.BlockSpec((1, 1, D, D_FF), expert)
    down = pl.BlockSpec((1, 1, D_FF, D), expert)
    return pl.pallas_call(
        _scan_kernel,
        grid=(steps,),
        in_specs=[fwd, bwd,
                  pl.BlockSpec((2, N_LRU_BLOCKS, LRU_BLOCK, 2 * LRU_BLOCK), lambda j: (0, 0, 0, 0)),
                  pl.BlockSpec((2, 2, HALF), lambda j: (0, 0, 0)),
                  pl.BlockSpec((2, 1, HALF), lambda j: (0, 0, 0)),
                  up, up, down],
        out_specs=[fwd, bwd, up, up, down],
        out_shape=[shp, shp, jax.ShapeDtypeStruct(moe_wg.shape, BF16), jax.ShapeDtypeStruct(moe_wu.shape, BF16),
                   jax.ShapeDtypeStruct(moe_wd.shape, BF16)],
        scratch_shapes=[pltpu.VMEM((8, HALF), F32), pltpu.VMEM((2, BATCH * TM, HALF), F32),
                        pltpu.VMEM((2, BATCH * TM, HALF), F32)],
        compiler_params=_params("arbitrary"),
        name="lru_scan",
    )(v, v, gate_w, gate_b, lam, moe_wg, moe_wu, moe_wd)


def _route(score, sel):
    rows = [sel[e:e + 1, :] for e in range(N_EXPERTS)]
    gbest = None
    gidx = None
    for g in range(N_GROUPS):
        top2 = None
        for p in range(PER_GROUP):
            for q in range(p + 1, PER_GROUP):
                s = rows[g * PER_GROUP + p] + rows[g * PER_GROUP + q]
                top2 = s if top2 is None else jnp.maximum(top2, s)
        if g == 0:
            gbest = top2
            gidx = jnp.zeros(top2.shape, jnp.int32)
        else:
            better = top2 > gbest
            gidx = jnp.where(better, g, gidx)
            gbest = jnp.where(better, top2, gbest)
    eint = lax.broadcasted_iota(jnp.int32, sel.shape, 0)
    eidx = eint.astype(F32)
    masked = jnp.where(jnp.right_shift(eint, 2) == gidx, sel, -jnp.inf)
    v1 = jnp.max(masked, axis=0, keepdims=True)
    i1 = jnp.min(jnp.where(masked == v1, eidx, float(N_EXPERTS)), axis=0, keepdims=True)
    masked2 = jnp.where(eidx == i1, -jnp.inf, masked)
    v2 = jnp.max(masked2, axis=0, keepdims=True)
    i2 = jnp.min(jnp.where(masked2 == v2, eidx, float(N_EXPERTS)), axis=0, keepdims=True)
    s1 = jnp.sum(jnp.where(eidx == i1, score, 0.0), axis=0, keepdims=True)
    s2 = jnp.sum(jnp.where(eidx == i2, score, 0.0), axis=0, keepdims=True)
    inv = 1.0 / (s1 + s2)
    return jnp.where(eidx == i1, s1 * inv, 0.0) + jnp.where(eidx == i2, s2 * inv, 0.0), gidx


def _group_ranks(gidx, carry_ref, tile):
    onehot = lax.broadcasted_iota(jnp.int32, (8, TM), 0) == gidx
    oh = jnp.where(onehot, 1.0, 0.0)
    before = lax.broadcasted_iota(jnp.int32, (TM, TM), 0) < lax.broadcasted_iota(jnp.int32, (TM, TM), 1)
    prefix = jnp.dot(oh.astype(BF16), jnp.where(before, 1.0, 0.0).astype(BF16), preferred_element_type=F32)
    carry = jnp.where(tile % (TMOE // TM) == 0, 0.0, carry_ref[...])
    rank = jnp.sum(jnp.where(onehot, prefix + carry[:, 0:1], 0.0), axis=0, keepdims=True)
    carry = carry + jnp.sum(oh, axis=1, keepdims=True)
    carry_ref[...] = carry
    return rank, carry


def _split_bf16(v):
    hi = v.astype(BF16)
    return hi, (v - hi.astype(F32)).astype(BF16)


def _outproj_step(m1, m2, x, mod_ref, wo_ref, g_ref, rw_ref, rb_ref, h_ref, n_ref, route_ref, cnt_ref, tok_ref, carry_ref,
                  nprev_ref):
    step = pl.program_id(0)
    n_hi, n_lo = _split_bf16(nprev_ref[...])
    w_hi, w_lo = _split_bf16(rw_ref[...])

    mod = mod_ref[0]
    mix = (jnp.dot(m1, wo_ref[0:HALF, :], preferred_element_type=F32)
           + jnp.dot(m2, wo_ref[HALF:2 * HALF, :], preferred_element_type=F32))

    logits = _nt_dot(w_hi, n_hi) + _nt_dot(w_hi, n_lo) + _nt_dot(w_lo, n_hi)

    h = x + mod[:, 2 * D:3 * D] * mix
    h_ref[...] = h
    n = _rms_mod(h, g_ref[...], mod[:, 4 * D:5 * D], mod[:, 3 * D:4 * D])
    n_ref[...] = n.astype(n_ref.dtype)
    nprev_ref[...] = n

    score = _sigmoid(logits)
    comb, gidx = _route(score, score + rb_ref[...])
    rank, counts = _group_ranks(gidx, carry_ref, step - 1)
    record = jnp.concatenate([comb, gidx.astype(F32), rank, jnp.zeros((ROUTE_LANES - ROUTE_RANK - 1, TM), F32)], axis=0)
    route_ref[...] = record[0:ROUTE_ROWS]
    tok_ref[...] = record.T
    cnt_ref[0] = counts


def _init_pipeline(carry_ref, nprev_ref):
    @pl.when(pl.program_id(0) == 0)
    def _():
        nprev_ref[...] = jnp.zeros_like(nprev_ref)
        carry_ref[...] = jnp.zeros_like(carry_ref)


def _outproj0_kernel(a_ref, yf_ref, yb_ref, gg_ref, x_ref, c_ref, mod_ref, wo_ref, g_ref, rw_ref, rb_ref,
                     h_ref, n_ref, route_ref, cnt_ref, tok_ref, carry_ref, nprev_ref):
    _init_pipeline(carry_ref, nprev_ref)
    tile = jnp.minimum(pl.program_id(0), N_TILES - 1)
    m2 = ((yf_ref[0] + yb_ref[0]) * gg_ref[...]).astype(BF16)
    _outproj_step(a_ref[...], m2, _token_tile(x_ref, c_ref, tile), mod_ref, wo_ref, g_ref, rw_ref, rb_ref,
                  h_ref, n_ref, route_ref, cnt_ref, tok_ref, carry_ref, nprev_ref)


def _outproj1_kernel(cx_ref, cxp_ref, cxn_ref, bg_ref, att_ref, cw_ref, x_ref, mod_ref, wo_ref, g_ref, rw_ref, rb_ref,
                     h_ref, n_ref, route_ref, cnt_ref, tok_ref, buf_ref, carry_ref, nprev_ref):
    _init_pipeline(carry_ref, nprev_ref)
    tile = jnp.minimum(pl.program_id(0), N_LAT_TILES - 1)
    _fill_padded(buf_ref, cxp_ref, cx_ref, cxn_ref, HALO_S, tile)
    conv = _depthwise(buf_ref, cw_ref, CONV_C, HALO_S - 1, TM, 0)
    m1 = (bg_ref[...] * conv).astype(BF16)
    _outproj_step(m1, att_ref[...], x_ref[...], mod_ref, wo_ref, g_ref, rw_ref, rb_ref,
                  h_ref, n_ref, route_ref, cnt_ref, tok_ref, carry_ref, nprev_ref)


def _outproj_common(n_tiles):
    hold = n_tiles - 1
    tile = _held(hold)
    routed = lambda i: jnp.maximum(i - 1, 0)
    in_specs = [pl.BlockSpec((1, 1, 6 * D), _mod_row(TM, hold)),
                pl.BlockSpec((D, D), lambda i: (0, 0)),
                pl.BlockSpec((1, D), lambda i: (0, 0)),
                pl.BlockSpec((N_EXPERTS, D), lambda i: (0, 0)),
                pl.BlockSpec((N_EXPERTS, 1), lambda i: (0, 0))]
    out_specs = [pl.BlockSpec((TM, D), lambda i: (tile(i), 0)),
                 pl.BlockSpec((TM, D), lambda i: (tile(i), 0)),
                 pl.BlockSpec((ROUTE_ROWS, TM), lambda i: (0, routed(i))),
                 pl.BlockSpec((1, 8, 128), lambda i: (routed(i), 0, 0)),
                 pl.BlockSpec((TM, ROUTE_LANES), lambda i: (routed(i), 0))]
    rows = n_tiles * TM
    out_shape = [jax.ShapeDtypeStruct((rows, D), F32), jax.ShapeDtypeStruct((rows, D), BF16),
                 jax.ShapeDtypeStruct((ROUTE_ROWS, rows), F32), jax.ShapeDtypeStruct((n_tiles, 8, 128), F32),
                 jax.ShapeDtypeStruct((rows, ROUTE_LANES), F32)]
    scratch = [pltpu.VMEM((8, 128), F32), pltpu.VMEM((TM, D), F32)]
    return in_specs, out_specs, out_shape, scratch


def _outproj0(a, yf, yb, gg, x, c, mod, wo, g, rw_t, rb):
    hold = N_TILES - 1
    tok = pl.BlockSpec((TM, HALF), lambda i: (_held(hold)(i), 0))
    scan_tok = pl.BlockSpec((1, TM, HALF), functools.partial(_seq_major, hold=hold))
    common_in, out_specs, out_shape, scratch = _outproj_common(N_TILES)
    return pl.pallas_call(
        _outproj0_kernel,
        grid=(N_TILES + 1,),
        in_specs=[tok, scan_tok, scan_tok, tok] + _token_specs(hold) + common_in,
        out_specs=out_specs,
        out_shape=out_shape,
        scratch_shapes=scratch,
        compiler_params=_params("arbitrary"),
        name="outproj0",
    )(a, yf, yb, gg, x, c, mod, wo, g, rw_t, rb)


def _outproj1(cx, bg, att, conv_w, x, mod, wo, g, rw_t, rb):
    hold = N_LAT_TILES - 1
    tok = pl.BlockSpec((TM, HALF), lambda i: (_held(hold)(i), 0))
    ps, ns = _halo_specs(HALO_S, hold)
    common_in, out_specs, out_shape, scratch = _outproj_common(N_LAT_TILES)
    return pl.pallas_call(
        _outproj1_kernel,
        grid=(N_LAT_TILES + 1,),
        in_specs=[tok, ps, ns, tok, tok, pl.BlockSpec((CONV_C, 8, HALF), lambda i: (0, 0, 0)),
                  pl.BlockSpec((TM, D), lambda i: (_held(hold)(i), 0))] + common_in,
        out_specs=out_specs,
        out_shape=out_shape,
        scratch_shapes=[pltpu.VMEM((TM + 2 * HALO_S, HALF), F32)] + scratch,
        compiler_params=_params("arbitrary"),
        name="outproj1",
    )(cx, cx, cx, bg, att, _sublane_replicated(conv_w), x, mod, wo, g, rw_t, rb)


def _moe_kernel(cnt_ref, n_ref, rt_ref, r_ref, wg_ref, wu_ref, wd_ref, h_ref, mod_ref, fg_ref, o_ref,
                hid_ref, *, final_norm, first_tile, subtiles):
    i = pl.program_id(0)
    g = pl.program_id(1)

    @pl.when(g == 0)
    def _():
        o_ref[...] = jnp.zeros_like(o_ref)

    gf = g.astype(F32)
    slot_row = lax.broadcasted_iota(jnp.int32, (CAP, TMOE), 0).astype(F32)
    slot_col = lax.broadcasted_iota(jnp.int32, (TMOE, CAP), 1).astype(F32)

    counts = [cnt_ref[(first_tile + i * subtiles + s) * N_GROUPS + g] for s in range(subtiles)]
    starts = [sum(counts[:s], jnp.int32(0)) for s in range(subtiles)]
    total = starts[-1] + counts[-1]
    tile_rows = [slice(s * TMOE, (s + 1) * TMOE) for s in range(subtiles)]

    def run_chunk(k, sources):
        base = k * CAP
        sels, shifts = [], []
        xg = None
        for s in sources:
            shift = (starts[s] - base).astype(F32)
            in_group = rt_ref[ROUTE_GROUP:ROUTE_GROUP + 1, tile_rows[s]] == gf
            slot = jnp.where(in_group, rt_ref[ROUTE_RANK:ROUTE_RANK + 1, tile_rows[s]] + shift, -1.0)
            sel = slot_row == slot
            part = jnp.dot(jnp.where(sel, 1.0, 0.0).astype(BF16), n_ref[tile_rows[s], :], preferred_element_type=F32)
            xg = part if xg is None else xg + part
            sels.append(sel)
            shifts.append(shift)
        xg = xg.astype(BF16)
        for j in range(PER_GROUP):
            cw = None
            for s, sel in zip(sources, sels):
                comb_row = rt_ref[pl.ds(g * PER_GROUP + j, 1), tile_rows[s]]
                part = jnp.sum(jnp.where(sel, comb_row, 0.0), axis=1, keepdims=True)
                cw = part if cw is None else cw + part
            hid = (_silu(jnp.dot(xg, wg_ref[0, j], preferred_element_type=F32))
                   * jnp.dot(xg, wu_ref[0, j], preferred_element_type=F32) * cw)
            hid_ref[:, j * D_FF:(j + 1) * D_FF] = hid.astype(BF16)
        y = jnp.dot(hid_ref[...], wd_ref[0].reshape(PER_GROUP * D_FF, D),
                    preferred_element_type=F32).astype(BF16)
        for s, shift in zip(sources, shifts):
            in_group = r_ref[tile_rows[s], ROUTE_GROUP:ROUTE_GROUP + 1] == gf
            slot = r_ref[tile_rows[s], ROUTE_RANK:ROUTE_RANK + 1] + shift
            back = jnp.where(jnp.logical_and(in_group, slot_col == slot), 1.0, 0.0).astype(BF16)
            o_ref[tile_rows[s], :] += jnp.dot(back, y, preferred_element_type=F32)

    def loop(lo, hi, sources):
        def body(k, carry):
            run_chunk(k, sources)
            return carry
        lax.fori_loop(lo, hi, body, 0)

    for s in range(subtiles):
        end = starts[s] + counts[s]
        first_inside = (starts[s] + (CAP - 1)) // CAP
        if s + 1 < subtiles:
            loop(first_inside, end // CAP, [s])

            @pl.when(end % CAP != 0)
            def _(s=s, end=end):
                run_chunk(end // CAP, [s, s + 1])
        else:
            loop(first_inside, (end + (CAP - 1)) // CAP, [s])

    @pl.when(g == N_GROUPS - 1)
    def _():
        out = h_ref[...] + mod_ref[0][:, 5 * D:6 * D] * o_ref[...]
        if final_norm:
            out = out * lax.rsqrt(jnp.mean(out * out, axis=-1, keepdims=True) + EPS) * fg_ref[...]
        o_ref[...] = out


def _moe_chunks(counts):
    sub = TMOE // TM
    return counts[sub - 1::sub, :N_GROUPS, 0].astype(jnp.int32).reshape(-1)


def _moe(n_chunks, n, route_t, route, wg, wu, wd, layer, h, mod, final_g, *, first_tile, n_tiles, subtiles, final_norm,
         name):
    assert subtiles in (1, 2) and first_tile % subtiles == 0 and n_tiles % subtiles == 0
    step = subtiles * TMOE
    first = first_tile // subtiles
    mod_row = lambda i, g, nch: (jnp.minimum(((first + i) * step) // SEQ, BATCH), 0, 0)
    tok = pl.BlockSpec((step, D), lambda i, g, nch: (first + i, 0))
    grid_spec = pltpu.PrefetchScalarGridSpec(
        num_scalar_prefetch=1,
        grid=(n_tiles // subtiles, N_GROUPS),
        in_specs=[tok,
                  pl.BlockSpec((ROUTE_ROWS, step), lambda i, g, nch: (0, first + i)),
                  pl.BlockSpec((step, ROUTE_LANES), lambda i, g, nch: (first + i, 0)),
                  pl.BlockSpec((1, PER_GROUP, D, D_FF), lambda i, g, nch: (layer, g, 0, 0)),
                  pl.BlockSpec((1, PER_GROUP, D, D_FF), lambda i, g, nch: (layer, g, 0, 0)),
                  pl.BlockSpec((1, PER_GROUP, D_FF, D), lambda i, g, nch: (layer, g, 0, 0)),
                  tok,
                  pl.BlockSpec((1, 1, 6 * D), mod_row),
                  pl.BlockSpec((1, D), lambda i, g, nch: (0, 0))],
        out_specs=pl.BlockSpec((step, D), lambda i, g, nch: (i, 0)),
        scratch_shapes=[pltpu.VMEM((CAP, PER_GROUP * D_FF), BF16)])
    return pl.pallas_call(
        functools.partial(_moe_kernel, final_norm=final_norm, first_tile=first_tile, subtiles=subtiles),
        grid_spec=grid_spec,
        out_shape=jax.ShapeDtypeStruct((n_tiles * TMOE, D), F32),
        compiler_params=_params("parallel", "arbitrary", vmem=VMEM_LIMIT_MOE),
        name=name,
    )(n_chunks, n, route_t, route, wg, wu, wd, h, mod, final_g)


def _inproj1_kernel(x_ref, c_ref, mod_ref, g_ref, w_ref, wvt_ref, cx_ref, bg_ref, q_ref, k_ref, vt_ref):
    mod = mod_ref[0]
    n = _rms_mod(_token_tile(x_ref, c_ref), g_ref[...], mod[:, D:2 * D], mod[:, 0:D]).astype(BF16)

    def proj(c):
        return jnp.dot(n, w_ref[:, c * HALF:(c + 1) * HALF], preferred_element_type=F32)

    conv_in, conv_gate = proj(0), proj(2)
    out_gate = proj(1)
    cx_ref[...] = conv_gate * conv_in
    q = proj(3)
    bg_ref[...] = out_gate
    k = proj(4)
    q_ref[...] = (q * (HEAD_DIM ** -0.5 * LOG2E)).astype(BF16)
    vt = _nt_dot(wvt_ref[...], n)
    k_ref[...] = k.astype(BF16)
    vt_ref[...] = vt.astype(BF16)


def _inproj1(x, c, mod, g, w):
    tok = pl.BlockSpec((TM, HALF), lambda i: (i, 0))
    f = jax.ShapeDtypeStruct((N_TOK, HALF), F32)
    h = jax.ShapeDtypeStruct((N_TOK, HALF), BF16)
    return pl.pallas_call(
        _inproj1_kernel,
        grid=(N_TILES,),
        in_specs=_token_specs() + [
                  pl.BlockSpec((1, 1, 6 * D), _mod_row(TM)),
                  pl.BlockSpec((1, D), lambda i: (0, 0)),
                  pl.BlockSpec((D, 5 * HALF), lambda i: (0, 0)),
                  pl.BlockSpec((HALF, D), lambda i: (0, 0))],
        out_specs=[tok] * 4 + [pl.BlockSpec((HALF, TM), lambda i: (0, i))],
        out_shape=[f, f, h, h, jax.ShapeDtypeStruct((HALF, N_TOK), BF16)],
        compiler_params=_params("parallel"),
        name="inproj1",
    )(x, c, mod, g, w[:, :5 * HALF], w[:, 5 * HALF:].T)


def _natten_kernel(q_ref, kp_ref, kc_ref, kn_ref, vp_ref, vc_ref, vn_ref, kx_ref, vx_ref, bias_ref, o_ref):
    pair = 2 * HEAD_DIM
    low = lax.broadcasted_iota(jnp.int32, (TM, pair), 1) < HEAD_DIM
    head_mask = [low.astype(F32).astype(BF16), jnp.logical_not(low).astype(F32).astype(BF16)]
    top = lax.broadcasted_iota(jnp.int32, (pair, TM), 0) < HEAD_DIM
    own_rows = [top.astype(F32).astype(BF16), jnp.logical_not(top).astype(F32).astype(BF16)]
    def raw_scores(head):
        g, hh = divmod(head, 2)
        sl = slice(pair * g, pair * (g + 1))
        qm = q_ref[:, sl] * head_mask[hh]
        return [_nt_dot(qm, k_ref[:, sl]) for k_ref in (kp_ref, kc_ref, kn_ref, kx_ref)]

    def biased(head, s):
        s = [s[t] + bias_ref[0, head, :, t * TM:(t + 1) * TM] for t in range(3)] + [s[3]]
        m = jnp.max(jnp.maximum(jnp.maximum(s[0], s[1]), jnp.maximum(s[2], s[3])), axis=-1, keepdims=True)
        return s, m

    def attend(head, s, m):
        g, hh = divmod(head, 2)
        sl = slice(pair * g, pair * (g + 1))
        p = jnp.concatenate([jnp.exp2(st - m).astype(BF16) for st in s], axis=1)
        lhs = jnp.concatenate([vt_ref[sl, :] * own_rows[hh] + own_rows[1 - hh]
                               for vt_ref in (vp_ref, vc_ref, vn_ref, vx_ref)], axis=1)
        acc = _nt_dot(lhs, p)
        if hh == 0:
            return acc[0:HEAD_DIM] * (1.0 / acc[HEAD_DIM:HEAD_DIM + 1])
        return acc[HEAD_DIM:pair] * (1.0 / acc[0:1])

    pending = biased(0, raw_scores(0))
    halves = []
    for head in range(N_HEADS):
        current = pending
        if head + 1 < N_HEADS:
            pending = biased(head + 1, raw_scores(head + 1))
        halves.append(attend(head, *current))
        if head % 2 == 1:
            sl = slice(pair * (head // 2), pair * (head // 2 + 1))
            o_ref[:, sl] = jnp.concatenate(halves, axis=0).T.astype(o_ref.dtype)
            halves = []


def _natten_bias(rpb):
    n_rows = SEQ // GRID_W
    n_dr, n_dc = 2 * WIN_H - 1, 2 * WIN_W - 1
    i = np.arange(ROWS_Q)
    j = np.arange(3 * ROWS_Q)
    col = np.arange(GRID_W)
    col_start = np.clip(col - WIN_W // 2, 0, GRID_W - WIN_W)
    col_ok = (col[None, :] >= col_start[:, None]) & (col[None, :] < col_start[:, None] + WIN_W)
    col_idx = col[None, :] - col[:, None] + (WIN_W - 1)
    onehot = ((col_idx[None] == np.arange(n_dc)[:, None, None]) & col_ok[None]).astype(np.float32)
    col_exp = jnp.dot(rpb.reshape(N_HEADS * n_dr, n_dc).astype(F32), onehot.reshape(n_dc, GRID_W * GRID_W),
                      precision=lax.Precision.HIGHEST).reshape(N_HEADS, n_dr, GRID_W, GRID_W)
    col_exp = jnp.where(col_ok[None, None], col_exp * LOG2E, NEG)
    masked = jnp.full((N_HEADS, GRID_W, GRID_W), NEG, F32)
    kinds = []
    for r0 in (0, ROWS_Q, n_rows - ROWS_Q):
        r = r0 + i
        kr = r0 - ROWS_Q + j
        r_start = np.clip(r - WIN_H // 2, 0, n_rows - WIN_H)
        row_ok = ((kr[None, :] >= r_start[:, None]) & (kr[None, :] < r_start[:, None] + WIN_H)
                  & (kr[None, :] >= 0) & (kr[None, :] < n_rows))
        row_idx = kr[None, :] - r[:, None] + (WIN_H - 1)
        rows = [jnp.concatenate([col_exp[:, row_idx[qi, kj]] if row_ok[qi, kj] else masked for kj in range(3 * ROWS_Q)],
                                axis=-1) for qi in range(ROWS_Q)]
        kinds.append(jnp.concatenate(rows, axis=1))
    return jnp.stack(kinds)


def _natten(q, k, vt, bias):
    def tile(b, i, off):
        return b * TILES_PER_SEQ + jnp.clip(i + off, 0, TILES_PER_SEQ - 1)

    def lat(off):
        return pl.BlockSpec((TM, HALF), lambda b, i: (tile(b, i, off), 0))

    def lat_t(off):
        return pl.BlockSpec((HALF, TM), lambda b, i: (0, tile(b, i, off)))

    ctx = pl.BlockSpec((TM, HALF), lambda b, i: (N_LAT_TILES + b, 0))
    ctx_t = pl.BlockSpec((HALF, TM), lambda b, i: (0, N_LAT_TILES + b))

    def kind(b, i):
        return (jnp.where(i == 0, 0, jnp.where(i == TILES_PER_SEQ - 1, 2, 1)), 0, 0, 0)

    return pl.pallas_call(
        _natten_kernel,
        grid=(BATCH, TILES_PER_SEQ),
        in_specs=[lat(0), lat(-1), lat(0), lat(1), lat_t(-1), lat_t(0), lat_t(1), ctx, ctx_t,
                  pl.BlockSpec((1, N_HEADS, TM, 3 * TM), kind)],
        out_specs=lat(0),
        out_shape=jax.ShapeDtypeStruct((N_LAT, HALF), BF16),
        compiler_params=_params("parallel", "arbitrary"),
        name="natten",
    )(q, k, k, k, vt, vt, vt, k, vt, bias)


def kernel(x, c, ctx, c_ctx, ada_w, ada_b, norm_mix_g, norm_ffn_g, w_out, ab_w_in, a_dw_w, a_dw_b, a_ln_g, a_ln_b,
           b_conv_w, b_conv_b, b_gate_w, b_gate_b, b_lambda, cd_w_in, c_conv_w, d_rpb, router_w, router_bias,
           moe_w_gate, moe_w_up, moe_w_down, final_g):
    x_lat = x.reshape(N_LAT, D)
    x_ctx = ctx.reshape(BATCH * CTX, D)
    cond = jnp.concatenate([c, c_ctx[None], jnp.zeros((8 - BATCH - 1, D), F32)], axis=0)
    mod = _modulation(cond, ada_w, ada_b)
    mod0 = mod[0].reshape(8, 1, 6 * D)
    mod1 = mod[1].reshape(8, 1, 6 * D)

    wo = w_out.astype(BF16)
    lat_tiles = N_LAT // TMOE
    ctx_tiles = BATCH * CTX // TMOE
    rw_t = router_w.T
    rb = router_bias.reshape(N_EXPERTS, 1)
    fg = final_g.reshape(1, D)

    ua, ub, gg = _inproj0(x_lat, x_ctx, mod0, norm_mix_g[0].reshape(1, D), ab_w_in[0].astype(BF16))
    a_out, v = _conv0(ua, ub, a_dw_w[0], a_dw_b[0], a_ln_g[0], a_ln_b[0], b_conv_w[0], b_conv_b[0])
    gw = b_gate_w[0]
    gate_w = (0.5 * jnp.concatenate([gw[:, 0], gw[:, 1]], axis=-1)).astype(BF16)
    yf, yb, wg, wu, wd = _scan(v, gate_w, 0.5 * b_gate_b[0], b_lambda[0].reshape(2, 1, HALF),
                               moe_w_gate, moe_w_up, moe_w_down)
    h1, n2, route0, cnt0, tok0 = _outproj0(a_out, yf, yb, gg, x_lat, x_ctx, mod0, wo[0],
                                           norm_ffn_g[0].reshape(1, D), rw_t, rb)
    moe0 = functools.partial(_moe, _moe_chunks(cnt0), n2, route0, tok0, wg, wu, wd, 0, h1, mod0, fg,
                             final_norm=False)
    h2_lat = moe0(first_tile=0, n_tiles=lat_tiles, subtiles=2, name="moe_lat")
    h2_ctx = moe0(first_tile=lat_tiles, n_tiles=ctx_tiles, subtiles=1, name="moe_ctx")

    cx, bg, q, k, vv = _inproj1(h2_lat, h2_ctx, mod1, norm_mix_g[1].reshape(1, D), cd_w_in[0].astype(BF16))
    att = _natten(q, k, vv, _natten_bias(d_rpb[0]))
    h3, n4, route1, cnt1, tok1 = _outproj1(cx, bg, att, c_conv_w[0], h2_lat, mod1, wo[1],
                                           norm_ffn_g[1].reshape(1, D), rw_t, rb)
    out = _moe(_moe_chunks(cnt1), n4, route1, tok1, wg, wu, wd, 1, h3, mod1, fg,
               first_tile=0, n_tiles=lat_tiles, subtiles=2, final_norm=True, name="moe_final")
    return out.reshape(BATCH, SEQ, D)
```

```python
import functools
import math

import jax
import jax.numpy as jnp
import numpy as np
from jax import lax
from jax.experimental import pallas as pl
from jax.experimental.pallas import tpu as pltpu

F32 = jnp.float32
BF16 = jnp.bfloat16

D = 1024
BATCH = 2
SEQ = 8192
CTX = 256
GRID_W = 64
N_LAT = BATCH * SEQ
N_TOK = N_LAT + BATCH * CTX
HALF = 512
CONV_A = 31
CONV_B = 4
CONV_C = 3
LRU_BLOCK = 128
N_LRU_BLOCKS = HALF // LRU_BLOCK
LRU_C = 8.0
HEAD_DIM = 64
N_HEADS = HALF // HEAD_DIM
WIN_H = 8
WIN_W = 16
N_EXPERTS = 16
N_GROUPS = 4
PER_GROUP = N_EXPERTS // N_GROUPS
D_FF = 512
EPS = 1e-6
NEG = -1e30
LOG2E = math.log2(math.e)

TM = 256
TILES_PER_SEQ = SEQ // TM
N_LAT_TILES = N_LAT // TM
N_TILES = N_TOK // TM
TIN = 512
TMOE = 512
CAP = 128
ROUTE_ROWS = 24
ROUTE_LANES = 128
ROUTE_GROUP = 16
ROUTE_RANK = 17
HALO_A = 16
HALO_S = 8
ROWS_Q = TM // GRID_W
VMEM_LIMIT = 48 * 1024 * 1024
VMEM_LIMIT_MOE = 56 * 1024 * 1024


def _params(*sem, vmem=VMEM_LIMIT):
    return pltpu.CompilerParams(dimension_semantics=sem, vmem_limit_bytes=vmem)


def _sigmoid(x):
    return 0.5 * jnp.tanh(0.5 * x) + 0.5


def _silu(x):
    return x * _sigmoid(x)


def _gelu_tanh(x):
    return 0.5 * x * (1.0 + jnp.tanh(0.7978845608028654 * (x + 0.044715 * (x * x * x))))


def _rms_mod(x, g, scale, shift):
    y = x * lax.rsqrt(jnp.mean(x * x, axis=-1, keepdims=True) + EPS) * g
    return y * (1.0 + scale) + shift


def _nt_dot(a, b):
    return lax.dot_general(a, b, (((1,), (1,)), ((), ())), preferred_element_type=F32)


def _held(hold):
    return (lambda i: i) if hold is None else (lambda i: jnp.minimum(i, hold))


def _mod_row(tile_rows, hold=None):
    per_seq = SEQ // tile_rows
    tile = _held(hold)
    return lambda i: (jnp.minimum(tile(i) // per_seq, BATCH), 0, 0)


def _mod_kernel(c_ref, w_ref, b_ref, o_ref):
    c = c_ref[...]
    s = _silu(c).astype(BF16)
    o_ref[0] = jnp.dot(s, w_ref[0].astype(BF16), preferred_element_type=F32) + b_ref[0]


def _modulation(cond, ada_w, ada_b):
    depth = ada_w.shape[0]
    nb = 1536
    return pl.pallas_call(
        _mod_kernel,
        grid=(depth, 6 * D // nb),
        in_specs=[pl.BlockSpec((8, D), lambda l, j: (0, 0)),
                  pl.BlockSpec((1, D, nb), lambda l, j: (l, 0, j)),
                  pl.BlockSpec((1, 1, nb), lambda l, j: (l, 0, j))],
        out_specs=pl.BlockSpec((1, 8, nb), lambda l, j: (l, 0, j)),
        out_shape=jax.ShapeDtypeStruct((depth, 8, 6 * D), F32),
        compiler_params=_params("parallel", "parallel"),
        name="modulation",
    )(cond, ada_w, ada_b.reshape(depth, 1, 6 * D))


def _token_specs(hold=None, rows=TM):
    tile = _held(hold)
    lat_tiles = N_LAT // rows
    lat = pl.BlockSpec((rows, D), lambda i: (jnp.minimum(tile(i), lat_tiles - 1), 0))
    ctx = pl.BlockSpec((rows, D), lambda i: (jnp.maximum(tile(i) - lat_tiles, 0), 0))
    return [lat, ctx]


def _token_tile(lat_ref, ctx_ref, tile=None, rows=TM):
    tile = pl.program_id(0) if tile is None else tile
    return jnp.where(tile < N_LAT // rows, lat_ref[...], ctx_ref[...])


def _inproj0_kernel(x_ref, c_ref, mod_ref, g_ref, w_ref, ua_ref, ub_ref, gg_ref):
    mod = mod_ref[0]
    n = _rms_mod(_token_tile(x_ref, c_ref, rows=TIN), g_ref[...], mod[:, D:2 * D], mod[:, 0:D]).astype(BF16)

    def proj(c):
        return jnp.dot(n, w_ref[:, c * HALF:(c + 1) * HALF], preferred_element_type=F32)

    value, gate = proj(0), proj(1)
    recur = proj(2)
    ua_ref[...] = value * _sigmoid(gate)
    gelu_in = proj(3)
    ub_ref[...] = recur
    gg_ref[...] = _gelu_tanh(gelu_in).astype(gg_ref.dtype)


def _inproj0(x, c, mod, g, w):
    tok = pl.BlockSpec((TIN, HALF), lambda i: (i, 0))
    shp = jax.ShapeDtypeStruct((N_TOK, HALF), F32)
    return pl.pallas_call(
        _inproj0_kernel,
        grid=(N_TOK // TIN,),
        in_specs=_token_specs(rows=TIN) + [
                  pl.BlockSpec((1, 1, 6 * D), _mod_row(TIN)),
                  pl.BlockSpec((1, D), lambda i: (0, 0)),
                  pl.BlockSpec((D, 4 * HALF), lambda i: (0, 0))],
        out_specs=[tok, tok, tok],
        out_shape=[shp, shp, jax.ShapeDtypeStruct((N_TOK, HALF), BF16)],
        compiler_params=_params("parallel"),
        name="inproj0",
    )(x, c, mod, g, w)


def _halo_specs(halo, hold=None):
    per_tile = TM // halo
    last = N_TOK // halo - 1
    tile = _held(hold)
    prev = pl.BlockSpec((halo, HALF), lambda i: (jnp.maximum(tile(i) * per_tile - 1, 0), 0))
    nxt = pl.BlockSpec((halo, HALF), lambda i: (jnp.minimum((tile(i) + 1) * per_tile, last), 0))
    return prev, nxt


def _seq_edges(i):
    is_ctx = i >= N_LAT_TILES
    first = jnp.logical_or(is_ctx, i % TILES_PER_SEQ == 0)
    last = jnp.logical_or(is_ctx, i % TILES_PER_SEQ == TILES_PER_SEQ - 1)
    return first, last


def _fill_padded(buf_ref, prev_ref, cur_ref, next_ref, halo, tile=None):
    first, last = _seq_edges(pl.program_id(0) if tile is None else tile)
    buf_ref[0:halo, :] = jnp.where(first, 0.0, prev_ref[...])
    buf_ref[halo:halo + TM, :] = cur_ref[...]
    buf_ref[halo + TM:halo + TM + halo, :] = jnp.where(last, 0.0, next_ref[...])


def _tap_weight(w_ref, k, rows):
    return jnp.concatenate([w_ref[k]] * (rows // 8), axis=0)


def _sublane_replicated(w):
    return jnp.broadcast_to(w[:, None, :], (w.shape[0], 8, w.shape[1]))


def _depthwise(buf_ref, w_ref, taps, first_off, rows, row0):
    acc = None
    for k in range(taps):
        term = _tap_weight(w_ref, k, rows) * buf_ref[row0 + first_off + k:row0 + first_off + k + rows, :]
        acc = term if acc is None else acc + term
    return acc


CONV_ROWS = 32


SHIFT_ROWS = TM + 2 * HALO_A - 8


def _conv0_kernel(ua_ref, uap_ref, uan_ref, ub_ref, ubp_ref, ubn_ref,
                  dww_ref, dwb_ref, lng_ref, lnb_ref, cw_ref, cb_ref,
                  a_ref, v_ref, bufa_ref, bufb_ref, shift_ref):
    _fill_padded(bufa_ref, uap_ref, ua_ref, uan_ref, HALO_A)
    _fill_padded(bufb_ref, ubp_ref, ub_ref, ubn_ref, HALO_S)
    for s in range(1, 8):
        shift_ref[s - 1] = bufa_ref[s:s + SHIFT_ROWS, :]
    first = HALO_A - CONV_A // 2
    for r in range(TM // CONV_ROWS):
        row0 = r * CONV_ROWS
        u = None
        for k in range(CONV_A):
            off = first + k
            base = row0 + off - off % 8
            rows = bufa_ref[base:base + CONV_ROWS, :] if off % 8 == 0 else shift_ref[off % 8 - 1, base:base + CONV_ROWS, :]
            term = _tap_weight(dww_ref, k, CONV_ROWS) * rows
            u = term if u is None else u + term
        u = u + dwb_ref[...]
        mu = jnp.mean(u, axis=-1, keepdims=True)
        uc = u - mu
        var = jnp.mean(uc * uc, axis=-1, keepdims=True)
        y = uc * lax.rsqrt(var + EPS) * lng_ref[...] + lnb_ref[...]
        a_ref[row0:row0 + CONV_ROWS, :] = _silu(y).astype(a_ref.dtype)
        v = _depthwise(bufb_ref, cw_ref, CONV_B, HALO_S - 2, CONV_ROWS, row0) + cb_ref[...]
        v_ref[0, row0:row0 + CONV_ROWS, :] = v


def _seq_major(i, hold=None):
    i = _held(hold)(i)
    is_lat = i < N_LAT_TILES
    return (jnp.where(is_lat, i // TILES_PER_SEQ, i - N_LAT_TILES), jnp.where(is_lat, 1 + i % TILES_PER_SEQ, 0), 0)


def _conv0(ua, ub, dw_w, dw_b, ln_g, ln_b, conv_w, conv_b):
    tok = pl.BlockSpec((TM, HALF), lambda i: (i, 0))
    pa, na = _halo_specs(HALO_A)
    ps, ns = _halo_specs(HALO_S)
    vec = pl.BlockSpec((1, HALF), lambda i: (0, 0))
    return pl.pallas_call(
        _conv0_kernel,
        grid=(N_TILES,),
        in_specs=[tok, pa, na, tok, ps, ns,
                  pl.BlockSpec((CONV_A, 8, HALF), lambda i: (0, 0, 0)), vec, vec, vec,
                  pl.BlockSpec((CONV_B, 8, HALF), lambda i: (0, 0, 0)), vec],
        out_specs=[tok, pl.BlockSpec((1, TM, HALF), _seq_major)],
        out_shape=[jax.ShapeDtypeStruct((N_TOK, HALF), BF16),
                   jax.ShapeDtypeStruct((BATCH, SEQ + CTX, HALF), F32)],
        scratch_shapes=[pltpu.VMEM((TM + 2 * HALO_A, HALF), F32), pltpu.VMEM((TM + 2 * HALO_S, HALF), F32),
                        pltpu.VMEM((7, SHIFT_ROWS, HALF), F32)],
        compiler_params=_params("parallel"),
        name="conv0",
    )(ua, ua, ua, ub, ub, ub, _sublane_replicated(dw_w), dw_b.reshape(1, HALF), ln_g.reshape(1, HALF),
      ln_b.reshape(1, HALF), _sublane_replicated(conv_w), conv_b.reshape(1, HALF))


SCAN_UNROLL = 8


def _cast_specs(layer, steps_per_expert, step_of):
    def expert(*idx):
        return jnp.minimum(step_of(*idx) // steps_per_expert, N_EXPERTS - 1)

    in_specs = [pl.BlockSpec((1, 1, D, D_FF), lambda *idx: (layer, expert(*idx), 0, 0)),
                pl.BlockSpec((1, 1, D, D_FF), lambda *idx: (layer, expert(*idx), 0, 0)),
                pl.BlockSpec((1, 1, D_FF, D), lambda *idx: (layer, expert(*idx), 0, 0))]
    out_specs = [pl.BlockSpec((1, D, D_FF), lambda *idx: (expert(*idx), 0, 0)),
                 pl.BlockSpec((1, D, D_FF), lambda *idx: (expert(*idx), 0, 0)),
                 pl.BlockSpec((1, D_FF, D), lambda *idx: (expert(*idx), 0, 0))]
    out_shape = [jax.ShapeDtypeStruct((N_EXPERTS, D, D_FF), BF16), jax.ShapeDtypeStruct((N_EXPERTS, D, D_FF), BF16),
                 jax.ShapeDtypeStruct((N_EXPERTS, D_FF, D), BF16)]
    return in_specs, out_specs, out_shape


def _cast_weights(step, steps_per_expert, srcs, dsts):
    @pl.when(jnp.logical_and(step % steps_per_expert == 0, step // steps_per_expert < N_EXPERTS))
    def _():
        for src, dst in zip(srcs, dsts):
            dst[0] = src[0, 0].astype(BF16)


SCAN_STEPS_PER_EXPERT = 2
NATTEN_STEPS_PER_EXPERT = 4


def _scan_kernel(vf_ref, vb_ref, w_ref, gb_ref, lam_ref, wg_ref, wu_ref, wd_ref,
                 yf_ref, yb_ref, wg_out, wu_out, wd_out, h_ref, a_ref, b_ref):
    _cast_weights(pl.program_id(0), SCAN_STEPS_PER_EXPERT, (wg_ref, wu_ref, wd_ref), (wg_out, wu_out, wd_out))

    @pl.when(pl.program_id(0) == 0)
    def _():
        h_ref[...] = jnp.zeros_like(h_ref)

    for d, v_ref in enumerate((vf_ref, vb_ref)):
        v = v_ref[...].reshape(BATCH * TM, HALF)
        vb = v.astype(BF16)
        neg = -lam_ref[d]
        softplus = jnp.maximum(neg, 0.0) + jnp.log(1.0 + jnp.exp(-jnp.abs(neg)))
        rate = (-0.5 * LRU_C * LOG2E) * softplus
        for n in range(N_LRU_BLOCKS):
            sl = slice(n * LRU_BLOCK, (n + 1) * LRU_BLOCK)
            g = jnp.dot(vb[:, sl], w_ref[d, n], preferred_element_type=F32)
            tr = jnp.tanh(g[:, 0:LRU_BLOCK] + gb_ref[d, 0:1, sl])
            ti = jnp.tanh(g[:, LRU_BLOCK:2 * LRU_BLOCK] + gb_ref[d, 1:2, sl])
            a = jnp.exp2(rate[:, sl] * tr + rate[:, sl])
            a_ref[d, :, sl] = a
            b_ref[d, :, sl] = jnp.sqrt(1.0 - a * a) * ((0.5 * ti + 0.5) * v[:, sl])

    def body(s, hs):
        hs = list(hs)
        for u in range(SCAN_UNROLL):
            t = s * SCAN_UNROLL + u
            for d, y_ref in enumerate((yf_ref, yb_ref)):
                row = t if d == 0 else TM - 1 - t
                for bt in range(BATCH):
                    c = d * BATCH + bt
                    src = bt * TM + row
                    hs[c] = a_ref[d, pl.ds(src, 1), :] * hs[c] + b_ref[d, pl.ds(src, 1), :]
                    y_ref[bt, pl.ds(row, 1), :] = hs[c]
        return tuple(hs)

    init = tuple(h_ref[c:c + 1, :] for c in range(2 * BATCH))
    final = init
    for s in range(TM // SCAN_UNROLL):
        final = body(s, final)
    for c in range(2 * BATCH):
        h_ref[c:c + 1, :] = final[c]


def _scan(v, gate_w, gate_b, lam, moe_wg, moe_wu, moe_wd):
    steps = TILES_PER_SEQ + 1
    assert N_EXPERTS * SCAN_STEPS_PER_EXPERT <= steps
    fwd = pl.BlockSpec((BATCH, TM, HALF), lambda j: (0, j, 0))
    bwd = pl.BlockSpec((BATCH, TM, HALF), lambda j: (0, jnp.where(j == 0, 0, TILES_PER_SEQ + 1 - j), 0))
    shp = jax.ShapeDtypeStruct((BATCH, SEQ + CTX, HALF), F32)
    cast_in, cast_out, cast_shape = _cast_specs(0, SCAN_STEPS_PER_EXPERT, lambda j: j)
    return pl.pallas_call(
        _scan_kernel,
        grid=(steps,),
        in_specs=[fwd, bwd,
                  pl.BlockSpec((2, N_LRU_BLOCKS, LRU_BLOCK, 2 * LRU_BLOCK), lambda j: (0, 0, 0, 0)),
                  pl.BlockSpec((2, 2, HALF), lambda j: (0, 0, 0)),
                  pl.BlockSpec((2, 1, HALF), lambda j: (0, 0, 0))] + cast_in,
        out_specs=[fwd, bwd] + cast_out,
        out_shape=[shp, shp] + cast_shape,
        scratch_shapes=[pltpu.VMEM((8, HALF), F32), pltpu.VMEM((2, BATCH * TM, HALF), F32),
                        pltpu.VMEM((2, BATCH * TM, HALF), F32)],
        compiler_params=_params("arbitrary"),
        name="lru_scan",
    )(v, v, gate_w, gate_b, lam, moe_wg, moe_wu, moe_wd)


def _route(score, sel):
    rows = [sel[e:e + 1, :] for e in range(N_EXPERTS)]
    gbest = None
    gidx = None
    for g in range(N_GROUPS):
        top2 = None
        for p in range(PER_GROUP):
            for q in range(p + 1, PER_GROUP):
                s = rows[g * PER_GROUP + p] + rows[g * PER_GROUP + q]
                top2 = s if top2 is None else jnp.maximum(top2, s)
        if g == 0:
            gbest = top2
            gidx = jnp.zeros(top2.shape, jnp.int32)
        else:
            better = top2 > gbest
            gidx = jnp.where(better, g, gidx)
            gbest = jnp.where(better, top2, gbest)
    eint = lax.broadcasted_iota(jnp.int32, sel.shape, 0)
    eidx = eint.astype(F32)
    masked = jnp.where(jnp.right_shift(eint, 2) == gidx, sel, -jnp.inf)
    v1 = jnp.max(masked, axis=0, keepdims=True)
    i1 = jnp.min(jnp.where(masked == v1, eidx, float(N_EXPERTS)), axis=0, keepdims=True)
    masked2 = jnp.where(eidx == i1, -jnp.inf, masked)
    v2 = jnp.max(masked2, axis=0, keepdims=True)
    i2 = jnp.min(jnp.where(masked2 == v2, eidx, float(N_EXPERTS)), axis=0, keepdims=True)
    s1 = jnp.sum(jnp.where(eidx == i1, score, 0.0), axis=0, keepdims=True)
    s2 = jnp.sum(jnp.where(eidx == i2, score, 0.0), axis=0, keepdims=True)
    inv = 1.0 / (s1 + s2)
    return jnp.where(eidx == i1, s1 * inv, 0.0) + jnp.where(eidx == i2, s2 * inv, 0.0), gidx


def _group_ranks(gidx, carry_ref, tile):
    onehot = lax.broadcasted_iota(jnp.int32, (8, TM), 0) == gidx
    oh = jnp.where(onehot, 1.0, 0.0)
    before = lax.broadcasted_iota(jnp.int32, (TM, TM), 0) < lax.broadcasted_iota(jnp.int32, (TM, TM), 1)
    prefix = jnp.dot(oh.astype(BF16), jnp.where(before, 1.0, 0.0).astype(BF16), preferred_element_type=F32)
    carry = jnp.where(tile % (TMOE // TM) == 0, 0.0, carry_ref[...])
    rank = jnp.sum(jnp.where(onehot, prefix + carry[:, 0:1], 0.0), axis=0, keepdims=True)
    carry = carry + jnp.sum(oh, axis=1, keepdims=True)
    carry_ref[...] = carry
    return rank, carry


def _split_bf16(v):
    hi = v.astype(BF16)
    return hi, (v - hi.astype(F32)).astype(BF16)


def _outproj_step(m1, m2, x, mod_ref, wo_ref, g_ref, rw_ref, rb_ref, h_ref, n_ref, route_ref, cnt_ref, tok_ref, carry_ref,
                  nprev_ref):
    step = pl.program_id(0)
    n_hi, n_lo = _split_bf16(nprev_ref[...])
    w_hi, w_lo = _split_bf16(rw_ref[...])

    mod = mod_ref[0]
    mix = (jnp.dot(m1, wo_ref[0:HALF, :], preferred_element_type=F32)
           + jnp.dot(m2, wo_ref[HALF:2 * HALF, :], preferred_element_type=F32))

    logits = _nt_dot(w_hi, n_hi) + _nt_dot(w_hi, n_lo) + _nt_dot(w_lo, n_hi)

    h = x + mod[:, 2 * D:3 * D] * mix
    h_ref[...] = h
    n = _rms_mod(h, g_ref[...], mod[:, 4 * D:5 * D], mod[:, 3 * D:4 * D])
    n_ref[...] = n.astype(n_ref.dtype)
    nprev_ref[...] = n

    score = _sigmoid(logits)
    comb, gidx = _route(score, score + rb_ref[...])
    rank, counts = _group_ranks(gidx, carry_ref, step - 1)
    record = jnp.concatenate([comb, gidx.astype(F32), rank, jnp.zeros((ROUTE_LANES - ROUTE_RANK - 1, TM), F32)], axis=0)
    route_ref[...] = record[0:ROUTE_ROWS]
    tok_ref[...] = record.T
    cnt_ref[0] = counts


def _init_pipeline(carry_ref, nprev_ref):
    @pl.when(pl.program_id(0) == 0)
    def _():
        nprev_ref[...] = jnp.zeros_like(nprev_ref)
        carry_ref[...] = jnp.zeros_like(carry_ref)


def _outproj0_kernel(a_ref, yf_ref, yb_ref, gg_ref, x_ref, c_ref, mod_ref, wo_ref, g_ref, rw_ref, rb_ref,
                     h_ref, n_ref, route_ref, cnt_ref, tok_ref, carry_ref, nprev_ref):
    _init_pipeline(carry_ref, nprev_ref)
    tile = jnp.minimum(pl.program_id(0), N_TILES - 1)
    m2 = ((yf_ref[0] + yb_ref[0]) * gg_ref[...]).astype(BF16)
    _outproj_step(a_ref[...], m2, _token_tile(x_ref, c_ref, tile), mod_ref, wo_ref, g_ref, rw_ref, rb_ref,
                  h_ref, n_ref, route_ref, cnt_ref, tok_ref, carry_ref, nprev_ref)


def _outproj1_kernel(cx_ref, cxp_ref, cxn_ref, bg_ref, att_ref, cw_ref, x_ref, mod_ref, wo_ref, g_ref, rw_ref, rb_ref,
                     h_ref, n_ref, route_ref, cnt_ref, tok_ref, buf_ref, carry_ref, nprev_ref):
    _init_pipeline(carry_ref, nprev_ref)
    tile = jnp.minimum(pl.program_id(0), N_LAT_TILES - 1)
    _fill_padded(buf_ref, cxp_ref, cx_ref, cxn_ref, HALO_S, tile)
    conv = _depthwise(buf_ref, cw_ref, CONV_C, HALO_S - 1, TM, 0)
    m1 = (bg_ref[...] * conv).astype(BF16)
    _outproj_step(m1, att_ref[...], x_ref[...], mod_ref, wo_ref, g_ref, rw_ref, rb_ref,
                  h_ref, n_ref, route_ref, cnt_ref, tok_ref, carry_ref, nprev_ref)


def _outproj_common(n_tiles):
    hold = n_tiles - 1
    tile = _held(hold)
    routed = lambda i: jnp.maximum(i - 1, 0)
    in_specs = [pl.BlockSpec((1, 1, 6 * D), _mod_row(TM, hold)),
                pl.BlockSpec((D, D), lambda i: (0, 0)),
                pl.BlockSpec((1, D), lambda i: (0, 0)),
                pl.BlockSpec((N_EXPERTS, D), lambda i: (0, 0)),
                pl.BlockSpec((N_EXPERTS, 1), lambda i: (0, 0))]
    out_specs = [pl.BlockSpec((TM, D), lambda i: (tile(i), 0)),
                 pl.BlockSpec((TM, D), lambda i: (tile(i), 0)),
                 pl.BlockSpec((ROUTE_ROWS, TM), lambda i: (0, routed(i))),
                 pl.BlockSpec((1, 8, 128), lambda i: (routed(i), 0, 0)),
                 pl.BlockSpec((TM, ROUTE_LANES), lambda i: (routed(i), 0))]
    rows = n_tiles * TM
    out_shape = [jax.ShapeDtypeStruct((rows, D), F32), jax.ShapeDtypeStruct((rows, D), BF16),
                 jax.ShapeDtypeStruct((ROUTE_ROWS, rows), F32), jax.ShapeDtypeStruct((n_tiles, 8, 128), F32),
                 jax.ShapeDtypeStruct((rows, ROUTE_LANES), F32)]
    scratch = [pltpu.VMEM((8, 128), F32), pltpu.VMEM((TM, D), F32)]
    return in_specs, out_specs, out_shape, scratch


def _outproj0(a, yf, yb, gg, x, c, mod, wo, g, rw_t, rb):
    hold = N_TILES - 1
    tok = pl.BlockSpec((TM, HALF), lambda i: (_held(hold)(i), 0))
    scan_tok = pl.BlockSpec((1, TM, HALF), functools.partial(_seq_major, hold=hold))
    common_in, out_specs, out_shape, scratch = _outproj_common(N_TILES)
    return pl.pallas_call(
        _outproj0_kernel,
        grid=(N_TILES + 1,),
        in_specs=[tok, scan_tok, scan_tok, tok] + _token_specs(hold) + common_in,
        out_specs=out_specs,
        out_shape=out_shape,
        scratch_shapes=scratch,
        compiler_params=_params("arbitrary"),
        name="outproj0",
    )(a, yf, yb, gg, x, c, mod, wo, g, rw_t, rb)


def _outproj1(cx, bg, att, conv_w, x, mod, wo, g, rw_t, rb):
    hold = N_LAT_TILES - 1
    tok = pl.BlockSpec((TM, HALF), lambda i: (_held(hold)(i), 0))
    ps, ns = _halo_specs(HALO_S, hold)
    common_in, out_specs, out_shape, scratch = _outproj_common(N_LAT_TILES)
    return pl.pallas_call(
        _outproj1_kernel,
        grid=(N_LAT_TILES + 1,),
        in_specs=[tok, ps, ns, tok, tok, pl.BlockSpec((CONV_C, 8, HALF), lambda i: (0, 0, 0)),
                  pl.BlockSpec((TM, D), lambda i: (_held(hold)(i), 0))] + common_in,
        out_specs=out_specs,
        out_shape=out_shape,
        scratch_shapes=[pltpu.VMEM((TM + 2 * HALO_S, HALF), F32)] + scratch,
        compiler_params=_params("arbitrary"),
        name="outproj1",
    )(cx, cx, cx, bg, att, _sublane_replicated(conv_w), x, mod, wo, g, rw_t, rb)


def _moe_kernel(cnt_ref, n_ref, rt_ref, r_ref, wg_ref, wu_ref, wd_ref, h_ref, mod_ref, fg_ref, o_ref,
                hid_ref, *, final_norm, first_tile, subtiles):
    i = pl.program_id(0)
    g = pl.program_id(1)

    @pl.when(g == 0)
    def _():
        o_ref[...] = jnp.zeros_like(o_ref)

    gf = g.astype(F32)
    slot_row = lax.broadcasted_iota(jnp.int32, (CAP, TMOE), 0).astype(F32)
    slot_col = lax.broadcasted_iota(jnp.int32, (TMOE, CAP), 1).astype(F32)

    counts = [cnt_ref[(first_tile + i * subtiles + s) * N_GROUPS + g] for s in range(subtiles)]
    starts = [sum(counts[:s], jnp.int32(0)) for s in range(subtiles)]
    total = starts[-1] + counts[-1]
    tile_rows = [slice(s * TMOE, (s + 1) * TMOE) for s in range(subtiles)]

    def run_chunk(k, sources):
        base = k * CAP
        sels, shifts = [], []
        xg = None
        for s in sources:
            shift = (starts[s] - base).astype(F32)
            in_group = rt_ref[ROUTE_GROUP:ROUTE_GROUP + 1, tile_rows[s]] == gf
            slot = jnp.where(in_group, rt_ref[ROUTE_RANK:ROUTE_RANK + 1, tile_rows[s]] + shift, -1.0)
            sel = slot_row == slot
            part = jnp.dot(jnp.where(sel, 1.0, 0.0).astype(BF16), n_ref[tile_rows[s], :], preferred_element_type=F32)
            xg = part if xg is None else xg + part
            sels.append(sel)
            shifts.append(shift)
        xg = xg.astype(BF16)
        for j in range(PER_GROUP):
            cw = None
            for s, sel in zip(sources, sels):
                comb_row = rt_ref[pl.ds(g * PER_GROUP + j, 1), tile_rows[s]]
                part = jnp.sum(jnp.where(sel, comb_row, 0.0), axis=1, keepdims=True)
                cw = part if cw is None else cw + part
            hid = (_silu(jnp.dot(xg, wg_ref[j], preferred_element_type=F32))
                   * jnp.dot(xg, wu_ref[j], preferred_element_type=F32) * cw)
            hid_ref[:, j * D_FF:(j + 1) * D_FF] = hid.astype(BF16)
        y = jnp.dot(hid_ref[...], wd_ref[...].reshape(PER_GROUP * D_FF, D),
                    preferred_element_type=F32).astype(BF16)
        for s, shift in zip(sources, shifts):
            in_group = r_ref[tile_rows[s], ROUTE_GROUP:ROUTE_GROUP + 1] == gf
            slot = r_ref[tile_rows[s], ROUTE_RANK:ROUTE_RANK + 1] + shift
            back = jnp.where(jnp.logical_and(in_group, slot_col == slot), 1.0, 0.0).astype(BF16)
            o_ref[tile_rows[s], :] += jnp.dot(back, y, preferred_element_type=F32)

    def loop(lo, hi, sources):
        def body(k, carry):
            run_chunk(k, sources)
            return carry
        lax.fori_loop(lo, hi, body, 0)

    for s in range(subtiles):
        end = starts[s] + counts[s]
        first_inside = (starts[s] + (CAP - 1)) // CAP
        if s + 1 < subtiles:
            loop(first_inside, end // CAP, [s])

            @pl.when(end % CAP != 0)
            def _(s=s, end=end):
                run_chunk(end // CAP, [s, s + 1])
        else:
            loop(first_inside, (end + (CAP - 1)) // CAP, [s])

    @pl.when(g == N_GROUPS - 1)
    def _():
        out = h_ref[...] + mod_ref[0][:, 5 * D:6 * D] * o_ref[...]
        if final_norm:
            out = out * lax.rsqrt(jnp.mean(out * out, axis=-1, keepdims=True) + EPS) * fg_ref[...]
        o_ref[...] = out


def _moe_chunks(counts):
    sub = TMOE // TM
    return counts[sub - 1::sub, :N_GROUPS, 0].astype(jnp.int32).reshape(-1)


def _moe(n_chunks, n, route_t, route, wg, wu, wd, h, mod, final_g, *, first_tile, n_tiles, subtiles, final_norm,
         name):
    assert subtiles in (1, 2) and first_tile % subtiles == 0 and n_tiles % subtiles == 0
    step = subtiles * TMOE
    first = first_tile // subtiles
    mod_row = lambda i, g, nch: (jnp.minimum(((first + i) * step) // SEQ, BATCH), 0, 0)
    tok = pl.BlockSpec((step, D), lambda i, g, nch: (first + i, 0))
    grid_spec = pltpu.PrefetchScalarGridSpec(
        num_scalar_prefetch=1,
        grid=(n_tiles // subtiles, N_GROUPS),
        in_specs=[tok,
                  pl.BlockSpec((ROUTE_ROWS, step), lambda i, g, nch: (0, first + i)),
                  pl.BlockSpec((step, ROUTE_LANES), lambda i, g, nch: (first + i, 0)),
                  pl.BlockSpec((PER_GROUP, D, D_FF), lambda i, g, nch: (g, 0, 0)),
                  pl.BlockSpec((PER_GROUP, D, D_FF), lambda i, g, nch: (g, 0, 0)),
                  pl.BlockSpec((PER_GROUP, D_FF, D), lambda i, g, nch: (g, 0, 0)),
                  tok,
                  pl.BlockSpec((1, 1, 6 * D), mod_row),
                  pl.BlockSpec((1, D), lambda i, g, nch: (0, 0))],
        out_specs=pl.BlockSpec((step, D), lambda i, g, nch: (i, 0)),
        scratch_shapes=[pltpu.VMEM((CAP, PER_GROUP * D_FF), BF16)])
    return pl.pallas_call(
        functools.partial(_moe_kernel, final_norm=final_norm, first_tile=first_tile, subtiles=subtiles),
        grid_spec=grid_spec,
        out_shape=jax.ShapeDtypeStruct((n_tiles * TMOE, D), F32),
        compiler_params=_params("parallel", "arbitrary", vmem=VMEM_LIMIT_MOE),
        name=name,
    )(n_chunks, n, route_t, route, wg, wu, wd, h, mod, final_g)


def _inproj1_kernel(x_ref, c_ref, mod_ref, g_ref, w_ref, wvt_ref, cx_ref, bg_ref, q_ref, k_ref, vt_ref):
    mod = mod_ref[0]
    n = _rms_mod(_token_tile(x_ref, c_ref, rows=TIN), g_ref[...], mod[:, D:2 * D], mod[:, 0:D]).astype(BF16)

    def proj(c):
        return jnp.dot(n, w_ref[:, c * HALF:(c + 1) * HALF], preferred_element_type=F32)

    conv_in, conv_gate = proj(0), proj(2)
    out_gate = proj(1)
    cx_ref[...] = conv_gate * conv_in
    q = proj(3)
    bg_ref[...] = out_gate
    k = proj(4)
    q_ref[...] = (q * (HEAD_DIM ** -0.5 * LOG2E)).astype(BF16)
    vt = _nt_dot(wvt_ref[...], n)
    k_ref[...] = k.astype(BF16)
    vt_ref[...] = vt.astype(BF16)


def _inproj1(x, c, mod, g, w):
    tok = pl.BlockSpec((TIN, HALF), lambda i: (i, 0))
    f = jax.ShapeDtypeStruct((N_TOK, HALF), F32)
    h = jax.ShapeDtypeStruct((N_TOK, HALF), BF16)
    return pl.pallas_call(
        _inproj1_kernel,
        grid=(N_TOK // TIN,),
        in_specs=_token_specs(rows=TIN) + [
                  pl.BlockSpec((1, 1, 6 * D), _mod_row(TIN)),
                  pl.BlockSpec((1, D), lambda i: (0, 0)),
                  pl.BlockSpec((D, 5 * HALF), lambda i: (0, 0)),
                  pl.BlockSpec((HALF, D), lambda i: (0, 0))],
        out_specs=[tok] * 4 + [pl.BlockSpec((HALF, TIN), lambda i: (0, i))],
        out_shape=[f, f, h, h, jax.ShapeDtypeStruct((HALF, N_TOK), BF16)],
        compiler_params=_params("parallel"),
        name="inproj1",
    )(x, c, mod, g, w[:, :5 * HALF], w[:, 5 * HALF:].T)


def _natten_kernel(q_ref, kp_ref, kc_ref, kn_ref, vp_ref, vc_ref, vn_ref, kx_ref, vx_ref, bias_ref,
                   wg_ref, wu_ref, wd_ref, o_ref, wg_out, wu_out, wd_out):
    _cast_weights(pl.program_id(0) * TILES_PER_SEQ + pl.program_id(1), NATTEN_STEPS_PER_EXPERT,
                  (wg_ref, wu_ref, wd_ref), (wg_out, wu_out, wd_out))
    pair = 2 * HEAD_DIM
    low = lax.broadcasted_iota(jnp.int32, (TM, pair), 1) < HEAD_DIM
    head_mask = [low.astype(F32).astype(BF16), jnp.logical_not(low).astype(F32).astype(BF16)]
    top = lax.broadcasted_iota(jnp.int32, (pair, TM), 0) < HEAD_DIM
    own_rows = [top.astype(F32).astype(BF16), jnp.logical_not(top).astype(F32).astype(BF16)]
    def raw_scores(head):
        g, hh = divmod(head, 2)
        sl = slice(pair * g, pair * (g + 1))
        qm = q_ref[:, sl] * head_mask[hh]
        return [_nt_dot(qm, k_ref[:, sl]) for k_ref in (kp_ref, kc_ref, kn_ref, kx_ref)]

    def biased(head, s):
        s = [s[t] + bias_ref[0, head, :, t * TM:(t + 1) * TM] for t in range(3)] + [s[3]]
        m = jnp.max(jnp.maximum(jnp.maximum(s[0], s[1]), jnp.maximum(s[2], s[3])), axis=-1, keepdims=True)
        return s, m

    def attend(head, s, m):
        g, hh = divmod(head, 2)
        sl = slice(pair * g, pair * (g + 1))
        p = jnp.concatenate([jnp.exp2(st - m).astype(BF16) for st in s], axis=1)
        lhs = jnp.concatenate([vt_ref[sl, :] * own_rows[hh] + own_rows[1 - hh]
                               for vt_ref in (vp_ref, vc_ref, vn_ref, vx_ref)], axis=1)
        acc = _nt_dot(lhs, p)
        if hh == 0:
            return acc[0:HEAD_DIM] * (1.0 / acc[HEAD_DIM:HEAD_DIM + 1])
        return acc[HEAD_DIM:pair] * (1.0 / acc[0:1])

    pending = biased(0, raw_scores(0))
    halves = []
    for head in range(N_HEADS):
        current = pending
        if head + 1 < N_HEADS:
            pending = biased(head + 1, raw_scores(head + 1))
        halves.append(attend(head, *current))
        if head % 2 == 1:
            sl = slice(pair * (head // 2), pair * (head // 2 + 1))
            o_ref[:, sl] = jnp.concatenate(halves, axis=0).T.astype(o_ref.dtype)
            halves = []


def _natten_bias(rpb):
    n_rows = SEQ // GRID_W
    n_dr, n_dc = 2 * WIN_H - 1, 2 * WIN_W - 1
    i = np.arange(ROWS_Q)
    j = np.arange(3 * ROWS_Q)
    col = np.arange(GRID_W)
    col_start = np.clip(col - WIN_W // 2, 0, GRID_W - WIN_W)
    col_ok = (col[None, :] >= col_start[:, None]) & (col[None, :] < col_start[:, None] + WIN_W)
    col_idx = col[None, :] - col[:, None] + (WIN_W - 1)
    onehot = ((col_idx[None] == np.arange(n_dc)[:, None, None]) & col_ok[None]).astype(np.float32)
    col_exp = jnp.dot(rpb.reshape(N_HEADS * n_dr, n_dc).astype(F32), onehot.reshape(n_dc, GRID_W * GRID_W),
                      precision=lax.Precision.HIGHEST).reshape(N_HEADS, n_dr, GRID_W, GRID_W)
    col_exp = jnp.where(col_ok[None, None], col_exp * LOG2E, NEG)
    masked = jnp.full((N_HEADS, GRID_W, GRID_W), NEG, F32)
    kinds = []
    for r0 in (0, ROWS_Q, n_rows - ROWS_Q):
        r = r0 + i
        kr = r0 - ROWS_Q + j
        r_start = np.clip(r - WIN_H // 2, 0, n_rows - WIN_H)
        row_ok = ((kr[None, :] >= r_start[:, None]) & (kr[None, :] < r_start[:, None] + WIN_H)
                  & (kr[None, :] >= 0) & (kr[None, :] < n_rows))
        row_idx = kr[None, :] - r[:, None] + (WIN_H - 1)
        rows = [jnp.concatenate([col_exp[:, row_idx[qi, kj]] if row_ok[qi, kj] else masked for kj in range(3 * ROWS_Q)],
                                axis=-1) for qi in range(ROWS_Q)]
        kinds.append(jnp.concatenate(rows, axis=1))
    return jnp.stack(kinds)


def _natten(q, k, vt, bias, moe_wg, moe_wu, moe_wd):
    assert N_EXPERTS * NATTEN_STEPS_PER_EXPERT <= BATCH * TILES_PER_SEQ
    cast_in, cast_out, cast_shape = _cast_specs(1, NATTEN_STEPS_PER_EXPERT, lambda b, i: b * TILES_PER_SEQ + i)

    def tile(b, i, off):
        return b * TILES_PER_SEQ + jnp.clip(i + off, 0, TILES_PER_SEQ - 1)

    def lat(off):
        return pl.BlockSpec((TM, HALF), lambda b, i: (tile(b, i, off), 0))

    def lat_t(off):
        return pl.BlockSpec((HALF, TM), lambda b, i: (0, tile(b, i, off)))

    ctx = pl.BlockSpec((TM, HALF), lambda b, i: (N_LAT_TILES + b, 0))
    ctx_t = pl.BlockSpec((HALF, TM), lambda b, i: (0, N_LAT_TILES + b))

    def kind(b, i):
        return (jnp.where(i == 0, 0, jnp.where(i == TILES_PER_SEQ - 1, 2, 1)), 0, 0, 0)

    return pl.pallas_call(
        _natten_kernel,
        grid=(BATCH, TILES_PER_SEQ),
        in_specs=[lat(0), lat(-1), lat(0), lat(1), lat_t(-1), lat_t(0), lat_t(1), ctx, ctx_t,
                  pl.BlockSpec((1, N_HEADS, TM, 3 * TM), kind)] + cast_in,
        out_specs=[lat(0)] + cast_out,
        out_shape=[jax.ShapeDtypeStruct((N_LAT, HALF), BF16)] + cast_shape,
        compiler_params=_params("arbitrary", "arbitrary"),
        name="natten",
    )(q, k, k, k, vt, vt, vt, k, vt, bias, moe_wg, moe_wu, moe_wd)


def kernel(x, c, ctx, c_ctx, ada_w, ada_b, norm_mix_g, norm_ffn_g, w_out, ab_w_in, a_dw_w, a_dw_b, a_ln_g, a_ln_b,
           b_conv_w, b_conv_b, b_gate_w, b_gate_b, b_lambda, cd_w_in, c_conv_w, d_rpb, router_w, router_bias,
           moe_w_gate, moe_w_up, moe_w_down, final_g):
    x_lat = x.reshape(N_LAT, D)
    x_ctx = ctx.reshape(BATCH * CTX, D)
    cond = jnp.concatenate([c, c_ctx[None], jnp.zeros((8 - BATCH - 1, D), F32)], axis=0)
    mod = _modulation(cond, ada_w, ada_b)
    mod0 = mod[0].reshape(8, 1, 6 * D)
    mod1 = mod[1].reshape(8, 1, 6 * D)

    wo = w_out.astype(BF16)
    lat_tiles = N_LAT // TMOE
    ctx_tiles = BATCH * CTX // TMOE
    rw_t = router_w.T
    rb = router_bias.reshape(N_EXPERTS, 1)
    fg = final_g.reshape(1, D)

    ua, ub, gg = _inproj0(x_lat, x_ctx, mod0, norm_mix_g[0].reshape(1, D), ab_w_in[0].astype(BF16))
    a_out, v = _conv0(ua, ub, a_dw_w[0], a_dw_b[0], a_ln_g[0], a_ln_b[0], b_conv_w[0], b_conv_b[0])
    gw = b_gate_w[0]
    gate_w = (0.5 * jnp.concatenate([gw[:, 0], gw[:, 1]], axis=-1)).astype(BF16)
    yf, yb, wg0, wu0, wd0 = _scan(v, gate_w, 0.5 * b_gate_b[0], b_lambda[0].reshape(2, 1, HALF),
                                  moe_w_gate, moe_w_up, moe_w_down)
    h1, n2, route0, cnt0, tok0 = _outproj0(a_out, yf, yb, gg, x_lat, x_ctx, mod0, wo[0],
                                           norm_ffn_g[0].reshape(1, D), rw_t, rb)
    moe0 = functools.partial(_moe, _moe_chunks(cnt0), n2, route0, tok0, wg0, wu0, wd0, h1, mod0, fg,
                             final_norm=False)
    h2_lat = moe0(first_tile=0, n_tiles=lat_tiles, subtiles=2, name="moe_lat")
    h2_ctx = moe0(first_tile=lat_tiles, n_tiles=ctx_tiles, subtiles=1, name="moe_ctx")

    cx, bg, q, k, vv = _inproj1(h2_lat, h2_ctx, mod1, norm_mix_g[1].reshape(1, D), cd_w_in[0].astype(BF16))
    att, wg1, wu1, wd1 = _natten(q, k, vv, _natten_bias(d_rpb[0]), moe_w_gate, moe_w_up, moe_w_down)
    h3, n4, route1, cnt1, tok1 = _outproj1(cx, bg, att, c_conv_w[0], h2_lat, mod1, wo[1],
                                           norm_ffn_g[1].reshape(1, D), rw_t, rb)
    out = _moe(_moe_chunks(cnt1), n4, route1, tok1, wg1, wu1, wd1, h3, mod1, fg,
               first_tile=0, n_tiles=lat_tiles, subtiles=2, final_norm=True, name="moe_final")
    return out.reshape(BATCH, SEQ, D)
```

```python
import functools
import math

import jax
import jax.numpy as jnp
import numpy as np
from jax import lax
from jax.experimental import pallas as pl
from jax.experimental.pallas import tpu as pltpu

F32 = jnp.float32
BF16 = jnp.bfloat16

D = 1024
BATCH = 2
SEQ = 8192
CTX = 256
GRID_W = 64
N_LAT = BATCH * SEQ
N_TOK = N_LAT + BATCH * CTX
HALF = 512
CONV_A = 31
CONV_B = 4
CONV_C = 3
LRU_BLOCK = 128
N_LRU_BLOCKS = HALF // LRU_BLOCK
LRU_C = 8.0
HEAD_DIM = 64
N_HEADS = HALF // HEAD_DIM
WIN_H = 8
WIN_W = 16
N_EXPERTS = 16
N_GROUPS = 4
PER_GROUP = N_EXPERTS // N_GROUPS
D_FF = 512
EPS = 1e-6
NEG = -1e30
LOG2E = math.log2(math.e)

TM = 256
TILES_PER_SEQ = SEQ // TM
N_LAT_TILES = N_LAT // TM
N_TILES = N_TOK // TM
TIN = 512
TMOE = 512
CAP = 128
ROUTE_ROWS = 24
ROUTE_LANES = 128
ROUTE_GROUP = 16
ROUTE_RANK = 17
HALO_A = 16
HALO_S = 8
ROWS_Q = TM // GRID_W
VMEM_LIMIT = 48 * 1024 * 1024
VMEM_LIMIT_MOE = 56 * 1024 * 1024


def _params(*sem, vmem=VMEM_LIMIT):
    return pltpu.CompilerParams(dimension_semantics=sem, vmem_limit_bytes=vmem)


def _sigmoid(x):
    return 0.5 * jnp.tanh(0.5 * x) + 0.5


def _silu(x):
    return x * _sigmoid(x)


def _gelu_tanh(x):
    return 0.5 * x * (1.0 + jnp.tanh(0.7978845608028654 * (x + 0.044715 * (x * x * x))))


def _rms_mod(x, g, scale, shift):
    y = x * lax.rsqrt(jnp.mean(x * x, axis=-1, keepdims=True) + EPS) * g
    return y * (1.0 + scale) + shift


def _nt_dot(a, b):
    return lax.dot_general(a, b, (((1,), (1,)), ((), ())), preferred_element_type=F32)


def _held(hold):
    return (lambda i: i) if hold is None else (lambda i: jnp.minimum(i, hold))


def _mod_row(tile_rows, hold=None):
    per_seq = SEQ // tile_rows
    tile = _held(hold)
    return lambda i: (jnp.minimum(tile(i) // per_seq, BATCH), 0, 0)


def _mod_kernel(c_ref, w_ref, b_ref, o_ref):
    c = c_ref[...]
    s = _silu(c).astype(BF16)
    o_ref[0] = jnp.dot(s, w_ref[0].astype(BF16), preferred_element_type=F32) + b_ref[0]


def _modulation(cond, ada_w, ada_b):
    depth = ada_w.shape[0]
    nb = 1536
    return pl.pallas_call(
        _mod_kernel,
        grid=(depth, 6 * D // nb),
        in_specs=[pl.BlockSpec((8, D), lambda l, j: (0, 0)),
                  pl.BlockSpec((1, D, nb), lambda l, j: (l, 0, j)),
                  pl.BlockSpec((1, 1, nb), lambda l, j: (l, 0, j))],
        out_specs=pl.BlockSpec((1, 8, nb), lambda l, j: (l, 0, j)),
        out_shape=jax.ShapeDtypeStruct((depth, 8, 6 * D), F32),
        compiler_params=_params("parallel", "parallel"),
        name="modulation",
    )(cond, ada_w, ada_b.reshape(depth, 1, 6 * D))


def _token_specs(hold=None, rows=TM):
    tile = _held(hold)
    lat_tiles = N_LAT // rows
    lat = pl.BlockSpec((rows, D), lambda i: (jnp.minimum(tile(i), lat_tiles - 1), 0))
    ctx = pl.BlockSpec((rows, D), lambda i: (jnp.maximum(tile(i) - lat_tiles, 0), 0))
    return [lat, ctx]


def _token_tile(lat_ref, ctx_ref, tile=None, rows=TM):
    tile = pl.program_id(0) if tile is None else tile
    return jnp.where(tile < N_LAT // rows, lat_ref[...], ctx_ref[...])


def _inproj0_kernel(x_ref, c_ref, mod_ref, g_ref, w_ref, ua_ref, ub_ref, gg_ref):
    mod = mod_ref[0]
    n = _rms_mod(_token_tile(x_ref, c_ref, rows=TIN), g_ref[...], mod[:, D:2 * D], mod[:, 0:D]).astype(BF16)

    def proj(c):
        return jnp.dot(n, w_ref[:, c * HALF:(c + 1) * HALF], preferred_element_type=F32)

    value, gate = proj(0), proj(1)
    recur = proj(2)
    ua_ref[...] = value * _sigmoid(gate)
    gelu_in = proj(3)
    ub_ref[...] = recur
    gg_ref[...] = _gelu_tanh(gelu_in).astype(gg_ref.dtype)


def _inproj0(x, c, mod, g, w):
    tok = pl.BlockSpec((TIN, HALF), lambda i: (i, 0))
    shp = jax.ShapeDtypeStruct((N_TOK, HALF), F32)
    return pl.pallas_call(
        _inproj0_kernel,
        grid=(N_TOK // TIN,),
        in_specs=_token_specs(rows=TIN) + [
                  pl.BlockSpec((1, 1, 6 * D), _mod_row(TIN)),
                  pl.BlockSpec((1, D), lambda i: (0, 0)),
                  pl.BlockSpec((D, 4 * HALF), lambda i: (0, 0))],
        out_specs=[tok, tok, tok],
        out_shape=[shp, shp, jax.ShapeDtypeStruct((N_TOK, HALF), BF16)],
        compiler_params=_params("parallel"),
        name="inproj0",
    )(x, c, mod, g, w)


def _halo_specs(halo, hold=None):
    per_tile = TM // halo
    last = N_TOK // halo - 1
    tile = _held(hold)
    prev = pl.BlockSpec((halo, HALF), lambda i: (jnp.maximum(tile(i) * per_tile - 1, 0), 0))
    nxt = pl.BlockSpec((halo, HALF), lambda i: (jnp.minimum((tile(i) + 1) * per_tile, last), 0))
    return prev, nxt


def _seq_edges(i):
    is_ctx = i >= N_LAT_TILES
    first = jnp.logical_or(is_ctx, i % TILES_PER_SEQ == 0)
    last = jnp.logical_or(is_ctx, i % TILES_PER_SEQ == TILES_PER_SEQ - 1)
    return first, last


def _fill_padded(buf_ref, prev_ref, cur_ref, next_ref, halo, tile=None):
    first, last = _seq_edges(pl.program_id(0) if tile is None else tile)
    buf_ref[0:halo, :] = jnp.where(first, 0.0, prev_ref[...])
    buf_ref[halo:halo + TM, :] = cur_ref[...]
    buf_ref[halo + TM:halo + TM + halo, :] = jnp.where(last, 0.0, next_ref[...])


def _tap_weight(w_ref, k, rows):
    return jnp.concatenate([w_ref[k]] * (rows // 8), axis=0)


def _sublane_replicated(w):
    return jnp.broadcast_to(w[:, None, :], (w.shape[0], 8, w.shape[1]))


def _depthwise(buf_ref, w_ref, taps, first_off, rows, row0):
    acc = None
    for k in range(taps):
        term = _tap_weight(w_ref, k, rows) * buf_ref[row0 + first_off + k:row0 + first_off + k + rows, :]
        acc = term if acc is None else acc + term
    return acc


CONV_ROWS = 32


SHIFT_ROWS = TM + 2 * HALO_A - 8


def _conv0_kernel(ua_ref, uap_ref, uan_ref, ub_ref, ubp_ref, ubn_ref,
                  dww_ref, dwb_ref, lng_ref, lnb_ref, cw_ref, cb_ref,
                  a_ref, v_ref, bufa_ref, bufb_ref, shift_ref):
    _fill_padded(bufa_ref, uap_ref, ua_ref, uan_ref, HALO_A)
    _fill_padded(bufb_ref, ubp_ref, ub_ref, ubn_ref, HALO_S)
    for s in range(1, 8):
        shift_ref[s - 1] = bufa_ref[s:s + SHIFT_ROWS, :]
    first = HALO_A - CONV_A // 2
    for r in range(TM // CONV_ROWS):
        row0 = r * CONV_ROWS
        u = None
        for k in range(CONV_A):
            off = first + k
            base = row0 + off - off % 8
            rows = bufa_ref[base:base + CONV_ROWS, :] if off % 8 == 0 else shift_ref[off % 8 - 1, base:base + CONV_ROWS, :]
            term = _tap_weight(dww_ref, k, CONV_ROWS) * rows
            u = term if u is None else u + term
        u = u + dwb_ref[...]
        mu = jnp.mean(u, axis=-1, keepdims=True)
        uc = u - mu
        var = jnp.mean(uc * uc, axis=-1, keepdims=True)
        y = uc * lax.rsqrt(var + EPS) * lng_ref[...] + lnb_ref[...]
        a_ref[row0:row0 + CONV_ROWS, :] = _silu(y).astype(a_ref.dtype)
        v = _depthwise(bufb_ref, cw_ref, CONV_B, HALO_S - 2, CONV_ROWS, row0) + cb_ref[...]
        v_ref[0, row0:row0 + CONV_ROWS, :] = v


def _seq_major(i, hold=None):
    i = _held(hold)(i)
    is_lat = i < N_LAT_TILES
    return (jnp.where(is_lat, i // TILES_PER_SEQ, i - N_LAT_TILES), jnp.where(is_lat, 1 + i % TILES_PER_SEQ, 0), 0)


def _conv0(ua, ub, dw_w, dw_b, ln_g, ln_b, conv_w, conv_b):
    tok = pl.BlockSpec((TM, HALF), lambda i: (i, 0))
    pa, na = _halo_specs(HALO_A)
    ps, ns = _halo_specs(HALO_S)
    vec = pl.BlockSpec((1, HALF), lambda i: (0, 0))
    return pl.pallas_call(
        _conv0_kernel,
        grid=(N_TILES,),
        in_specs=[tok, pa, na, tok, ps, ns,
                  pl.BlockSpec((CONV_A, 8, HALF), lambda i: (0, 0, 0)), vec, vec, vec,
                  pl.BlockSpec((CONV_B, 8, HALF), lambda i: (0, 0, 0)), vec],
        out_specs=[tok, pl.BlockSpec((1, TM, HALF), _seq_major)],
        out_shape=[jax.ShapeDtypeStruct((N_TOK, HALF), BF16),
                   jax.ShapeDtypeStruct((BATCH, SEQ + CTX, HALF), F32)],
        scratch_shapes=[pltpu.VMEM((TM + 2 * HALO_A, HALF), F32), pltpu.VMEM((TM + 2 * HALO_S, HALF), F32),
                        pltpu.VMEM((7, SHIFT_ROWS, HALF), F32)],
        compiler_params=_params("parallel"),
        name="conv0",
    )(ua, ua, ua, ub, ub, ub, _sublane_replicated(dw_w), dw_b.reshape(1, HALF), ln_g.reshape(1, HALF),
      ln_b.reshape(1, HALF), _sublane_replicated(conv_w), conv_b.reshape(1, HALF))


SCAN_UNROLL = 8


CAST_STEPS = 2 * N_EXPERTS


def _cast_specs(layer, step_of):
    def half(*idx):
        s = jnp.minimum(step_of(*idx), CAST_STEPS - 1)
        return s // 2, s % 2

    def src(*idx):
        e, h = half(*idx)
        return (layer, e, h, 0)

    def dst(*idx):
        e, h = half(*idx)
        return (e, h, 0)

    in_specs = [pl.BlockSpec((1, 1, D // 2, D_FF), src), pl.BlockSpec((1, 1, D // 2, D_FF), src),
                pl.BlockSpec((1, 1, D_FF // 2, D), src)]
    out_specs = [pl.BlockSpec((1, D // 2, D_FF), dst), pl.BlockSpec((1, D // 2, D_FF), dst),
                 pl.BlockSpec((1, D_FF // 2, D), dst)]
    out_shape = [jax.ShapeDtypeStruct((N_EXPERTS, D, D_FF), BF16), jax.ShapeDtypeStruct((N_EXPERTS, D, D_FF), BF16),
                 jax.ShapeDtypeStruct((N_EXPERTS, D_FF, D), BF16)]
    return in_specs, out_specs, out_shape


def _cast_weights(srcs, dsts):
    for src, dst in zip(srcs, dsts):
        dst[0] = src[0, 0].astype(BF16)


def _scan_kernel(vf_ref, vb_ref, w_ref, gb_ref, lam_ref, wg_ref, wu_ref, wd_ref,
                 yf_ref, yb_ref, wg_out, wu_out, wd_out, h_ref, a_ref, b_ref):
    _cast_weights((wg_ref, wu_ref, wd_ref), (wg_out, wu_out, wd_out))

    @pl.when(pl.program_id(0) == 0)
    def _():
        h_ref[...] = jnp.zeros_like(h_ref)

    for d, v_ref in enumerate((vf_ref, vb_ref)):
        v = v_ref[...].reshape(BATCH * TM, HALF)
        vb = v.astype(BF16)
        neg = -lam_ref[d]
        softplus = jnp.maximum(neg, 0.0) + jnp.log(1.0 + jnp.exp(-jnp.abs(neg)))
        rate = (-0.5 * LRU_C * LOG2E) * softplus
        for n in range(N_LRU_BLOCKS):
            sl = slice(n * LRU_BLOCK, (n + 1) * LRU_BLOCK)
            g = jnp.dot(vb[:, sl], w_ref[d, n], preferred_element_type=F32)
            tr = jnp.tanh(g[:, 0:LRU_BLOCK] + gb_ref[d, 0:1, sl])
            ti = jnp.tanh(g[:, LRU_BLOCK:2 * LRU_BLOCK] + gb_ref[d, 1:2, sl])
            a = jnp.exp2(rate[:, sl] * tr + rate[:, sl])
            a_ref[d, :, sl] = a
            b_ref[d, :, sl] = jnp.sqrt(1.0 - a * a) * ((0.5 * ti + 0.5) * v[:, sl])

    def body(s, hs):
        hs = list(hs)
        for u in range(SCAN_UNROLL):
            t = s * SCAN_UNROLL + u
            for d, y_ref in enumerate((yf_ref, yb_ref)):
                row = t if d == 0 else TM - 1 - t
                for bt in range(BATCH):
                    c = d * BATCH + bt
                    src = bt * TM + row
                    hs[c] = a_ref[d, pl.ds(src, 1), :] * hs[c] + b_ref[d, pl.ds(src, 1), :]
                    y_ref[bt, pl.ds(row, 1), :] = hs[c]
        return tuple(hs)

    init = tuple(h_ref[c:c + 1, :] for c in range(2 * BATCH))
    final = init
    for s in range(TM // SCAN_UNROLL):
        final = body(s, final)
    for c in range(2 * BATCH):
        h_ref[c:c + 1, :] = final[c]


def _scan(v, gate_w, gate_b, lam, moe_wg, moe_wu, moe_wd):
    steps = TILES_PER_SEQ + 1
    assert CAST_STEPS <= steps
    fwd = pl.BlockSpec((BATCH, TM, HALF), lambda j: (0, j, 0))
    bwd = pl.BlockSpec((BATCH, TM, HALF), lambda j: (0, jnp.where(j == 0, 0, TILES_PER_SEQ + 1 - j), 0))
    shp = jax.ShapeDtypeStruct((BATCH, SEQ + CTX, HALF), F32)
    cast_in, cast_out, cast_shape = _cast_specs(0, lambda j: j)
    return pl.pallas_call(
        _scan_kernel,
        grid=(steps,),
        in_specs=[fwd, bwd,
                  pl.BlockSpec((2, N_LRU_BLOCKS, LRU_BLOCK, 2 * LRU_BLOCK), lambda j: (0, 0, 0, 0)),
                  pl.BlockSpec((2, 2, HALF), lambda j: (0, 0, 0)),
                  pl.BlockSpec((2, 1, HALF), lambda j: (0, 0, 0))] + cast_in,
        out_specs=[fwd, bwd] + cast_out,
        out_shape=[shp, shp] + cast_shape,
        scratch_shapes=[pltpu.VMEM((8, HALF), F32), pltpu.VMEM((2, BATCH * TM, HALF), F32),
                        pltpu.VMEM((2, BATCH * TM, HALF), F32)],
        compiler_params=_params("arbitrary"),
        name="lru_scan",
    )(v, v, gate_w, gate_b, lam, moe_wg, moe_wu, moe_wd)


def _route(score, sel):
    rows = [sel[e:e + 1, :] for e in range(N_EXPERTS)]
    gbest = None
    gidx = None
    for g in range(N_GROUPS):
        top2 = None
        for p in range(PER_GROUP):
            for q in range(p + 1, PER_GROUP):
                s = rows[g * PER_GROUP + p] + rows[g * PER_GROUP + q]
                top2 = s if top2 is None else jnp.maximum(top2, s)
        if g == 0:
            gbest = top2
            gidx = jnp.zeros(top2.shape, jnp.int32)
        else:
            better = top2 > gbest
            gidx = jnp.where(better, g, gidx)
            gbest = jnp.where(better, top2, gbest)
    eint = lax.broadcasted_iota(jnp.int32, sel.shape, 0)
    eidx = eint.astype(F32)
    masked = jnp.where(jnp.right_shift(eint, 2) == gidx, sel, -jnp.inf)
    v1 = jnp.max(masked, axis=0, keepdims=True)
    i1 = jnp.min(jnp.where(masked == v1, eidx, float(N_EXPERTS)), axis=0, keepdims=True)
    masked2 = jnp.where(eidx == i1, -jnp.inf, masked)
    v2 = jnp.max(masked2, axis=0, keepdims=True)
    i2 = jnp.min(jnp.where(masked2 == v2, eidx, float(N_EXPERTS)), axis=0, keepdims=True)
    s1 = jnp.sum(jnp.where(eidx == i1, score, 0.0), axis=0, keepdims=True)
    s2 = jnp.sum(jnp.where(eidx == i2, score, 0.0), axis=0, keepdims=True)
    inv = 1.0 / (s1 + s2)
    return jnp.where(eidx == i1, s1 * inv, 0.0) + jnp.where(eidx == i2, s2 * inv, 0.0), gidx


def _group_ranks(gidx, carry_ref, tile):
    onehot = lax.broadcasted_iota(jnp.int32, (8, TM), 0) == gidx
    oh = jnp.where(onehot, 1.0, 0.0)
    before = lax.broadcasted_iota(jnp.int32, (TM, TM), 0) < lax.broadcasted_iota(jnp.int32, (TM, TM), 1)
    prefix = jnp.dot(oh.astype(BF16), jnp.where(before, 1.0, 0.0).astype(BF16), preferred_element_type=F32)
    carry = jnp.where(tile % (TMOE // TM) == 0, 0.0, carry_ref[...])
    rank = jnp.sum(jnp.where(onehot, prefix + carry[:, 0:1], 0.0), axis=0, keepdims=True)
    carry = carry + jnp.sum(oh, axis=1, keepdims=True)
    carry_ref[...] = carry
    return rank, carry


def _split_bf16(v):
    hi = v.astype(BF16)
    return hi, (v - hi.astype(F32)).astype(BF16)


def _outproj_step(m1, m2, x, mod_ref, wo_ref, g_ref, rw_ref, rb_ref, h_ref, n_ref, route_ref, cnt_ref, tok_ref, carry_ref,
                  nprev_ref):
    step = pl.program_id(0)
    n_hi, n_lo = _split_bf16(nprev_ref[...])
    w_hi, w_lo = _split_bf16(rw_ref[...])

    mod = mod_ref[0]
    mix = (jnp.dot(m1, wo_ref[0:HALF, :], preferred_element_type=F32)
           + jnp.dot(m2, wo_ref[HALF:2 * HALF, :], preferred_element_type=F32))

    logits = _nt_dot(w_hi, n_hi) + _nt_dot(w_hi, n_lo) + _nt_dot(w_lo, n_hi)

    h = x + mod[:, 2 * D:3 * D] * mix
    h_ref[...] = h
    n = _rms_mod(h, g_ref[...], mod[:, 4 * D:5 * D], mod[:, 3 * D:4 * D])
    n_ref[...] = n.astype(n_ref.dtype)
    nprev_ref[...] = n

    score = _sigmoid(logits)
    comb, gidx = _route(score, score + rb_ref[...])
    rank, counts = _group_ranks(gidx, carry_ref, step - 1)
    record = jnp.concatenate([comb, gidx.astype(F32), rank, jnp.zeros((ROUTE_LANES - ROUTE_RANK - 1, TM), F32)], axis=0)
    route_ref[...] = record[0:ROUTE_ROWS]
    tok_ref[...] = record.T
    cnt_ref[0] = counts


def _init_pipeline(carry_ref, nprev_ref):
    @pl.when(pl.program_id(0) == 0)
    def _():
        nprev_ref[...] = jnp.zeros_like(nprev_ref)
        carry_ref[...] = jnp.zeros_like(carry_ref)


def _outproj0_kernel(a_ref, yf_ref, yb_ref, gg_ref, x_ref, c_ref, mod_ref, wo_ref, g_ref, rw_ref, rb_ref,
                     h_ref, n_ref, route_ref, cnt_ref, tok_ref, carry_ref, nprev_ref):
    _init_pipeline(carry_ref, nprev_ref)
    tile = jnp.minimum(pl.program_id(0), N_TILES - 1)
    m2 = ((yf_ref[0] + yb_ref[0]) * gg_ref[...]).astype(BF16)
    _outproj_step(a_ref[...], m2, _token_tile(x_ref, c_ref, tile), mod_ref, wo_ref, g_ref, rw_ref, rb_ref,
                  h_ref, n_ref, route_ref, cnt_ref, tok_ref, carry_ref, nprev_ref)


def _outproj1_kernel(cx_ref, cxp_ref, cxn_ref, bg_ref, att_ref, cw_ref, x_ref, mod_ref, wo_ref, g_ref, rw_ref, rb_ref,
                     h_ref, n_ref, route_ref, cnt_ref, tok_ref, buf_ref, carry_ref, nprev_ref):
    _init_pipeline(carry_ref, nprev_ref)
    tile = jnp.minimum(pl.program_id(0), N_LAT_TILES - 1)
    _fill_padded(buf_ref, cxp_ref, cx_ref, cxn_ref, HALO_S, tile)
    conv = _depthwise(buf_ref, cw_ref, CONV_C, HALO_S - 1, TM, 0)
    m1 = (bg_ref[...] * conv).astype(BF16)
    _outproj_step(m1, att_ref[...], x_ref[...], mod_ref, wo_ref, g_ref, rw_ref, rb_ref,
                  h_ref, n_ref, route_ref, cnt_ref, tok_ref, carry_ref, nprev_ref)


def _outproj_common(n_tiles):
    hold = n_tiles - 1
    tile = _held(hold)
    routed = lambda i: jnp.maximum(i - 1, 0)
    in_specs = [pl.BlockSpec((1, 1, 6 * D), _mod_row(TM, hold)),
                pl.BlockSpec((D, D), lambda i: (0, 0)),
                pl.BlockSpec((1, D), lambda i: (0, 0)),
                pl.BlockSpec((N_EXPERTS, D), lambda i: (0, 0)),
                pl.BlockSpec((N_EXPERTS, 1), lambda i: (0, 0))]
    out_specs = [pl.BlockSpec((TM, D), lambda i: (tile(i), 0)),
                 pl.BlockSpec((TM, D), lambda i: (tile(i), 0)),
                 pl.BlockSpec((ROUTE_ROWS, TM), lambda i: (0, routed(i))),
                 pl.BlockSpec((1, 8, 128), lambda i: (routed(i), 0, 0)),
                 pl.BlockSpec((TM, ROUTE_LANES), lambda i: (routed(i), 0))]
    rows = n_tiles * TM
    out_shape = [jax.ShapeDtypeStruct((rows, D), F32), jax.ShapeDtypeStruct((rows, D), BF16),
                 jax.ShapeDtypeStruct((ROUTE_ROWS, rows), F32), jax.ShapeDtypeStruct((n_tiles, 8, 128), F32),
                 jax.ShapeDtypeStruct((rows, ROUTE_LANES), F32)]
    scratch = [pltpu.VMEM((8, 128), F32), pltpu.VMEM((TM, D), F32)]
    return in_specs, out_specs, out_shape, scratch


def _outproj0(a, yf, yb, gg, x, c, mod, wo, g, rw_t, rb):
    hold = N_TILES - 1
    tok = pl.BlockSpec((TM, HALF), lambda i: (_held(hold)(i), 0))
    scan_tok = pl.BlockSpec((1, TM, HALF), functools.partial(_seq_major, hold=hold))
    common_in, out_specs, out_shape, scratch = _outproj_common(N_TILES)
    return pl.pallas_call(
        _outproj0_kernel,
        grid=(N_TILES + 1,),
        in_specs=[tok, scan_tok, scan_tok, tok] + _token_specs(hold) + common_in,
        out_specs=out_specs,
        out_shape=out_shape,
        scratch_shapes=scratch,
        compiler_params=_params("arbitrary"),
        name="outproj0",
    )(a, yf, yb, gg, x, c, mod, wo, g, rw_t, rb)


def _outproj1(cx, bg, att, conv_w, x, mod, wo, g, rw_t, rb):
    hold = N_LAT_TILES - 1
    tok = pl.BlockSpec((TM, HALF), lambda i: (_held(hold)(i), 0))
    ps, ns = _halo_specs(HALO_S, hold)
    common_in, out_specs, out_shape, scratch = _outproj_common(N_LAT_TILES)
    return pl.pallas_call(
        _outproj1_kernel,
        grid=(N_LAT_TILES + 1,),
        in_specs=[tok, ps, ns, tok, tok, pl.BlockSpec((CONV_C, 8, HALF), lambda i: (0, 0, 0)),
                  pl.BlockSpec((TM, D), lambda i: (_held(hold)(i), 0))] + common_in,
        out_specs=out_specs,
        out_shape=out_shape,
        scratch_shapes=[pltpu.VMEM((TM + 2 * HALO_S, HALF), F32)] + scratch,
        compiler_params=_params("arbitrary"),
        name="outproj1",
    )(cx, cx, cx, bg, att, _sublane_replicated(conv_w), x, mod, wo, g, rw_t, rb)


def _moe_kernel(cnt_ref, n_ref, rt_ref, r_ref, wg_ref, wu_ref, wd_ref, h_ref, mod_ref, fg_ref, o_ref,
                hid_ref, *, final_norm, first_tile, subtiles):
    i = pl.program_id(0)
    g = pl.program_id(1)

    @pl.when(g == 0)
    def _():
        o_ref[...] = jnp.zeros_like(o_ref)

    gf = g.astype(F32)
    slot_row = lax.broadcasted_iota(jnp.int32, (CAP, TMOE), 0).astype(F32)
    slot_col = lax.broadcasted_iota(jnp.int32, (TMOE, CAP), 1).astype(F32)

    counts = [cnt_ref[(first_tile + i * subtiles + s) * N_GROUPS + g] for s in range(subtiles)]
    starts = [sum(counts[:s], jnp.int32(0)) for s in range(subtiles)]
    total = starts[-1] + counts[-1]
    tile_rows = [slice(s * TMOE, (s + 1) * TMOE) for s in range(subtiles)]

    def run_chunk(k, sources):
        base = k * CAP
        sels, shifts = [], []
        xg = None
        for s in sources:
            shift = (starts[s] - base).astype(F32)
            in_group = rt_ref[ROUTE_GROUP:ROUTE_GROUP + 1, tile_rows[s]] == gf
            slot = jnp.where(in_group, rt_ref[ROUTE_RANK:ROUTE_RANK + 1, tile_rows[s]] + shift, -1.0)
            sel = slot_row == slot
            part = jnp.dot(jnp.where(sel, 1.0, 0.0).astype(BF16), n_ref[tile_rows[s], :], preferred_element_type=F32)
            xg = part if xg is None else xg + part
            sels.append(sel)
            shifts.append(shift)
        xg = xg.astype(BF16)
        for j in range(PER_GROUP):
            cw = None
            for s, sel in zip(sources, sels):
                comb_row = rt_ref[pl.ds(g * PER_GROUP + j, 1), tile_rows[s]]
                part = jnp.sum(jnp.where(sel, comb_row, 0.0), axis=1, keepdims=True)
                cw = part if cw is None else cw + part
            hid = (_silu(jnp.dot(xg, wg_ref[j], preferred_element_type=F32))
                   * jnp.dot(xg, wu_ref[j], preferred_element_type=F32) * cw)
            hid_ref[:, j * D_FF:(j + 1) * D_FF] = hid.astype(BF16)
        y = jnp.dot(hid_ref[...], wd_ref[...].reshape(PER_GROUP * D_FF, D),
                    preferred_element_type=F32).astype(BF16)
        for s, shift in zip(sources, shifts):
            in_group = r_ref[tile_rows[s], ROUTE_GROUP:ROUTE_GROUP + 1] == gf
            slot = r_ref[tile_rows[s], ROUTE_RANK:ROUTE_RANK + 1] + shift
            back = jnp.where(jnp.logical_and(in_group, slot_col == slot), 1.0, 0.0).astype(BF16)
            o_ref[tile_rows[s], :] += jnp.dot(back, y, preferred_element_type=F32)

    def loop(lo, hi, sources):
        def body(k, carry):
            run_chunk(k, sources)
            return carry
        lax.fori_loop(lo, hi, body, 0)

    for s in range(subtiles):
        end = starts[s] + counts[s]
        first_inside = (starts[s] + (CAP - 1)) // CAP
        if s + 1 < subtiles:
            loop(first_inside, end // CAP, [s])

            @pl.when(end % CAP != 0)
            def _(s=s, end=end):
                run_chunk(end // CAP, [s, s + 1])
        else:
            loop(first_inside, (end + (CAP - 1)) // CAP, [s])

    @pl.when(g == N_GROUPS - 1)
    def _():
        out = h_ref[...] + mod_ref[0][:, 5 * D:6 * D] * o_ref[...]
        if final_norm:
            out = out * lax.rsqrt(jnp.mean(out * out, axis=-1, keepdims=True) + EPS) * fg_ref[...]
        o_ref[...] = out


def _moe_chunks(counts):
    sub = TMOE // TM
    return counts[sub - 1::sub, :N_GROUPS, 0].astype(jnp.int32).reshape(-1)


def _moe(n_chunks, n, route_t, route, wg, wu, wd, h, mod, final_g, *, first_tile, n_tiles, subtiles, final_norm,
         name):
    assert subtiles in (1, 2) and first_tile % subtiles == 0 and n_tiles % subtiles == 0
    step = subtiles * TMOE
    first = first_tile // subtiles
    mod_row = lambda i, g, nch: (jnp.minimum(((first + i) * step) // SEQ, BATCH), 0, 0)
    tok = pl.BlockSpec((step, D), lambda i, g, nch: (first + i, 0))
    grid_spec = pltpu.PrefetchScalarGridSpec(
        num_scalar_prefetch=1,
        grid=(n_tiles // subtiles, N_GROUPS),
        in_specs=[tok,
                  pl.BlockSpec((ROUTE_ROWS, step), lambda i, g, nch: (0, first + i)),
                  pl.BlockSpec((step, ROUTE_LANES), lambda i, g, nch: (first + i, 0)),
                  pl.BlockSpec((PER_GROUP, D, D_FF), lambda i, g, nch: (g, 0, 0)),
                  pl.BlockSpec((PER_GROUP, D, D_FF), lambda i, g, nch: (g, 0, 0)),
                  pl.BlockSpec((PER_GROUP, D_FF, D), lambda i, g, nch: (g, 0, 0)),
                  tok,
                  pl.BlockSpec((1, 1, 6 * D), mod_row),
                  pl.BlockSpec((1, D), lambda i, g, nch: (0, 0))],
        out_specs=pl.BlockSpec((step, D), lambda i, g, nch: (i, 0)),
        scratch_shapes=[pltpu.VMEM((CAP, PER_GROUP * D_FF), BF16)])
    return pl.pallas_call(
        functools.partial(_moe_kernel, final_norm=final_norm, first_tile=first_tile, subtiles=subtiles),
        grid_spec=grid_spec,
        out_shape=jax.ShapeDtypeStruct((n_tiles * TMOE, D), F32),
        compiler_params=_params("parallel", "arbitrary", vmem=VMEM_LIMIT_MOE),
        name=name,
    )(n_chunks, n, route_t, route, wg, wu, wd, h, mod, final_g)


def _inproj1_kernel(x_ref, c_ref, mod_ref, g_ref, w_ref, wvt_ref, wg_ref, wu_ref, wd_ref,
                    cx_ref, bg_ref, q_ref, k_ref, vt_ref, wg_out, wu_out, wd_out):
    _cast_weights((wg_ref, wu_ref, wd_ref), (wg_out, wu_out, wd_out))
    mod = mod_ref[0]
    n = _rms_mod(_token_tile(x_ref, c_ref, rows=TIN), g_ref[...], mod[:, D:2 * D], mod[:, 0:D]).astype(BF16)

    def proj(c):
        return jnp.dot(n, w_ref[:, c * HALF:(c + 1) * HALF], preferred_element_type=F32)

    conv_in, conv_gate = proj(0), proj(2)
    out_gate = proj(1)
    cx_ref[...] = conv_gate * conv_in
    q = proj(3)
    bg_ref[...] = out_gate
    k = proj(4)
    q_ref[...] = (q * (HEAD_DIM ** -0.5 * LOG2E)).astype(BF16)
    vt = _nt_dot(wvt_ref[...], n)
    k_ref[...] = k.astype(BF16)
    vt_ref[...] = vt.astype(BF16)


def _inproj1(x, c, mod, g, w, moe_wg, moe_wu, moe_wd):
    steps = N_TOK // TIN
    assert CAST_STEPS <= steps
    tok = pl.BlockSpec((TIN, HALF), lambda i: (i, 0))
    f = jax.ShapeDtypeStruct((N_TOK, HALF), F32)
    h = jax.ShapeDtypeStruct((N_TOK, HALF), BF16)
    cast_in, cast_out, cast_shape = _cast_specs(1, lambda i: i)
    return pl.pallas_call(
        _inproj1_kernel,
        grid=(steps,),
        in_specs=_token_specs(rows=TIN) + [
                  pl.BlockSpec((1, 1, 6 * D), _mod_row(TIN)),
                  pl.BlockSpec((1, D), lambda i: (0, 0)),
                  pl.BlockSpec((D, 5 * HALF), lambda i: (0, 0)),
                  pl.BlockSpec((HALF, D), lambda i: (0, 0))] + cast_in,
        out_specs=[tok] * 4 + [pl.BlockSpec((HALF, TIN), lambda i: (0, i))] + cast_out,
        out_shape=[f, f, h, h, jax.ShapeDtypeStruct((HALF, N_TOK), BF16)] + cast_shape,
        compiler_params=_params("arbitrary"),
        name="inproj1",
    )(x, c, mod, g, w[:, :5 * HALF], w[:, 5 * HALF:].T, moe_wg, moe_wu, moe_wd)


def _natten_kernel(q_ref, kp_ref, kc_ref, kn_ref, vp_ref, vc_ref, vn_ref, kx_ref, vx_ref, bias_ref, o_ref):
    pair = 2 * HEAD_DIM
    low = lax.broadcasted_iota(jnp.int32, (TM, pair), 1) < HEAD_DIM
    head_mask = [low.astype(F32).astype(BF16), jnp.logical_not(low).astype(F32).astype(BF16)]
    top = lax.broadcasted_iota(jnp.int32, (pair, TM), 0) < HEAD_DIM
    own_rows = [top.astype(F32).astype(BF16), jnp.logical_not(top).astype(F32).astype(BF16)]
    def raw_scores(head):
        g, hh = divmod(head, 2)
        sl = slice(pair * g, pair * (g + 1))
        qm = q_ref[:, sl] * head_mask[hh]
        return [_nt_dot(qm, k_ref[:, sl]) for k_ref in (kp_ref, kc_ref, kn_ref, kx_ref)]

    def biased(head, s):
        s = [s[t] + bias_ref[0, head, :, t * TM:(t + 1) * TM] for t in range(3)] + [s[3]]
        m = jnp.max(jnp.maximum(jnp.maximum(s[0], s[1]), jnp.maximum(s[2], s[3])), axis=-1, keepdims=True)
        return s, m

    def attend(head, s, m):
        g, hh = divmod(head, 2)
        sl = slice(pair * g, pair * (g + 1))
        p = jnp.concatenate([jnp.exp2(st - m).astype(BF16) for st in s], axis=1)
        lhs = jnp.concatenate([vt_ref[sl, :] * own_rows[hh] + own_rows[1 - hh]
                               for vt_ref in (vp_ref, vc_ref, vn_ref, vx_ref)], axis=1)
        acc = _nt_dot(lhs, p)
        if hh == 0:
            return acc[0:HEAD_DIM] * (1.0 / acc[HEAD_DIM:HEAD_DIM + 1])
        return acc[HEAD_DIM:pair] * (1.0 / acc[0:1])

    pending = biased(0, raw_scores(0))
    halves = []
    for head in range(N_HEADS):
        current = pending
        if head + 1 < N_HEADS:
            pending = biased(head + 1, raw_scores(head + 1))
        halves.append(attend(head, *current))
        if head % 2 == 1:
            sl = slice(pair * (head // 2), pair * (head // 2 + 1))
            o_ref[:, sl] = jnp.concatenate(halves, axis=0).T.astype(o_ref.dtype)
            halves = []


def _natten_bias(rpb):
    n_rows = SEQ // GRID_W
    n_dr, n_dc = 2 * WIN_H - 1, 2 * WIN_W - 1
    i = np.arange(ROWS_Q)
    j = np.arange(3 * ROWS_Q)
    col = np.arange(GRID_W)
    col_start = np.clip(col - WIN_W // 2, 0, GRID_W - WIN_W)
    col_ok = (col[None, :] >= col_start[:, None]) & (col[None, :] < col_start[:, None] + WIN_W)
    col_idx = col[None, :] - col[:, None] + (WIN_W - 1)
    onehot = ((col_idx[None] == np.arange(n_dc)[:, None, None]) & col_ok[None]).astype(np.float32)
    col_exp = jnp.dot(rpb.reshape(N_HEADS * n_dr, n_dc).astype(F32), onehot.reshape(n_dc, GRID_W * GRID_W),
                      precision=lax.Precision.HIGHEST).reshape(N_HEADS, n_dr, GRID_W, GRID_W)
    col_exp = jnp.where(col_ok[None, None], col_exp * LOG2E, NEG)
    masked = jnp.full((N_HEADS, GRID_W, GRID_W), NEG, F32)
    kinds = []
    for r0 in (0, ROWS_Q, n_rows - ROWS_Q):
        r = r0 + i
        kr = r0 - ROWS_Q + j
        r_start = np.clip(r - WIN_H // 2, 0, n_rows - WIN_H)
        row_ok = ((kr[None, :] >= r_start[:, None]) & (kr[None, :] < r_start[:, None] + WIN_H)
                  & (kr[None, :] >= 0) & (kr[None, :] < n_rows))
        row_idx = kr[None, :] - r[:, None] + (WIN_H - 1)
        rows = [jnp.concatenate([col_exp[:, row_idx[qi, kj]] if row_ok[qi, kj] else masked for kj in range(3 * ROWS_Q)],
                                axis=-1) for qi in range(ROWS_Q)]
        kinds.append(jnp.concatenate(rows, axis=1))
    return jnp.stack(kinds)


def _natten(q, k, vt, bias):
    def tile(b, i, off):
        return b * TILES_PER_SEQ + jnp.clip(i + off, 0, TILES_PER_SEQ - 1)

    def lat(off):
        return pl.BlockSpec((TM, HALF), lambda b, i: (tile(b, i, off), 0))

    def lat_t(off):
        return pl.BlockSpec((HALF, TM), lambda b, i: (0, tile(b, i, off)))

    ctx = pl.BlockSpec((TM, HALF), lambda b, i: (N_LAT_TILES + b, 0))
    ctx_t = pl.BlockSpec((HALF, TM), lambda b, i: (0, N_LAT_TILES + b))

    def kind(b, i):
        return (jnp.where(i == 0, 0, jnp.where(i == TILES_PER_SEQ - 1, 2, 1)), 0, 0, 0)

    return pl.pallas_call(
        _natten_kernel,
        grid=(BATCH, TILES_PER_SEQ),
        in_specs=[lat(0), lat(-1), lat(0), lat(1), lat_t(-1), lat_t(0), lat_t(1), ctx, ctx_t,
                  pl.BlockSpec((1, N_HEADS, TM, 3 * TM), kind)],
        out_specs=lat(0),
        out_shape=jax.ShapeDtypeStruct((N_LAT, HALF), BF16),
        compiler_params=_params("parallel", "arbitrary"),
        name="natten",
    )(q, k, k, k, vt, vt, vt, k, vt, bias)


def kernel(x, c, ctx, c_ctx, ada_w, ada_b, norm_mix_g, norm_ffn_g, w_out, ab_w_in, a_dw_w, a_dw_b, a_ln_g, a_ln_b,
           b_conv_w, b_conv_b, b_gate_w, b_gate_b, b_lambda, cd_w_in, c_conv_w, d_rpb, router_w, router_bias,
           moe_w_gate, moe_w_up, moe_w_down, final_g):
    x_lat = x.reshape(N_LAT, D)
    x_ctx = ctx.reshape(BATCH * CTX, D)
    cond = jnp.concatenate([c, c_ctx[None], jnp.zeros((8 - BATCH - 1, D), F32)], axis=0)
    mod = _modulation(cond, ada_w, ada_b)
    mod0 = mod[0].reshape(8, 1, 6 * D)
    mod1 = mod[1].reshape(8, 1, 6 * D)

    wo = w_out.astype(BF16)
    lat_tiles = N_LAT // TMOE
    ctx_tiles = BATCH * CTX // TMOE
    rw_t = router_w.T
    rb = router_bias.reshape(N_EXPERTS, 1)
    fg = final_g.reshape(1, D)

    ua, ub, gg = _inproj0(x_lat, x_ctx, mod0, norm_mix_g[0].reshape(1, D), ab_w_in[0].astype(BF16))
    a_out, v = _conv0(ua, ub, a_dw_w[0], a_dw_b[0], a_ln_g[0], a_ln_b[0], b_conv_w[0], b_conv_b[0])
    gw = b_gate_w[0]
    gate_w = (0.5 * jnp.concatenate([gw[:, 0], gw[:, 1]], axis=-1)).astype(BF16)
    yf, yb, wg0, wu0, wd0 = _scan(v, gate_w, 0.5 * b_gate_b[0], b_lambda[0].reshape(2, 1, HALF),
                                  moe_w_gate, moe_w_up, moe_w_down)
    h1, n2, route0, cnt0, tok0 = _outproj0(a_out, yf, yb, gg, x_lat, x_ctx, mod0, wo[0],
                                           norm_ffn_g[0].reshape(1, D), rw_t, rb)
    moe0 = functools.partial(_moe, _moe_chunks(cnt0), n2, route0, tok0, wg0, wu0, wd0, h1, mod0, fg,
                             final_norm=False)
    h2_lat = moe0(first_tile=0, n_tiles=lat_tiles, subtiles=2, name="moe_lat")
    h2_ctx = moe0(first_tile=lat_tiles, n_tiles=ctx_tiles, subtiles=1, name="moe_ctx")

    cx, bg, q, k, vv, wg1, wu1, wd1 = _inproj1(h2_lat, h2_ctx, mod1, norm_mix_g[1].reshape(1, D),
                                               cd_w_in[0].astype(BF16), moe_w_gate, moe_w_up, moe_w_down)
    att = _natten(q, k, vv, _natten_bias(d_rpb[0]))
    h3, n4, route1, cnt1, tok1 = _outproj1(cx, bg, att, c_conv_w[0], h2_lat, mod1, wo[1],
                                           norm_ffn_g[1].reshape(1, D), rw_t, rb)
    out = _moe(_moe_chunks(cnt1), n4, route1, tok1, wg1, wu1, wd1, h3, mod1, fg,
               first_tile=0, n_tiles=lat_tiles, subtiles=2, final_norm=True, name="moe_final")
    return out.reshape(BATCH, SEQ, D)
```

```python
import functools
import math

import jax
import jax.numpy as jnp
import numpy as np
from jax import lax
from jax.experimental import pallas as pl
from jax.experimental.pallas import tpu as pltpu

F32 = jnp.float32
BF16 = jnp.bfloat16

D = 1024
BATCH = 2
SEQ = 8192
CTX = 256
GRID_W = 64
N_LAT = BATCH * SEQ
N_TOK = N_LAT + BATCH * CTX
HALF = 512
CONV_A = 31
CONV_B = 4
CONV_C = 3
LRU_BLOCK = 128
N_LRU_BLOCKS = HALF // LRU_BLOCK
LRU_C = 8.0
HEAD_DIM = 64
N_HEADS = HALF // HEAD_DIM
WIN_H = 8
WIN_W = 16
N_EXPERTS = 16
N_GROUPS = 4
PER_GROUP = N_EXPERTS // N_GROUPS
D_FF = 512
EPS = 1e-6
NEG = -1e30
LOG2E = math.log2(math.e)

TM = 256
TILES_PER_SEQ = SEQ // TM
N_LAT_TILES = N_LAT // TM
N_TILES = N_TOK // TM
TIN = 512
TMOE = 512
CAP = 128
ROUTE_ROWS = 24
ROUTE_LANES = 128
ROUTE_GROUP = 16
ROUTE_RANK = 17
HALO_A = 16
HALO_S = 8
ROWS_Q = TM // GRID_W
VMEM_LIMIT = 48 * 1024 * 1024
VMEM_LIMIT_MOE = 56 * 1024 * 1024


def _params(*sem, vmem=VMEM_LIMIT):
    return pltpu.CompilerParams(dimension_semantics=sem, vmem_limit_bytes=vmem)


def _sigmoid(x):
    return 0.5 * jnp.tanh(0.5 * x) + 0.5


def _silu(x):
    return x * _sigmoid(x)


def _gelu_tanh(x):
    return 0.5 * x * (1.0 + jnp.tanh(0.7978845608028654 * (x + 0.044715 * (x * x * x))))


def _rms_mod(x, g, scale, shift):
    y = x * lax.rsqrt(jnp.mean(x * x, axis=-1, keepdims=True) + EPS) * g
    return y * (1.0 + scale) + shift


def _nt_dot(a, b):
    return lax.dot_general(a, b, (((1,), (1,)), ((), ())), preferred_element_type=F32)


def _held(hold):
    return (lambda i: i) if hold is None else (lambda i: jnp.minimum(i, hold))


def _mod_row(tile_rows, hold=None):
    per_seq = SEQ // tile_rows
    tile = _held(hold)
    return lambda i: (jnp.minimum(tile(i) // per_seq, BATCH), 0, 0)


def _mod_kernel(c_ref, w_ref, b_ref, o_ref):
    c = c_ref[...]
    s = _silu(c).astype(BF16)
    o_ref[0] = jnp.dot(s, w_ref[0].astype(BF16), preferred_element_type=F32) + b_ref[0]


def _modulation(cond, ada_w, ada_b):
    depth = ada_w.shape[0]
    nb = 1536
    return pl.pallas_call(
        _mod_kernel,
        grid=(depth, 6 * D // nb),
        in_specs=[pl.BlockSpec((8, D), lambda l, j: (0, 0)),
                  pl.BlockSpec((1, D, nb), lambda l, j: (l, 0, j)),
                  pl.BlockSpec((1, 1, nb), lambda l, j: (l, 0, j))],
        out_specs=pl.BlockSpec((1, 8, nb), lambda l, j: (l, 0, j)),
        out_shape=jax.ShapeDtypeStruct((depth, 8, 6 * D), F32),
        compiler_params=_params("parallel", "parallel"),
        name="modulation",
    )(cond, ada_w, ada_b.reshape(depth, 1, 6 * D))


def _token_specs(hold=None, rows=TM):
    tile = _held(hold)
    lat_tiles = N_LAT // rows
    lat = pl.BlockSpec((rows, D), lambda i: (jnp.minimum(tile(i), lat_tiles - 1), 0))
    ctx = pl.BlockSpec((rows, D), lambda i: (jnp.maximum(tile(i) - lat_tiles, 0), 0))
    return [lat, ctx]


def _token_tile(lat_ref, ctx_ref, tile=None, rows=TM):
    tile = pl.program_id(0) if tile is None else tile
    return jnp.where(tile < N_LAT // rows, lat_ref[...], ctx_ref[...])


def _inproj0_kernel(x_ref, c_ref, mod_ref, g_ref, w_ref, ua_ref, ub_ref, gg_ref):
    mod = mod_ref[0]
    n = _rms_mod(_token_tile(x_ref, c_ref, rows=TIN), g_ref[...], mod[:, D:2 * D], mod[:, 0:D]).astype(BF16)

    def proj(c):
        return jnp.dot(n, w_ref[:, c * HALF:(c + 1) * HALF], preferred_element_type=F32)

    value, gate = proj(0), proj(1)
    recur = proj(2)
    ua_ref[...] = value * _sigmoid(gate)
    gelu_in = proj(3)
    ub_ref[...] = recur
    gg_ref[...] = _gelu_tanh(gelu_in).astype(gg_ref.dtype)


def _inproj0(x, c, mod, g, w):
    tok = pl.BlockSpec((TIN, HALF), lambda i: (i, 0))
    shp = jax.ShapeDtypeStruct((N_TOK, HALF), F32)
    return pl.pallas_call(
        _inproj0_kernel,
        grid=(N_TOK // TIN,),
        in_specs=_token_specs(rows=TIN) + [
                  pl.BlockSpec((1, 1, 6 * D), _mod_row(TIN)),
                  pl.BlockSpec((1, D), lambda i: (0, 0)),
                  pl.BlockSpec((D, 4 * HALF), lambda i: (0, 0))],
        out_specs=[tok, tok, tok],
        out_shape=[shp, shp, jax.ShapeDtypeStruct((N_TOK, HALF), BF16)],
        compiler_params=_params("parallel"),
        name="inproj0",
    )(x, c, mod, g, w)


def _halo_specs(halo, hold=None):
    per_tile = TM // halo
    last = N_TOK // halo - 1
    tile = _held(hold)
    prev = pl.BlockSpec((halo, HALF), lambda i: (jnp.maximum(tile(i) * per_tile - 1, 0), 0))
    nxt = pl.BlockSpec((halo, HALF), lambda i: (jnp.minimum((tile(i) + 1) * per_tile, last), 0))
    return prev, nxt


def _seq_edges(i):
    is_ctx = i >= N_LAT_TILES
    first = jnp.logical_or(is_ctx, i % TILES_PER_SEQ == 0)
    last = jnp.logical_or(is_ctx, i % TILES_PER_SEQ == TILES_PER_SEQ - 1)
    return first, last


def _fill_padded(buf_ref, prev_ref, cur_ref, next_ref, halo, tile=None):
    first, last = _seq_edges(pl.program_id(0) if tile is None else tile)
    buf_ref[0:halo, :] = jnp.where(first, 0.0, prev_ref[...])
    buf_ref[halo:halo + TM, :] = cur_ref[...]
    buf_ref[halo + TM:halo + TM + halo, :] = jnp.where(last, 0.0, next_ref[...])


def _tap_weight(w_ref, k, rows):
    return jnp.concatenate([w_ref[k]] * (rows // 8), axis=0)


def _sublane_replicated(w):
    return jnp.broadcast_to(w[:, None, :], (w.shape[0], 8, w.shape[1]))


def _depthwise(buf_ref, w_ref, taps, first_off, rows, row0):
    acc = None
    for k in range(taps):
        term = _tap_weight(w_ref, k, rows) * buf_ref[row0 + first_off + k:row0 + first_off + k + rows, :]
        acc = term if acc is None else acc + term
    return acc


CONV_ROWS = 32


SHIFT_ROWS = TM + 2 * HALO_A - 8


def _conv0_kernel(ua_ref, uap_ref, uan_ref, ub_ref, ubp_ref, ubn_ref,
                  dww_ref, dwb_ref, lng_ref, lnb_ref, cw_ref, cb_ref,
                  a_ref, v_ref, bufa_ref, bufb_ref, shift_ref):
    _fill_padded(bufa_ref, uap_ref, ua_ref, uan_ref, HALO_A)
    _fill_padded(bufb_ref, ubp_ref, ub_ref, ubn_ref, HALO_S)
    for s in range(1, 8):
        shift_ref[s - 1] = bufa_ref[s:s + SHIFT_ROWS, :]
    first = HALO_A - CONV_A // 2
    for r in range(TM // CONV_ROWS):
        row0 = r * CONV_ROWS
        u = None
        for k in range(CONV_A):
            off = first + k
            base = row0 + off - off % 8
            rows = bufa_ref[base:base + CONV_ROWS, :] if off % 8 == 0 else shift_ref[off % 8 - 1, base:base + CONV_ROWS, :]
            term = _tap_weight(dww_ref, k, CONV_ROWS) * rows
            u = term if u is None else u + term
        u = u + dwb_ref[...]
        mu = jnp.mean(u, axis=-1, keepdims=True)
        uc = u - mu
        var = jnp.mean(uc * uc, axis=-1, keepdims=True)
        y = uc * lax.rsqrt(var + EPS) * lng_ref[...] + lnb_ref[...]
        a_ref[row0:row0 + CONV_ROWS, :] = _silu(y).astype(a_ref.dtype)
        v = _depthwise(bufb_ref, cw_ref, CONV_B, HALO_S - 2, CONV_ROWS, row0) + cb_ref[...]
        v_ref[0, row0:row0 + CONV_ROWS, :] = v


def _seq_major(i, hold=None):
    i = _held(hold)(i)
    is_lat = i < N_LAT_TILES
    return (jnp.where(is_lat, i // TILES_PER_SEQ, i - N_LAT_TILES), jnp.where(is_lat, 1 + i % TILES_PER_SEQ, 0), 0)


def _conv0(ua, ub, dw_w, dw_b, ln_g, ln_b, conv_w, conv_b):
    tok = pl.BlockSpec((TM, HALF), lambda i: (i, 0))
    pa, na = _halo_specs(HALO_A)
    ps, ns = _halo_specs(HALO_S)
    vec = pl.BlockSpec((1, HALF), lambda i: (0, 0))
    return pl.pallas_call(
        _conv0_kernel,
        grid=(N_TILES,),
        in_specs=[tok, pa, na, tok, ps, ns,
                  pl.BlockSpec((CONV_A, 8, HALF), lambda i: (0, 0, 0)), vec, vec, vec,
                  pl.BlockSpec((CONV_B, 8, HALF), lambda i: (0, 0, 0)), vec],
        out_specs=[tok, pl.BlockSpec((1, TM, HALF), _seq_major)],
        out_shape=[jax.ShapeDtypeStruct((N_TOK, HALF), BF16),
                   jax.ShapeDtypeStruct((BATCH, SEQ + CTX, HALF), F32)],
        scratch_shapes=[pltpu.VMEM((TM + 2 * HALO_A, HALF), F32), pltpu.VMEM((TM + 2 * HALO_S, HALF), F32),
                        pltpu.VMEM((7, SHIFT_ROWS, HALF), F32)],
        compiler_params=_params("parallel"),
        name="conv0",
    )(ua, ua, ua, ub, ub, ub, _sublane_replicated(dw_w), dw_b.reshape(1, HALF), ln_g.reshape(1, HALF),
      ln_b.reshape(1, HALF), _sublane_replicated(conv_w), conv_b.reshape(1, HALF))


PAD_A_ROWS = TM + 2 * HALO_A
PAD_B_ROWS = TM + 2 * HALO_S


def _front0_kernel(x_ref, c_ref, xp_ref, xn_ref, mod_ref, g_ref, w_ref,
                   dww_ref, dwb_ref, lng_ref, lnb_ref, cw_ref, cb_ref,
                   gg_ref, a_ref, v_ref, bufa_ref, bufb_ref, shift_ref):
    step = pl.program_id(0)

    @pl.when(step == 0)
    def _():
        bufa_ref[...] = jnp.zeros_like(bufa_ref)
        bufb_ref[...] = jnp.zeros_like(bufb_ref)

    tile = jnp.minimum(step, N_TILES - 1)
    first, last = _seq_edges(tile)

    def run(new, old):
        mod = mod_ref[0]
        rows_in = jnp.concatenate([xp_ref[...], _token_tile(x_ref, c_ref, tile), xn_ref[...]], axis=0)
        n = _rms_mod(rows_in, g_ref[...], mod[:, D:2 * D], mod[:, 0:D]).astype(BF16)

        def proj(c, rows=slice(None)):
            return jnp.dot(n[rows], w_ref[:, c * HALF:(c + 1) * HALF], preferred_element_type=F32)

        def conv_chunk(r):
            row0 = r * CONV_ROWS
            u = None
            for k in range(CONV_A):
                off = HALO_A - CONV_A // 2 + k
                base = row0 + off - off % 8
                term = _tap_weight(dww_ref, k, CONV_ROWS) * shift_ref[off % 8, base:base + CONV_ROWS, :]
                u = term if u is None else u + term
            u = u + dwb_ref[...]
            mu = jnp.mean(u, axis=-1, keepdims=True)
            uc = u - mu
            var = jnp.mean(uc * uc, axis=-1, keepdims=True)
            y = uc * lax.rsqrt(var + EPS) * lng_ref[...] + lnb_ref[...]
            a_ref[row0:row0 + CONV_ROWS, :] = _silu(y).astype(a_ref.dtype)
            acc = None
            for k in range(CONV_B):
                lo = row0 + HALO_S - 2 + k
                term = _tap_weight(cw_ref, k, CONV_ROWS) * bufb_ref[old, lo:lo + CONV_ROWS, :]
                acc = term if acc is None else acc + term
            v_ref[0, row0:row0 + CONV_ROWS, :] = acc + cb_ref[...]

        for s in range(8):
            shift_ref[s] = bufa_ref[old, s:s + SHIFT_ROWS, :]
        value, gate = proj(0), proj(1)
        conv_chunk(0)
        conv_chunk(1)
        ua = value * _sigmoid(gate)
        bufa_ref[new, 0:HALO_A, :] = jnp.where(first, 0.0, ua[0:HALO_A])
        bufa_ref[new, HALO_A:HALO_A + TM, :] = ua[HALO_A:HALO_A + TM]
        bufa_ref[new, HALO_A + TM:PAD_A_ROWS, :] = jnp.where(last, 0.0, ua[HALO_A + TM:PAD_A_ROWS])
        recur = proj(2)
        conv_chunk(2)
        conv_chunk(3)
        lo = HALO_A - HALO_S
        bufb_ref[new, 0:HALO_S, :] = jnp.where(first, 0.0, recur[lo:HALO_A])
        bufb_ref[new, HALO_S:HALO_S + TM, :] = recur[HALO_A:HALO_A + TM]
        bufb_ref[new, HALO_S + TM:PAD_B_ROWS, :] = jnp.where(last, 0.0, recur[HALO_A + TM:HALO_A + TM + HALO_S])
        gelu_in = proj(3, slice(HALO_A, HALO_A + TM))
        conv_chunk(4)
        conv_chunk(5)
        gg_ref[...] = _gelu_tanh(gelu_in).astype(gg_ref.dtype)
        conv_chunk(6)
        conv_chunk(7)

    @pl.when(step % 2 == 0)
    def _():
        run(0, 1)

    @pl.when(step % 2 == 1)
    def _():
        run(1, 0)


def _front0(x, c, mod, g, w, dw_w, dw_b, ln_g, ln_b, conv_w, conv_b):
    hold = N_TILES - 1
    tile = _held(hold)
    lagged = lambda i: jnp.maximum(i - 1, 0)
    per_tile = TM // HALO_A
    last_halo = N_LAT // HALO_A - 1
    prev = pl.BlockSpec((HALO_A, D), lambda i: (jnp.clip(tile(i) * per_tile - 1, 0, last_halo), 0))
    nxt = pl.BlockSpec((HALO_A, D), lambda i: (jnp.clip((tile(i) + 1) * per_tile, 0, last_halo), 0))
    vec = pl.BlockSpec((1, HALF), lambda i: (0, 0))
    return pl.pallas_call(
        _front0_kernel,
        grid=(N_TILES + 1,),
        in_specs=_token_specs(hold) + [prev, nxt,
                  pl.BlockSpec((1, 1, 6 * D), _mod_row(TM, hold)),
                  pl.BlockSpec((1, D), lambda i: (0, 0)),
                  pl.BlockSpec((D, 4 * HALF), lambda i: (0, 0)),
                  pl.BlockSpec((CONV_A, 8, HALF), lambda i: (0, 0, 0)), vec, vec, vec,
                  pl.BlockSpec((CONV_B, 8, HALF), lambda i: (0, 0, 0)), vec],
        out_specs=[pl.BlockSpec((TM, HALF), lambda i: (tile(i), 0)),
                   pl.BlockSpec((TM, HALF), lambda i: (lagged(i), 0)),
                   pl.BlockSpec((1, TM, HALF), lambda i: _seq_major(lagged(i)))],
        out_shape=[jax.ShapeDtypeStruct((N_TOK, HALF), BF16), jax.ShapeDtypeStruct((N_TOK, HALF), BF16),
                   jax.ShapeDtypeStruct((BATCH, SEQ + CTX, HALF), F32)],
        scratch_shapes=[pltpu.VMEM((2, PAD_A_ROWS, HALF), F32), pltpu.VMEM((2, PAD_B_ROWS, HALF), F32),
                        pltpu.VMEM((8, SHIFT_ROWS, HALF), F32)],
        compiler_params=_params("arbitrary"),
        name="front0",
    )(x, c, x, x, mod, g, w, _sublane_replicated(dw_w), dw_b.reshape(1, HALF), ln_g.reshape(1, HALF),
      ln_b.reshape(1, HALF), _sublane_replicated(conv_w), conv_b.reshape(1, HALF))


SCAN_UNROLL = 8


CAST_STEPS = 2 * N_EXPERTS


def _cast_specs(layer, step_of):
    def half(*idx):
        s = jnp.minimum(step_of(*idx), CAST_STEPS - 1)
        return s // 2, s % 2

    def src(*idx):
        e, h = half(*idx)
        return (layer, e, h, 0)

    def dst(*idx):
        e, h = half(*idx)
        return (e, h, 0)

    in_specs = [pl.BlockSpec((1, 1, D // 2, D_FF), src), pl.BlockSpec((1, 1, D // 2, D_FF), src),
                pl.BlockSpec((1, 1, D_FF // 2, D), src)]
    out_specs = [pl.BlockSpec((1, D // 2, D_FF), dst), pl.BlockSpec((1, D // 2, D_FF), dst),
                 pl.BlockSpec((1, D_FF // 2, D), dst)]
    out_shape = [jax.ShapeDtypeStruct((N_EXPERTS, D, D_FF), BF16), jax.ShapeDtypeStruct((N_EXPERTS, D, D_FF), BF16),
                 jax.ShapeDtypeStruct((N_EXPERTS, D_FF, D), BF16)]
    return in_specs, out_specs, out_shape


def _cast_weights(srcs, dsts):
    for src, dst in zip(srcs, dsts):
        dst[0] = src[0, 0].astype(BF16)


def _scan_kernel(vf_ref, vb_ref, w_ref, gb_ref, lam_ref, wg_ref, wu_ref, wd_ref,
                 yf_ref, yb_ref, wg_out, wu_out, wd_out, h_ref, a_ref, b_ref):
    _cast_weights((wg_ref, wu_ref, wd_ref), (wg_out, wu_out, wd_out))

    @pl.when(pl.program_id(0) == 0)
    def _():
        h_ref[...] = jnp.zeros_like(h_ref)

    for d, v_ref in enumerate((vf_ref, vb_ref)):
        v = v_ref[...].reshape(BATCH * TM, HALF)
        vb = v.astype(BF16)
        neg = -lam_ref[d]
        softplus = jnp.maximum(neg, 0.0) + jnp.log(1.0 + jnp.exp(-jnp.abs(neg)))
        rate = (-0.5 * LRU_C * LOG2E) * softplus
        for n in range(N_LRU_BLOCKS):
            sl = slice(n * LRU_BLOCK, (n + 1) * LRU_BLOCK)
            g = jnp.dot(vb[:, sl], w_ref[d, n], preferred_element_type=F32)
            tr = jnp.tanh(g[:, 0:LRU_BLOCK] + gb_ref[d, 0:1, sl])
            ti = jnp.tanh(g[:, LRU_BLOCK:2 * LRU_BLOCK] + gb_ref[d, 1:2, sl])
            a = jnp.exp2(rate[:, sl] * tr + rate[:, sl])
            a_ref[d, :, sl] = a
            b_ref[d, :, sl] = jnp.sqrt(1.0 - a * a) * ((0.5 * ti + 0.5) * v[:, sl])

    def body(s, hs):
        hs = list(hs)
        for u in range(SCAN_UNROLL):
            t = s * SCAN_UNROLL + u
            for d, y_ref in enumerate((yf_ref, yb_ref)):
                row = t if d == 0 else TM - 1 - t
                for bt in range(BATCH):
                    c = d * BATCH + bt
                    src = bt * TM + row
                    hs[c] = a_ref[d, pl.ds(src, 1), :] * hs[c] + b_ref[d, pl.ds(src, 1), :]
                    y_ref[bt, pl.ds(row, 1), :] = hs[c]
        return tuple(hs)

    init = tuple(h_ref[c:c + 1, :] for c in range(2 * BATCH))
    final = init
    for s in range(TM // SCAN_UNROLL):
        final = body(s, final)
    for c in range(2 * BATCH):
        h_ref[c:c + 1, :] = final[c]


def _scan(v, gate_w, gate_b, lam, moe_wg, moe_wu, moe_wd):
    steps = TILES_PER_SEQ + 1
    assert CAST_STEPS <= steps
    fwd = pl.BlockSpec((BATCH, TM, HALF), lambda j: (0, j, 0))
    bwd = pl.BlockSpec((BATCH, TM, HALF), lambda j: (0, jnp.where(j == 0, 0, TILES_PER_SEQ + 1 - j), 0))
    shp = jax.ShapeDtypeStruct((BATCH, SEQ + CTX, HALF), F32)
    cast_in, cast_out, cast_shape = _cast_specs(0, lambda j: j)
    return pl.pallas_call(
        _scan_kernel,
        grid=(steps,),
        in_specs=[fwd, bwd,
                  pl.BlockSpec((2, N_LRU_BLOCKS, LRU_BLOCK, 2 * LRU_BLOCK), lambda j: (0, 0, 0, 0)),
                  pl.BlockSpec((2, 2, HALF), lambda j: (0, 0, 0)),
                  pl.BlockSpec((2, 1, HALF), lambda j: (0, 0, 0))] + cast_in,
        out_specs=[fwd, bwd] + cast_out,
        out_shape=[shp, shp] + cast_shape,
        scratch_shapes=[pltpu.VMEM((8, HALF), F32), pltpu.VMEM((2, BATCH * TM, HALF), F32),
                        pltpu.VMEM((2, BATCH * TM, HALF), F32)],
        compiler_params=_params("arbitrary"),
        name="lru_scan",
    )(v, v, gate_w, gate_b, lam, moe_wg, moe_wu, moe_wd)


def _route(score, sel):
    rows = [sel[e:e + 1, :] for e in range(N_EXPERTS)]
    gbest = None
    gidx = None
    for g in range(N_GROUPS):
        top2 = None
        for p in range(PER_GROUP):
            for q in range(p + 1, PER_GROUP):
                s = rows[g * PER_GROUP + p] + rows[g * PER_GROUP + q]
                top2 = s if top2 is None else jnp.maximum(top2, s)
        if g == 0:
            gbest = top2
            gidx = jnp.zeros(top2.shape, jnp.int32)
        else:
            better = top2 > gbest
            gidx = jnp.where(better, g, gidx)
            gbest = jnp.where(better, top2, gbest)
    eint = lax.broadcasted_iota(jnp.int32, sel.shape, 0)
    eidx = eint.astype(F32)
    masked = jnp.where(jnp.right_shift(eint, 2) == gidx, sel, -jnp.inf)
    v1 = jnp.max(masked, axis=0, keepdims=True)
    i1 = jnp.min(jnp.where(masked == v1, eidx, float(N_EXPERTS)), axis=0, keepdims=True)
    masked2 = jnp.where(eidx == i1, -jnp.inf, masked)
    v2 = jnp.max(masked2, axis=0, keepdims=True)
    i2 = jnp.min(jnp.where(masked2 == v2, eidx, float(N_EXPERTS)), axis=0, keepdims=True)
    s1 = jnp.sum(jnp.where(eidx == i1, score, 0.0), axis=0, keepdims=True)
    s2 = jnp.sum(jnp.where(eidx == i2, score, 0.0), axis=0, keepdims=True)
    inv = 1.0 / (s1 + s2)
    return jnp.where(eidx == i1, s1 * inv, 0.0) + jnp.where(eidx == i2, s2 * inv, 0.0), gidx


def _group_ranks(gidx, carry_ref, tile):
    onehot = lax.broadcasted_iota(jnp.int32, (8, TM), 0) == gidx
    oh = jnp.where(onehot, 1.0, 0.0)
    before = lax.broadcasted_iota(jnp.int32, (TM, TM), 0) < lax.broadcasted_iota(jnp.int32, (TM, TM), 1)
    prefix = jnp.dot(oh.astype(BF16), jnp.where(before, 1.0, 0.0).astype(BF16), preferred_element_type=F32)
    carry = jnp.where(tile % (TMOE // TM) == 0, 0.0, carry_ref[...])
    rank = jnp.sum(jnp.where(onehot, prefix + carry[:, 0:1], 0.0), axis=0, keepdims=True)
    carry = carry + jnp.sum(oh, axis=1, keepdims=True)
    carry_ref[...] = carry
    return rank, carry


def _split_bf16(v):
    hi = v.astype(BF16)
    return hi, (v - hi.astype(F32)).astype(BF16)


def _outproj_step(m1, m2, x, mod_ref, wo_ref, g_ref, rw_ref, rb_ref, h_ref, n_ref, route_ref, cnt_ref, tok_ref, carry_ref,
                  nprev_ref):
    step = pl.program_id(0)
    n_hi, n_lo = _split_bf16(nprev_ref[...])
    w_hi, w_lo = _split_bf16(rw_ref[...])

    mod = mod_ref[0]
    mix = (jnp.dot(m1, wo_ref[0:HALF, :], preferred_element_type=F32)
           + jnp.dot(m2, wo_ref[HALF:2 * HALF, :], preferred_element_type=F32))

    logits = _nt_dot(w_hi, n_hi) + _nt_dot(w_hi, n_lo) + _nt_dot(w_lo, n_hi)

    h = x + mod[:, 2 * D:3 * D] * mix
    h_ref[...] = h
    n = _rms_mod(h, g_ref[...], mod[:, 4 * D:5 * D], mod[:, 3 * D:4 * D])
    n_ref[...] = n.astype(n_ref.dtype)
    nprev_ref[...] = n

    score = _sigmoid(logits)
    comb, gidx = _route(score, score + rb_ref[...])
    rank, counts = _group_ranks(gidx, carry_ref, step - 1)
    record = jnp.concatenate([comb, gidx.astype(F32), rank, jnp.zeros((ROUTE_LANES - ROUTE_RANK - 1, TM), F32)], axis=0)
    route_ref[...] = record[0:ROUTE_ROWS]
    tok_ref[...] = record.T
    cnt_ref[0] = counts


def _init_pipeline(carry_ref, nprev_ref):
    @pl.when(pl.program_id(0) == 0)
    def _():
        nprev_ref[...] = jnp.zeros_like(nprev_ref)
        carry_ref[...] = jnp.zeros_like(carry_ref)


def _outproj0_kernel(a_ref, yf_ref, yb_ref, gg_ref, x_ref, c_ref, mod_ref, wo_ref, g_ref, rw_ref, rb_ref,
                     h_ref, n_ref, route_ref, cnt_ref, tok_ref, carry_ref, nprev_ref):
    _init_pipeline(carry_ref, nprev_ref)
    tile = jnp.minimum(pl.program_id(0), N_TILES - 1)
    m2 = ((yf_ref[0] + yb_ref[0]) * gg_ref[...]).astype(BF16)
    _outproj_step(a_ref[...], m2, _token_tile(x_ref, c_ref, tile), mod_ref, wo_ref, g_ref, rw_ref, rb_ref,
                  h_ref, n_ref, route_ref, cnt_ref, tok_ref, carry_ref, nprev_ref)


def _outproj1_kernel(cx_ref, cxp_ref, cxn_ref, bg_ref, att_ref, cw_ref, x_ref, mod_ref, wo_ref, g_ref, rw_ref, rb_ref,
                     h_ref, n_ref, route_ref, cnt_ref, tok_ref, buf_ref, carry_ref, nprev_ref):
    _init_pipeline(carry_ref, nprev_ref)
    tile = jnp.minimum(pl.program_id(0), N_LAT_TILES - 1)
    _fill_padded(buf_ref, cxp_ref, cx_ref, cxn_ref, HALO_S, tile)
    conv = _depthwise(buf_ref, cw_ref, CONV_C, HALO_S - 1, TM, 0)
    m1 = (bg_ref[...] * conv).astype(BF16)
    _outproj_step(m1, att_ref[...], x_ref[...], mod_ref, wo_ref, g_ref, rw_ref, rb_ref,
                  h_ref, n_ref, route_ref, cnt_ref, tok_ref, carry_ref, nprev_ref)


def _outproj_common(n_tiles):
    hold = n_tiles - 1
    tile = _held(hold)
    routed = lambda i: jnp.maximum(i - 1, 0)
    in_specs = [pl.BlockSpec((1, 1, 6 * D), _mod_row(TM, hold)),
                pl.BlockSpec((D, D), lambda i: (0, 0)),
                pl.BlockSpec((1, D), lambda i: (0, 0)),
                pl.BlockSpec((N_EXPERTS, D), lambda i: (0, 0)),
                pl.BlockSpec((N_EXPERTS, 1), lambda i: (0, 0))]
    out_specs = [pl.BlockSpec((TM, D), lambda i: (tile(i), 0)),
                 pl.BlockSpec((TM, D), lambda i: (tile(i), 0)),
                 pl.BlockSpec((ROUTE_ROWS, TM), lambda i: (0, routed(i))),
                 pl.BlockSpec((1, 8, 128), lambda i: (routed(i), 0, 0)),
                 pl.BlockSpec((TM, ROUTE_LANES), lambda i: (routed(i), 0))]
    rows = n_tiles * TM
    out_shape = [jax.ShapeDtypeStruct((rows, D), F32), jax.ShapeDtypeStruct((rows, D), BF16),
                 jax.ShapeDtypeStruct((ROUTE_ROWS, rows), F32), jax.ShapeDtypeStruct((n_tiles, 8, 128), F32),
                 jax.ShapeDtypeStruct((rows, ROUTE_LANES), F32)]
    scratch = [pltpu.VMEM((8, 128), F32), pltpu.VMEM((TM, D), F32)]
    return in_specs, out_specs, out_shape, scratch


def _outproj0(a, yf, yb, gg, x, c, mod, wo, g, rw_t, rb):
    hold = N_TILES - 1
    tok = pl.BlockSpec((TM, HALF), lambda i: (_held(hold)(i), 0))
    scan_tok = pl.BlockSpec((1, TM, HALF), functools.partial(_seq_major, hold=hold))
    common_in, out_specs, out_shape, scratch = _outproj_common(N_TILES)
    return pl.pallas_call(
        _outproj0_kernel,
        grid=(N_TILES + 1,),
        in_specs=[tok, scan_tok, scan_tok, tok] + _token_specs(hold) + common_in,
        out_specs=out_specs,
        out_shape=out_shape,
        scratch_shapes=scratch,
        compiler_params=_params("arbitrary"),
        name="outproj0",
    )(a, yf, yb, gg, x, c, mod, wo, g, rw_t, rb)


def _outproj1(cx, bg, att, conv_w, x, mod, wo, g, rw_t, rb):
    hold = N_LAT_TILES - 1
    tok = pl.BlockSpec((TM, HALF), lambda i: (_held(hold)(i), 0))
    ps, ns = _halo_specs(HALO_S, hold)
    common_in, out_specs, out_shape, scratch = _outproj_common(N_LAT_TILES)
    return pl.pallas_call(
        _outproj1_kernel,
        grid=(N_LAT_TILES + 1,),
        in_specs=[tok, ps, ns, tok, tok, pl.BlockSpec((CONV_C, 8, HALF), lambda i: (0, 0, 0)),
                  pl.BlockSpec((TM, D), lambda i: (_held(hold)(i), 0))] + common_in,
        out_specs=out_specs,
        out_shape=out_shape,
        scratch_shapes=[pltpu.VMEM((TM + 2 * HALO_S, HALF), F32)] + scratch,
        compiler_params=_params("arbitrary"),
        name="outproj1",
    )(cx, cx, cx, bg, att, _sublane_replicated(conv_w), x, mod, wo, g, rw_t, rb)


def _moe_kernel(cnt_ref, n_ref, rt_ref, r_ref, wg_ref, wu_ref, wd_ref, h_ref, mod_ref, fg_ref, o_ref,
                hid_ref, *, final_norm, first_tile, subtiles):
    i = pl.program_id(0)
    g = pl.program_id(1)

    @pl.when(g == 0)
    def _():
        o_ref[...] = jnp.zeros_like(o_ref)

    gf = g.astype(F32)
    slot_row = lax.broadcasted_iota(jnp.int32, (CAP, TMOE), 0).astype(F32)
    slot_col = lax.broadcasted_iota(jnp.int32, (TMOE, CAP), 1).astype(F32)

    counts = [cnt_ref[(first_tile + i * subtiles + s) * N_GROUPS + g] for s in range(subtiles)]
    starts = [sum(counts[:s], jnp.int32(0)) for s in range(subtiles)]
    total = starts[-1] + counts[-1]
    tile_rows = [slice(s * TMOE, (s + 1) * TMOE) for s in range(subtiles)]

    def run_chunk(k, sources):
        base = k * CAP
        sels, shifts = [], []
        xg = None
        for s in sources:
            shift = (starts[s] - base).astype(F32)
            in_group = rt_ref[ROUTE_GROUP:ROUTE_GROUP + 1, tile_rows[s]] == gf
            slot = jnp.where(in_group, rt_ref[ROUTE_RANK:ROUTE_RANK + 1, tile_rows[s]] + shift, -1.0)
            sel = slot_row == slot
            part = jnp.dot(jnp.where(sel, 1.0, 0.0).astype(BF16), n_ref[tile_rows[s], :], preferred_element_type=F32)
            xg = part if xg is None else xg + part
            sels.append(sel)
            shifts.append(shift)
        xg = xg.astype(BF16)
        for j in range(PER_GROUP):
            cw = None
            for s, sel in zip(sources, sels):
                comb_row = rt_ref[pl.ds(g * PER_GROUP + j, 1), tile_rows[s]]
                part = jnp.sum(jnp.where(sel, comb_row, 0.0), axis=1, keepdims=True)
                cw = part if cw is None else cw + part
            hid = (_silu(jnp.dot(xg, wg_ref[j], preferred_element_type=F32))
                   * jnp.dot(xg, wu_ref[j], preferred_element_type=F32) * cw)
            hid_ref[:, j * D_FF:(j + 1) * D_FF] = hid.astype(BF16)
        y = jnp.dot(hid_ref[...], wd_ref[...].reshape(PER_GROUP * D_FF, D),
                    preferred_element_type=F32).astype(BF16)
        for s, shift in zip(sources, shifts):
            in_group = r_ref[tile_rows[s], ROUTE_GROUP:ROUTE_GROUP + 1] == gf
            slot = r_ref[tile_rows[s], ROUTE_RANK:ROUTE_RANK + 1] + shift
            back = jnp.where(jnp.logical_and(in_group, slot_col == slot), 1.0, 0.0).astype(BF16)
            o_ref[tile_rows[s], :] += jnp.dot(back, y, preferred_element_type=F32)

    def loop(lo, hi, sources):
        def body(k, carry):
            run_chunk(k, sources)
            return carry
        lax.fori_loop(lo, hi, body, 0)

    for s in range(subtiles):
        end = starts[s] + counts[s]
        first_inside = (starts[s] + (CAP - 1)) // CAP
        if s + 1 < subtiles:
            loop(first_inside, end // CAP, [s])

            @pl.when(end % CAP != 0)
            def _(s=s, end=end):
                run_chunk(end // CAP, [s, s + 1])
        else:
            loop(first_inside, (end + (CAP - 1)) // CAP, [s])

    @pl.when(g == N_GROUPS - 1)
    def _():
        out = h_ref[...] + mod_ref[0][:, 5 * D:6 * D] * o_ref[...]
        if final_norm:
            out = out * lax.rsqrt(jnp.mean(out * out, axis=-1, keepdims=True) + EPS) * fg_ref[...]
        o_ref[...] = out


def _moe_chunks(counts):
    sub = TMOE // TM
    return counts[sub - 1::sub, :N_GROUPS, 0].astype(jnp.int32).reshape(-1)


def _moe(n_chunks, n, route_t, route, wg, wu, wd, h, mod, final_g, *, first_tile, n_tiles, subtiles, final_norm,
         name):
    assert subtiles in (1, 2) and first_tile % subtiles == 0 and n_tiles % subtiles == 0
    step = subtiles * TMOE
    first = first_tile // subtiles
    mod_row = lambda i, g, nch: (jnp.minimum(((first + i) * step) // SEQ, BATCH), 0, 0)
    tok = pl.BlockSpec((step, D), lambda i, g, nch: (first + i, 0))
    grid_spec = pltpu.PrefetchScalarGridSpec(
        num_scalar_prefetch=1,
        grid=(n_tiles // subtiles, N_GROUPS),
        in_specs=[tok,
                  pl.BlockSpec((ROUTE_ROWS, step), lambda i, g, nch: (0, first + i)),
                  pl.BlockSpec((step, ROUTE_LANES), lambda i, g, nch: (first + i, 0)),
                  pl.BlockSpec((PER_GROUP, D, D_FF), lambda i, g, nch: (g, 0, 0)),
                  pl.BlockSpec((PER_GROUP, D, D_FF), lambda i, g, nch: (g, 0, 0)),
                  pl.BlockSpec((PER_GROUP, D_FF, D), lambda i, g, nch: (g, 0, 0)),
                  tok,
                  pl.BlockSpec((1, 1, 6 * D), mod_row),
                  pl.BlockSpec((1, D), lambda i, g, nch: (0, 0))],
        out_specs=pl.BlockSpec((step, D), lambda i, g, nch: (i, 0)),
        scratch_shapes=[pltpu.VMEM((CAP, PER_GROUP * D_FF), BF16)])
    return pl.pallas_call(
        functools.partial(_moe_kernel, final_norm=final_norm, first_tile=first_tile, subtiles=subtiles),
        grid_spec=grid_spec,
        out_shape=jax.ShapeDtypeStruct((n_tiles * TMOE, D), F32),
        compiler_params=_params("parallel", "arbitrary", vmem=VMEM_LIMIT_MOE),
        name=name,
    )(n_chunks, n, route_t, route, wg, wu, wd, h, mod, final_g)


def _inproj1_kernel(x_ref, c_ref, mod_ref, g_ref, w_ref, wvt_ref, wg_ref, wu_ref, wd_ref,
                    cx_ref, bg_ref, q_ref, k_ref, vt_ref, wg_out, wu_out, wd_out):
    _cast_weights((wg_ref, wu_ref, wd_ref), (wg_out, wu_out, wd_out))
    mod = mod_ref[0]
    n = _rms_mod(_token_tile(x_ref, c_ref, rows=TIN), g_ref[...], mod[:, D:2 * D], mod[:, 0:D]).astype(BF16)

    def proj(c):
        return jnp.dot(n, w_ref[:, c * HALF:(c + 1) * HALF], preferred_element_type=F32)

    conv_in, conv_gate = proj(0), proj(2)
    out_gate = proj(1)
    cx_ref[...] = conv_gate * conv_in
    q = proj(3)
    bg_ref[...] = out_gate
    k = proj(4)
    q_ref[...] = (q * (HEAD_DIM ** -0.5 * LOG2E)).astype(BF16)
    vt = _nt_dot(wvt_ref[...], n)
    k_ref[...] = k.astype(BF16)
    vt_ref[...] = vt.astype(BF16)


def _inproj1(x, c, mod, g, w, moe_wg, moe_wu, moe_wd):
    steps = N_TOK // TIN
    assert CAST_STEPS <= steps
    tok = pl.BlockSpec((TIN, HALF), lambda i: (i, 0))
    f = jax.ShapeDtypeStruct((N_TOK, HALF), F32)
    h = jax.ShapeDtypeStruct((N_TOK, HALF), BF16)
    cast_in, cast_out, cast_shape = _cast_specs(1, lambda i: i)
    return pl.pallas_call(
        _inproj1_kernel,
        grid=(steps,),
        in_specs=_token_specs(rows=TIN) + [
                  pl.BlockSpec((1, 1, 6 * D), _mod_row(TIN)),
                  pl.BlockSpec((1, D), lambda i: (0, 0)),
                  pl.BlockSpec((D, 5 * HALF), lambda i: (0, 0)),
                  pl.BlockSpec((HALF, D), lambda i: (0, 0))] + cast_in,
        out_specs=[tok] * 4 + [pl.BlockSpec((HALF, TIN), lambda i: (0, i))] + cast_out,
        out_shape=[f, f, h, h, jax.ShapeDtypeStruct((HALF, N_TOK), BF16)] + cast_shape,
        compiler_params=_params("arbitrary"),
        name="inproj1",
    )(x, c, mod, g, w[:, :5 * HALF], w[:, 5 * HALF:].T, moe_wg, moe_wu, moe_wd)


def _natten_kernel(q_ref, kp_ref, kc_ref, kn_ref, vp_ref, vc_ref, vn_ref, kx_ref, vx_ref, bias_ref, o_ref):
    pair = 2 * HEAD_DIM
    low = lax.broadcasted_iota(jnp.int32, (TM, pair), 1) < HEAD_DIM
    head_mask = [low.astype(F32).astype(BF16), jnp.logical_not(low).astype(F32).astype(BF16)]
    top = lax.broadcasted_iota(jnp.int32, (pair, TM), 0) < HEAD_DIM
    own_rows = [top.astype(F32).astype(BF16), jnp.logical_not(top).astype(F32).astype(BF16)]
    def raw_scores(head):
        g, hh = divmod(head, 2)
        sl = slice(pair * g, pair * (g + 1))
        qm = q_ref[:, sl] * head_mask[hh]
        return [_nt_dot(qm, k_ref[:, sl]) for k_ref in (kp_ref, kc_ref, kn_ref, kx_ref)]

    def biased(head, s):
        s = [s[t] + bias_ref[0, head, :, t * TM:(t + 1) * TM] for t in range(3)] + [s[3]]
        m = jnp.max(jnp.maximum(jnp.maximum(s[0], s[1]), jnp.maximum(s[2], s[3])), axis=-1, keepdims=True)
        return s, m

    def attend(head, s, m):
        g, hh = divmod(head, 2)
        sl = slice(pair * g, pair * (g + 1))
        p = jnp.concatenate([jnp.exp2(st - m).astype(BF16) for st in s], axis=1)
        lhs = jnp.concatenate([vt_ref[sl, :] * own_rows[hh] + own_rows[1 - hh]
                               for vt_ref in (vp_ref, vc_ref, vn_ref, vx_ref)], axis=1)
        acc = _nt_dot(lhs, p)
        if hh == 0:
            return acc[0:HEAD_DIM] * (1.0 / acc[HEAD_DIM:HEAD_DIM + 1])
        return acc[HEAD_DIM:pair] * (1.0 / acc[0:1])

    pending = biased(0, raw_scores(0))
    halves = []
    for head in range(N_HEADS):
        current = pending
        if head + 1 < N_HEADS:
            pending = biased(head + 1, raw_scores(head + 1))
        halves.append(attend(head, *current))
        if head % 2 == 1:
            sl = slice(pair * (head // 2), pair * (head // 2 + 1))
            o_ref[:, sl] = jnp.concatenate(halves, axis=0).T.astype(o_ref.dtype)
            halves = []


def _natten_bias(rpb):
    n_rows = SEQ // GRID_W
    n_dr, n_dc = 2 * WIN_H - 1, 2 * WIN_W - 1
    i = np.arange(ROWS_Q)
    j = np.arange(3 * ROWS_Q)
    col = np.arange(GRID_W)
    col_start = np.clip(col - WIN_W // 2, 0, GRID_W - WIN_W)
    col_ok = (col[None, :] >= col_start[:, None]) & (col[None, :] < col_start[:, None] + WIN_W)
    col_idx = col[None, :] - col[:, None] + (WIN_W - 1)
    onehot = ((col_idx[None] == np.arange(n_dc)[:, None, None]) & col_ok[None]).astype(np.float32)
    col_exp = jnp.dot(rpb.reshape(N_HEADS * n_dr, n_dc).astype(F32), onehot.reshape(n_dc, GRID_W * GRID_W),
                      precision=lax.Precision.HIGHEST).reshape(N_HEADS, n_dr, GRID_W, GRID_W)
    col_exp = jnp.where(col_ok[None, None], col_exp * LOG2E, NEG)
    masked = jnp.full((N_HEADS, GRID_W, GRID_W), NEG, F32)
    kinds = []
    for r0 in (0, ROWS_Q, n_rows - ROWS_Q):
        r = r0 + i
        kr = r0 - ROWS_Q + j
        r_start = np.clip(r - WIN_H // 2, 0, n_rows - WIN_H)
        row_ok = ((kr[None, :] >= r_start[:, None]) & (kr[None, :] < r_start[:, None] + WIN_H)
                  & (kr[None, :] >= 0) & (kr[None, :] < n_rows))
        row_idx = kr[None, :] - r[:, None] + (WIN_H - 1)
        rows = [jnp.concatenate([col_exp[:, row_idx[qi, kj]] if row_ok[qi, kj] else masked for kj in range(3 * ROWS_Q)],
                                axis=-1) for qi in range(ROWS_Q)]
        kinds.append(jnp.concatenate(rows, axis=1))
    return jnp.stack(kinds)


def _natten(q, k, vt, bias):
    def tile(b, i, off):
        return b * TILES_PER_SEQ + jnp.clip(i + off, 0, TILES_PER_SEQ - 1)

    def lat(off):
        return pl.BlockSpec((TM, HALF), lambda b, i: (tile(b, i, off), 0))

    def lat_t(off):
        return pl.BlockSpec((HALF, TM), lambda b, i: (0, tile(b, i, off)))

    ctx = pl.BlockSpec((TM, HALF), lambda b, i: (N_LAT_TILES + b, 0))
    ctx_t = pl.BlockSpec((HALF, TM), lambda b, i: (0, N_LAT_TILES + b))

    def kind(b, i):
        return (jnp.where(i == 0, 0, jnp.where(i == TILES_PER_SEQ - 1, 2, 1)), 0, 0, 0)

    return pl.pallas_call(
        _natten_kernel,
        grid=(BATCH, TILES_PER_SEQ),
        in_specs=[lat(0), lat(-1), lat(0), lat(1), lat_t(-1), lat_t(0), lat_t(1), ctx, ctx_t,
                  pl.BlockSpec((1, N_HEADS, TM, 3 * TM), kind)],
        out_specs=lat(0),
        out_shape=jax.ShapeDtypeStruct((N_LAT, HALF), BF16),
        compiler_params=_params("parallel", "arbitrary"),
        name="natten",
    )(q, k, k, k, vt, vt, vt, k, vt, bias)


def kernel(x, c, ctx, c_ctx, ada_w, ada_b, norm_mix_g, norm_ffn_g, w_out, ab_w_in, a_dw_w, a_dw_b, a_ln_g, a_ln_b,
           b_conv_w, b_conv_b, b_gate_w, b_gate_b, b_lambda, cd_w_in, c_conv_w, d_rpb, router_w, router_bias,
           moe_w_gate, moe_w_up, moe_w_down, final_g):
    x_lat = x.reshape(N_LAT, D)
    x_ctx = ctx.reshape(BATCH * CTX, D)
    cond = jnp.concatenate([c, c_ctx[None], jnp.zeros((8 - BATCH - 1, D), F32)], axis=0)
    mod = _modulation(cond, ada_w, ada_b)
    mod0 = mod[0].reshape(8, 1, 6 * D)
    mod1 = mod[1].reshape(8, 1, 6 * D)

    wo = w_out.astype(BF16)
    lat_tiles = N_LAT // TMOE
    ctx_tiles = BATCH * CTX // TMOE
    rw_t = router_w.T
    rb = router_bias.reshape(N_EXPERTS, 1)
    fg = final_g.reshape(1, D)

    gg, a_out, v = _front0(x_lat, x_ctx, mod0, norm_mix_g[0].reshape(1, D), ab_w_in[0].astype(BF16),
                           a_dw_w[0], a_dw_b[0], a_ln_g[0], a_ln_b[0], b_conv_w[0], b_conv_b[0])
    gw = b_gate_w[0]
    gate_w = (0.5 * jnp.concatenate([gw[:, 0], gw[:, 1]], axis=-1)).astype(BF16)
    yf, yb, wg0, wu0, wd0 = _scan(v, gate_w, 0.5 * b_gate_b[0], b_lambda[0].reshape(2, 1, HALF),
                                  moe_w_gate, moe_w_up, moe_w_down)
    h1, n2, route0, cnt0, tok0 = _outproj0(a_out, yf, yb, gg, x_lat, x_ctx, mod0, wo[0],
                                           norm_ffn_g[0].reshape(1, D), rw_t, rb)
    moe0 = functools.partial(_moe, _moe_chunks(cnt0), n2, route0, tok0, wg0, wu0, wd0, h1, mod0, fg,
                             final_norm=False)
    h2_lat = moe0(first_tile=0, n_tiles=lat_tiles, subtiles=2, name="moe_lat")
    h2_ctx = moe0(first_tile=lat_tiles, n_tiles=ctx_tiles, subtiles=1, name="moe_ctx")

    cx, bg, q, k, vv, wg1, wu1, wd1 = _inproj1(h2_lat, h2_ctx, mod1, norm_mix_g[1].reshape(1, D),
                                               cd_w_in[0].astype(BF16), moe_w_gate, moe_w_up, moe_w_down)
    att = _natten(q, k, vv, _natten_bias(d_rpb[0]))
    h3, n4, route1, cnt1, tok1 = _outproj1(cx, bg, att, c_conv_w[0], h2_lat, mod1, wo[1],
                                           norm_ffn_g[1].reshape(1, D), rw_t, rb)
    out = _moe(_moe_chunks(cnt1), n4, route1, tok1, wg1, wu1, wd1, h3, mod1, fg,
               first_tile=0, n_tiles=lat_tiles, subtiles=2, final_norm=True, name="moe_final")
    return out.reshape(BATCH, SEQ, D)
```

```python
import functools
import math

import jax
import jax.numpy as jnp
import numpy as np
from jax import lax
from jax.experimental import pallas as pl
from jax.experimental.pallas import tpu as pltpu

F32 = jnp.float32
BF16 = jnp.bfloat16

D = 1024
BATCH = 2
SEQ = 8192
CTX = 256
GRID_W = 64
N_LAT = BATCH * SEQ
N_TOK = N_LAT + BATCH * CTX
HALF = 512
CONV_A = 31
CONV_B = 4
CONV_C = 3
LRU_BLOCK = 128
N_LRU_BLOCKS = HALF // LRU_BLOCK
LRU_C = 8.0
HEAD_DIM = 64
N_HEADS = HALF // HEAD_DIM
WIN_H = 8
WIN_W = 16
N_EXPERTS = 16
N_GROUPS = 4
PER_GROUP = N_EXPERTS // N_GROUPS
D_FF = 512
EPS = 1e-6
NEG = -1e30
LOG2E = math.log2(math.e)

TM = 256
TILES_PER_SEQ = SEQ // TM
N_LAT_TILES = N_LAT // TM
N_TILES = N_TOK // TM
TIN = 512
TMOE = 512
CAP = 128
ROUTE_ROWS = 24
ROUTE_LANES = 128
ROUTE_GROUP = 16
ROUTE_RANK = 17
HALO_A = 16
HALO_S = 8
ROWS_Q = TM // GRID_W
VMEM_LIMIT = 48 * 1024 * 1024
VMEM_LIMIT_MOE = 56 * 1024 * 1024


def _params(*sem, vmem=VMEM_LIMIT):
    return pltpu.CompilerParams(dimension_semantics=sem, vmem_limit_bytes=vmem)


def _sigmoid(x):
    return 0.5 * jnp.tanh(0.5 * x) + 0.5


def _silu(x):
    return x * _sigmoid(x)


def _gelu_tanh(x):
    return 0.5 * x * (1.0 + jnp.tanh(0.7978845608028654 * (x + 0.044715 * (x * x * x))))


def _rms_mod(x, g, scale, shift):
    y = x * lax.rsqrt(jnp.mean(x * x, axis=-1, keepdims=True) + EPS) * g
    return y * (1.0 + scale) + shift


def _nt_dot(a, b):
    return lax.dot_general(a, b, (((1,), (1,)), ((), ())), preferred_element_type=F32)


def _held(hold):
    return (lambda i: i) if hold is None else (lambda i: jnp.minimum(i, hold))


def _mod_row(tile_rows, hold=None):
    per_seq = SEQ // tile_rows
    tile = _held(hold)
    return lambda i: (jnp.minimum(tile(i) // per_seq, BATCH), 0, 0)


def _mod_kernel(c_ref, w_ref, b_ref, o_ref):
    c = c_ref[...]
    s = _silu(c).astype(BF16)
    o_ref[0] = jnp.dot(s, w_ref[0].astype(BF16), preferred_element_type=F32) + b_ref[0]


def _modulation(cond, ada_w, ada_b):
    depth = ada_w.shape[0]
    nb = 1536
    return pl.pallas_call(
        _mod_kernel,
        grid=(depth, 6 * D // nb),
        in_specs=[pl.BlockSpec((8, D), lambda l, j: (0, 0)),
                  pl.BlockSpec((1, D, nb), lambda l, j: (l, 0, j)),
                  pl.BlockSpec((1, 1, nb), lambda l, j: (l, 0, j))],
        out_specs=pl.BlockSpec((1, 8, nb), lambda l, j: (l, 0, j)),
        out_shape=jax.ShapeDtypeStruct((depth, 8, 6 * D), F32),
        compiler_params=_params("parallel", "parallel"),
        name="modulation",
    )(cond, ada_w, ada_b.reshape(depth, 1, 6 * D))


def _token_specs(hold=None, rows=TM):
    tile = _held(hold)
    lat_tiles = N_LAT // rows
    lat = pl.BlockSpec((rows, D), lambda i: (jnp.minimum(tile(i), lat_tiles - 1), 0))
    ctx = pl.BlockSpec((rows, D), lambda i: (jnp.maximum(tile(i) - lat_tiles, 0), 0))
    return [lat, ctx]


def _token_tile(lat_ref, ctx_ref, tile=None, rows=TM):
    tile = pl.program_id(0) if tile is None else tile
    return jnp.where(tile < N_LAT // rows, lat_ref[...], ctx_ref[...])


def _halo_specs(halo, hold=None):
    per_tile = TM // halo
    last = N_TOK // halo - 1
    tile = _held(hold)
    prev = pl.BlockSpec((halo, HALF), lambda i: (jnp.maximum(tile(i) * per_tile - 1, 0), 0))
    nxt = pl.BlockSpec((halo, HALF), lambda i: (jnp.minimum((tile(i) + 1) * per_tile, last), 0))
    return prev, nxt


def _seq_edges(i):
    is_ctx = i >= N_LAT_TILES
    first = jnp.logical_or(is_ctx, i % TILES_PER_SEQ == 0)
    last = jnp.logical_or(is_ctx, i % TILES_PER_SEQ == TILES_PER_SEQ - 1)
    return first, last


def _fill_padded(buf_ref, prev_ref, cur_ref, next_ref, halo, tile=None):
    first, last = _seq_edges(pl.program_id(0) if tile is None else tile)
    buf_ref[0:halo, :] = jnp.where(first, 0.0, prev_ref[...])
    buf_ref[halo:halo + TM, :] = cur_ref[...]
    buf_ref[halo + TM:halo + TM + halo, :] = jnp.where(last, 0.0, next_ref[...])


def _tap_weight(w_ref, k, rows):
    return jnp.concatenate([w_ref[k]] * (rows // 8), axis=0)


def _sublane_replicated(w):
    return jnp.broadcast_to(w[:, None, :], (w.shape[0], 8, w.shape[1]))


def _depthwise(buf_ref, w_ref, taps, first_off, rows, row0):
    acc = None
    for k in range(taps):
        term = _tap_weight(w_ref, k, rows) * buf_ref[row0 + first_off + k:row0 + first_off + k + rows, :]
        acc = term if acc is None else acc + term
    return acc


def _seq_major(i, hold=None):
    i = _held(hold)(i)
    is_lat = i < N_LAT_TILES
    return (jnp.where(is_lat, i // TILES_PER_SEQ, i - N_LAT_TILES), jnp.where(is_lat, 1 + i % TILES_PER_SEQ, 0), 0)


CONV_ROWS = 32
PAD_A_ROWS = TM + 2 * HALO_A
PAD_B_ROWS = TM + 2 * HALO_S
SHIFT_ROWS = PAD_A_ROWS - 8


def _front0_kernel(x_ref, c_ref, xp_ref, xn_ref, mod_ref, g_ref, w_ref,
                   dww_ref, dwb_ref, lng_ref, lnb_ref, cw_ref, cb_ref,
                   gg_ref, a_ref, v_ref, bufa_ref, bufb_ref, shift_ref):
    step = pl.program_id(0)

    @pl.when(step == 0)
    def _():
        bufa_ref[...] = jnp.zeros_like(bufa_ref)
        bufb_ref[...] = jnp.zeros_like(bufb_ref)

    tile = jnp.minimum(step, N_TILES - 1)
    first, last = _seq_edges(tile)

    def run(new, old):
        mod = mod_ref[0]
        rows_in = jnp.concatenate([xp_ref[...], _token_tile(x_ref, c_ref, tile), xn_ref[...]], axis=0)
        n = _rms_mod(rows_in, g_ref[...], mod[:, D:2 * D], mod[:, 0:D]).astype(BF16)

        def proj(c, rows=slice(None)):
            return jnp.dot(n[rows], w_ref[:, c * HALF:(c + 1) * HALF], preferred_element_type=F32)

        def conv_chunk(r):
            row0 = r * CONV_ROWS
            u = None
            for k in range(CONV_A):
                off = HALO_A - CONV_A // 2 + k
                base = row0 + off - off % 8
                term = _tap_weight(dww_ref, k, CONV_ROWS) * shift_ref[off % 8, base:base + CONV_ROWS, :]
                u = term if u is None else u + term
            u = u + dwb_ref[...]
            mu = jnp.mean(u, axis=-1, keepdims=True)
            uc = u - mu
            var = jnp.mean(uc * uc, axis=-1, keepdims=True)
            y = uc * lax.rsqrt(var + EPS) * lng_ref[...] + lnb_ref[...]
            a_ref[row0:row0 + CONV_ROWS, :] = _silu(y).astype(a_ref.dtype)
            acc = None
            for k in range(CONV_B):
                lo = row0 + HALO_S - 2 + k
                term = _tap_weight(cw_ref, k, CONV_ROWS) * bufb_ref[old, lo:lo + CONV_ROWS, :]
                acc = term if acc is None else acc + term
            v_ref[0, row0:row0 + CONV_ROWS, :] = acc + cb_ref[...]

        for s in range(8):
            shift_ref[s] = bufa_ref[old, s:s + SHIFT_ROWS, :]
        def conv_quarter(q):
            per_quarter = TM // CONV_ROWS // 4
            for r in range(q * per_quarter, (q + 1) * per_quarter):
                conv_chunk(r)

        value, gate = proj(0), proj(1)
        conv_quarter(0)
        ua = value * _sigmoid(gate)
        bufa_ref[new, 0:HALO_A, :] = jnp.where(first, 0.0, ua[0:HALO_A])
        bufa_ref[new, HALO_A:HALO_A + TM, :] = ua[HALO_A:HALO_A + TM]
        bufa_ref[new, HALO_A + TM:PAD_A_ROWS, :] = jnp.where(last, 0.0, ua[HALO_A + TM:PAD_A_ROWS])
        recur = proj(2)
        conv_quarter(1)
        lo = HALO_A - HALO_S
        bufb_ref[new, 0:HALO_S, :] = jnp.where(first, 0.0, recur[lo:HALO_A])
        bufb_ref[new, HALO_S:HALO_S + TM, :] = recur[HALO_A:HALO_A + TM]
        bufb_ref[new, HALO_S + TM:PAD_B_ROWS, :] = jnp.where(last, 0.0, recur[HALO_A + TM:HALO_A + TM + HALO_S])
        gelu_in = proj(3, slice(HALO_A, HALO_A + TM))
        conv_quarter(2)
        gg_ref[...] = _gelu_tanh(gelu_in).astype(gg_ref.dtype)
        conv_quarter(3)

    @pl.when(step % 2 == 0)
    def _():
        run(0, 1)

    @pl.when(step % 2 == 1)
    def _():
        run(1, 0)


def _front0(x, c, mod, g, w, dw_w, dw_b, ln_g, ln_b, conv_w, conv_b):
    hold = N_TILES - 1
    tile = _held(hold)
    lagged = lambda i: jnp.maximum(i - 1, 0)
    per_tile = TM // HALO_A
    last_halo = N_LAT // HALO_A - 1
    prev = pl.BlockSpec((HALO_A, D), lambda i: (jnp.clip(tile(i) * per_tile - 1, 0, last_halo), 0))
    nxt = pl.BlockSpec((HALO_A, D), lambda i: (jnp.clip((tile(i) + 1) * per_tile, 0, last_halo), 0))
    vec = pl.BlockSpec((1, HALF), lambda i: (0, 0))
    return pl.pallas_call(
        _front0_kernel,
        grid=(N_TILES + 1,),
        in_specs=_token_specs(hold) + [prev, nxt,
                  pl.BlockSpec((1, 1, 6 * D), _mod_row(TM, hold)),
                  pl.BlockSpec((1, D), lambda i: (0, 0)),
                  pl.BlockSpec((D, 4 * HALF), lambda i: (0, 0)),
                  pl.BlockSpec((CONV_A, 8, HALF), lambda i: (0, 0, 0)), vec, vec, vec,
                  pl.BlockSpec((CONV_B, 8, HALF), lambda i: (0, 0, 0)), vec],
        out_specs=[pl.BlockSpec((TM, HALF), lambda i: (tile(i), 0)),
                   pl.BlockSpec((TM, HALF), lambda i: (lagged(i), 0)),
                   pl.BlockSpec((1, TM, HALF), lambda i: _seq_major(lagged(i)))],
        out_shape=[jax.ShapeDtypeStruct((N_TOK, HALF), BF16), jax.ShapeDtypeStruct((N_TOK, HALF), BF16),
                   jax.ShapeDtypeStruct((BATCH, SEQ + CTX, HALF), F32)],
        scratch_shapes=[pltpu.VMEM((2, PAD_A_ROWS, HALF), F32), pltpu.VMEM((2, PAD_B_ROWS, HALF), F32),
                        pltpu.VMEM((8, SHIFT_ROWS, HALF), F32)],
        compiler_params=_params("arbitrary"),
        name="front0",
    )(x, c, x, x, mod, g, w, _sublane_replicated(dw_w), dw_b.reshape(1, HALF), ln_g.reshape(1, HALF),
      ln_b.reshape(1, HALF), _sublane_replicated(conv_w), conv_b.reshape(1, HALF))


SCAN_UNROLL = 8


CAST_STEPS = 2 * N_EXPERTS


def _cast_specs(layer, step_of):
    def half(*idx):
        s = jnp.minimum(step_of(*idx), CAST_STEPS - 1)
        return s // 2, s % 2

    def src(*idx):
        e, h = half(*idx)
        return (layer, e, h, 0)

    def dst(*idx):
        e, h = half(*idx)
        return (e, h, 0)

    in_specs = [pl.BlockSpec((1, 1, D // 2, D_FF), src), pl.BlockSpec((1, 1, D // 2, D_FF), src),
                pl.BlockSpec((1, 1, D_FF // 2, D), src)]
    out_specs = [pl.BlockSpec((1, D // 2, D_FF), dst), pl.BlockSpec((1, D // 2, D_FF), dst),
                 pl.BlockSpec((1, D_FF // 2, D), dst)]
    out_shape = [jax.ShapeDtypeStruct((N_EXPERTS, D, D_FF), BF16), jax.ShapeDtypeStruct((N_EXPERTS, D, D_FF), BF16),
                 jax.ShapeDtypeStruct((N_EXPERTS, D_FF, D), BF16)]
    return in_specs, out_specs, out_shape


def _cast_weights(srcs, dsts):
    for src, dst in zip(srcs, dsts):
        dst[0] = src[0, 0].astype(BF16)


def _scan_kernel(vf_ref, vb_ref, w_ref, gb_ref, lam_ref, wg_ref, wu_ref, wd_ref,
                 yf_ref, yb_ref, wg_out, wu_out, wd_out, h_ref, a_ref, b_ref):
    _cast_weights((wg_ref, wu_ref, wd_ref), (wg_out, wu_out, wd_out))

    @pl.when(pl.program_id(0) == 0)
    def _():
        h_ref[...] = jnp.zeros_like(h_ref)

    for d, v_ref in enumerate((vf_ref, vb_ref)):
        v = v_ref[...].reshape(BATCH * TM, HALF)
        vb = v.astype(BF16)
        neg = -lam_ref[d]
        softplus = jnp.maximum(neg, 0.0) + jnp.log(1.0 + jnp.exp(-jnp.abs(neg)))
        rate = (-0.5 * LRU_C * LOG2E) * softplus
        for n in range(N_LRU_BLOCKS):
            sl = slice(n * LRU_BLOCK, (n + 1) * LRU_BLOCK)
            g = jnp.dot(vb[:, sl], w_ref[d, n], preferred_element_type=F32)
            tr = jnp.tanh(g[:, 0:LRU_BLOCK] + gb_ref[d, 0:1, sl])
            ti = jnp.tanh(g[:, LRU_BLOCK:2 * LRU_BLOCK] + gb_ref[d, 1:2, sl])
            a = jnp.exp2(rate[:, sl] * tr + rate[:, sl])
            a_ref[d, :, sl] = a
            b_ref[d, :, sl] = jnp.sqrt(1.0 - a * a) * ((0.5 * ti + 0.5) * v[:, sl])

    def body(s, hs):
        hs = list(hs)
        for u in range(SCAN_UNROLL):
            t = s * SCAN_UNROLL + u
            for d, y_ref in enumerate((yf_ref, yb_ref)):
                row = t if d == 0 else TM - 1 - t
                for bt in range(BATCH):
                    c = d * BATCH + bt
                    src = bt * TM + row
                    hs[c] = a_ref[d, pl.ds(src, 1), :] * hs[c] + b_ref[d, pl.ds(src, 1), :]
                    y_ref[bt, pl.ds(row, 1), :] = hs[c]
        return tuple(hs)

    init = tuple(h_ref[c:c + 1, :] for c in range(2 * BATCH))
    final = init
    for s in range(TM // SCAN_UNROLL):
        final = body(s, final)
    for c in range(2 * BATCH):
        h_ref[c:c + 1, :] = final[c]


def _scan(v, gate_w, gate_b, lam, moe_wg, moe_wu, moe_wd):
    steps = TILES_PER_SEQ + 1
    assert CAST_STEPS <= steps
    fwd = pl.BlockSpec((BATCH, TM, HALF), lambda j: (0, j, 0))
    bwd = pl.BlockSpec((BATCH, TM, HALF), lambda j: (0, jnp.where(j == 0, 0, TILES_PER_SEQ + 1 - j), 0))
    shp = jax.ShapeDtypeStruct((BATCH, SEQ + CTX, HALF), F32)
    cast_in, cast_out, cast_shape = _cast_specs(0, lambda j: j)
    return pl.pallas_call(
        _scan_kernel,
        grid=(steps,),
        in_specs=[fwd, bwd,
                  pl.BlockSpec((2, N_LRU_BLOCKS, LRU_BLOCK, 2 * LRU_BLOCK), lambda j: (0, 0, 0, 0)),
                  pl.BlockSpec((2, 2, HALF), lambda j: (0, 0, 0)),
                  pl.BlockSpec((2, 1, HALF), lambda j: (0, 0, 0))] + cast_in,
        out_specs=[fwd, bwd] + cast_out,
        out_shape=[shp, shp] + cast_shape,
        scratch_shapes=[pltpu.VMEM((8, HALF), F32), pltpu.VMEM((2, BATCH * TM, HALF), F32),
                        pltpu.VMEM((2, BATCH * TM, HALF), F32)],
        compiler_params=_params("arbitrary"),
        name="lru_scan",
    )(v, v, gate_w, gate_b, lam, moe_wg, moe_wu, moe_wd)


def _route(score, sel):
    rows = [sel[e:e + 1, :] for e in range(N_EXPERTS)]
    gbest = None
    gidx = None
    for g in range(N_GROUPS):
        top2 = None
        for p in range(PER_GROUP):
            for q in range(p + 1, PER_GROUP):
                s = rows[g * PER_GROUP + p] + rows[g * PER_GROUP + q]
                top2 = s if top2 is None else jnp.maximum(top2, s)
        if g == 0:
            gbest = top2
            gidx = jnp.zeros(top2.shape, jnp.int32)
        else:
            better = top2 > gbest
            gidx = jnp.where(better, g, gidx)
            gbest = jnp.where(better, top2, gbest)
    eint = lax.broadcasted_iota(jnp.int32, sel.shape, 0)
    eidx = eint.astype(F32)
    masked = jnp.where(jnp.right_shift(eint, 2) == gidx, sel, -jnp.inf)
    v1 = jnp.max(masked, axis=0, keepdims=True)
    i1 = jnp.min(jnp.where(masked == v1, eidx, float(N_EXPERTS)), axis=0, keepdims=True)
    masked2 = jnp.where(eidx == i1, -jnp.inf, masked)
    v2 = jnp.max(masked2, axis=0, keepdims=True)
    i2 = jnp.min(jnp.where(masked2 == v2, eidx, float(N_EXPERTS)), axis=0, keepdims=True)
    s1 = jnp.sum(jnp.where(eidx == i1, score, 0.0), axis=0, keepdims=True)
    s2 = jnp.sum(jnp.where(eidx == i2, score, 0.0), axis=0, keepdims=True)
    inv = 1.0 / (s1 + s2)
    return jnp.where(eidx == i1, s1 * inv, 0.0) + jnp.where(eidx == i2, s2 * inv, 0.0), gidx


def _group_ranks(gidx, carry_ref, tile):
    onehot = lax.broadcasted_iota(jnp.int32, (8, TM), 0) == gidx
    oh = jnp.where(onehot, 1.0, 0.0)
    before = lax.broadcasted_iota(jnp.int32, (TM, TM), 0) < lax.broadcasted_iota(jnp.int32, (TM, TM), 1)
    prefix = jnp.dot(oh.astype(BF16), jnp.where(before, 1.0, 0.0).astype(BF16), preferred_element_type=F32)
    carry = jnp.where(tile % (TMOE // TM) == 0, 0.0, carry_ref[...])
    rank = jnp.sum(jnp.where(onehot, prefix + carry[:, 0:1], 0.0), axis=0, keepdims=True)
    carry = carry + jnp.sum(oh, axis=1, keepdims=True)
    carry_ref[...] = carry
    return rank, carry


def _split_bf16(v):
    hi = v.astype(BF16)
    return hi, (v - hi.astype(F32)).astype(BF16)


def _outproj_step(m1, m2, x, mod_ref, wo_ref, g_ref, rw_ref, rb_ref, h_ref, n_ref, route_ref, cnt_ref, tok_ref, carry_ref,
                  nprev_ref):
    step = pl.program_id(0)
    n_hi, n_lo = _split_bf16(nprev_ref[...])
    w_hi, w_lo = _split_bf16(rw_ref[...])

    mod = mod_ref[0]
    mix = (jnp.dot(m1, wo_ref[0:HALF, :], preferred_element_type=F32)
           + jnp.dot(m2, wo_ref[HALF:2 * HALF, :], preferred_element_type=F32))

    logits = _nt_dot(w_hi, n_hi) + _nt_dot(w_hi, n_lo) + _nt_dot(w_lo, n_hi)

    h = x + mod[:, 2 * D:3 * D] * mix
    h_ref[...] = h
    n = _rms_mod(h, g_ref[...], mod[:, 4 * D:5 * D], mod[:, 3 * D:4 * D])
    n_ref[...] = n.astype(n_ref.dtype)
    nprev_ref[...] = n

    score = _sigmoid(logits)
    comb, gidx = _route(score, score + rb_ref[...])
    rank, counts = _group_ranks(gidx, carry_ref, step - 1)
    record = jnp.concatenate([comb, gidx.astype(F32), rank, jnp.zeros((ROUTE_LANES - ROUTE_RANK - 1, TM), F32)], axis=0)
    route_ref[...] = record[0:ROUTE_ROWS]
    tok_ref[...] = record.T
    cnt_ref[0] = counts


def _init_pipeline(carry_ref, nprev_ref):
    @pl.when(pl.program_id(0) == 0)
    def _():
        nprev_ref[...] = jnp.zeros_like(nprev_ref)
        carry_ref[...] = jnp.zeros_like(carry_ref)


def _outproj0_kernel(a_ref, yf_ref, yb_ref, gg_ref, x_ref, c_ref, mod_ref, wo_ref, g_ref, rw_ref, rb_ref,
                     h_ref, n_ref, route_ref, cnt_ref, tok_ref, carry_ref, nprev_ref):
    _init_pipeline(carry_ref, nprev_ref)
    tile = jnp.minimum(pl.program_id(0), N_TILES - 1)
    m2 = ((yf_ref[0] + yb_ref[0]) * gg_ref[...]).astype(BF16)
    _outproj_step(a_ref[...], m2, _token_tile(x_ref, c_ref, tile), mod_ref, wo_ref, g_ref, rw_ref, rb_ref,
                  h_ref, n_ref, route_ref, cnt_ref, tok_ref, carry_ref, nprev_ref)


def _outproj1_kernel(cx_ref, cxp_ref, cxn_ref, bg_ref, att_ref, cw_ref, x_ref, mod_ref, wo_ref, g_ref, rw_ref, rb_ref,
                     h_ref, n_ref, route_ref, cnt_ref, tok_ref, buf_ref, carry_ref, nprev_ref):
    _init_pipeline(carry_ref, nprev_ref)
    tile = jnp.minimum(pl.program_id(0), N_LAT_TILES - 1)
    _fill_padded(buf_ref, cxp_ref, cx_ref, cxn_ref, HALO_S, tile)
    conv = _depthwise(buf_ref, cw_ref, CONV_C, HALO_S - 1, TM, 0)
    m1 = (bg_ref[...] * conv).astype(BF16)
    _outproj_step(m1, att_ref[...], x_ref[...], mod_ref, wo_ref, g_ref, rw_ref, rb_ref,
                  h_ref, n_ref, route_ref, cnt_ref, tok_ref, carry_ref, nprev_ref)


def _outproj_common(n_tiles):
    hold = n_tiles - 1
    tile = _held(hold)
    routed = lambda i: jnp.maximum(i - 1, 0)
    in_specs = [pl.BlockSpec((1, 1, 6 * D), _mod_row(TM, hold)),
                pl.BlockSpec((D, D), lambda i: (0, 0)),
                pl.BlockSpec((1, D), lambda i: (0, 0)),
                pl.BlockSpec((N_EXPERTS, D), lambda i: (0, 0)),
                pl.BlockSpec((N_EXPERTS, 1), lambda i: (0, 0))]
    out_specs = [pl.BlockSpec((TM, D), lambda i: (tile(i), 0)),
                 pl.BlockSpec((TM, D), lambda i: (tile(i), 0)),
                 pl.BlockSpec((ROUTE_ROWS, TM), lambda i: (0, routed(i))),
                 pl.BlockSpec((1, 8, 128), lambda i: (routed(i), 0, 0)),
                 pl.BlockSpec((TM, ROUTE_LANES), lambda i: (routed(i), 0))]
    rows = n_tiles * TM
    out_shape = [jax.ShapeDtypeStruct((rows, D), F32), jax.ShapeDtypeStruct((rows, D), BF16),
                 jax.ShapeDtypeStruct((ROUTE_ROWS, rows), F32), jax.ShapeDtypeStruct((n_tiles, 8, 128), F32),
                 jax.ShapeDtypeStruct((rows, ROUTE_LANES), F32)]
    scratch = [pltpu.VMEM((8, 128), F32), pltpu.VMEM((TM, D), F32)]
    return in_specs, out_specs, out_shape, scratch


def _outproj0(a, yf, yb, gg, x, c, mod, wo, g, rw_t, rb):
    hold = N_TILES - 1
    tok = pl.BlockSpec((TM, HALF), lambda i: (_held(hold)(i), 0))
    scan_tok = pl.BlockSpec((1, TM, HALF), functools.partial(_seq_major, hold=hold))
    common_in, out_specs, out_shape, scratch = _outproj_common(N_TILES)
    return pl.pallas_call(
        _outproj0_kernel,
        grid=(N_TILES + 1,),
        in_specs=[tok, scan_tok, scan_tok, tok] + _token_specs(hold) + common_in,
        out_specs=out_specs,
        out_shape=out_shape,
        scratch_shapes=scratch,
        compiler_params=_params("arbitrary"),
        name="outproj0",
    )(a, yf, yb, gg, x, c, mod, wo, g, rw_t, rb)


def _outproj1(cx, bg, att, conv_w, x, mod, wo, g, rw_t, rb):
    hold = N_LAT_TILES - 1
    tok = pl.BlockSpec((TM, HALF), lambda i: (_held(hold)(i), 0))
    ps, ns = _halo_specs(HALO_S, hold)
    common_in, out_specs, out_shape, scratch = _outproj_common(N_LAT_TILES)
    return pl.pallas_call(
        _outproj1_kernel,
        grid=(N_LAT_TILES + 1,),
        in_specs=[tok, ps, ns, tok, tok, pl.BlockSpec((CONV_C, 8, HALF), lambda i: (0, 0, 0)),
                  pl.BlockSpec((TM, D), lambda i: (_held(hold)(i), 0))] + common_in,
        out_specs=out_specs,
        out_shape=out_shape,
        scratch_shapes=[pltpu.VMEM((TM + 2 * HALO_S, HALF), F32)] + scratch,
        compiler_params=_params("arbitrary"),
        name="outproj1",
    )(cx, cx, cx, bg, att, _sublane_replicated(conv_w), x, mod, wo, g, rw_t, rb)


def _moe_kernel(cnt_ref, n_ref, rt_ref, r_ref, wg_ref, wu_ref, wd_ref, h_ref, mod_ref, fg_ref, o_ref,
                hid_ref, *, final_norm, first_tile, subtiles):
    i = pl.program_id(0)
    g = pl.program_id(1)

    @pl.when(g == 0)
    def _():
        o_ref[...] = jnp.zeros_like(o_ref)

    gf = g.astype(F32)
    slot_row = lax.broadcasted_iota(jnp.int32, (CAP, TMOE), 0).astype(F32)
    slot_col = lax.broadcasted_iota(jnp.int32, (TMOE, CAP), 1).astype(F32)

    counts = [cnt_ref[(first_tile + i * subtiles + s) * N_GROUPS + g] for s in range(subtiles)]
    starts = [sum(counts[:s], jnp.int32(0)) for s in range(subtiles)]
    total = starts[-1] + counts[-1]
    tile_rows = [slice(s * TMOE, (s + 1) * TMOE) for s in range(subtiles)]

    def run_chunk(k, sources):
        base = k * CAP
        sels, shifts = [], []
        xg = None
        for s in sources:
            shift = (starts[s] - base).astype(F32)
            in_group = rt_ref[ROUTE_GROUP:ROUTE_GROUP + 1, tile_rows[s]] == gf
            slot = jnp.where(in_group, rt_ref[ROUTE_RANK:ROUTE_RANK + 1, tile_rows[s]] + shift, -1.0)
            sel = slot_row == slot
            part = jnp.dot(jnp.where(sel, 1.0, 0.0).astype(BF16), n_ref[tile_rows[s], :], preferred_element_type=F32)
            xg = part if xg is None else xg + part
            sels.append(sel)
            shifts.append(shift)
        xg = xg.astype(BF16)
        for j in range(PER_GROUP):
            cw = None
            for s, sel in zip(sources, sels):
                comb_row = rt_ref[pl.ds(g * PER_GROUP + j, 1), tile_rows[s]]
                part = jnp.sum(jnp.where(sel, comb_row, 0.0), axis=1, keepdims=True)
                cw = part if cw is None else cw + part
            hid = (_silu(jnp.dot(xg, wg_ref[j], preferred_element_type=F32))
                   * jnp.dot(xg, wu_ref[j], preferred_element_type=F32) * cw)
            hid_ref[:, j * D_FF:(j + 1) * D_FF] = hid.astype(BF16)
        y = jnp.dot(hid_ref[...], wd_ref[...].reshape(PER_GROUP * D_FF, D),
                    preferred_element_type=F32).astype(BF16)
        for s, shift in zip(sources, shifts):
            in_group = r_ref[tile_rows[s], ROUTE_GROUP:ROUTE_GROUP + 1] == gf
            slot = r_ref[tile_rows[s], ROUTE_RANK:ROUTE_RANK + 1] + shift
            back = jnp.where(jnp.logical_and(in_group, slot_col == slot), 1.0, 0.0).astype(BF16)
            o_ref[tile_rows[s], :] += jnp.dot(back, y, preferred_element_type=F32)

    def loop(lo, hi, sources):
        def body(k, carry):
            run_chunk(k, sources)
            return carry
        lax.fori_loop(lo, hi, body, 0)

    for s in range(subtiles):
        end = starts[s] + counts[s]
        first_inside = (starts[s] + (CAP - 1)) // CAP
        if s + 1 < subtiles:
            loop(first_inside, end // CAP, [s])

            @pl.when(end % CAP != 0)
            def _(s=s, end=end):
                run_chunk(end // CAP, [s, s + 1])
        else:
            loop(first_inside, (end + (CAP - 1)) // CAP, [s])

    @pl.when(g == N_GROUPS - 1)
    def _():
        out = h_ref[...] + mod_ref[0][:, 5 * D:6 * D] * o_ref[...]
        if final_norm:
            out = out * lax.rsqrt(jnp.mean(out * out, axis=-1, keepdims=True) + EPS) * fg_ref[...]
        o_ref[...] = out


def _moe_chunks(counts):
    sub = TMOE // TM
    return counts[sub - 1::sub, :N_GROUPS, 0].astype(jnp.int32).reshape(-1)


def _moe(n_chunks, n, route_t, route, wg, wu, wd, h, mod, final_g, *, first_tile, n_tiles, subtiles, final_norm,
         name):
    assert subtiles in (1, 2) and first_tile % subtiles == 0 and n_tiles % subtiles == 0
    step = subtiles * TMOE
    first = first_tile // subtiles
    mod_row = lambda i, g, nch: (jnp.minimum(((first + i) * step) // SEQ, BATCH), 0, 0)
    tok = pl.BlockSpec((step, D), lambda i, g, nch: (first + i, 0))
    grid_spec = pltpu.PrefetchScalarGridSpec(
        num_scalar_prefetch=1,
        grid=(n_tiles // subtiles, N_GROUPS),
        in_specs=[tok,
                  pl.BlockSpec((ROUTE_ROWS, step), lambda i, g, nch: (0, first + i)),
                  pl.BlockSpec((step, ROUTE_LANES), lambda i, g, nch: (first + i, 0)),
                  pl.BlockSpec((PER_GROUP, D, D_FF), lambda i, g, nch: (g, 0, 0)),
                  pl.BlockSpec((PER_GROUP, D, D_FF), lambda i, g, nch: (g, 0, 0)),
                  pl.BlockSpec((PER_GROUP, D_FF, D), lambda i, g, nch: (g, 0, 0)),
                  tok,
                  pl.BlockSpec((1, 1, 6 * D), mod_row),
                  pl.BlockSpec((1, D), lambda i, g, nch: (0, 0))],
        out_specs=pl.BlockSpec((step, D), lambda i, g, nch: (i, 0)),
        scratch_shapes=[pltpu.VMEM((CAP, PER_GROUP * D_FF), BF16)])
    return pl.pallas_call(
        functools.partial(_moe_kernel, final_norm=final_norm, first_tile=first_tile, subtiles=subtiles),
        grid_spec=grid_spec,
        out_shape=jax.ShapeDtypeStruct((n_tiles * TMOE, D), F32),
        compiler_params=_params("parallel", "arbitrary", vmem=VMEM_LIMIT_MOE),
        name=name,
    )(n_chunks, n, route_t, route, wg, wu, wd, h, mod, final_g)


def _inproj1_kernel(x_ref, c_ref, mod_ref, g_ref, w_ref, wvt_ref, wg_ref, wu_ref, wd_ref,
                    cx_ref, bg_ref, q_ref, k_ref, vt_ref, wg_out, wu_out, wd_out):
    _cast_weights((wg_ref, wu_ref, wd_ref), (wg_out, wu_out, wd_out))
    mod = mod_ref[0]
    n = _rms_mod(_token_tile(x_ref, c_ref, rows=TIN), g_ref[...], mod[:, D:2 * D], mod[:, 0:D]).astype(BF16)

    def proj(c):
        return jnp.dot(n, w_ref[:, c * HALF:(c + 1) * HALF], preferred_element_type=F32)

    conv_in, conv_gate = proj(0), proj(2)
    out_gate = proj(1)
    cx_ref[...] = conv_gate * conv_in
    q = proj(3)
    bg_ref[...] = out_gate
    k = proj(4)
    q_ref[...] = (q * (HEAD_DIM ** -0.5 * LOG2E)).astype(BF16)
    vt = _nt_dot(wvt_ref[...], n)
    k_ref[...] = k.astype(BF16)
    vt_ref[...] = vt.astype(BF16)


def _inproj1(x, c, mod, g, w, moe_wg, moe_wu, moe_wd):
    steps = N_TOK // TIN
    assert CAST_STEPS <= steps
    tok = pl.BlockSpec((TIN, HALF), lambda i: (i, 0))
    f = jax.ShapeDtypeStruct((N_TOK, HALF), F32)
    h = jax.ShapeDtypeStruct((N_TOK, HALF), BF16)
    cast_in, cast_out, cast_shape = _cast_specs(1, lambda i: i)
    return pl.pallas_call(
        _inproj1_kernel,
        grid=(steps,),
        in_specs=_token_specs(rows=TIN) + [
                  pl.BlockSpec((1, 1, 6 * D), _mod_row(TIN)),
                  pl.BlockSpec((1, D), lambda i: (0, 0)),
                  pl.BlockSpec((D, 5 * HALF), lambda i: (0, 0)),
                  pl.BlockSpec((HALF, D), lambda i: (0, 0))] + cast_in,
        out_specs=[tok] * 4 + [pl.BlockSpec((HALF, TIN), lambda i: (0, i))] + cast_out,
        out_shape=[f, f, h, h, jax.ShapeDtypeStruct((HALF, N_TOK), BF16)] + cast_shape,
        compiler_params=_params("arbitrary"),
        name="inproj1",
    )(x, c, mod, g, w[:, :5 * HALF], w[:, 5 * HALF:].T, moe_wg, moe_wu, moe_wd)


def _natten_kernel(q_ref, kp_ref, kc_ref, kn_ref, vp_ref, vc_ref, vn_ref, kx_ref, vx_ref, bias_ref, o_ref):
    pair = 2 * HEAD_DIM
    low = lax.broadcasted_iota(jnp.int32, (TM, pair), 1) < HEAD_DIM
    head_mask = [low.astype(F32).astype(BF16), jnp.logical_not(low).astype(F32).astype(BF16)]
    top = lax.broadcasted_iota(jnp.int32, (pair, TM), 0) < HEAD_DIM
    own_rows = [top.astype(F32).astype(BF16), jnp.logical_not(top).astype(F32).astype(BF16)]
    def raw_scores(head):
        g, hh = divmod(head, 2)
        sl = slice(pair * g, pair * (g + 1))
        qm = q_ref[:, sl] * head_mask[hh]
        return [_nt_dot(qm, k_ref[:, sl]) for k_ref in (kp_ref, kc_ref, kn_ref, kx_ref)]

    def biased(head, s):
        s = [s[t] + bias_ref[0, head, :, t * TM:(t + 1) * TM] for t in range(3)] + [s[3]]
        m = jnp.max(jnp.maximum(jnp.maximum(s[0], s[1]), jnp.maximum(s[2], s[3])), axis=-1, keepdims=True)
        return s, m

    def attend(head, s, m):
        g, hh = divmod(head, 2)
        sl = slice(pair * g, pair * (g + 1))
        p = jnp.concatenate([jnp.exp2(st - m).astype(BF16) for st in s], axis=1)
        lhs = jnp.concatenate([vt_ref[sl, :] * own_rows[hh] + own_rows[1 - hh]
                               for vt_ref in (vp_ref, vc_ref, vn_ref, vx_ref)], axis=1)
        acc = _nt_dot(lhs, p)
        if hh == 0:
            return acc[0:HEAD_DIM] * (1.0 / acc[HEAD_DIM:HEAD_DIM + 1])
        return acc[HEAD_DIM:pair] * (1.0 / acc[0:1])

    pending = biased(0, raw_scores(0))
    halves = []
    for head in range(N_HEADS):
        current = pending
        if head + 1 < N_HEADS:
            pending = biased(head + 1, raw_scores(head + 1))
        halves.append(attend(head, *current))
        if head % 2 == 1:
            sl = slice(pair * (head // 2), pair * (head // 2 + 1))
            o_ref[:, sl] = jnp.concatenate(halves, axis=0).T.astype(o_ref.dtype)
            halves = []


def _natten_bias(rpb):
    n_rows = SEQ // GRID_W
    n_dr, n_dc = 2 * WIN_H - 1, 2 * WIN_W - 1
    i = np.arange(ROWS_Q)
    j = np.arange(3 * ROWS_Q)
    col = np.arange(GRID_W)
    col_start = np.clip(col - WIN_W // 2, 0, GRID_W - WIN_W)
    col_ok = (col[None, :] >= col_start[:, None]) & (col[None, :] < col_start[:, None] + WIN_W)
    col_idx = col[None, :] - col[:, None] + (WIN_W - 1)
    onehot = ((col_idx[None] == np.arange(n_dc)[:, None, None]) & col_ok[None]).astype(np.float32)
    col_exp = jnp.dot(rpb.reshape(N_HEADS * n_dr, n_dc).astype(F32), onehot.reshape(n_dc, GRID_W * GRID_W),
                      precision=lax.Precision.HIGHEST).reshape(N_HEADS, n_dr, GRID_W, GRID_W)
    col_exp = jnp.where(col_ok[None, None], col_exp * LOG2E, NEG)
    col_exp = jnp.concatenate([col_exp, jnp.full((N_HEADS, 1, GRID_W, GRID_W), NEG, F32)], axis=1)
    block_of = []
    for r0 in (0, ROWS_Q, n_rows - ROWS_Q):
        r = r0 + i
        kr = r0 - ROWS_Q + j
        r_start = np.clip(r - WIN_H // 2, 0, n_rows - WIN_H)
        row_ok = ((kr[None, :] >= r_start[:, None]) & (kr[None, :] < r_start[:, None] + WIN_H)
                  & (kr[None, :] >= 0) & (kr[None, :] < n_rows))
        block_of.append(np.where(row_ok, kr[None, :] - r[:, None] + (WIN_H - 1), n_dr))
    blocks = col_exp[:, np.stack(block_of)]
    return blocks.transpose(1, 0, 2, 4, 3, 5).reshape(3, N_HEADS, TM, 3 * TM)


def _natten(q, k, vt, bias):
    def tile(b, i, off):
        return b * TILES_PER_SEQ + jnp.clip(i + off, 0, TILES_PER_SEQ - 1)

    def lat(off):
        return pl.BlockSpec((TM, HALF), lambda b, i: (tile(b, i, off), 0))

    def lat_t(off):
        return pl.BlockSpec((HALF, TM), lambda b, i: (0, tile(b, i, off)))

    ctx = pl.BlockSpec((TM, HALF), lambda b, i: (N_LAT_TILES + b, 0))
    ctx_t = pl.BlockSpec((HALF, TM), lambda b, i: (0, N_LAT_TILES + b))

    def kind(b, i):
        return (jnp.where(i == 0, 0, jnp.where(i == TILES_PER_SEQ - 1, 2, 1)), 0, 0, 0)

    return pl.pallas_call(
        _natten_kernel,
        grid=(BATCH, TILES_PER_SEQ),
        in_specs=[lat(0), lat(-1), lat(0), lat(1), lat_t(-1), lat_t(0), lat_t(1), ctx, ctx_t,
                  pl.BlockSpec((1, N_HEADS, TM, 3 * TM), kind)],
        out_specs=lat(0),
        out_shape=jax.ShapeDtypeStruct((N_LAT, HALF), BF16),
        compiler_params=_params("parallel", "arbitrary"),
        name="natten",
    )(q, k, k, k, vt, vt, vt, k, vt, bias)


def kernel(x, c, ctx, c_ctx, ada_w, ada_b, norm_mix_g, norm_ffn_g, w_out, ab_w_in, a_dw_w, a_dw_b, a_ln_g, a_ln_b,
           b_conv_w, b_conv_b, b_gate_w, b_gate_b, b_lambda, cd_w_in, c_conv_w, d_rpb, router_w, router_bias,
           moe_w_gate, moe_w_up, moe_w_down, final_g):
    x_lat = x.reshape(N_LAT, D)
    x_ctx = ctx.reshape(BATCH * CTX, D)
    cond = jnp.concatenate([c, c_ctx[None], jnp.zeros((8 - BATCH - 1, D), F32)], axis=0)
    mod = _modulation(cond, ada_w, ada_b)
    mod0 = mod[0].reshape(8, 1, 6 * D)
    mod1 = mod[1].reshape(8, 1, 6 * D)

    wo = w_out.astype(BF16)
    lat_tiles = N_LAT // TMOE
    ctx_tiles = BATCH * CTX // TMOE
    rw_t = router_w.T
    rb = router_bias.reshape(N_EXPERTS, 1)
    fg = final_g.reshape(1, D)

    gg, a_out, v = _front0(x_lat, x_ctx, mod0, norm_mix_g[0].reshape(1, D), ab_w_in[0].astype(BF16),
                           a_dw_w[0], a_dw_b[0], a_ln_g[0], a_ln_b[0], b_conv_w[0], b_conv_b[0])
    gw = b_gate_w[0]
    gate_w = (0.5 * jnp.concatenate([gw[:, 0], gw[:, 1]], axis=-1)).astype(BF16)
    yf, yb, wg0, wu0, wd0 = _scan(v, gate_w, 0.5 * b_gate_b[0], b_lambda[0].reshape(2, 1, HALF),
                                  moe_w_gate, moe_w_up, moe_w_down)
    h1, n2, route0, cnt0, tok0 = _outproj0(a_out, yf, yb, gg, x_lat, x_ctx, mod0, wo[0],
                                           norm_ffn_g[0].reshape(1, D), rw_t, rb)
    moe0 = functools.partial(_moe, _moe_chunks(cnt0), n2, route0, tok0, wg0, wu0, wd0, h1, mod0, fg,
                             final_norm=False)
    h2_lat = moe0(first_tile=0, n_tiles=lat_tiles, subtiles=2, name="moe_lat")
    h2_ctx = moe0(first_tile=lat_tiles, n_tiles=ctx_tiles, subtiles=1, name="moe_ctx")

    cx, bg, q, k, vv, wg1, wu1, wd1 = _inproj1(h2_lat, h2_ctx, mod1, norm_mix_g[1].reshape(1, D),
                                               cd_w_in[0].astype(BF16), moe_w_gate, moe_w_up, moe_w_down)
    att = _natten(q, k, vv, _natten_bias(d_rpb[0]))
    h3, n4, route1, cnt1, tok1 = _outproj1(cx, bg, att, c_conv_w[0], h2_lat, mod1, wo[1],
                                           norm_ffn_g[1].reshape(1, D), rw_t, rb)
    out = _moe(_moe_chunks(cnt1), n4, route1, tok1, wg1, wu1, wd1, h3, mod1, fg,
               first_tile=0, n_tiles=lat_tiles, subtiles=2, final_norm=True, name="moe_final")
    return out.reshape(BATCH, SEQ, D)
```

```python
import functools
import math

import jax
import jax.numpy as jnp
import numpy as np
from jax import lax
from jax.experimental import pallas as pl
from jax.experimental.pallas import tpu as pltpu

F32 = jnp.float32
BF16 = jnp.bfloat16

D = 1024
BATCH = 2
SEQ = 8192
CTX = 256
GRID_W = 64
N_LAT = BATCH * SEQ
N_TOK = N_LAT + BATCH * CTX
HALF = 512
CONV_A = 31
CONV_B = 4
CONV_C = 3
LRU_BLOCK = 128
N_LRU_BLOCKS = HALF // LRU_BLOCK
LRU_C = 8.0
HEAD_DIM = 64
N_HEADS = HALF // HEAD_DIM
WIN_H = 8
WIN_W = 16
N_EXPERTS = 16
N_GROUPS = 4
PER_GROUP = N_EXPERTS // N_GROUPS
D_FF = 512
EPS = 1e-6
NEG = -1e30
LOG2E = math.log2(math.e)

TM = 256
TILES_PER_SEQ = SEQ // TM
N_LAT_TILES = N_LAT // TM
N_TILES = N_TOK // TM
TIN = 512
TMOE = 512
CAP = 128
ROUTE_ROWS = 24
ROUTE_LANES = 128
ROUTE_GROUP = 16
ROUTE_RANK = 17
HALO_A = 16
HALO_S = 8
ROWS_Q = TM // GRID_W
VMEM_LIMIT = 48 * 1024 * 1024
VMEM_LIMIT_MOE = 56 * 1024 * 1024


def _params(*sem, vmem=VMEM_LIMIT):
    return pltpu.CompilerParams(dimension_semantics=sem, vmem_limit_bytes=vmem)


def _sigmoid(x):
    return 0.5 * jnp.tanh(0.5 * x) + 0.5


def _silu(x):
    return x * _sigmoid(x)


def _gelu_tanh(x):
    return 0.5 * x * (1.0 + jnp.tanh(0.7978845608028654 * (x + 0.044715 * (x * x * x))))


def _rms_mod(x, g, scale, shift):
    y = x * lax.rsqrt(jnp.mean(x * x, axis=-1, keepdims=True) + EPS) * g
    return y * (1.0 + scale) + shift


def _nt_dot(a, b):
    return lax.dot_general(a, b, (((1,), (1,)), ((), ())), preferred_element_type=F32)


def _held(hold):
    return (lambda i: i) if hold is None else (lambda i: jnp.minimum(i, hold))


def _mod_row(tile_rows, hold=None):
    per_seq = SEQ // tile_rows
    tile = _held(hold)
    return lambda i: (jnp.minimum(tile(i) // per_seq, BATCH), 0, 0)


def _mod_kernel(c_ref, w_ref, b_ref, o_ref):
    c = c_ref[...]
    s = _silu(c).astype(BF16)
    o_ref[0] = jnp.dot(s, w_ref[0].astype(BF16), preferred_element_type=F32) + b_ref[0]


def _modulation(cond, ada_w, ada_b):
    depth = ada_w.shape[0]
    nb = 1536
    return pl.pallas_call(
        _mod_kernel,
        grid=(depth, 6 * D // nb),
        in_specs=[pl.BlockSpec((8, D), lambda l, j: (0, 0)),
                  pl.BlockSpec((1, D, nb), lambda l, j: (l, 0, j)),
                  pl.BlockSpec((1, 1, nb), lambda l, j: (l, 0, j))],
        out_specs=pl.BlockSpec((1, 8, nb), lambda l, j: (l, 0, j)),
        out_shape=jax.ShapeDtypeStruct((depth, 8, 6 * D), F32),
        compiler_params=_params("parallel", "parallel"),
        name="modulation",
    )(cond, ada_w, ada_b.reshape(depth, 1, 6 * D))


def _token_specs(hold=None, rows=TM):
    tile = _held(hold)
    lat_tiles = N_LAT // rows
    lat = pl.BlockSpec((rows, D), lambda i: (jnp.minimum(tile(i), lat_tiles - 1), 0))
    ctx = pl.BlockSpec((rows, D), lambda i: (jnp.maximum(tile(i) - lat_tiles, 0), 0))
    return [lat, ctx]


def _token_tile(lat_ref, ctx_ref, tile=None, rows=TM):
    tile = pl.program_id(0) if tile is None else tile
    return jnp.where(tile < N_LAT // rows, lat_ref[...], ctx_ref[...])


def _halo_specs(halo, hold=None):
    per_tile = TM // halo
    last = N_TOK // halo - 1
    tile = _held(hold)
    prev = pl.BlockSpec((halo, HALF), lambda i: (jnp.maximum(tile(i) * per_tile - 1, 0), 0))
    nxt = pl.BlockSpec((halo, HALF), lambda i: (jnp.minimum((tile(i) + 1) * per_tile, last), 0))
    return prev, nxt


def _seq_edges(i):
    is_ctx = i >= N_LAT_TILES
    first = jnp.logical_or(is_ctx, i % TILES_PER_SEQ == 0)
    last = jnp.logical_or(is_ctx, i % TILES_PER_SEQ == TILES_PER_SEQ - 1)
    return first, last


def _fill_padded(buf_ref, prev_ref, cur_ref, next_ref, halo, tile=None):
    first, last = _seq_edges(pl.program_id(0) if tile is None else tile)
    buf_ref[0:halo, :] = jnp.where(first, 0.0, prev_ref[...])
    buf_ref[halo:halo + TM, :] = cur_ref[...]
    buf_ref[halo + TM:halo + TM + halo, :] = jnp.where(last, 0.0, next_ref[...])


def _tap_weight(w_ref, k, rows):
    return jnp.concatenate([w_ref[k]] * (rows // 8), axis=0)


def _sublane_replicated(w):
    return jnp.broadcast_to(w[:, None, :], (w.shape[0], 8, w.shape[1]))


def _depthwise(buf_ref, w_ref, taps, first_off, rows, row0):
    acc = None
    for k in range(taps):
        term = _tap_weight(w_ref, k, rows) * buf_ref[row0 + first_off + k:row0 + first_off + k + rows, :]
        acc = term if acc is None else acc + term
    return acc


def _seq_major(i, hold=None):
    i = _held(hold)(i)
    is_lat = i < N_LAT_TILES
    return (jnp.where(is_lat, i // TILES_PER_SEQ, i - N_LAT_TILES), jnp.where(is_lat, 1 + i % TILES_PER_SEQ, 0), 0)


CONV_ROWS = 32
PAD_A_ROWS = TM + 2 * HALO_A
PAD_B_ROWS = TM + 2 * HALO_S
SHIFT_ROWS = PAD_A_ROWS - 8


def _front0_kernel(x_ref, c_ref, xp_ref, xn_ref, mod_ref, g_ref, w_ref,
                   dww_ref, dwb_ref, lng_ref, lnb_ref, cw_ref, cb_ref,
                   gg_ref, a_ref, v_ref, bufa_ref, bufb_ref, shift_ref):
    step = pl.program_id(0)

    @pl.when(step == 0)
    def _():
        bufa_ref[...] = jnp.zeros_like(bufa_ref)
        bufb_ref[...] = jnp.zeros_like(bufb_ref)

    tile = jnp.minimum(step, N_TILES - 1)
    first, last = _seq_edges(tile)

    def run(new, old):
        mod = mod_ref[0]
        rows_in = jnp.concatenate([xp_ref[...], _token_tile(x_ref, c_ref, tile), xn_ref[...]], axis=0)
        n = _rms_mod(rows_in, g_ref[...], mod[:, D:2 * D], mod[:, 0:D]).astype(BF16)

        def proj(c, rows=slice(None)):
            return jnp.dot(n[rows], w_ref[:, c * HALF:(c + 1) * HALF], preferred_element_type=F32)

        def conv_chunk(r):
            row0 = r * CONV_ROWS
            u = None
            for k in range(CONV_A):
                off = HALO_A - CONV_A // 2 + k
                base = row0 + off - off % 8
                term = _tap_weight(dww_ref, k, CONV_ROWS) * shift_ref[off % 8, base:base + CONV_ROWS, :]
                u = term if u is None else u + term
            u = u + dwb_ref[...]
            mu = jnp.mean(u, axis=-1, keepdims=True)
            uc = u - mu
            var = jnp.mean(uc * uc, axis=-1, keepdims=True)
            y = uc * lax.rsqrt(var + EPS) * lng_ref[...] + lnb_ref[...]
            a_ref[row0:row0 + CONV_ROWS, :] = _silu(y).astype(a_ref.dtype)
            acc = None
            for k in range(CONV_B):
                lo = row0 + HALO_S - 2 + k
                term = _tap_weight(cw_ref, k, CONV_ROWS) * bufb_ref[old, lo:lo + CONV_ROWS, :]
                acc = term if acc is None else acc + term
            v_ref[0, row0:row0 + CONV_ROWS, :] = acc + cb_ref[...]

        for s in range(8):
            shift_ref[s] = bufa_ref[old, s:s + SHIFT_ROWS, :]
        def conv_quarter(q):
            per_quarter = TM // CONV_ROWS // 4
            for r in range(q * per_quarter, (q + 1) * per_quarter):
                conv_chunk(r)

        value, gate = proj(0), proj(1)
        conv_quarter(0)
        ua = value * _sigmoid(gate)
        bufa_ref[new, 0:HALO_A, :] = jnp.where(first, 0.0, ua[0:HALO_A])
        bufa_ref[new, HALO_A:HALO_A + TM, :] = ua[HALO_A:HALO_A + TM]
        bufa_ref[new, HALO_A + TM:PAD_A_ROWS, :] = jnp.where(last, 0.0, ua[HALO_A + TM:PAD_A_ROWS])
        recur = proj(2)
        conv_quarter(1)
        lo = HALO_A - HALO_S
        bufb_ref[new, 0:HALO_S, :] = jnp.where(first, 0.0, recur[lo:HALO_A])
        bufb_ref[new, HALO_S:HALO_S + TM, :] = recur[HALO_A:HALO_A + TM]
        bufb_ref[new, HALO_S + TM:PAD_B_ROWS, :] = jnp.where(last, 0.0, recur[HALO_A + TM:HALO_A + TM + HALO_S])
        gelu_in = proj(3, slice(HALO_A, HALO_A + TM))
        conv_quarter(2)
        gg_ref[...] = _gelu_tanh(gelu_in).astype(gg_ref.dtype)
        conv_quarter(3)

    @pl.when(step % 2 == 0)
    def _():
        run(0, 1)

    @pl.when(step % 2 == 1)
    def _():
        run(1, 0)


def _front0(x, c, mod, g, w, dw_w, dw_b, ln_g, ln_b, conv_w, conv_b):
    hold = N_TILES - 1
    tile = _held(hold)
    lagged = lambda i: jnp.maximum(i - 1, 0)
    per_tile = TM // HALO_A
    last_halo = N_LAT // HALO_A - 1
    prev = pl.BlockSpec((HALO_A, D), lambda i: (jnp.clip(tile(i) * per_tile - 1, 0, last_halo), 0))
    nxt = pl.BlockSpec((HALO_A, D), lambda i: (jnp.clip((tile(i) + 1) * per_tile, 0, last_halo), 0))
    vec = pl.BlockSpec((1, HALF), lambda i: (0, 0))
    return pl.pallas_call(
        _front0_kernel,
        grid=(N_TILES + 1,),
        in_specs=_token_specs(hold) + [prev, nxt,
                  pl.BlockSpec((1, 1, 6 * D), _mod_row(TM, hold)),
                  pl.BlockSpec((1, D), lambda i: (0, 0)),
                  pl.BlockSpec((D, 4 * HALF), lambda i: (0, 0)),
                  pl.BlockSpec((CONV_A, 8, HALF), lambda i: (0, 0, 0)), vec, vec, vec,
                  pl.BlockSpec((CONV_B, 8, HALF), lambda i: (0, 0, 0)), vec],
        out_specs=[pl.BlockSpec((TM, HALF), lambda i: (tile(i), 0)),
                   pl.BlockSpec((TM, HALF), lambda i: (lagged(i), 0)),
                   pl.BlockSpec((1, TM, HALF), lambda i: _seq_major(lagged(i)))],
        out_shape=[jax.ShapeDtypeStruct((N_TOK, HALF), BF16), jax.ShapeDtypeStruct((N_TOK, HALF), BF16),
                   jax.ShapeDtypeStruct((BATCH, SEQ + CTX, HALF), F32)],
        scratch_shapes=[pltpu.VMEM((2, PAD_A_ROWS, HALF), F32), pltpu.VMEM((2, PAD_B_ROWS, HALF), F32),
                        pltpu.VMEM((8, SHIFT_ROWS, HALF), F32)],
        compiler_params=_params("arbitrary"),
        name="front0",
    )(x, c, x, x, mod, g, w, _sublane_replicated(dw_w), dw_b.reshape(1, HALF), ln_g.reshape(1, HALF),
      ln_b.reshape(1, HALF), _sublane_replicated(conv_w), conv_b.reshape(1, HALF))


SCAN_UNROLL = 8


CAST_STEPS = 2 * N_EXPERTS


def _cast_specs(layer, step_of):
    def half(*idx):
        s = jnp.minimum(step_of(*idx), CAST_STEPS - 1)
        return s // 2, s % 2

    def src(*idx):
        e, h = half(*idx)
        return (layer, e, h, 0)

    def dst(*idx):
        e, h = half(*idx)
        return (e, h, 0)

    in_specs = [pl.BlockSpec((1, 1, D // 2, D_FF), src), pl.BlockSpec((1, 1, D // 2, D_FF), src),
                pl.BlockSpec((1, 1, D_FF // 2, D), src)]
    out_specs = [pl.BlockSpec((1, D // 2, D_FF), dst), pl.BlockSpec((1, D // 2, D_FF), dst),
                 pl.BlockSpec((1, D_FF // 2, D), dst)]
    out_shape = [jax.ShapeDtypeStruct((N_EXPERTS, D, D_FF), BF16), jax.ShapeDtypeStruct((N_EXPERTS, D, D_FF), BF16),
                 jax.ShapeDtypeStruct((N_EXPERTS, D_FF, D), BF16)]
    return in_specs, out_specs, out_shape


def _cast_weights(srcs, dsts):
    for src, dst in zip(srcs, dsts):
        dst[0] = src[0, 0].astype(BF16)


def _scan_kernel(vf_ref, vb_ref, w_ref, gb_ref, lam_ref, wg_ref, wu_ref, wd_ref,
                 yf_ref, yb_ref, wg_out, wu_out, wd_out, h_ref, a_ref, b_ref):
    _cast_weights((wg_ref, wu_ref, wd_ref), (wg_out, wu_out, wd_out))

    @pl.when(pl.program_id(0) == 0)
    def _():
        h_ref[...] = jnp.zeros_like(h_ref)

    for d, v_ref in enumerate((vf_ref, vb_ref)):
        v = v_ref[...].reshape(BATCH * TM, HALF)
        vb = v.astype(BF16)
        neg = -lam_ref[d]
        softplus = jnp.maximum(neg, 0.0) + jnp.log(1.0 + jnp.exp(-jnp.abs(neg)))
        rate = (-0.5 * LRU_C * LOG2E) * softplus
        for n in range(N_LRU_BLOCKS):
            sl = slice(n * LRU_BLOCK, (n + 1) * LRU_BLOCK)
            g = jnp.dot(vb[:, sl], w_ref[d, n], preferred_element_type=F32)
            tr = jnp.tanh(g[:, 0:LRU_BLOCK] + gb_ref[d, 0:1, sl])
            ti = jnp.tanh(g[:, LRU_BLOCK:2 * LRU_BLOCK] + gb_ref[d, 1:2, sl])
            a = jnp.exp2(rate[:, sl] * tr + rate[:, sl])
            a_ref[d, :, sl] = a
            b_ref[d, :, sl] = jnp.sqrt(1.0 - a * a) * ((0.5 * ti + 0.5) * v[:, sl])

    def body(s, hs):
        hs = list(hs)
        for u in range(SCAN_UNROLL):
            t = s * SCAN_UNROLL + u
            for d, y_ref in enumerate((yf_ref, yb_ref)):
                row = t if d == 0 else TM - 1 - t
                for bt in range(BATCH):
                    c = d * BATCH + bt
                    src = bt * TM + row
                    hs[c] = a_ref[d, pl.ds(src, 1), :] * hs[c] + b_ref[d, pl.ds(src, 1), :]
                    y_ref[bt, pl.ds(row, 1), :] = hs[c]
        return tuple(hs)

    init = tuple(h_ref[c:c + 1, :] for c in range(2 * BATCH))
    final = init
    for s in range(TM // SCAN_UNROLL):
        final = body(s, final)
    for c in range(2 * BATCH):
        h_ref[c:c + 1, :] = final[c]


def _scan(v, gate_w, gate_b, lam, moe_wg, moe_wu, moe_wd):
    steps = TILES_PER_SEQ + 1
    assert CAST_STEPS <= steps
    fwd = pl.BlockSpec((BATCH, TM, HALF), lambda j: (0, j, 0))
    bwd = pl.BlockSpec((BATCH, TM, HALF), lambda j: (0, jnp.where(j == 0, 0, TILES_PER_SEQ + 1 - j), 0))
    shp = jax.ShapeDtypeStruct((BATCH, SEQ + CTX, HALF), F32)
    cast_in, cast_out, cast_shape = _cast_specs(0, lambda j: j)
    return pl.pallas_call(
        _scan_kernel,
        grid=(steps,),
        in_specs=[fwd, bwd,
                  pl.BlockSpec((2, N_LRU_BLOCKS, LRU_BLOCK, 2 * LRU_BLOCK), lambda j: (0, 0, 0, 0)),
                  pl.BlockSpec((2, 2, HALF), lambda j: (0, 0, 0)),
                  pl.BlockSpec((2, 1, HALF), lambda j: (0, 0, 0))] + cast_in,
        out_specs=[fwd, bwd] + cast_out,
        out_shape=[shp, shp] + cast_shape,
        scratch_shapes=[pltpu.VMEM((8, HALF), F32), pltpu.VMEM((2, BATCH * TM, HALF), F32),
                        pltpu.VMEM((2, BATCH * TM, HALF), F32)],
        compiler_params=_params("arbitrary"),
        name="lru_scan",
    )(v, v, gate_w, gate_b, lam, moe_wg, moe_wu, moe_wd)


def _route(score, sel):
    rows = [sel[e:e + 1, :] for e in range(N_EXPERTS)]
    gbest = None
    gidx = None
    for g in range(N_GROUPS):
        top2 = None
        for p in range(PER_GROUP):
            for q in range(p + 1, PER_GROUP):
                s = rows[g * PER_GROUP + p] + rows[g * PER_GROUP + q]
                top2 = s if top2 is None else jnp.maximum(top2, s)
        if g == 0:
            gbest = top2
            gidx = jnp.zeros(top2.shape, jnp.int32)
        else:
            better = top2 > gbest
            gidx = jnp.where(better, g, gidx)
            gbest = jnp.where(better, top2, gbest)
    eint = lax.broadcasted_iota(jnp.int32, sel.shape, 0)
    eidx = eint.astype(F32)
    masked = jnp.where(jnp.right_shift(eint, 2) == gidx, sel, -jnp.inf)
    v1 = jnp.max(masked, axis=0, keepdims=True)
    i1 = jnp.min(jnp.where(masked == v1, eidx, float(N_EXPERTS)), axis=0, keepdims=True)
    masked2 = jnp.where(eidx == i1, -jnp.inf, masked)
    v2 = jnp.max(masked2, axis=0, keepdims=True)
    i2 = jnp.min(jnp.where(masked2 == v2, eidx, float(N_EXPERTS)), axis=0, keepdims=True)
    s1 = jnp.sum(jnp.where(eidx == i1, score, 0.0), axis=0, keepdims=True)
    s2 = jnp.sum(jnp.where(eidx == i2, score, 0.0), axis=0, keepdims=True)
    inv = 1.0 / (s1 + s2)
    return jnp.where(eidx == i1, s1 * inv, 0.0) + jnp.where(eidx == i2, s2 * inv, 0.0), gidx


def _group_ranks(gidx, carry_ref, tile):
    onehot = lax.broadcasted_iota(jnp.int32, (8, TM), 0) == gidx
    oh = jnp.where(onehot, 1.0, 0.0)
    before = lax.broadcasted_iota(jnp.int32, (TM, TM), 0) < lax.broadcasted_iota(jnp.int32, (TM, TM), 1)
    prefix = jnp.dot(oh.astype(BF16), jnp.where(before, 1.0, 0.0).astype(BF16), preferred_element_type=F32)
    carry = jnp.where(tile % (TMOE // TM) == 0, 0.0, carry_ref[...])
    rank = jnp.sum(jnp.where(onehot, prefix + carry[:, 0:1], 0.0), axis=0, keepdims=True)
    carry = carry + jnp.sum(oh, axis=1, keepdims=True)
    carry_ref[...] = carry
    return rank, carry


def _split_bf16(v):
    hi = v.astype(BF16)
    return hi, (v - hi.astype(F32)).astype(BF16)


def _outproj_step(m1, m2, x, mod_ref, wo_ref, g_ref, rw_ref, rb_ref, h_ref, n_ref, route_ref, cnt_ref, tok_ref, carry_ref,
                  nprev_ref):
    step = pl.program_id(0)
    n_hi, n_lo = _split_bf16(nprev_ref[...])
    w_hi, w_lo = _split_bf16(rw_ref[...])

    mod = mod_ref[0]
    mix = (jnp.dot(m1, wo_ref[0:HALF, :], preferred_element_type=F32)
           + jnp.dot(m2, wo_ref[HALF:2 * HALF, :], preferred_element_type=F32))

    logits = _nt_dot(w_hi, n_hi) + _nt_dot(w_hi, n_lo) + _nt_dot(w_lo, n_hi)

    h = x + mod[:, 2 * D:3 * D] * mix
    h_ref[...] = h
    n = _rms_mod(h, g_ref[...], mod[:, 4 * D:5 * D], mod[:, 3 * D:4 * D])
    n_ref[...] = n.astype(n_ref.dtype)
    nprev_ref[...] = n

    score = _sigmoid(logits)
    comb, gidx = _route(score, score + rb_ref[...])
    rank, counts = _group_ranks(gidx, carry_ref, step - 1)
    record = jnp.concatenate([comb, gidx.astype(F32), rank, jnp.zeros((ROUTE_LANES - ROUTE_RANK - 1, TM), F32)], axis=0)
    route_ref[...] = record[0:ROUTE_ROWS]
    tok_ref[...] = record.T
    cnt_ref[0] = counts


def _init_pipeline(carry_ref, nprev_ref):
    @pl.when(pl.program_id(0) == 0)
    def _():
        nprev_ref[...] = jnp.zeros_like(nprev_ref)
        carry_ref[...] = jnp.zeros_like(carry_ref)


def _outproj0_kernel(a_ref, yf_ref, yb_ref, gg_ref, x_ref, c_ref, mod_ref, wo_ref, g_ref, rw_ref, rb_ref,
                     h_ref, n_ref, route_ref, cnt_ref, tok_ref, carry_ref, nprev_ref):
    _init_pipeline(carry_ref, nprev_ref)
    tile = jnp.minimum(pl.program_id(0), N_TILES - 1)
    m2 = ((yf_ref[0] + yb_ref[0]) * gg_ref[...]).astype(BF16)
    _outproj_step(a_ref[...], m2, _token_tile(x_ref, c_ref, tile), mod_ref, wo_ref, g_ref, rw_ref, rb_ref,
                  h_ref, n_ref, route_ref, cnt_ref, tok_ref, carry_ref, nprev_ref)


def _outproj1_kernel(cx_ref, cxp_ref, cxn_ref, bg_ref, att_ref, cw_ref, x_ref, mod_ref, wo_ref, g_ref, rw_ref, rb_ref,
                     h_ref, n_ref, route_ref, cnt_ref, tok_ref, buf_ref, carry_ref, nprev_ref):
    _init_pipeline(carry_ref, nprev_ref)
    tile = jnp.minimum(pl.program_id(0), N_LAT_TILES - 1)
    _fill_padded(buf_ref, cxp_ref, cx_ref, cxn_ref, HALO_S, tile)
    conv = _depthwise(buf_ref, cw_ref, CONV_C, HALO_S - 1, TM, 0)
    m1 = (bg_ref[...] * conv).astype(BF16)
    _outproj_step(m1, att_ref[...], x_ref[...], mod_ref, wo_ref, g_ref, rw_ref, rb_ref,
                  h_ref, n_ref, route_ref, cnt_ref, tok_ref, carry_ref, nprev_ref)


def _outproj_common(n_tiles):
    hold = n_tiles - 1
    tile = _held(hold)
    routed = lambda i: jnp.maximum(i - 1, 0)
    in_specs = [pl.BlockSpec((1, 1, 6 * D), _mod_row(TM, hold)),
                pl.BlockSpec((D, D), lambda i: (0, 0)),
                pl.BlockSpec((1, D), lambda i: (0, 0)),
                pl.BlockSpec((N_EXPERTS, D), lambda i: (0, 0)),
                pl.BlockSpec((N_EXPERTS, 1), lambda i: (0, 0))]
    out_specs = [pl.BlockSpec((TM, D), lambda i: (tile(i), 0)),
                 pl.BlockSpec((TM, D), lambda i: (tile(i), 0)),
                 pl.BlockSpec((ROUTE_ROWS, TM), lambda i: (0, routed(i))),
                 pl.BlockSpec((1, 8, 128), lambda i: (routed(i), 0, 0)),
                 pl.BlockSpec((TM, ROUTE_LANES), lambda i: (routed(i), 0))]
    rows = n_tiles * TM
    out_shape = [jax.ShapeDtypeStruct((rows, D), F32), jax.ShapeDtypeStruct((rows, D), BF16),
                 jax.ShapeDtypeStruct((ROUTE_ROWS, rows), F32), jax.ShapeDtypeStruct((n_tiles, 8, 128), F32),
                 jax.ShapeDtypeStruct((rows, ROUTE_LANES), F32)]
    scratch = [pltpu.VMEM((8, 128), F32), pltpu.VMEM((TM, D), F32)]
    return in_specs, out_specs, out_shape, scratch


def _outproj0(a, yf, yb, gg, x, c, mod, wo, g, rw_t, rb):
    hold = N_TILES - 1
    tok = pl.BlockSpec((TM, HALF), lambda i: (_held(hold)(i), 0))
    scan_tok = pl.BlockSpec((1, TM, HALF), functools.partial(_seq_major, hold=hold))
    common_in, out_specs, out_shape, scratch = _outproj_common(N_TILES)
    return pl.pallas_call(
        _outproj0_kernel,
        grid=(N_TILES + 1,),
        in_specs=[tok, scan_tok, scan_tok, tok] + _token_specs(hold) + common_in,
        out_specs=out_specs,
        out_shape=out_shape,
        scratch_shapes=scratch,
        compiler_params=_params("arbitrary"),
        name="outproj0",
    )(a, yf, yb, gg, x, c, mod, wo, g, rw_t, rb)


def _outproj1(cx, bg, att, conv_w, x, mod, wo, g, rw_t, rb):
    hold = N_LAT_TILES - 1
    tok = pl.BlockSpec((TM, HALF), lambda i: (_held(hold)(i), 0))
    ps, ns = _halo_specs(HALO_S, hold)
    common_in, out_specs, out_shape, scratch = _outproj_common(N_LAT_TILES)
    return pl.pallas_call(
        _outproj1_kernel,
        grid=(N_LAT_TILES + 1,),
        in_specs=[tok, ps, ns, tok, tok, pl.BlockSpec((CONV_C, 8, HALF), lambda i: (0, 0, 0)),
                  pl.BlockSpec((TM, D), lambda i: (_held(hold)(i), 0))] + common_in,
        out_specs=out_specs,
        out_shape=out_shape,
        scratch_shapes=[pltpu.VMEM((TM + 2 * HALO_S, HALF), F32)] + scratch,
        compiler_params=_params("arbitrary"),
        name="outproj1",
    )(cx, cx, cx, bg, att, _sublane_replicated(conv_w), x, mod, wo, g, rw_t, rb)


def _moe_kernel(cnt_ref, n_ref, rt_ref, r_ref, wg_ref, wu_ref, wd_ref, h_ref, mod_ref, fg_ref, o_ref,
                hid_ref, *, final_norm, first_tile, subtiles):
    i = pl.program_id(0)
    g = pl.program_id(1)

    @pl.when(g == 0)
    def _():
        o_ref[...] = jnp.zeros_like(o_ref)

    gf = g.astype(F32)
    slot_row = lax.broadcasted_iota(jnp.int32, (CAP, TMOE), 0).astype(F32)
    slot_col = lax.broadcasted_iota(jnp.int32, (TMOE, CAP), 1).astype(F32)

    counts = [cnt_ref[(first_tile + i * subtiles + s) * N_GROUPS + g] for s in range(subtiles)]
    starts = [sum(counts[:s], jnp.int32(0)) for s in range(subtiles)]
    total = starts[-1] + counts[-1]
    tile_rows = [slice(s * TMOE, (s + 1) * TMOE) for s in range(subtiles)]

    def run_chunk(k, sources):
        base = k * CAP
        sels, shifts = [], []
        xg = None
        for s in sources:
            shift = (starts[s] - base).astype(F32)
            in_group = rt_ref[ROUTE_GROUP:ROUTE_GROUP + 1, tile_rows[s]] == gf
            slot = jnp.where(in_group, rt_ref[ROUTE_RANK:ROUTE_RANK + 1, tile_rows[s]] + shift, -1.0)
            sel = slot_row == slot
            part = jnp.dot(jnp.where(sel, 1.0, 0.0).astype(BF16), n_ref[tile_rows[s], :], preferred_element_type=F32)
            xg = part if xg is None else xg + part
            sels.append(sel)
            shifts.append(shift)
        xg = xg.astype(BF16)
        for j in range(PER_GROUP):
            cw = None
            for s, sel in zip(sources, sels):
                comb_row = rt_ref[pl.ds(g * PER_GROUP + j, 1), tile_rows[s]]
                part = jnp.sum(jnp.where(sel, comb_row, 0.0), axis=1, keepdims=True)
                cw = part if cw is None else cw + part
            hid = (_silu(jnp.dot(xg, wg_ref[j], preferred_element_type=F32))
                   * jnp.dot(xg, wu_ref[j], preferred_element_type=F32) * cw)
            hid_ref[:, j * D_FF:(j + 1) * D_FF] = hid.astype(BF16)
        y = jnp.dot(hid_ref[...], wd_ref[...].reshape(PER_GROUP * D_FF, D),
                    preferred_element_type=F32).astype(BF16)
        for s, shift in zip(sources, shifts):
            in_group = r_ref[tile_rows[s], ROUTE_GROUP:ROUTE_GROUP + 1] == gf
            slot = r_ref[tile_rows[s], ROUTE_RANK:ROUTE_RANK + 1] + shift
            back = jnp.where(jnp.logical_and(in_group, slot_col == slot), 1.0, 0.0).astype(BF16)
            o_ref[tile_rows[s], :] += jnp.dot(back, y, preferred_element_type=F32)

    def loop(lo, hi, sources):
        def body(k, carry):
            run_chunk(k, sources)
            return carry
        lax.fori_loop(lo, hi, body, 0)

    for s in range(subtiles):
        end = starts[s] + counts[s]
        first_inside = (starts[s] + (CAP - 1)) // CAP
        if s + 1 < subtiles:
            loop(first_inside, end // CAP, [s])

            @pl.when(end % CAP != 0)
            def _(s=s, end=end):
                run_chunk(end // CAP, [s, s + 1])
        else:
            loop(first_inside, (end + (CAP - 1)) // CAP, [s])

    @pl.when(g == N_GROUPS - 1)
    def _():
        out = h_ref[...] + mod_ref[0][:, 5 * D:6 * D] * o_ref[...]
        if final_norm:
            out = out * lax.rsqrt(jnp.mean(out * out, axis=-1, keepdims=True) + EPS) * fg_ref[...]
        o_ref[...] = out


def _moe_chunks(counts):
    sub = TMOE // TM
    return counts[sub - 1::sub, :N_GROUPS, 0].astype(jnp.int32).reshape(-1)


def _moe(n_chunks, n, route_t, route, wg, wu, wd, h, mod, final_g, *, first_tile, n_tiles, subtiles, final_norm,
         name):
    assert subtiles in (1, 2) and first_tile % subtiles == 0 and n_tiles % subtiles == 0
    step = subtiles * TMOE
    first = first_tile // subtiles
    mod_row = lambda i, g, nch: (jnp.minimum(((first + i) * step) // SEQ, BATCH), 0, 0)
    tok = pl.BlockSpec((step, D), lambda i, g, nch: (first + i, 0))
    grid_spec = pltpu.PrefetchScalarGridSpec(
        num_scalar_prefetch=1,
        grid=(n_tiles // subtiles, N_GROUPS),
        in_specs=[tok,
                  pl.BlockSpec((ROUTE_ROWS, step), lambda i, g, nch: (0, first + i)),
                  pl.BlockSpec((step, ROUTE_LANES), lambda i, g, nch: (first + i, 0)),
                  pl.BlockSpec((PER_GROUP, D, D_FF), lambda i, g, nch: (g, 0, 0)),
                  pl.BlockSpec((PER_GROUP, D, D_FF), lambda i, g, nch: (g, 0, 0)),
                  pl.BlockSpec((PER_GROUP, D_FF, D), lambda i, g, nch: (g, 0, 0)),
                  tok,
                  pl.BlockSpec((1, 1, 6 * D), mod_row),
                  pl.BlockSpec((1, D), lambda i, g, nch: (0, 0))],
        out_specs=pl.BlockSpec((step, D), lambda i, g, nch: (i, 0)),
        scratch_shapes=[pltpu.VMEM((CAP, PER_GROUP * D_FF), BF16)])
    return pl.pallas_call(
        functools.partial(_moe_kernel, final_norm=final_norm, first_tile=first_tile, subtiles=subtiles),
        grid_spec=grid_spec,
        out_shape=jax.ShapeDtypeStruct((n_tiles * TMOE, D), F32),
        compiler_params=_params("parallel", "arbitrary", vmem=VMEM_LIMIT_MOE),
        name=name,
    )(n_chunks, n, route_t, route, wg, wu, wd, h, mod, final_g)


def _inproj1_kernel(x_ref, c_ref, mod_ref, g_ref, w_ref, wvt_ref, wg_ref, wu_ref, wd_ref,
                    cx_ref, bg_ref, q_ref, k_ref, vt_ref, wg_out, wu_out, wd_out):
    _cast_weights((wg_ref, wu_ref, wd_ref), (wg_out, wu_out, wd_out))
    mod = mod_ref[0]
    n = _rms_mod(_token_tile(x_ref, c_ref, rows=TIN), g_ref[...], mod[:, D:2 * D], mod[:, 0:D]).astype(BF16)

    def proj(c):
        return jnp.dot(n, w_ref[:, c * HALF:(c + 1) * HALF], preferred_element_type=F32)

    conv_in, conv_gate = proj(0), proj(2)
    out_gate = proj(1)
    cx_ref[...] = conv_gate * conv_in
    q = proj(3)
    bg_ref[...] = out_gate
    k = proj(4)
    q_ref[...] = (q * (HEAD_DIM ** -0.5 * LOG2E)).astype(BF16)
    vt = _nt_dot(wvt_ref[...], n)
    k_ref[...] = k.astype(BF16)
    vt_ref[...] = vt.astype(BF16)


def _inproj1(x, c, mod, g, w, moe_wg, moe_wu, moe_wd):
    steps = N_TOK // TIN
    assert CAST_STEPS <= steps
    tok = pl.BlockSpec((TIN, HALF), lambda i: (i, 0))
    f = jax.ShapeDtypeStruct((N_TOK, HALF), F32)
    h = jax.ShapeDtypeStruct((N_TOK, HALF), BF16)
    cast_in, cast_out, cast_shape = _cast_specs(1, lambda i: i)
    return pl.pallas_call(
        _inproj1_kernel,
        grid=(steps,),
        in_specs=_token_specs(rows=TIN) + [
                  pl.BlockSpec((1, 1, 6 * D), _mod_row(TIN)),
                  pl.BlockSpec((1, D), lambda i: (0, 0)),
                  pl.BlockSpec((D, 5 * HALF), lambda i: (0, 0)),
                  pl.BlockSpec((HALF, D), lambda i: (0, 0))] + cast_in,
        out_specs=[tok] * 4 + [pl.BlockSpec((HALF, TIN), lambda i: (0, i))] + cast_out,
        out_shape=[f, f, h, h, jax.ShapeDtypeStruct((HALF, N_TOK), BF16)] + cast_shape,
        compiler_params=_params("arbitrary"),
        name="inproj1",
    )(x, c, mod, g, w[:, :5 * HALF], w[:, 5 * HALF:].T, moe_wg, moe_wu, moe_wd)


def _natten_kernel(q_ref, kp_ref, kc_ref, kn_ref, vp_ref, vc_ref, vn_ref, kx_ref, vx_ref, bias_ref, o_ref):
    pair = 2 * HEAD_DIM
    low = lax.broadcasted_iota(jnp.int32, (TM, pair), 1) < HEAD_DIM
    head_mask = [low.astype(F32).astype(BF16), jnp.logical_not(low).astype(F32).astype(BF16)]
    top = lax.broadcasted_iota(jnp.int32, (pair, TM), 0) < HEAD_DIM
    own_rows = [top.astype(F32).astype(BF16), jnp.logical_not(top).astype(F32).astype(BF16)]
    def raw_scores(head):
        g, hh = divmod(head, 2)
        sl = slice(pair * g, pair * (g + 1))
        qm = q_ref[:, sl] * head_mask[hh]
        return [_nt_dot(qm, k_ref[:, sl]) for k_ref in (kp_ref, kc_ref, kn_ref, kx_ref)]

    def biased(head, s):
        s = [s[t] + bias_ref[0, head, :, t * TM:(t + 1) * TM] for t in range(3)] + [s[3]]
        m = jnp.max(jnp.maximum(jnp.maximum(s[0], s[1]), jnp.maximum(s[2], s[3])), axis=-1, keepdims=True)
        return s, m

    def attend(head, s, m):
        g, hh = divmod(head, 2)
        sl = slice(pair * g, pair * (g + 1))
        p = jnp.concatenate([jnp.exp2(st - m).astype(BF16) for st in s], axis=1)
        lhs = jnp.concatenate([vt_ref[sl, :] * own_rows[hh] + own_rows[1 - hh]
                               for vt_ref in (vp_ref, vc_ref, vn_ref, vx_ref)], axis=1)
        acc = _nt_dot(lhs, p)
        if hh == 0:
            return acc[0:HEAD_DIM] * (1.0 / acc[HEAD_DIM:HEAD_DIM + 1])
        return acc[HEAD_DIM:pair] * (1.0 / acc[0:1])

    pending = biased(0, raw_scores(0))
    halves = []
    for head in range(N_HEADS):
        current = pending
        if head + 1 < N_HEADS:
            pending = biased(head + 1, raw_scores(head + 1))
        halves.append(attend(head, *current))
        if head % 2 == 1:
            sl = slice(pair * (head // 2), pair * (head // 2 + 1))
            o_ref[:, sl] = jnp.concatenate(halves, axis=0).T.astype(o_ref.dtype)
            halves = []


def _natten_bias(rpb):
    n_rows = SEQ // GRID_W
    n_dr, n_dc = 2 * WIN_H - 1, 2 * WIN_W - 1
    i = np.arange(ROWS_Q)
    j = np.arange(3 * ROWS_Q)
    col = np.arange(GRID_W)
    col_start = np.clip(col - WIN_W // 2, 0, GRID_W - WIN_W)
    col_ok = (col[None, :] >= col_start[:, None]) & (col[None, :] < col_start[:, None] + WIN_W)
    col_idx = col[None, :] - col[:, None] + (WIN_W - 1)
    onehot = ((col_idx[None] == np.arange(n_dc)[:, None, None]) & col_ok[None]).astype(np.float32)
    col_exp = jnp.dot(rpb.reshape(N_HEADS * n_dr, n_dc).astype(F32), onehot.reshape(n_dc, GRID_W * GRID_W),
                      precision=lax.Precision.HIGHEST).reshape(N_HEADS, n_dr, GRID_W, GRID_W)
    col_exp = jnp.where(col_ok[None, None], col_exp * LOG2E, NEG)
    masked = jnp.full((N_HEADS, GRID_W, GRID_W), NEG, F32)
    kinds = []
    for r0 in (0, ROWS_Q, n_rows - ROWS_Q):
        r = r0 + i
        kr = r0 - ROWS_Q + j
        r_start = np.clip(r - WIN_H // 2, 0, n_rows - WIN_H)
        row_ok = ((kr[None, :] >= r_start[:, None]) & (kr[None, :] < r_start[:, None] + WIN_H)
                  & (kr[None, :] >= 0) & (kr[None, :] < n_rows))
        row_idx = kr[None, :] - r[:, None] + (WIN_H - 1)
        rows = [jnp.concatenate([col_exp[:, row_idx[qi, kj]] if row_ok[qi, kj] else masked for kj in range(3 * ROWS_Q)],
                                axis=-1) for qi in range(ROWS_Q)]
        kinds.append(jnp.concatenate(rows, axis=1))
    return jnp.stack(kinds)


def _natten(q, k, vt, bias):
    def tile(b, i, off):
        return b * TILES_PER_SEQ + jnp.clip(i + off, 0, TILES_PER_SEQ - 1)

    def lat(off):
        return pl.BlockSpec((TM, HALF), lambda b, i: (tile(b, i, off), 0))

    def lat_t(off):
        return pl.BlockSpec((HALF, TM), lambda b, i: (0, tile(b, i, off)))

    ctx = pl.BlockSpec((TM, HALF), lambda b, i: (N_LAT_TILES + b, 0))
    ctx_t = pl.BlockSpec((HALF, TM), lambda b, i: (0, N_LAT_TILES + b))

    def kind(b, i):
        return (jnp.where(i == 0, 0, jnp.where(i == TILES_PER_SEQ - 1, 2, 1)), 0, 0, 0)

    return pl.pallas_call(
        _natten_kernel,
        grid=(BATCH, TILES_PER_SEQ),
        in_specs=[lat(0), lat(-1), lat(0), lat(1), lat_t(-1), lat_t(0), lat_t(1), ctx, ctx_t,
                  pl.BlockSpec((1, N_HEADS, TM, 3 * TM), kind)],
        out_specs=lat(0),
        out_shape=jax.ShapeDtypeStruct((N_LAT, HALF), BF16),
        compiler_params=_params("parallel", "arbitrary"),
        name="natten",
    )(q, k, k, k, vt, vt, vt, k, vt, bias)


def kernel(x, c, ctx, c_ctx, ada_w, ada_b, norm_mix_g, norm_ffn_g, w_out, ab_w_in, a_dw_w, a_dw_b, a_ln_g, a_ln_b,
           b_conv_w, b_conv_b, b_gate_w, b_gate_b, b_lambda, cd_w_in, c_conv_w, d_rpb, router_w, router_bias,
           moe_w_gate, moe_w_up, moe_w_down, final_g):
    x_lat = x.reshape(N_LAT, D)
    x_ctx = ctx.reshape(BATCH * CTX, D)
    cond = jnp.concatenate([c, c_ctx[None], jnp.zeros((8 - BATCH - 1, D), F32)], axis=0)
    mod = _modulation(cond, ada_w, ada_b)
    mod0 = mod[0].reshape(8, 1, 6 * D)
    mod1 = mod[1].reshape(8, 1, 6 * D)

    wo = w_out.astype(BF16)
    lat_tiles = N_LAT // TMOE
    ctx_tiles = BATCH * CTX // TMOE
    rw_t = router_w.T
    rb = router_bias.reshape(N_EXPERTS, 1)
    fg = final_g.reshape(1, D)

    gg, a_out, v = _front0(x_lat, x_ctx, mod0, norm_mix_g[0].reshape(1, D), ab_w_in[0].astype(BF16),
                           a_dw_w[0], a_dw_b[0], a_ln_g[0], a_ln_b[0], b_conv_w[0], b_conv_b[0])
    gw = b_gate_w[0]
    gate_w = (0.5 * jnp.concatenate([gw[:, 0], gw[:, 1]], axis=-1)).astype(BF16)
    yf, yb, wg0, wu0, wd0 = _scan(v, gate_w, 0.5 * b_gate_b[0], b_lambda[0].reshape(2, 1, HALF),
                                  moe_w_gate, moe_w_up, moe_w_down)
    h1, n2, route0, cnt0, tok0 = _outproj0(a_out, yf, yb, gg, x_lat, x_ctx, mod0, wo[0],
                                           norm_ffn_g[0].reshape(1, D), rw_t, rb)
    moe0 = functools.partial(_moe, _moe_chunks(cnt0), n2, route0, tok0, wg0, wu0, wd0, h1, mod0, fg,
                             final_norm=False)
    h2_lat = moe0(first_tile=0, n_tiles=lat_tiles, subtiles=2, name="moe_lat")
    h2_ctx = moe0(first_tile=lat_tiles, n_tiles=ctx_tiles, subtiles=1, name="moe_ctx")

    cx, bg, q, k, vv, wg1, wu1, wd1 = _inproj1(h2_lat, h2_ctx, mod1, norm_mix_g[1].reshape(1, D),
                                               cd_w_in[0].astype(BF16), moe_w_gate, moe_w_up, moe_w_down)
    att = _natten(q, k, vv, _natten_bias(d_rpb[0]))
    h3, n4, route1, cnt1, tok1 = _outproj1(cx, bg, att, c_conv_w[0], h2_lat, mod1, wo[1],
                                           norm_ffn_g[1].reshape(1, D), rw_t, rb)
    out = _moe(_moe_chunks(cnt1), n4, route1, tok1, wg1, wu1, wd1, h3, mod1, fg,
               first_tile=0, n_tiles=lat_tiles, subtiles=2, final_norm=True, name="moe_final")
    return out.reshape(BATCH, SEQ, D)
```

```python
import functools
import math

import jax
import jax.numpy as jnp
import numpy as np
from jax import lax
from jax.experimental import pallas as pl
from jax.experimental.pallas import tpu as pltpu

F32 = jnp.float32
BF16 = jnp.bfloat16

D = 1024
BATCH = 2
SEQ = 8192
CTX = 256
GRID_W = 64
N_LAT = BATCH * SEQ
N_TOK = N_LAT + BATCH * CTX
HALF = 512
CONV_A = 31
CONV_B = 4
CONV_C = 3
LRU_BLOCK = 128
N_LRU_BLOCKS = HALF // LRU_BLOCK
LRU_C = 8.0
HEAD_DIM = 64
N_HEADS = HALF // HEAD_DIM
WIN_H = 8
WIN_W = 16
N_EXPERTS = 16
N_GROUPS = 4
PER_GROUP = N_EXPERTS // N_GROUPS
D_FF = 512
EPS = 1e-6
NEG = -1e30
LOG2E = math.log2(math.e)

TM = 256
TILES_PER_SEQ = SEQ // TM
N_LAT_TILES = N_LAT // TM
N_TILES = N_TOK // TM
TIN = 512
TMOE = 512
CAP = 128
ROUTE_ROWS = 24
ROUTE_LANES = 128
ROUTE_GROUP = 16
ROUTE_RANK = 17
HALO_A = 16
HALO_S = 8
ROWS_Q = TM // GRID_W
VMEM_LIMIT = 48 * 1024 * 1024
VMEM_LIMIT_MOE = 56 * 1024 * 1024


def _params(*sem, vmem=VMEM_LIMIT):
    return pltpu.CompilerParams(dimension_semantics=sem, vmem_limit_bytes=vmem)


def _sigmoid(x):
    return 0.5 * jnp.tanh(0.5 * x) + 0.5


def _silu(x):
    return x * _sigmoid(x)


def _gelu_tanh(x):
    return 0.5 * x * (1.0 + jnp.tanh(0.7978845608028654 * (x + 0.044715 * (x * x * x))))


def _rms_mod(x, g, scale, shift):
    y = x * lax.rsqrt(jnp.mean(x * x, axis=-1, keepdims=True) + EPS) * g
    return y * (1.0 + scale) + shift


def _nt_dot(a, b):
    return lax.dot_general(a, b, (((1,), (1,)), ((), ())), preferred_element_type=F32)


def _held(hold):
    return (lambda i: i) if hold is None else (lambda i: jnp.minimum(i, hold))


def _mod_row(tile_rows, hold=None):
    per_seq = SEQ // tile_rows
    tile = _held(hold)
    return lambda i: (jnp.minimum(tile(i) // per_seq, BATCH), 0, 0)


def _mod_kernel(c_ref, w_ref, b_ref, o_ref):
    c = c_ref[...]
    s = _silu(c).astype(BF16)
    o_ref[0] = jnp.dot(s, w_ref[0].astype(BF16), preferred_element_type=F32) + b_ref[0]


def _modulation(cond, ada_w, ada_b):
    depth = ada_w.shape[0]
    nb = 1536
    return pl.pallas_call(
        _mod_kernel,
        grid=(depth, 6 * D // nb),
        in_specs=[pl.BlockSpec((8, D), lambda l, j: (0, 0)),
                  pl.BlockSpec((1, D, nb), lambda l, j: (l, 0, j)),
                  pl.BlockSpec((1, 1, nb), lambda l, j: (l, 0, j))],
        out_specs=pl.BlockSpec((1, 8, nb), lambda l, j: (l, 0, j)),
        out_shape=jax.ShapeDtypeStruct((depth, 8, 6 * D), F32),
        compiler_params=_params("parallel", "parallel"),
        name="modulation",
    )(cond, ada_w, ada_b.reshape(depth, 1, 6 * D))


def _token_specs(hold=None, rows=TM):
    tile = _held(hold)
    lat_tiles = N_LAT // rows
    lat = pl.BlockSpec((rows, D), lambda i: (jnp.minimum(tile(i), lat_tiles - 1), 0))
    ctx = pl.BlockSpec((rows, D), lambda i: (jnp.maximum(tile(i) - lat_tiles, 0), 0))
    return [lat, ctx]


def _token_tile(lat_ref, ctx_ref, tile=None, rows=TM):
    tile = pl.program_id(0) if tile is None else tile
    return jnp.where(tile < N_LAT // rows, lat_ref[...], ctx_ref[...])


def _halo_specs(halo, hold=None):
    per_tile = TM // halo
    last = N_TOK // halo - 1
    tile = _held(hold)
    prev = pl.BlockSpec((halo, HALF), lambda i: (jnp.maximum(tile(i) * per_tile - 1, 0), 0))
    nxt = pl.BlockSpec((halo, HALF), lambda i: (jnp.minimum((tile(i) + 1) * per_tile, last), 0))
    return prev, nxt


def _seq_edges(i):
    is_ctx = i >= N_LAT_TILES
    first = jnp.logical_or(is_ctx, i % TILES_PER_SEQ == 0)
    last = jnp.logical_or(is_ctx, i % TILES_PER_SEQ == TILES_PER_SEQ - 1)
    return first, last


def _fill_padded(buf_ref, prev_ref, cur_ref, next_ref, halo, tile=None):
    first, last = _seq_edges(pl.program_id(0) if tile is None else tile)
    buf_ref[0:halo, :] = jnp.where(first, 0.0, prev_ref[...])
    buf_ref[halo:halo + TM, :] = cur_ref[...]
    buf_ref[halo + TM:halo + TM + halo, :] = jnp.where(last, 0.0, next_ref[...])


def _tap_weight(w_ref, k, rows):
    return jnp.concatenate([w_ref[k]] * (rows // 8), axis=0)


def _sublane_replicated(w):
    return jnp.broadcast_to(w[:, None, :], (w.shape[0], 8, w.shape[1]))


def _depthwise(buf_ref, w_ref, taps, first_off, rows, row0):
    acc = None
    for k in range(taps):
        term = _tap_weight(w_ref, k, rows) * buf_ref[row0 + first_off + k:row0 + first_off + k + rows, :]
        acc = term if acc is None else acc + term
    return acc


def _seq_major(i, hold=None):
    i = _held(hold)(i)
    is_lat = i < N_LAT_TILES
    return (jnp.where(is_lat, i // TILES_PER_SEQ, i - N_LAT_TILES), jnp.where(is_lat, 1 + i % TILES_PER_SEQ, 0), 0)


CONV_ROWS = 32
PAD_A_ROWS = TM + 2 * HALO_A
PAD_B_ROWS = TM + 2 * HALO_S
SHIFT_ROWS = PAD_A_ROWS - 8


def _front0_kernel(x_ref, c_ref, xp_ref, xn_ref, mod_ref, g_ref, w_ref,
                   dww_ref, dwb_ref, lng_ref, lnb_ref, cw_ref, cb_ref, wg_ref, wu_ref, wd_ref,
                   gg_ref, a_ref, v_ref, wg_out, wu_out, wd_out, bufa_ref, bufb_ref, shift_ref):
    _cast_weights((wg_ref, wu_ref, wd_ref), (wg_out, wu_out, wd_out))
    step = pl.program_id(0)

    @pl.when(step == 0)
    def _():
        bufa_ref[...] = jnp.zeros_like(bufa_ref)
        bufb_ref[...] = jnp.zeros_like(bufb_ref)

    tile = jnp.minimum(step, N_TILES - 1)
    first, last = _seq_edges(tile)

    def run(new, old):
        mod = mod_ref[0]
        rows_in = jnp.concatenate([xp_ref[...], _token_tile(x_ref, c_ref, tile), xn_ref[...]], axis=0)
        n = _rms_mod(rows_in, g_ref[...], mod[:, D:2 * D], mod[:, 0:D]).astype(BF16)

        def proj(c, rows=slice(None)):
            return jnp.dot(n[rows], w_ref[:, c * HALF:(c + 1) * HALF], preferred_element_type=F32)

        def conv_chunk(r):
            row0 = r * CONV_ROWS
            u = None
            for k in range(CONV_A):
                off = HALO_A - CONV_A // 2 + k
                base = row0 + off - off % 8
                term = _tap_weight(dww_ref, k, CONV_ROWS) * shift_ref[off % 8, base:base + CONV_ROWS, :]
                u = term if u is None else u + term
            u = u + dwb_ref[...]
            mu = jnp.mean(u, axis=-1, keepdims=True)
            uc = u - mu
            var = jnp.mean(uc * uc, axis=-1, keepdims=True)
            y = uc * lax.rsqrt(var + EPS) * lng_ref[...] + lnb_ref[...]
            a_ref[row0:row0 + CONV_ROWS, :] = _silu(y).astype(a_ref.dtype)
            acc = None
            for k in range(CONV_B):
                lo = row0 + HALO_S - 2 + k
                term = _tap_weight(cw_ref, k, CONV_ROWS) * bufb_ref[old, lo:lo + CONV_ROWS, :]
                acc = term if acc is None else acc + term
            v_ref[0, row0:row0 + CONV_ROWS, :] = acc + cb_ref[...]

        for s in range(8):
            shift_ref[s] = bufa_ref[old, s:s + SHIFT_ROWS, :]
        def conv_quarter(q):
            per_quarter = TM // CONV_ROWS // 4
            for r in range(q * per_quarter, (q + 1) * per_quarter):
                conv_chunk(r)

        value, gate = proj(0), proj(1)
        conv_quarter(0)
        ua = value * _sigmoid(gate)
        bufa_ref[new, 0:HALO_A, :] = jnp.where(first, 0.0, ua[0:HALO_A])
        bufa_ref[new, HALO_A:HALO_A + TM, :] = ua[HALO_A:HALO_A + TM]
        bufa_ref[new, HALO_A + TM:PAD_A_ROWS, :] = jnp.where(last, 0.0, ua[HALO_A + TM:PAD_A_ROWS])
        recur = proj(2)
        conv_quarter(1)
        lo = HALO_A - HALO_S
        bufb_ref[new, 0:HALO_S, :] = jnp.where(first, 0.0, recur[lo:HALO_A])
        bufb_ref[new, HALO_S:HALO_S + TM, :] = recur[HALO_A:HALO_A + TM]
        bufb_ref[new, HALO_S + TM:PAD_B_ROWS, :] = jnp.where(last, 0.0, recur[HALO_A + TM:HALO_A + TM + HALO_S])
        gelu_in = proj(3, slice(HALO_A, HALO_A + TM))
        conv_quarter(2)
        gg_ref[...] = _gelu_tanh(gelu_in).astype(gg_ref.dtype)
        conv_quarter(3)

    @pl.when(step % 2 == 0)
    def _():
        run(0, 1)

    @pl.when(step % 2 == 1)
    def _():
        run(1, 0)


def _front0(x, c, mod, g, w, dw_w, dw_b, ln_g, ln_b, conv_w, conv_b, moe_wg, moe_wu, moe_wd):
    assert CAST_STEPS <= N_TILES + 1
    cast_in, cast_out, cast_shape = _cast_specs(0, lambda i: i)
    hold = N_TILES - 1
    tile = _held(hold)
    lagged = lambda i: jnp.maximum(i - 1, 0)
    per_tile = TM // HALO_A
    last_halo = N_LAT // HALO_A - 1
    prev = pl.BlockSpec((HALO_A, D), lambda i: (jnp.clip(tile(i) * per_tile - 1, 0, last_halo), 0))
    nxt = pl.BlockSpec((HALO_A, D), lambda i: (jnp.clip((tile(i) + 1) * per_tile, 0, last_halo), 0))
    vec = pl.BlockSpec((1, HALF), lambda i: (0, 0))
    return pl.pallas_call(
        _front0_kernel,
        grid=(N_TILES + 1,),
        in_specs=_token_specs(hold) + [prev, nxt,
                  pl.BlockSpec((1, 1, 6 * D), _mod_row(TM, hold)),
                  pl.BlockSpec((1, D), lambda i: (0, 0)),
                  pl.BlockSpec((D, 4 * HALF), lambda i: (0, 0)),
                  pl.BlockSpec((CONV_A, 8, HALF), lambda i: (0, 0, 0)), vec, vec, vec,
                  pl.BlockSpec((CONV_B, 8, HALF), lambda i: (0, 0, 0)), vec] + cast_in,
        out_specs=[pl.BlockSpec((TM, HALF), lambda i: (tile(i), 0)),
                   pl.BlockSpec((TM, HALF), lambda i: (lagged(i), 0)),
                   pl.BlockSpec((1, TM, HALF), lambda i: _seq_major(lagged(i)))] + cast_out,
        out_shape=[jax.ShapeDtypeStruct((N_TOK, HALF), BF16), jax.ShapeDtypeStruct((N_TOK, HALF), BF16),
                   jax.ShapeDtypeStruct((BATCH, SEQ + CTX, HALF), F32)] + cast_shape,
        scratch_shapes=[pltpu.VMEM((2, PAD_A_ROWS, HALF), F32), pltpu.VMEM((2, PAD_B_ROWS, HALF), F32),
                        pltpu.VMEM((8, SHIFT_ROWS, HALF), F32)],
        compiler_params=_params("arbitrary"),
        name="front0",
    )(x, c, x, x, mod, g, w, _sublane_replicated(dw_w), dw_b.reshape(1, HALF), ln_g.reshape(1, HALF),
      ln_b.reshape(1, HALF), _sublane_replicated(conv_w), conv_b.reshape(1, HALF), moe_wg, moe_wu, moe_wd)


SCAN_UNROLL = 8


CAST_STEPS = 2 * N_EXPERTS


def _cast_specs(layer, step_of):
    def half(*idx):
        s = jnp.minimum(step_of(*idx), CAST_STEPS - 1)
        return s // 2, s % 2

    def src(*idx):
        e, h = half(*idx)
        return (layer, e, h, 0)

    def dst(*idx):
        e, h = half(*idx)
        return (e, h, 0)

    in_specs = [pl.BlockSpec((1, 1, D // 2, D_FF), src), pl.BlockSpec((1, 1, D // 2, D_FF), src),
                pl.BlockSpec((1, 1, D_FF // 2, D), src)]
    out_specs = [pl.BlockSpec((1, D // 2, D_FF), dst), pl.BlockSpec((1, D // 2, D_FF), dst),
                 pl.BlockSpec((1, D_FF // 2, D), dst)]
    out_shape = [jax.ShapeDtypeStruct((N_EXPERTS, D, D_FF), BF16), jax.ShapeDtypeStruct((N_EXPERTS, D, D_FF), BF16),
                 jax.ShapeDtypeStruct((N_EXPERTS, D_FF, D), BF16)]
    return in_specs, out_specs, out_shape


def _cast_weights(srcs, dsts):
    for src, dst in zip(srcs, dsts):
        dst[0] = src[0, 0].astype(BF16)


def _scan_kernel(vf_ref, vb_ref, w_ref, gb_ref, lam_ref, yf_ref, yb_ref, h_ref, a_ref, b_ref):
    @pl.when(pl.program_id(0) == 0)
    def _():
        h_ref[...] = jnp.zeros_like(h_ref)

    for d, v_ref in enumerate((vf_ref, vb_ref)):
        v = v_ref[...].reshape(BATCH * TM, HALF)
        vb = v.astype(BF16)
        neg = -lam_ref[d]
        softplus = jnp.maximum(neg, 0.0) + jnp.log(1.0 + jnp.exp(-jnp.abs(neg)))
        rate = (-0.5 * LRU_C * LOG2E) * softplus
        for n in range(N_LRU_BLOCKS):
            sl = slice(n * LRU_BLOCK, (n + 1) * LRU_BLOCK)
            g = jnp.dot(vb[:, sl], w_ref[d, n], preferred_element_type=F32)
            tr = jnp.tanh(g[:, 0:LRU_BLOCK] + gb_ref[d, 0:1, sl])
            ti = jnp.tanh(g[:, LRU_BLOCK:2 * LRU_BLOCK] + gb_ref[d, 1:2, sl])
            a = jnp.exp2(rate[:, sl] * tr + rate[:, sl])
            a_ref[d, :, sl] = a
            b_ref[d, :, sl] = jnp.sqrt(1.0 - a * a) * ((0.5 * ti + 0.5) * v[:, sl])

    def body(s, hs):
        hs = list(hs)
        for u in range(SCAN_UNROLL):
            t = s * SCAN_UNROLL + u
            for d, y_ref in enumerate((yf_ref, yb_ref)):
                row = t if d == 0 else TM - 1 - t
                for bt in range(BATCH):
                    c = d * BATCH + bt
                    src = bt * TM + row
                    hs[c] = a_ref[d, pl.ds(src, 1), :] * hs[c] + b_ref[d, pl.ds(src, 1), :]
                    y_ref[bt, pl.ds(row, 1), :] = hs[c]
        return tuple(hs)

    init = tuple(h_ref[c:c + 1, :] for c in range(2 * BATCH))
    final = init
    for s in range(TM // SCAN_UNROLL):
        final = body(s, final)
    for c in range(2 * BATCH):
        h_ref[c:c + 1, :] = final[c]


def _scan(v, gate_w, gate_b, lam):
    fwd = pl.BlockSpec((BATCH, TM, HALF), lambda j: (0, j, 0))
    bwd = pl.BlockSpec((BATCH, TM, HALF), lambda j: (0, jnp.where(j == 0, 0, TILES_PER_SEQ + 1 - j), 0))
    shp = jax.ShapeDtypeStruct((BATCH, SEQ + CTX, HALF), F32)
    return pl.pallas_call(
        _scan_kernel,
        grid=(TILES_PER_SEQ + 1,),
        in_specs=[fwd, bwd,
                  pl.BlockSpec((2, N_LRU_BLOCKS, LRU_BLOCK, 2 * LRU_BLOCK), lambda j: (0, 0, 0, 0)),
                  pl.BlockSpec((2, 2, HALF), lambda j: (0, 0, 0)),
                  pl.BlockSpec((2, 1, HALF), lambda j: (0, 0, 0))],
        out_specs=[fwd, bwd],
        out_shape=[shp, shp],
        scratch_shapes=[pltpu.VMEM((8, HALF), F32), pltpu.VMEM((2, BATCH * TM, HALF), F32),
                        pltpu.VMEM((2, BATCH * TM, HALF), F32)],
        compiler_params=_params("arbitrary"),
        name="lru_scan",
    )(v, v, gate_w, gate_b, lam)


def _route(score, sel):
    rows = [sel[e:e + 1, :] for e in range(N_EXPERTS)]
    gbest = None
    gidx = None
    for g in range(N_GROUPS):
        top2 = None
        for p in range(PER_GROUP):
            for q in range(p + 1, PER_GROUP):
                s = rows[g * PER_GROUP + p] + rows[g * PER_GROUP + q]
                top2 = s if top2 is None else jnp.maximum(top2, s)
        if g == 0:
            gbest = top2
            gidx = jnp.zeros(top2.shape, jnp.int32)
        else:
            better = top2 > gbest
            gidx = jnp.where(better, g, gidx)
            gbest = jnp.where(better, top2, gbest)
    eint = lax.broadcasted_iota(jnp.int32, sel.shape, 0)
    eidx = eint.astype(F32)
    masked = jnp.where(jnp.right_shift(eint, 2) == gidx, sel, -jnp.inf)
    v1 = jnp.max(masked, axis=0, keepdims=True)
    i1 = jnp.min(jnp.where(masked == v1, eidx, float(N_EXPERTS)), axis=0, keepdims=True)
    masked2 = jnp.where(eidx == i1, -jnp.inf, masked)
    v2 = jnp.max(masked2, axis=0, keepdims=True)
    i2 = jnp.min(jnp.where(masked2 == v2, eidx, float(N_EXPERTS)), axis=0, keepdims=True)
    s1 = jnp.sum(jnp.where(eidx == i1, score, 0.0), axis=0, keepdims=True)
    s2 = jnp.sum(jnp.where(eidx == i2, score, 0.0), axis=0, keepdims=True)
    inv = 1.0 / (s1 + s2)
    return jnp.where(eidx == i1, s1 * inv, 0.0) + jnp.where(eidx == i2, s2 * inv, 0.0), gidx


def _group_ranks(gidx, carry_ref, tile):
    onehot = lax.broadcasted_iota(jnp.int32, (8, TM), 0) == gidx
    oh = jnp.where(onehot, 1.0, 0.0)
    before = lax.broadcasted_iota(jnp.int32, (TM, TM), 0) < lax.broadcasted_iota(jnp.int32, (TM, TM), 1)
    prefix = jnp.dot(oh.astype(BF16), jnp.where(before, 1.0, 0.0).astype(BF16), preferred_element_type=F32)
    carry = jnp.where(tile % (TMOE // TM) == 0, 0.0, carry_ref[...])
    rank = jnp.sum(jnp.where(onehot, prefix + carry[:, 0:1], 0.0), axis=0, keepdims=True)
    carry = carry + jnp.sum(oh, axis=1, keepdims=True)
    carry_ref[...] = carry
    return rank, carry


def _split_bf16(v):
    hi = v.astype(BF16)
    return hi, (v - hi.astype(F32)).astype(BF16)


def _outproj_step(m1, m2, x, mod_ref, wo_ref, g_ref, rw_ref, rb_ref, h_ref, n_ref, route_ref, cnt_ref, tok_ref, carry_ref,
                  nprev_ref):
    step = pl.program_id(0)
    n_hi, n_lo = _split_bf16(nprev_ref[...])
    w_hi, w_lo = _split_bf16(rw_ref[...])

    mod = mod_ref[0]
    mix = (jnp.dot(m1, wo_ref[0:HALF, :], preferred_element_type=F32)
           + jnp.dot(m2, wo_ref[HALF:2 * HALF, :], preferred_element_type=F32))

    logits = _nt_dot(w_hi, n_hi) + _nt_dot(w_hi, n_lo) + _nt_dot(w_lo, n_hi)

    h = x + mod[:, 2 * D:3 * D] * mix
    h_ref[...] = h
    n = _rms_mod(h, g_ref[...], mod[:, 4 * D:5 * D], mod[:, 3 * D:4 * D])
    n_ref[...] = n.astype(n_ref.dtype)
    nprev_ref[...] = n

    score = _sigmoid(logits)
    comb, gidx = _route(score, score + rb_ref[...])
    rank, counts = _group_ranks(gidx, carry_ref, step - 1)
    record = jnp.concatenate([comb, gidx.astype(F32), rank, jnp.zeros((ROUTE_LANES - ROUTE_RANK - 1, TM), F32)], axis=0)
    route_ref[...] = record[0:ROUTE_ROWS]
    tok_ref[...] = record.T
    cnt_ref[0] = counts


def _init_pipeline(carry_ref, nprev_ref):
    @pl.when(pl.program_id(0) == 0)
    def _():
        nprev_ref[...] = jnp.zeros_like(nprev_ref)
        carry_ref[...] = jnp.zeros_like(carry_ref)


def _outproj0_kernel(a_ref, yf_ref, yb_ref, gg_ref, x_ref, c_ref, mod_ref, wo_ref, g_ref, rw_ref, rb_ref,
                     h_ref, n_ref, route_ref, cnt_ref, tok_ref, carry_ref, nprev_ref):
    _init_pipeline(carry_ref, nprev_ref)
    tile = jnp.minimum(pl.program_id(0), N_TILES - 1)
    m2 = ((yf_ref[0] + yb_ref[0]) * gg_ref[...]).astype(BF16)
    _outproj_step(a_ref[...], m2, _token_tile(x_ref, c_ref, tile), mod_ref, wo_ref, g_ref, rw_ref, rb_ref,
                  h_ref, n_ref, route_ref, cnt_ref, tok_ref, carry_ref, nprev_ref)


def _outproj1_kernel(cx_ref, cxp_ref, cxn_ref, bg_ref, att_ref, cw_ref, x_ref, mod_ref, wo_ref, g_ref, rw_ref, rb_ref,
                     h_ref, n_ref, route_ref, cnt_ref, tok_ref, buf_ref, carry_ref, nprev_ref):
    _init_pipeline(carry_ref, nprev_ref)
    tile = jnp.minimum(pl.program_id(0), N_LAT_TILES - 1)
    _fill_padded(buf_ref, cxp_ref, cx_ref, cxn_ref, HALO_S, tile)
    conv = _depthwise(buf_ref, cw_ref, CONV_C, HALO_S - 1, TM, 0)
    m1 = (bg_ref[...] * conv).astype(BF16)
    _outproj_step(m1, att_ref[...], x_ref[...], mod_ref, wo_ref, g_ref, rw_ref, rb_ref,
                  h_ref, n_ref, route_ref, cnt_ref, tok_ref, carry_ref, nprev_ref)


def _outproj_common(n_tiles):
    hold = n_tiles - 1
    tile = _held(hold)
    routed = lambda i: jnp.maximum(i - 1, 0)
    in_specs = [pl.BlockSpec((1, 1, 6 * D), _mod_row(TM, hold)),
                pl.BlockSpec((D, D), lambda i: (0, 0)),
                pl.BlockSpec((1, D), lambda i: (0, 0)),
                pl.BlockSpec((N_EXPERTS, D), lambda i: (0, 0)),
                pl.BlockSpec((N_EXPERTS, 1), lambda i: (0, 0))]
    out_specs = [pl.BlockSpec((TM, D), lambda i: (tile(i), 0)),
                 pl.BlockSpec((TM, D), lambda i: (tile(i), 0)),
                 pl.BlockSpec((ROUTE_ROWS, TM), lambda i: (0, routed(i))),
                 pl.BlockSpec((1, 8, 128), lambda i: (routed(i), 0, 0)),
                 pl.BlockSpec((TM, ROUTE_LANES), lambda i: (routed(i), 0))]
    rows = n_tiles * TM
    out_shape = [jax.ShapeDtypeStruct((rows, D), F32), jax.ShapeDtypeStruct((rows, D), BF16),
                 jax.ShapeDtypeStruct((ROUTE_ROWS, rows), F32), jax.ShapeDtypeStruct((n_tiles, 8, 128), F32),
                 jax.ShapeDtypeStruct((rows, ROUTE_LANES), F32)]
    scratch = [pltpu.VMEM((8, 128), F32), pltpu.VMEM((TM, D), F32)]
    return in_specs, out_specs, out_shape, scratch


def _outproj0(a, yf, yb, gg, x, c, mod, wo, g, rw_t, rb):
    hold = N_TILES - 1
    tok = pl.BlockSpec((TM, HALF), lambda i: (_held(hold)(i), 0))
    scan_tok = pl.BlockSpec((1, TM, HALF), functools.partial(_seq_major, hold=hold))
    common_in, out_specs, out_shape, scratch = _outproj_common(N_TILES)
    return pl.pallas_call(
        _outproj0_kernel,
        grid=(N_TILES + 1,),
        in_specs=[tok, scan_tok, scan_tok, tok] + _token_specs(hold) + common_in,
        out_specs=out_specs,
        out_shape=out_shape,
        scratch_shapes=scratch,
        compiler_params=_params("arbitrary"),
        name="outproj0",
    )(a, yf, yb, gg, x, c, mod, wo, g, rw_t, rb)


def _outproj1(cx, bg, att, conv_w, x, mod, wo, g, rw_t, rb):
    hold = N_LAT_TILES - 1
    tok = pl.BlockSpec((TM, HALF), lambda i: (_held(hold)(i), 0))
    ps, ns = _halo_specs(HALO_S, hold)
    common_in, out_specs, out_shape, scratch = _outproj_common(N_LAT_TILES)
    return pl.pallas_call(
        _outproj1_kernel,
        grid=(N_LAT_TILES + 1,),
        in_specs=[tok, ps, ns, tok, tok, pl.BlockSpec((CONV_C, 8, HALF), lambda i: (0, 0, 0)),
                  pl.BlockSpec((TM, D), lambda i: (_held(hold)(i), 0))] + common_in,
        out_specs=out_specs,
        out_shape=out_shape,
        scratch_shapes=[pltpu.VMEM((TM + 2 * HALO_S, HALF), F32)] + scratch,
        compiler_params=_params("arbitrary"),
        name="outproj1",
    )(cx, cx, cx, bg, att, _sublane_replicated(conv_w), x, mod, wo, g, rw_t, rb)


def _moe_kernel(cnt_ref, n_ref, rt_ref, r_ref, wg_ref, wu_ref, wd_ref, h_ref, mod_ref, fg_ref, o_ref,
                hid_ref, *, final_norm, first_tile, subtiles):
    i = pl.program_id(0)
    g = pl.program_id(1)

    @pl.when(g == 0)
    def _():
        o_ref[...] = jnp.zeros_like(o_ref)

    gf = g.astype(F32)
    slot_row = lax.broadcasted_iota(jnp.int32, (CAP, TMOE), 0).astype(F32)
    slot_col = lax.broadcasted_iota(jnp.int32, (TMOE, CAP), 1).astype(F32)

    counts = [cnt_ref[(first_tile + i * subtiles + s) * N_GROUPS + g] for s in range(subtiles)]
    starts = [sum(counts[:s], jnp.int32(0)) for s in range(subtiles)]
    total = starts[-1] + counts[-1]
    tile_rows = [slice(s * TMOE, (s + 1) * TMOE) for s in range(subtiles)]

    def run_chunk(k, sources):
        base = k * CAP
        sels, shifts = [], []
        xg = None
        for s in sources:
            shift = (starts[s] - base).astype(F32)
            in_group = rt_ref[ROUTE_GROUP:ROUTE_GROUP + 1, tile_rows[s]] == gf
            slot = jnp.where(in_group, rt_ref[ROUTE_RANK:ROUTE_RANK + 1, tile_rows[s]] + shift, -1.0)
            sel = slot_row == slot
            part = jnp.dot(jnp.where(sel, 1.0, 0.0).astype(BF16), n_ref[tile_rows[s], :], preferred_element_type=F32)
            xg = part if xg is None else xg + part
            sels.append(sel)
            shifts.append(shift)
        xg = xg.astype(BF16)
        for j in range(PER_GROUP):
            cw = None
            for s, sel in zip(sources, sels):
                comb_row = rt_ref[pl.ds(g * PER_GROUP + j, 1), tile_rows[s]]
                part = jnp.sum(jnp.where(sel, comb_row, 0.0), axis=1, keepdims=True)
                cw = part if cw is None else cw + part
            hid = (_silu(jnp.dot(xg, wg_ref[j], preferred_element_type=F32))
                   * jnp.dot(xg, wu_ref[j], preferred_element_type=F32) * cw)
            hid_ref[:, j * D_FF:(j + 1) * D_FF] = hid.astype(BF16)
        y = jnp.dot(hid_ref[...], wd_ref[...].reshape(PER_GROUP * D_FF, D),
                    preferred_element_type=F32).astype(BF16)
        for s, shift in zip(sources, shifts):
            in_group = r_ref[tile_rows[s], ROUTE_GROUP:ROUTE_GROUP + 1] == gf
            slot = r_ref[tile_rows[s], ROUTE_RANK:ROUTE_RANK + 1] + shift
            back = jnp.where(jnp.logical_and(in_group, slot_col == slot), 1.0, 0.0).astype(BF16)
            o_ref[tile_rows[s], :] += jnp.dot(back, y, preferred_element_type=F32)

    def loop(lo, hi, sources):
        def body(k, carry):
            run_chunk(k, sources)
            return carry
        lax.fori_loop(lo, hi, body, 0)

    for s in range(subtiles):
        end = starts[s] + counts[s]
        first_inside = (starts[s] + (CAP - 1)) // CAP
        if s + 1 < subtiles:
            loop(first_inside, end // CAP, [s])

            @pl.when(end % CAP != 0)
            def _(s=s, end=end):
                run_chunk(end // CAP, [s, s + 1])
        else:
            loop(first_inside, (end + (CAP - 1)) // CAP, [s])

    @pl.when(g == N_GROUPS - 1)
    def _():
        out = h_ref[...] + mod_ref[0][:, 5 * D:6 * D] * o_ref[...]
        if final_norm:
            out = out * lax.rsqrt(jnp.mean(out * out, axis=-1, keepdims=True) + EPS) * fg_ref[...]
        o_ref[...] = out


def _moe_chunks(counts):
    sub = TMOE // TM
    return counts[sub - 1::sub, :N_GROUPS, 0].astype(jnp.int32).reshape(-1)


def _moe(n_chunks, n, route_t, route, wg, wu, wd, h, mod, final_g, *, first_tile, n_tiles, subtiles, final_norm,
         name):
    assert subtiles in (1, 2) and first_tile % subtiles == 0 and n_tiles % subtiles == 0
    step = subtiles * TMOE
    first = first_tile // subtiles
    mod_row = lambda i, g, nch: (jnp.minimum(((first + i) * step) // SEQ, BATCH), 0, 0)
    tok = pl.BlockSpec((step, D), lambda i, g, nch: (first + i, 0))
    grid_spec = pltpu.PrefetchScalarGridSpec(
        num_scalar_prefetch=1,
        grid=(n_tiles // subtiles, N_GROUPS),
        in_specs=[tok,
                  pl.BlockSpec((ROUTE_ROWS, step), lambda i, g, nch: (0, first + i)),
                  pl.BlockSpec((step, ROUTE_LANES), lambda i, g, nch: (first + i, 0)),
                  pl.BlockSpec((PER_GROUP, D, D_FF), lambda i, g, nch: (g, 0, 0)),
                  pl.BlockSpec((PER_GROUP, D, D_FF), lambda i, g, nch: (g, 0, 0)),
                  pl.BlockSpec((PER_GROUP, D_FF, D), lambda i, g, nch: (g, 0, 0)),
                  tok,
                  pl.BlockSpec((1, 1, 6 * D), mod_row),
                  pl.BlockSpec((1, D), lambda i, g, nch: (0, 0))],
        out_specs=pl.BlockSpec((step, D), lambda i, g, nch: (i, 0)),
        scratch_shapes=[pltpu.VMEM((CAP, PER_GROUP * D_FF), BF16)])
    return pl.pallas_call(
        functools.partial(_moe_kernel, final_norm=final_norm, first_tile=first_tile, subtiles=subtiles),
        grid_spec=grid_spec,
        out_shape=jax.ShapeDtypeStruct((n_tiles * TMOE, D), F32),
        compiler_params=_params("parallel", "arbitrary", vmem=VMEM_LIMIT_MOE),
        name=name,
    )(n_chunks, n, route_t, route, wg, wu, wd, h, mod, final_g)


def _inproj1_kernel(x_ref, c_ref, mod_ref, g_ref, w_ref, wvt_ref, wg_ref, wu_ref, wd_ref,
                    cx_ref, bg_ref, q_ref, k_ref, vt_ref, wg_out, wu_out, wd_out):
    _cast_weights((wg_ref, wu_ref, wd_ref), (wg_out, wu_out, wd_out))
    mod = mod_ref[0]
    n = _rms_mod(_token_tile(x_ref, c_ref, rows=TIN), g_ref[...], mod[:, D:2 * D], mod[:, 0:D]).astype(BF16)

    def proj(c):
        return jnp.dot(n, w_ref[:, c * HALF:(c + 1) * HALF], preferred_element_type=F32)

    conv_in, conv_gate = proj(0), proj(2)
    out_gate = proj(1)
    cx_ref[...] = conv_gate * conv_in
    q = proj(3)
    bg_ref[...] = out_gate
    k = proj(4)
    q_ref[...] = (q * (HEAD_DIM ** -0.5 * LOG2E)).astype(BF16)
    vt = _nt_dot(wvt_ref[...], n)
    k_ref[...] = k.astype(BF16)
    vt_ref[...] = vt.astype(BF16)


def _inproj1(x, c, mod, g, w, moe_wg, moe_wu, moe_wd):
    steps = N_TOK // TIN
    assert CAST_STEPS <= steps
    tok = pl.BlockSpec((TIN, HALF), lambda i: (i, 0))
    f = jax.ShapeDtypeStruct((N_TOK, HALF), F32)
    h = jax.ShapeDtypeStruct((N_TOK, HALF), BF16)
    cast_in, cast_out, cast_shape = _cast_specs(1, lambda i: i)
    return pl.pallas_call(
        _inproj1_kernel,
        grid=(steps,),
        in_specs=_token_specs(rows=TIN) + [
                  pl.BlockSpec((1, 1, 6 * D), _mod_row(TIN)),
                  pl.BlockSpec((1, D), lambda i: (0, 0)),
                  pl.BlockSpec((D, 5 * HALF), lambda i: (0, 0)),
                  pl.BlockSpec((HALF, D), lambda i: (0, 0))] + cast_in,
        out_specs=[tok] * 4 + [pl.BlockSpec((HALF, TIN), lambda i: (0, i))] + cast_out,
        out_shape=[f, f, h, h, jax.ShapeDtypeStruct((HALF, N_TOK), BF16)] + cast_shape,
        compiler_params=_params("arbitrary"),
        name="inproj1",
    )(x, c, mod, g, w[:, :5 * HALF], w[:, 5 * HALF:].T, moe_wg, moe_wu, moe_wd)


def _natten_kernel(q_ref, kp_ref, kc_ref, kn_ref, vp_ref, vc_ref, vn_ref, kx_ref, vx_ref, bias_ref, o_ref):
    pair = 2 * HEAD_DIM
    low = lax.broadcasted_iota(jnp.int32, (TM, pair), 1) < HEAD_DIM
    head_mask = [low.astype(F32).astype(BF16), jnp.logical_not(low).astype(F32).astype(BF16)]
    top = lax.broadcasted_iota(jnp.int32, (pair, TM), 0) < HEAD_DIM
    own_rows = [top.astype(F32).astype(BF16), jnp.logical_not(top).astype(F32).astype(BF16)]
    def raw_scores(head):
        g, hh = divmod(head, 2)
        sl = slice(pair * g, pair * (g + 1))
        qm = q_ref[:, sl] * head_mask[hh]
        return [_nt_dot(qm, k_ref[:, sl]) for k_ref in (kp_ref, kc_ref, kn_ref, kx_ref)]

    def biased(head, s):
        s = [s[t] + bias_ref[0, head, :, t * TM:(t + 1) * TM] for t in range(3)] + [s[3]]
        m = jnp.max(jnp.maximum(jnp.maximum(s[0], s[1]), jnp.maximum(s[2], s[3])), axis=-1, keepdims=True)
        return s, m

    def attend(head, s, m):
        g, hh = divmod(head, 2)
        sl = slice(pair * g, pair * (g + 1))
        p = jnp.concatenate([jnp.exp2(st - m).astype(BF16) for st in s], axis=1)
        lhs = jnp.concatenate([vt_ref[sl, :] * own_rows[hh] + own_rows[1 - hh]
                               for vt_ref in (vp_ref, vc_ref, vn_ref, vx_ref)], axis=1)
        acc = _nt_dot(lhs, p)
        if hh == 0:
            return acc[0:HEAD_DIM] * (1.0 / acc[HEAD_DIM:HEAD_DIM + 1])
        return acc[HEAD_DIM:pair] * (1.0 / acc[0:1])

    pending = biased(0, raw_scores(0))
    halves = []
    for head in range(N_HEADS):
        current = pending
        if head + 1 < N_HEADS:
            pending = biased(head + 1, raw_scores(head + 1))
        halves.append(attend(head, *current))
        if head % 2 == 1:
            sl = slice(pair * (head // 2), pair * (head // 2 + 1))
            o_ref[:, sl] = jnp.concatenate(halves, axis=0).T.astype(o_ref.dtype)
            halves = []


def _natten_bias(rpb):
    n_rows = SEQ // GRID_W
    n_dr, n_dc = 2 * WIN_H - 1, 2 * WIN_W - 1
    i = np.arange(ROWS_Q)
    j = np.arange(3 * ROWS_Q)
    col = np.arange(GRID_W)
    col_start = np.clip(col - WIN_W // 2, 0, GRID_W - WIN_W)
    col_ok = (col[None, :] >= col_start[:, None]) & (col[None, :] < col_start[:, None] + WIN_W)
    col_idx = col[None, :] - col[:, None] + (WIN_W - 1)
    onehot = ((col_idx[None] == np.arange(n_dc)[:, None, None]) & col_ok[None]).astype(np.float32)
    col_exp = jnp.dot(rpb.reshape(N_HEADS * n_dr, n_dc).astype(F32), onehot.reshape(n_dc, GRID_W * GRID_W),
                      precision=lax.Precision.HIGHEST).reshape(N_HEADS, n_dr, GRID_W, GRID_W)
    col_exp = jnp.where(col_ok[None, None], col_exp * LOG2E, NEG)
    masked = jnp.full((N_HEADS, GRID_W, GRID_W), NEG, F32)
    kinds = []
    for r0 in (0, ROWS_Q, n_rows - ROWS_Q):
        r = r0 + i
        kr = r0 - ROWS_Q + j
        r_start = np.clip(r - WIN_H // 2, 0, n_rows - WIN_H)
        row_ok = ((kr[None, :] >= r_start[:, None]) & (kr[None, :] < r_start[:, None] + WIN_H)
                  & (kr[None, :] >= 0) & (kr[None, :] < n_rows))
        row_idx = kr[None, :] - r[:, None] + (WIN_H - 1)
        rows = [jnp.concatenate([col_exp[:, row_idx[qi, kj]] if row_ok[qi, kj] else masked for kj in range(3 * ROWS_Q)],
                                axis=-1) for qi in range(ROWS_Q)]
        kinds.append(jnp.concatenate(rows, axis=1))
    return jnp.stack(kinds)


def _natten(q, k, vt, bias):
    def tile(b, i, off):
        return b * TILES_PER_SEQ + jnp.clip(i + off, 0, TILES_PER_SEQ - 1)

    def lat(off):
        return pl.BlockSpec((TM, HALF), lambda b, i: (tile(b, i, off), 0))

    def lat_t(off):
        return pl.BlockSpec((HALF, TM), lambda b, i: (0, tile(b, i, off)))

    ctx = pl.BlockSpec((TM, HALF), lambda b, i: (N_LAT_TILES + b, 0))
    ctx_t = pl.BlockSpec((HALF, TM), lambda b, i: (0, N_LAT_TILES + b))

    def kind(b, i):
        return (jnp.where(i == 0, 0, jnp.where(i == TILES_PER_SEQ - 1, 2, 1)), 0, 0, 0)

    return pl.pallas_call(
        _natten_kernel,
        grid=(BATCH, TILES_PER_SEQ),
        in_specs=[lat(0), lat(-1), lat(0), lat(1), lat_t(-1), lat_t(0), lat_t(1), ctx, ctx_t,
                  pl.BlockSpec((1, N_HEADS, TM, 3 * TM), kind)],
        out_specs=lat(0),
        out_shape=jax.ShapeDtypeStruct((N_LAT, HALF), BF16),
        compiler_params=_params("parallel", "arbitrary"),
        name="natten",
    )(q, k, k, k, vt, vt, vt, k, vt, bias)


def kernel(x, c, ctx, c_ctx, ada_w, ada_b, norm_mix_g, norm_ffn_g, w_out, ab_w_in, a_dw_w, a_dw_b, a_ln_g, a_ln_b,
           b_conv_w, b_conv_b, b_gate_w, b_gate_b, b_lambda, cd_w_in, c_conv_w, d_rpb, router_w, router_bias,
           moe_w_gate, moe_w_up, moe_w_down, final_g):
    x_lat = x.reshape(N_LAT, D)
    x_ctx = ctx.reshape(BATCH * CTX, D)
    cond = jnp.concatenate([c, c_ctx[None], jnp.zeros((8 - BATCH - 1, D), F32)], axis=0)
    mod = _modulation(cond, ada_w, ada_b)
    mod0 = mod[0].reshape(8, 1, 6 * D)
    mod1 = mod[1].reshape(8, 1, 6 * D)

    wo = w_out.astype(BF16)
    lat_tiles = N_LAT // TMOE
    ctx_tiles = BATCH * CTX // TMOE
    rw_t = router_w.T
    rb = router_bias.reshape(N_EXPERTS, 1)
    fg = final_g.reshape(1, D)

    gg, a_out, v, wg0, wu0, wd0 = _front0(x_lat, x_ctx, mod0, norm_mix_g[0].reshape(1, D), ab_w_in[0].astype(BF16),
                                          a_dw_w[0], a_dw_b[0], a_ln_g[0], a_ln_b[0], b_conv_w[0], b_conv_b[0],
                                          moe_w_gate, moe_w_up, moe_w_down)
    gw = b_gate_w[0]
    gate_w = (0.5 * jnp.concatenate([gw[:, 0], gw[:, 1]], axis=-1)).astype(BF16)
    yf, yb = _scan(v, gate_w, 0.5 * b_gate_b[0], b_lambda[0].reshape(2, 1, HALF))
    h1, n2, route0, cnt0, tok0 = _outproj0(a_out, yf, yb, gg, x_lat, x_ctx, mod0, wo[0],
                                           norm_ffn_g[0].reshape(1, D), rw_t, rb)
    moe0 = functools.partial(_moe, _moe_chunks(cnt0), n2, route0, tok0, wg0, wu0, wd0, h1, mod0, fg,
                             final_norm=False)
    h2_lat = moe0(first_tile=0, n_tiles=lat_tiles, subtiles=2, name="moe_lat")
    h2_ctx = moe0(first_tile=lat_tiles, n_tiles=ctx_tiles, subtiles=1, name="moe_ctx")

    cx, bg, q, k, vv, wg1, wu1, wd1 = _inproj1(h2_lat, h2_ctx, mod1, norm_mix_g[1].reshape(1, D),
                                               cd_w_in[0].astype(BF16), moe_w_gate, moe_w_up, moe_w_down)
    att = _natten(q, k, vv, _natten_bias(d_rpb[0]))
    h3, n4, route1, cnt1, tok1 = _outproj1(cx, bg, att, c_conv_w[0], h2_lat, mod1, wo[1],
                                           norm_ffn_g[1].reshape(1, D), rw_t, rb)
    out = _moe(_moe_chunks(cnt1), n4, route1, tok1, wg1, wu1, wd1, h3, mod1, fg,
               first_tile=0, n_tiles=lat_tiles, subtiles=2, final_norm=True, name="moe_final")
    return out.reshape(BATCH, SEQ, D)
```

```python
import functools
import math

import jax
import jax.numpy as jnp
import numpy as np
from jax import lax
from jax.experimental import pallas as pl
from jax.experimental.pallas import tpu as pltpu

F32 = jnp.float32
BF16 = jnp.bfloat16

D = 1024
BATCH = 2
SEQ = 8192
CTX = 256
GRID_W = 64
N_LAT = BATCH * SEQ
N_TOK = N_LAT + BATCH * CTX
HALF = 512
CONV_A = 31
CONV_B = 4
CONV_C = 3
LRU_BLOCK = 128
N_LRU_BLOCKS = HALF // LRU_BLOCK
LRU_C = 8.0
HEAD_DIM = 64
N_HEADS = HALF // HEAD_DIM
WIN_H = 8
WIN_W = 16
N_EXPERTS = 16
N_GROUPS = 4
PER_GROUP = N_EXPERTS // N_GROUPS
D_FF = 512
EPS = 1e-6
NEG = -1e30
LOG2E = math.log2(math.e)

TM = 256
TILES_PER_SEQ = SEQ // TM
N_LAT_TILES = N_LAT // TM
N_TILES = N_TOK // TM
TIN = 512
TMOE = 512
CAP = 128
ROUTE_ROWS = 24
ROUTE_LANES = 128
ROUTE_GROUP = 16
ROUTE_RANK = 17
HALO_A = 16
HALO_S = 8
ROWS_Q = TM // GRID_W
VMEM_LIMIT = 48 * 1024 * 1024
VMEM_LIMIT_MOE = 56 * 1024 * 1024


def _params(*sem, vmem=VMEM_LIMIT):
    return pltpu.CompilerParams(dimension_semantics=sem, vmem_limit_bytes=vmem)


def _sigmoid(x):
    return 0.5 * jnp.tanh(0.5 * x) + 0.5


def _silu(x):
    return x * _sigmoid(x)


GELU_C = 0.7978845608028654
GELU_K = 0.044715


def _gelu_tanh_of_half(xh):
    return xh * (1.0 + jnp.tanh(xh * (2.0 * GELU_C + (8.0 * GELU_C * GELU_K) * (xh * xh))))


def _rms_mod(x, g, scale, shift):
    return x * lax.rsqrt(jnp.mean(x * x, axis=-1, keepdims=True) + EPS) * (g * (1.0 + scale)) + shift


def _nt_dot(a, b):
    return lax.dot_general(a, b, (((1,), (1,)), ((), ())), preferred_element_type=F32)


def _held(hold):
    return (lambda i: i) if hold is None else (lambda i: jnp.minimum(i, hold))


def _mod_row(tile_rows, hold=None):
    per_seq = SEQ // tile_rows
    tile = _held(hold)
    return lambda i: (jnp.minimum(tile(i) // per_seq, BATCH), 0, 0)


def _mod_kernel(c_ref, w_ref, b_ref, o_ref):
    c = c_ref[...]
    s = _silu(c).astype(BF16)
    o_ref[0] = jnp.dot(s, w_ref[0].astype(BF16), preferred_element_type=F32) + b_ref[0]


def _modulation(cond, ada_w, ada_b):
    depth = ada_w.shape[0]
    nb = 1536
    return pl.pallas_call(
        _mod_kernel,
        grid=(depth, 6 * D // nb),
        in_specs=[pl.BlockSpec((8, D), lambda l, j: (0, 0)),
                  pl.BlockSpec((1, D, nb), lambda l, j: (l, 0, j)),
                  pl.BlockSpec((1, 1, nb), lambda l, j: (l, 0, j))],
        out_specs=pl.BlockSpec((1, 8, nb), lambda l, j: (l, 0, j)),
        out_shape=jax.ShapeDtypeStruct((depth, 8, 6 * D), F32),
        compiler_params=_params("parallel", "parallel"),
        name="modulation",
    )(cond, ada_w, ada_b.reshape(depth, 1, 6 * D))


def _token_specs(hold=None, rows=TM):
    tile = _held(hold)
    lat_tiles = N_LAT // rows
    lat = pl.BlockSpec((rows, D), lambda i: (jnp.minimum(tile(i), lat_tiles - 1), 0))
    ctx = pl.BlockSpec((rows, D), lambda i: (jnp.maximum(tile(i) - lat_tiles, 0), 0))
    return [lat, ctx]


def _token_tile(lat_ref, ctx_ref, tile=None, rows=TM):
    tile = pl.program_id(0) if tile is None else tile
    return jnp.where(tile < N_LAT // rows, lat_ref[...], ctx_ref[...])


def _halo_specs(halo, hold=None):
    per_tile = TM // halo
    last = N_TOK // halo - 1
    tile = _held(hold)
    prev = pl.BlockSpec((halo, HALF), lambda i: (jnp.maximum(tile(i) * per_tile - 1, 0), 0))
    nxt = pl.BlockSpec((halo, HALF), lambda i: (jnp.minimum((tile(i) + 1) * per_tile, last), 0))
    return prev, nxt


def _seq_edges(i):
    is_ctx = i >= N_LAT_TILES
    first = jnp.logical_or(is_ctx, i % TILES_PER_SEQ == 0)
    last = jnp.logical_or(is_ctx, i % TILES_PER_SEQ == TILES_PER_SEQ - 1)
    return first, last


def _fill_padded(buf_ref, prev_ref, cur_ref, next_ref, halo, tile=None):
    first, last = _seq_edges(pl.program_id(0) if tile is None else tile)
    buf_ref[0:halo, :] = jnp.where(first, 0.0, prev_ref[...])
    buf_ref[halo:halo + TM, :] = cur_ref[...]
    buf_ref[halo + TM:halo + TM + halo, :] = jnp.where(last, 0.0, next_ref[...])


def _tap_weight(w_ref, k, rows):
    return jnp.concatenate([w_ref[k]] * (rows // 8), axis=0)


def _sublane_replicated(w):
    return jnp.broadcast_to(w[:, None, :], (w.shape[0], 8, w.shape[1]))


def _depthwise(buf_ref, w_ref, taps, first_off, rows, row0):
    acc = None
    for k in range(taps):
        term = _tap_weight(w_ref, k, rows) * buf_ref[row0 + first_off + k:row0 + first_off + k + rows, :]
        acc = term if acc is None else acc + term
    return acc


def _seq_major(i, hold=None):
    i = _held(hold)(i)
    is_lat = i < N_LAT_TILES
    return (jnp.where(is_lat, i // TILES_PER_SEQ, i - N_LAT_TILES), jnp.where(is_lat, 1 + i % TILES_PER_SEQ, 0), 0)


CONV_ROWS = 32
PAD_A_ROWS = TM + 2 * HALO_A
PAD_B_ROWS = TM + 2 * HALO_S
SHIFT_ROWS = PAD_A_ROWS - 8


def _front0_kernel(x_ref, c_ref, xp_ref, xn_ref, mod_ref, g_ref, w_ref,
                   dww_ref, dwb_ref, lng_ref, lnb_ref, cw_ref, cb_ref, wg_ref, wu_ref, wd_ref,
                   gg_ref, a_ref, v_ref, wg_out, wu_out, wd_out, bufa_ref, bufb_ref, shift_ref):
    _cast_weights((wg_ref, wu_ref, wd_ref), (wg_out, wu_out, wd_out))
    step = pl.program_id(0)

    @pl.when(step == 0)
    def _():
        bufa_ref[...] = jnp.zeros_like(bufa_ref)
        bufb_ref[...] = jnp.zeros_like(bufb_ref)

    tile = jnp.minimum(step, N_TILES - 1)
    first, last = _seq_edges(tile)

    def run(new, old):
        mod = mod_ref[0]
        rows_in = jnp.concatenate([xp_ref[...], _token_tile(x_ref, c_ref, tile), xn_ref[...]], axis=0)
        n = _rms_mod(rows_in, g_ref[...], mod[:, D:2 * D], mod[:, 0:D]).astype(BF16)

        def proj(c, rows=slice(None)):
            return jnp.dot(n[rows], w_ref[:, c * HALF:(c + 1) * HALF], preferred_element_type=F32)

        def conv_chunk(r):
            row0 = r * CONV_ROWS
            u = None
            for k in range(CONV_A):
                off = HALO_A - CONV_A // 2 + k
                base = row0 + off - off % 8
                term = _tap_weight(dww_ref, k, CONV_ROWS) * shift_ref[off % 8, base:base + CONV_ROWS, :]
                u = term if u is None else u + term
            u = u + dwb_ref[...]
            mu = jnp.mean(u, axis=-1, keepdims=True)
            uc = u - mu
            var = jnp.mean(uc * uc, axis=-1, keepdims=True)
            y = uc * lax.rsqrt(var + EPS) * lng_ref[...] + lnb_ref[...]
            a_ref[row0:row0 + CONV_ROWS, :] = _silu(y).astype(a_ref.dtype)
            acc = None
            for k in range(CONV_B):
                lo = row0 + HALO_S - 2 + k
                term = _tap_weight(cw_ref, k, CONV_ROWS) * bufb_ref[old, lo:lo + CONV_ROWS, :]
                acc = term if acc is None else acc + term
            v_ref[0, row0:row0 + CONV_ROWS, :] = acc + cb_ref[...]

        for s in range(8):
            shift_ref[s] = bufa_ref[old, s:s + SHIFT_ROWS, :]
        def conv_quarter(q):
            per_quarter = TM // CONV_ROWS // 4
            for r in range(q * per_quarter, (q + 1) * per_quarter):
                conv_chunk(r)

        value, gate = proj(0), proj(1)
        conv_quarter(0)
        ua = value * (jnp.tanh(gate) + 1.0)
        bufa_ref[new, 0:HALO_A, :] = jnp.where(first, 0.0, ua[0:HALO_A])
        bufa_ref[new, HALO_A:HALO_A + TM, :] = ua[HALO_A:HALO_A + TM]
        bufa_ref[new, HALO_A + TM:PAD_A_ROWS, :] = jnp.where(last, 0.0, ua[HALO_A + TM:PAD_A_ROWS])
        recur = proj(2)
        conv_quarter(1)
        lo = HALO_A - HALO_S
        bufb_ref[new, 0:HALO_S, :] = jnp.where(first, 0.0, recur[lo:HALO_A])
        bufb_ref[new, HALO_S:HALO_S + TM, :] = recur[HALO_A:HALO_A + TM]
        bufb_ref[new, HALO_S + TM:PAD_B_ROWS, :] = jnp.where(last, 0.0, recur[HALO_A + TM:HALO_A + TM + HALO_S])
        gelu_in = proj(3, slice(HALO_A, HALO_A + TM))
        conv_quarter(2)
        gg_ref[...] = _gelu_tanh_of_half(gelu_in).astype(gg_ref.dtype)
        conv_quarter(3)

    @pl.when(step % 2 == 0)
    def _():
        run(0, 1)

    @pl.when(step % 2 == 1)
    def _():
        run(1, 0)


def _front0(x, c, mod, g, w, dw_w, dw_b, ln_g, ln_b, conv_w, conv_b, moe_wg, moe_wu, moe_wd):
    assert CAST_STEPS <= N_TILES + 1
    cast_in, cast_out, cast_shape = _cast_specs(0, lambda i: i)
    hold = N_TILES - 1
    tile = _held(hold)
    lagged = lambda i: jnp.maximum(i - 1, 0)
    per_tile = TM // HALO_A
    last_halo = N_LAT // HALO_A - 1
    prev = pl.BlockSpec((HALO_A, D), lambda i: (jnp.clip(tile(i) * per_tile - 1, 0, last_halo), 0))
    nxt = pl.BlockSpec((HALO_A, D), lambda i: (jnp.clip((tile(i) + 1) * per_tile, 0, last_halo), 0))
    vec = pl.BlockSpec((1, HALF), lambda i: (0, 0))
    return pl.pallas_call(
        _front0_kernel,
        grid=(N_TILES + 1,),
        in_specs=_token_specs(hold) + [prev, nxt,
                  pl.BlockSpec((1, 1, 6 * D), _mod_row(TM, hold)),
                  pl.BlockSpec((1, D), lambda i: (0, 0)),
                  pl.BlockSpec((D, 4 * HALF), lambda i: (0, 0)),
                  pl.BlockSpec((CONV_A, 8, HALF), lambda i: (0, 0, 0)), vec, vec, vec,
                  pl.BlockSpec((CONV_B, 8, HALF), lambda i: (0, 0, 0)), vec] + cast_in,
        out_specs=[pl.BlockSpec((TM, HALF), lambda i: (tile(i), 0)),
                   pl.BlockSpec((TM, HALF), lambda i: (lagged(i), 0)),
                   pl.BlockSpec((1, TM, HALF), lambda i: _seq_major(lagged(i)))] + cast_out,
        out_shape=[jax.ShapeDtypeStruct((N_TOK, HALF), BF16), jax.ShapeDtypeStruct((N_TOK, HALF), BF16),
                   jax.ShapeDtypeStruct((BATCH, SEQ + CTX, HALF), F32)] + cast_shape,
        scratch_shapes=[pltpu.VMEM((2, PAD_A_ROWS, HALF), F32), pltpu.VMEM((2, PAD_B_ROWS, HALF), F32),
                        pltpu.VMEM((8, SHIFT_ROWS, HALF), F32)],
        compiler_params=_params("arbitrary"),
        name="front0",
    )(x, c, x, x, mod, g, w, _sublane_replicated(dw_w), dw_b.reshape(1, HALF), ln_g.reshape(1, HALF),
      ln_b.reshape(1, HALF), _sublane_replicated(conv_w), conv_b.reshape(1, HALF), moe_wg, moe_wu, moe_wd)


SCAN_UNROLL = 8


CAST_STEPS = 2 * N_EXPERTS


def _cast_specs(layer, step_of):
    def half(*idx):
        s = jnp.minimum(step_of(*idx), CAST_STEPS - 1)
        return s // 2, s % 2

    def src(*idx):
        e, h = half(*idx)
        return (layer, e, h, 0)

    def dst(*idx):
        e, h = half(*idx)
        return (e, h, 0)

    in_specs = [pl.BlockSpec((1, 1, D // 2, D_FF), src), pl.BlockSpec((1, 1, D // 2, D_FF), src),
                pl.BlockSpec((1, 1, D_FF // 2, D), src)]
    out_specs = [pl.BlockSpec((1, D // 2, D_FF), dst), pl.BlockSpec((1, D // 2, D_FF), dst),
                 pl.BlockSpec((1, D_FF // 2, D), dst)]
    out_shape = [jax.ShapeDtypeStruct((N_EXPERTS, D, D_FF), BF16), jax.ShapeDtypeStruct((N_EXPERTS, D, D_FF), BF16),
                 jax.ShapeDtypeStruct((N_EXPERTS, D_FF, D), BF16)]
    return in_specs, out_specs, out_shape


def _cast_weights(srcs, dsts):
    for src, dst in zip(srcs, dsts):
        dst[0] = src[0, 0].astype(BF16)


def _scan_kernel(vf_ref, vb_ref, w_ref, gb_ref, lam_ref, yf_ref, yb_ref, h_ref, a_ref, b_ref):
    @pl.when(pl.program_id(0) == 0)
    def _():
        h_ref[...] = jnp.zeros_like(h_ref)

    for d, v_ref in enumerate((vf_ref, vb_ref)):
        v = v_ref[...].reshape(BATCH * TM, HALF)
        vb = v.astype(BF16)
        neg = -lam_ref[d]
        softplus = jnp.maximum(neg, 0.0) + jnp.log(1.0 + jnp.exp(-jnp.abs(neg)))
        rate = (-0.5 * LRU_C * LOG2E) * softplus
        for n in range(N_LRU_BLOCKS):
            sl = slice(n * LRU_BLOCK, (n + 1) * LRU_BLOCK)
            g = jnp.dot(vb[:, sl], w_ref[d, n], preferred_element_type=F32)
            tr = jnp.tanh(g[:, 0:LRU_BLOCK] + gb_ref[d, 0:1, sl])
            ti = jnp.tanh(g[:, LRU_BLOCK:2 * LRU_BLOCK] + gb_ref[d, 1:2, sl])
            a = jnp.exp2(rate[:, sl] * tr + rate[:, sl])
            a_ref[d, :, sl] = a
            b_ref[d, :, sl] = jnp.sqrt(1.0 - a * a) * ((0.5 * ti + 0.5) * v[:, sl])

    def body(s, hs):
        hs = list(hs)
        for u in range(SCAN_UNROLL):
            t = s * SCAN_UNROLL + u
            for d, y_ref in enumerate((yf_ref, yb_ref)):
                row = t if d == 0 else TM - 1 - t
                for bt in range(BATCH):
                    c = d * BATCH + bt
                    src = bt * TM + row
                    hs[c] = a_ref[d, pl.ds(src, 1), :] * hs[c] + b_ref[d, pl.ds(src, 1), :]
                    y_ref[bt, pl.ds(row, 1), :] = hs[c]
        return tuple(hs)

    init = tuple(h_ref[c:c + 1, :] for c in range(2 * BATCH))
    final = init
    for s in range(TM // SCAN_UNROLL):
        final = body(s, final)
    for c in range(2 * BATCH):
        h_ref[c:c + 1, :] = final[c]


def _scan(v, gate_w, gate_b, lam):
    fwd = pl.BlockSpec((BATCH, TM, HALF), lambda j: (0, j, 0))
    bwd = pl.BlockSpec((BATCH, TM, HALF), lambda j: (0, jnp.where(j == 0, 0, TILES_PER_SEQ + 1 - j), 0))
    shp = jax.ShapeDtypeStruct((BATCH, SEQ + CTX, HALF), F32)
    return pl.pallas_call(
        _scan_kernel,
        grid=(TILES_PER_SEQ + 1,),
        in_specs=[fwd, bwd,
                  pl.BlockSpec((2, N_LRU_BLOCKS, LRU_BLOCK, 2 * LRU_BLOCK), lambda j: (0, 0, 0, 0)),
                  pl.BlockSpec((2, 2, HALF), lambda j: (0, 0, 0)),
                  pl.BlockSpec((2, 1, HALF), lambda j: (0, 0, 0))],
        out_specs=[fwd, bwd],
        out_shape=[shp, shp],
        scratch_shapes=[pltpu.VMEM((8, HALF), F32), pltpu.VMEM((2, BATCH * TM, HALF), F32),
                        pltpu.VMEM((2, BATCH * TM, HALF), F32)],
        compiler_params=_params("arbitrary"),
        name="lru_scan",
    )(v, v, gate_w, gate_b, lam)


def _route(score, sel):
    rows = [sel[e:e + 1, :] for e in range(N_EXPERTS)]
    gbest = None
    gidx = None
    for g in range(N_GROUPS):
        top2 = None
        for p in range(PER_GROUP):
            for q in range(p + 1, PER_GROUP):
                s = rows[g * PER_GROUP + p] + rows[g * PER_GROUP + q]
                top2 = s if top2 is None else jnp.maximum(top2, s)
        if g == 0:
            gbest = top2
            gidx = jnp.zeros(top2.shape, jnp.int32)
        else:
            better = top2 > gbest
            gidx = jnp.where(better, g, gidx)
            gbest = jnp.where(better, top2, gbest)
    eint = lax.broadcasted_iota(jnp.int32, sel.shape, 0)
    eidx = eint.astype(F32)
    masked = jnp.where(jnp.right_shift(eint, 2) == gidx, sel, -jnp.inf)
    v1 = jnp.max(masked, axis=0, keepdims=True)
    i1 = jnp.min(jnp.where(masked == v1, eidx, float(N_EXPERTS)), axis=0, keepdims=True)
    masked2 = jnp.where(eidx == i1, -jnp.inf, masked)
    v2 = jnp.max(masked2, axis=0, keepdims=True)
    i2 = jnp.min(jnp.where(masked2 == v2, eidx, float(N_EXPERTS)), axis=0, keepdims=True)
    s1 = jnp.sum(jnp.where(eidx == i1, score, 0.0), axis=0, keepdims=True)
    s2 = jnp.sum(jnp.where(eidx == i2, score, 0.0), axis=0, keepdims=True)
    inv = 1.0 / (s1 + s2)
    return jnp.where(eidx == i1, s1 * inv, 0.0) + jnp.where(eidx == i2, s2 * inv, 0.0), gidx


def _group_ranks(gidx, carry_ref, tile):
    onehot = lax.broadcasted_iota(jnp.int32, (8, TM), 0) == gidx
    oh = jnp.where(onehot, 1.0, 0.0)
    before = lax.broadcasted_iota(jnp.int32, (TM, TM), 0) < lax.broadcasted_iota(jnp.int32, (TM, TM), 1)
    prefix = jnp.dot(oh.astype(BF16), jnp.where(before, 1.0, 0.0).astype(BF16), preferred_element_type=F32)
    carry = jnp.where(tile % (TMOE // TM) == 0, 0.0, carry_ref[...])
    rank = jnp.sum(jnp.where(onehot, prefix + carry[:, 0:1], 0.0), axis=0, keepdims=True)
    carry = carry + jnp.sum(oh, axis=1, keepdims=True)
    carry_ref[...] = carry
    return rank, carry


def _split_bf16(v):
    hi = v.astype(BF16)
    return hi, (v - hi.astype(F32)).astype(BF16)


def _outproj_step(m1, m2, x, mod_ref, wo_ref, g_ref, rw_ref, rb_ref, h_ref, n_ref, route_ref, cnt_ref, tok_ref, carry_ref,
                  nprev_ref):
    step = pl.program_id(0)
    n_hi, n_lo = _split_bf16(nprev_ref[...])
    w_hi, w_lo = _split_bf16(rw_ref[...])

    mod = mod_ref[0]
    mix = (jnp.dot(m1, wo_ref[0:HALF, :], preferred_element_type=F32)
           + jnp.dot(m2, wo_ref[HALF:2 * HALF, :], preferred_element_type=F32))

    logits = _nt_dot(w_hi, n_hi) + _nt_dot(w_hi, n_lo) + _nt_dot(w_lo, n_hi)

    h = x + mod[:, 2 * D:3 * D] * mix
    h_ref[...] = h
    n = _rms_mod(h, g_ref[...], mod[:, 4 * D:5 * D], mod[:, 3 * D:4 * D])
    n_ref[...] = n.astype(n_ref.dtype)
    nprev_ref[...] = n

    score = _sigmoid(logits)
    comb, gidx = _route(score, score + rb_ref[...])
    rank, counts = _group_ranks(gidx, carry_ref, step - 1)
    record = jnp.concatenate([comb, gidx.astype(F32), rank, jnp.zeros((ROUTE_LANES - ROUTE_RANK - 1, TM), F32)], axis=0)
    route_ref[...] = record[0:ROUTE_ROWS]
    tok_ref[...] = record.T
    cnt_ref[0] = counts


def _init_pipeline(carry_ref, nprev_ref):
    @pl.when(pl.program_id(0) == 0)
    def _():
        nprev_ref[...] = jnp.zeros_like(nprev_ref)
        carry_ref[...] = jnp.zeros_like(carry_ref)


def _outproj0_kernel(a_ref, yf_ref, yb_ref, gg_ref, x_ref, c_ref, mod_ref, wo_ref, g_ref, rw_ref, rb_ref,
                     h_ref, n_ref, route_ref, cnt_ref, tok_ref, carry_ref, nprev_ref):
    _init_pipeline(carry_ref, nprev_ref)
    tile = jnp.minimum(pl.program_id(0), N_TILES - 1)
    m2 = ((yf_ref[0] + yb_ref[0]) * gg_ref[...]).astype(BF16)
    _outproj_step(a_ref[...], m2, _token_tile(x_ref, c_ref, tile), mod_ref, wo_ref, g_ref, rw_ref, rb_ref,
                  h_ref, n_ref, route_ref, cnt_ref, tok_ref, carry_ref, nprev_ref)


def _outproj1_kernel(cx_ref, cxp_ref, cxn_ref, bg_ref, att_ref, cw_ref, x_ref, mod_ref, wo_ref, g_ref, rw_ref, rb_ref,
                     h_ref, n_ref, route_ref, cnt_ref, tok_ref, buf_ref, carry_ref, nprev_ref):
    _init_pipeline(carry_ref, nprev_ref)
    tile = jnp.minimum(pl.program_id(0), N_LAT_TILES - 1)
    _fill_padded(buf_ref, cxp_ref, cx_ref, cxn_ref, HALO_S, tile)
    conv = _depthwise(buf_ref, cw_ref, CONV_C, HALO_S - 1, TM, 0)
    m1 = (bg_ref[...] * conv).astype(BF16)
    _outproj_step(m1, att_ref[...], x_ref[...], mod_ref, wo_ref, g_ref, rw_ref, rb_ref,
                  h_ref, n_ref, route_ref, cnt_ref, tok_ref, carry_ref, nprev_ref)


def _outproj_common(n_tiles):
    hold = n_tiles - 1
    tile = _held(hold)
    routed = lambda i: jnp.maximum(i - 1, 0)
    in_specs = [pl.BlockSpec((1, 1, 6 * D), _mod_row(TM, hold)),
                pl.BlockSpec((D, D), lambda i: (0, 0)),
                pl.BlockSpec((1, D), lambda i: (0, 0)),
                pl.BlockSpec((N_EXPERTS, D), lambda i: (0, 0)),
                pl.BlockSpec((N_EXPERTS, 1), lambda i: (0, 0))]
    out_specs = [pl.BlockSpec((TM, D), lambda i: (tile(i), 0)),
                 pl.BlockSpec((TM, D), lambda i: (tile(i), 0)),
                 pl.BlockSpec((ROUTE_ROWS, TM), lambda i: (0, routed(i))),
                 pl.BlockSpec((1, 8, 128), lambda i: (routed(i), 0, 0)),
                 pl.BlockSpec((TM, ROUTE_LANES), lambda i: (routed(i), 0))]
    rows = n_tiles * TM
    out_shape = [jax.ShapeDtypeStruct((rows, D), F32), jax.ShapeDtypeStruct((rows, D), BF16),
                 jax.ShapeDtypeStruct((ROUTE_ROWS, rows), F32), jax.ShapeDtypeStruct((n_tiles, 8, 128), F32),
                 jax.ShapeDtypeStruct((rows, ROUTE_LANES), F32)]
    scratch = [pltpu.VMEM((8, 128), F32), pltpu.VMEM((TM, D), F32)]
    return in_specs, out_specs, out_shape, scratch


def _outproj0(a, yf, yb, gg, x, c, mod, wo, g, rw_t, rb):
    hold = N_TILES - 1
    tok = pl.BlockSpec((TM, HALF), lambda i: (_held(hold)(i), 0))
    scan_tok = pl.BlockSpec((1, TM, HALF), functools.partial(_seq_major, hold=hold))
    common_in, out_specs, out_shape, scratch = _outproj_common(N_TILES)
    return pl.pallas_call(
        _outproj0_kernel,
        grid=(N_TILES + 1,),
        in_specs=[tok, scan_tok, scan_tok, tok] + _token_specs(hold) + common_in,
        out_specs=out_specs,
        out_shape=out_shape,
        scratch_shapes=scratch,
        compiler_params=_params("arbitrary"),
        name="outproj0",
    )(a, yf, yb, gg, x, c, mod, wo, g, rw_t, rb)


def _outproj1(cx, bg, att, conv_w, x, mod, wo, g, rw_t, rb):
    hold = N_LAT_TILES - 1
    tok = pl.BlockSpec((TM, HALF), lambda i: (_held(hold)(i), 0))
    ps, ns = _halo_specs(HALO_S, hold)
    common_in, out_specs, out_shape, scratch = _outproj_common(N_LAT_TILES)
    return pl.pallas_call(
        _outproj1_kernel,
        grid=(N_LAT_TILES + 1,),
        in_specs=[tok, ps, ns, tok, tok, pl.BlockSpec((CONV_C, 8, HALF), lambda i: (0, 0, 0)),
                  pl.BlockSpec((TM, D), lambda i: (_held(hold)(i), 0))] + common_in,
        out_specs=out_specs,
        out_shape=out_shape,
        scratch_shapes=[pltpu.VMEM((TM + 2 * HALO_S, HALF), F32)] + scratch,
        compiler_params=_params("arbitrary"),
        name="outproj1",
    )(cx, cx, cx, bg, att, _sublane_replicated(conv_w), x, mod, wo, g, rw_t, rb)


def _moe_kernel(cnt_ref, n_ref, rt_ref, r_ref, wg_ref, wu_ref, wd_ref, h_ref, mod_ref, fg_ref, o_ref,
                hid_ref, *, final_norm, first_tile, subtiles):
    i = pl.program_id(0)
    g = pl.program_id(1)

    @pl.when(g == 0)
    def _():
        o_ref[...] = jnp.zeros_like(o_ref)

    gf = g.astype(F32)
    slot_row = lax.broadcasted_iota(jnp.int32, (CAP, TMOE), 0).astype(F32)
    slot_col = lax.broadcasted_iota(jnp.int32, (TMOE, CAP), 1).astype(F32)

    counts = [cnt_ref[(first_tile + i * subtiles + s) * N_GROUPS + g] for s in range(subtiles)]
    starts = [sum(counts[:s], jnp.int32(0)) for s in range(subtiles)]
    total = starts[-1] + counts[-1]
    tile_rows = [slice(s * TMOE, (s + 1) * TMOE) for s in range(subtiles)]

    def run_chunk(k, sources):
        base = k * CAP
        sels, shifts = [], []
        xg = None
        for s in sources:
            shift = (starts[s] - base).astype(F32)
            in_group = rt_ref[ROUTE_GROUP:ROUTE_GROUP + 1, tile_rows[s]] == gf
            slot = jnp.where(in_group, rt_ref[ROUTE_RANK:ROUTE_RANK + 1, tile_rows[s]] + shift, -1.0)
            sel = slot_row == slot
            part = jnp.dot(jnp.where(sel, 1.0, 0.0).astype(BF16), n_ref[tile_rows[s], :], preferred_element_type=F32)
            xg = part if xg is None else xg + part
            sels.append(sel)
            shifts.append(shift)
        xg = xg.astype(BF16)
        for j in range(PER_GROUP):
            cw = None
            for s, sel in zip(sources, sels):
                comb_row = rt_ref[pl.ds(g * PER_GROUP + j, 1), tile_rows[s]]
                part = jnp.sum(jnp.where(sel, comb_row, 0.0), axis=1, keepdims=True)
                cw = part if cw is None else cw + part
            hid = (_silu(jnp.dot(xg, wg_ref[j], preferred_element_type=F32))
                   * jnp.dot(xg, wu_ref[j], preferred_element_type=F32) * cw)
            hid_ref[:, j * D_FF:(j + 1) * D_FF] = hid.astype(BF16)
        y = jnp.dot(hid_ref[...], wd_ref[...].reshape(PER_GROUP * D_FF, D),
                    preferred_element_type=F32).astype(BF16)
        for s, shift in zip(sources, shifts):
            in_group = r_ref[tile_rows[s], ROUTE_GROUP:ROUTE_GROUP + 1] == gf
            slot = r_ref[tile_rows[s], ROUTE_RANK:ROUTE_RANK + 1] + shift
            back = jnp.where(jnp.logical_and(in_group, slot_col == slot), 1.0, 0.0).astype(BF16)
            o_ref[tile_rows[s], :] += jnp.dot(back, y, preferred_element_type=F32)

    def loop(lo, hi, sources):
        def body(k, carry):
            run_chunk(k, sources)
            return carry
        lax.fori_loop(lo, hi, body, 0)

    for s in range(subtiles):
        end = starts[s] + counts[s]
        first_inside = (starts[s] + (CAP - 1)) // CAP
        if s + 1 < subtiles:
            loop(first_inside, end // CAP, [s])

            @pl.when(end % CAP != 0)
            def _(s=s, end=end):
                run_chunk(end // CAP, [s, s + 1])
        else:
            loop(first_inside, (end + (CAP - 1)) // CAP, [s])

    @pl.when(g == N_GROUPS - 1)
    def _():
        out = h_ref[...] + mod_ref[0][:, 5 * D:6 * D] * o_ref[...]
        if final_norm:
            out = out * lax.rsqrt(jnp.mean(out * out, axis=-1, keepdims=True) + EPS) * fg_ref[...]
        o_ref[...] = out


def _moe_chunks(counts):
    sub = TMOE // TM
    return counts[sub - 1::sub, :N_GROUPS, 0].astype(jnp.int32).reshape(-1)


def _moe(n_chunks, n, route_t, route, wg, wu, wd, h, mod, final_g, *, first_tile, n_tiles, subtiles, final_norm,
         name):
    assert subtiles in (1, 2) and first_tile % subtiles == 0 and n_tiles % subtiles == 0
    step = subtiles * TMOE
    first = first_tile // subtiles
    mod_row = lambda i, g, nch: (jnp.minimum(((first + i) * step) // SEQ, BATCH), 0, 0)
    tok = pl.BlockSpec((step, D), lambda i, g, nch: (first + i, 0))
    grid_spec = pltpu.PrefetchScalarGridSpec(
        num_scalar_prefetch=1,
        grid=(n_tiles // subtiles, N_GROUPS),
        in_specs=[tok,
                  pl.BlockSpec((ROUTE_ROWS, step), lambda i, g, nch: (0, first + i)),
                  pl.BlockSpec((step, ROUTE_LANES), lambda i, g, nch: (first + i, 0)),
                  pl.BlockSpec((PER_GROUP, D, D_FF), lambda i, g, nch: (g, 0, 0)),
                  pl.BlockSpec((PER_GROUP, D, D_FF), lambda i, g, nch: (g, 0, 0)),
                  pl.BlockSpec((PER_GROUP, D_FF, D), lambda i, g, nch: (g, 0, 0)),
                  tok,
                  pl.BlockSpec((1, 1, 6 * D), mod_row),
                  pl.BlockSpec((1, D), lambda i, g, nch: (0, 0))],
        out_specs=pl.BlockSpec((step, D), lambda i, g, nch: (i, 0)),
        scratch_shapes=[pltpu.VMEM((CAP, PER_GROUP * D_FF), BF16)])
    return pl.pallas_call(
        functools.partial(_moe_kernel, final_norm=final_norm, first_tile=first_tile, subtiles=subtiles),
        grid_spec=grid_spec,
        out_shape=jax.ShapeDtypeStruct((n_tiles * TMOE, D), F32),
        compiler_params=_params("parallel", "arbitrary", vmem=VMEM_LIMIT_MOE),
        name=name,
    )(n_chunks, n, route_t, route, wg, wu, wd, h, mod, final_g)


def _inproj1_kernel(x_ref, c_ref, mod_ref, g_ref, w_ref, wvt_ref, wg_ref, wu_ref, wd_ref,
                    cx_ref, bg_ref, q_ref, k_ref, vt_ref, wg_out, wu_out, wd_out):
    _cast_weights((wg_ref, wu_ref, wd_ref), (wg_out, wu_out, wd_out))
    mod = mod_ref[0]
    n = _rms_mod(_token_tile(x_ref, c_ref, rows=TIN), g_ref[...], mod[:, D:2 * D], mod[:, 0:D]).astype(BF16)

    def proj(c):
        return jnp.dot(n, w_ref[:, c * HALF:(c + 1) * HALF], preferred_element_type=F32)

    conv_in, conv_gate = proj(0), proj(2)
    out_gate = proj(1)
    cx_ref[...] = conv_gate * conv_in
    q = proj(3)
    bg_ref[...] = out_gate
    k = proj(4)
    q_ref[...] = (q * (HEAD_DIM ** -0.5 * LOG2E)).astype(BF16)
    vt = _nt_dot(wvt_ref[...], n)
    k_ref[...] = k.astype(BF16)
    vt_ref[...] = vt.astype(BF16)


def _inproj1(x, c, mod, g, w, moe_wg, moe_wu, moe_wd):
    steps = N_TOK // TIN
    assert CAST_STEPS <= steps
    tok = pl.BlockSpec((TIN, HALF), lambda i: (i, 0))
    f = jax.ShapeDtypeStruct((N_TOK, HALF), F32)
    h = jax.ShapeDtypeStruct((N_TOK, HALF), BF16)
    cast_in, cast_out, cast_shape = _cast_specs(1, lambda i: i)
    return pl.pallas_call(
        _inproj1_kernel,
        grid=(steps,),
        in_specs=_token_specs(rows=TIN) + [
                  pl.BlockSpec((1, 1, 6 * D), _mod_row(TIN)),
                  pl.BlockSpec((1, D), lambda i: (0, 0)),
                  pl.BlockSpec((D, 5 * HALF), lambda i: (0, 0)),
                  pl.BlockSpec((HALF, D), lambda i: (0, 0))] + cast_in,
        out_specs=[tok] * 4 + [pl.BlockSpec((HALF, TIN), lambda i: (0, i))] + cast_out,
        out_shape=[f, f, h, h, jax.ShapeDtypeStruct((HALF, N_TOK), BF16)] + cast_shape,
        compiler_params=_params("arbitrary"),
        name="inproj1",
    )(x, c, mod, g, w[:, :5 * HALF], w[:, 5 * HALF:].T, moe_wg, moe_wu, moe_wd)


def _natten_kernel(q_ref, kp_ref, kc_ref, kn_ref, vp_ref, vc_ref, vn_ref, kx_ref, vx_ref, bias_ref, o_ref):
    pair = 2 * HEAD_DIM
    low = lax.broadcasted_iota(jnp.int32, (TM, pair), 1) < HEAD_DIM
    head_mask = [low.astype(F32).astype(BF16), jnp.logical_not(low).astype(F32).astype(BF16)]
    top = lax.broadcasted_iota(jnp.int32, (pair, TM), 0) < HEAD_DIM
    own_rows = [top.astype(F32).astype(BF16), jnp.logical_not(top).astype(F32).astype(BF16)]
    def raw_scores(head):
        g, hh = divmod(head, 2)
        sl = slice(pair * g, pair * (g + 1))
        qm = q_ref[:, sl] * head_mask[hh]
        return [_nt_dot(qm, k_ref[:, sl]) for k_ref in (kp_ref, kc_ref, kn_ref, kx_ref)]

    def biased(head, s):
        s = [s[t] + bias_ref[0, head, :, t * TM:(t + 1) * TM] for t in range(3)] + [s[3]]
        m = jnp.max(jnp.maximum(jnp.maximum(s[0], s[1]), jnp.maximum(s[2], s[3])), axis=-1, keepdims=True)
        return s, m

    def attend(head, s, m):
        g, hh = divmod(head, 2)
        sl = slice(pair * g, pair * (g + 1))
        p = jnp.concatenate([jnp.exp2(st - m).astype(BF16) for st in s], axis=1)
        lhs = jnp.concatenate([vt_ref[sl, :] * own_rows[hh] + own_rows[1 - hh]
                               for vt_ref in (vp_ref, vc_ref, vn_ref, vx_ref)], axis=1)
        acc = _nt_dot(lhs, p)
        if hh == 0:
            return acc[0:HEAD_DIM] * (1.0 / acc[HEAD_DIM:HEAD_DIM + 1])
        return acc[HEAD_DIM:pair] * (1.0 / acc[0:1])

    pending = biased(0, raw_scores(0))
    halves = []
    for head in range(N_HEADS):
        current = pending
        if head + 1 < N_HEADS:
            pending = biased(head + 1, raw_scores(head + 1))
        halves.append(attend(head, *current))
        if head % 2 == 1:
            sl = slice(pair * (head // 2), pair * (head // 2 + 1))
            o_ref[:, sl] = jnp.concatenate(halves, axis=0).T.astype(o_ref.dtype)
            halves = []


def _natten_bias(rpb):
    n_rows = SEQ // GRID_W
    n_dr, n_dc = 2 * WIN_H - 1, 2 * WIN_W - 1
    i = np.arange(ROWS_Q)
    j = np.arange(3 * ROWS_Q)
    col = np.arange(GRID_W)
    col_start = np.clip(col - WIN_W // 2, 0, GRID_W - WIN_W)
    col_ok = (col[None, :] >= col_start[:, None]) & (col[None, :] < col_start[:, None] + WIN_W)
    col_idx = col[None, :] - col[:, None] + (WIN_W - 1)
    onehot = ((col_idx[None] == np.arange(n_dc)[:, None, None]) & col_ok[None]).astype(np.float32)
    col_exp = jnp.dot(rpb.reshape(N_HEADS * n_dr, n_dc).astype(F32), onehot.reshape(n_dc, GRID_W * GRID_W),
                      precision=lax.Precision.HIGHEST).reshape(N_HEADS, n_dr, GRID_W, GRID_W)
    col_exp = jnp.where(col_ok[None, None], col_exp * LOG2E, NEG)
    masked = jnp.full((N_HEADS, GRID_W, GRID_W), NEG, F32)
    kinds = []
    for r0 in (0, ROWS_Q, n_rows - ROWS_Q):
        r = r0 + i
        kr = r0 - ROWS_Q + j
        r_start = np.clip(r - WIN_H // 2, 0, n_rows - WIN_H)
        row_ok = ((kr[None, :] >= r_start[:, None]) & (kr[None, :] < r_start[:, None] + WIN_H)
                  & (kr[None, :] >= 0) & (kr[None, :] < n_rows))
        row_idx = kr[None, :] - r[:, None] + (WIN_H - 1)
        rows = [jnp.concatenate([col_exp[:, row_idx[qi, kj]] if row_ok[qi, kj] else masked for kj in range(3 * ROWS_Q)],
                                axis=-1) for qi in range(ROWS_Q)]
        kinds.append(jnp.concatenate(rows, axis=1))
    return jnp.stack(kinds)


def _natten(q, k, vt, bias):
    def tile(b, i, off):
        return b * TILES_PER_SEQ + jnp.clip(i + off, 0, TILES_PER_SEQ - 1)

    def lat(off):
        return pl.BlockSpec((TM, HALF), lambda b, i: (tile(b, i, off), 0))

    def lat_t(off):
        return pl.BlockSpec((HALF, TM), lambda b, i: (0, tile(b, i, off)))

    ctx = pl.BlockSpec((TM, HALF), lambda b, i: (N_LAT_TILES + b, 0))
    ctx_t = pl.BlockSpec((HALF, TM), lambda b, i: (0, N_LAT_TILES + b))

    def kind(b, i):
        return (jnp.where(i == 0, 0, jnp.where(i == TILES_PER_SEQ - 1, 2, 1)), 0, 0, 0)

    return pl.pallas_call(
        _natten_kernel,
        grid=(BATCH, TILES_PER_SEQ),
        in_specs=[lat(0), lat(-1), lat(0), lat(1), lat_t(-1), lat_t(0), lat_t(1), ctx, ctx_t,
                  pl.BlockSpec((1, N_HEADS, TM, 3 * TM), kind)],
        out_specs=lat(0),
        out_shape=jax.ShapeDtypeStruct((N_LAT, HALF), BF16),
        compiler_params=_params("parallel", "arbitrary"),
        name="natten",
    )(q, k, k, k, vt, vt, vt, k, vt, bias)


def kernel(x, c, ctx, c_ctx, ada_w, ada_b, norm_mix_g, norm_ffn_g, w_out, ab_w_in, a_dw_w, a_dw_b, a_ln_g, a_ln_b,
           b_conv_w, b_conv_b, b_gate_w, b_gate_b, b_lambda, cd_w_in, c_conv_w, d_rpb, router_w, router_bias,
           moe_w_gate, moe_w_up, moe_w_down, final_g):
    x_lat = x.reshape(N_LAT, D)
    x_ctx = ctx.reshape(BATCH * CTX, D)
    cond = jnp.concatenate([c, c_ctx[None], jnp.zeros((8 - BATCH - 1, D), F32)], axis=0)
    mod = _modulation(cond, ada_w, ada_b)
    mod0 = mod[0].reshape(8, 1, 6 * D)
    mod1 = mod[1].reshape(8, 1, 6 * D)

    wo = w_out.astype(BF16)
    lat_tiles = N_LAT // TMOE
    ctx_tiles = BATCH * CTX // TMOE
    rw_t = router_w.T
    rb = router_bias.reshape(N_EXPERTS, 1)
    fg = final_g.reshape(1, D)

    col_scale = jnp.concatenate([jnp.full((2 * HALF,), 0.5, F32), jnp.ones((HALF,), F32), jnp.full((HALF,), 0.5, F32)])
    w_in0 = (ab_w_in[0] * col_scale).astype(BF16)
    gg, a_out, v, wg0, wu0, wd0 = _front0(x_lat, x_ctx, mod0, norm_mix_g[0].reshape(1, D), w_in0,
                                          a_dw_w[0], a_dw_b[0], a_ln_g[0], a_ln_b[0], b_conv_w[0], b_conv_b[0],
                                          moe_w_gate, moe_w_up, moe_w_down)
    gw = b_gate_w[0]
    gate_w = (0.5 * jnp.concatenate([gw[:, 0], gw[:, 1]], axis=-1)).astype(BF16)
    yf, yb = _scan(v, gate_w, 0.5 * b_gate_b[0], b_lambda[0].reshape(2, 1, HALF))
    h1, n2, route0, cnt0, tok0 = _outproj0(a_out, yf, yb, gg, x_lat, x_ctx, mod0, wo[0],
                                           norm_ffn_g[0].reshape(1, D), rw_t, rb)
    moe0 = functools.partial(_moe, _moe_chunks(cnt0), n2, route0, tok0, wg0, wu0, wd0, h1, mod0, fg,
                             final_norm=False)
    h2_lat = moe0(first_tile=0, n_tiles=lat_tiles, subtiles=2, name="moe_lat")
    h2_ctx = moe0(first_tile=lat_tiles, n_tiles=ctx_tiles, subtiles=1, name="moe_ctx")

    cx, bg, q, k, vv, wg1, wu1, wd1 = _inproj1(h2_lat, h2_ctx, mod1, norm_mix_g[1].reshape(1, D),
                                               cd_w_in[0].astype(BF16), moe_w_gate, moe_w_up, moe_w_down)
    att = _natten(q, k, vv, _natten_bias(d_rpb[0]))
    h3, n4, route1, cnt1, tok1 = _outproj1(cx, bg, att, c_conv_w[0], h2_lat, mod1, wo[1],
                                           norm_ffn_g[1].reshape(1, D), rw_t, rb)
    out = _moe(_moe_chunks(cnt1), n4, route1, tok1, wg1, wu1, wd1, h3, mod1, fg,
               first_tile=0, n_tiles=lat_tiles, subtiles=2, final_norm=True, name="moe_final")
    return out.reshape(BATCH, SEQ, D)
```

```python
import functools
import math

import jax
import jax.numpy as jnp
import numpy as np
from jax import lax
from jax.experimental import pallas as pl
from jax.experimental.pallas import tpu as pltpu

F32 = jnp.float32
BF16 = jnp.bfloat16

D = 1024
BATCH = 2
SEQ = 8192
CTX = 256
GRID_W = 64
N_LAT = BATCH * SEQ
N_TOK = N_LAT + BATCH * CTX
HALF = 512
CONV_A = 31
CONV_B = 4
CONV_C = 3
LRU_BLOCK = 128
N_LRU_BLOCKS = HALF // LRU_BLOCK
LRU_C = 8.0
HEAD_DIM = 64
N_HEADS = HALF // HEAD_DIM
WIN_H = 8
WIN_W = 16
N_EXPERTS = 16
N_GROUPS = 4
PER_GROUP = N_EXPERTS // N_GROUPS
D_FF = 512
EPS = 1e-6
NEG = -1e30
LOG2E = math.log2(math.e)

TM = 256
TILES_PER_SEQ = SEQ // TM
N_LAT_TILES = N_LAT // TM
N_TILES = N_TOK // TM
TIN = 512
TMOE = 512
CAP = 128
ROUTE_ROWS = 24
ROUTE_LANES = 128
ROUTE_GROUP = 16
ROUTE_RANK = 17
HALO_A = 16
HALO_S = 8
ROWS_Q = TM // GRID_W
VMEM_LIMIT = 48 * 1024 * 1024
VMEM_LIMIT_MOE = 56 * 1024 * 1024


def _params(*sem, vmem=VMEM_LIMIT):
    return pltpu.CompilerParams(dimension_semantics=sem, vmem_limit_bytes=vmem)


def _sigmoid(x):
    return 0.5 * jnp.tanh(0.5 * x) + 0.5


def _silu(x):
    return x * _sigmoid(x)


GELU_C = 0.7978845608028654
GELU_K = 0.044715


def _gelu_tanh_of_half(xh):
    return xh * (1.0 + jnp.tanh(xh * (2.0 * GELU_C + (8.0 * GELU_C * GELU_K) * (xh * xh))))


def _rms_mod(x, g, scale, shift):
    return x * lax.rsqrt(jnp.mean(x * x, axis=-1, keepdims=True) + EPS) * (g * (1.0 + scale)) + shift


def _nt_dot(a, b):
    return lax.dot_general(a, b, (((1,), (1,)), ((), ())), preferred_element_type=F32)


def _held(hold):
    return (lambda i: i) if hold is None else (lambda i: jnp.minimum(i, hold))


def _mod_row(tile_rows, hold=None):
    per_seq = SEQ // tile_rows
    tile = _held(hold)
    return lambda i: (jnp.minimum(tile(i) // per_seq, BATCH), 0, 0)


def _mod_kernel(c_ref, w_ref, b_ref, o_ref):
    c = c_ref[...]
    s = _silu(c).astype(BF16)
    o_ref[0] = jnp.dot(s, w_ref[0].astype(BF16), preferred_element_type=F32) + b_ref[0]


def _modulation(cond, ada_w, ada_b):
    depth = ada_w.shape[0]
    nb = 1536
    return pl.pallas_call(
        _mod_kernel,
        grid=(depth, 6 * D // nb),
        in_specs=[pl.BlockSpec((8, D), lambda l, j: (0, 0)),
                  pl.BlockSpec((1, D, nb), lambda l, j: (l, 0, j)),
                  pl.BlockSpec((1, 1, nb), lambda l, j: (l, 0, j))],
        out_specs=pl.BlockSpec((1, 8, nb), lambda l, j: (l, 0, j)),
        out_shape=jax.ShapeDtypeStruct((depth, 8, 6 * D), F32),
        compiler_params=_params("parallel", "parallel"),
        name="modulation",
    )(cond, ada_w, ada_b.reshape(depth, 1, 6 * D))


def _token_specs(hold=None, rows=TM):
    tile = _held(hold)
    lat_tiles = N_LAT // rows
    lat = pl.BlockSpec((rows, D), lambda i: (jnp.minimum(tile(i), lat_tiles - 1), 0))
    ctx = pl.BlockSpec((rows, D), lambda i: (jnp.maximum(tile(i) - lat_tiles, 0), 0))
    return [lat, ctx]


def _token_tile(lat_ref, ctx_ref, tile=None, rows=TM):
    tile = pl.program_id(0) if tile is None else tile
    return jnp.where(tile < N_LAT // rows, lat_ref[...], ctx_ref[...])


def _halo_specs(halo, hold=None):
    per_tile = TM // halo
    last = N_TOK // halo - 1
    tile = _held(hold)
    prev = pl.BlockSpec((halo, HALF), lambda i: (jnp.maximum(tile(i) * per_tile - 1, 0), 0))
    nxt = pl.BlockSpec((halo, HALF), lambda i: (jnp.minimum((tile(i) + 1) * per_tile, last), 0))
    return prev, nxt


def _seq_edges(i):
    is_ctx = i >= N_LAT_TILES
    first = jnp.logical_or(is_ctx, i % TILES_PER_SEQ == 0)
    last = jnp.logical_or(is_ctx, i % TILES_PER_SEQ == TILES_PER_SEQ - 1)
    return first, last


def _fill_padded(buf_ref, prev_ref, cur_ref, next_ref, halo, tile=None):
    first, last = _seq_edges(pl.program_id(0) if tile is None else tile)
    buf_ref[0:halo, :] = jnp.where(first, 0.0, prev_ref[...])
    buf_ref[halo:halo + TM, :] = cur_ref[...]
    buf_ref[halo + TM:halo + TM + halo, :] = jnp.where(last, 0.0, next_ref[...])


def _tap_weight(w_ref, k, rows):
    return jnp.concatenate([w_ref[k]] * (rows // 8), axis=0)


def _sublane_replicated(w):
    return jnp.broadcast_to(w[:, None, :], (w.shape[0], 8, w.shape[1]))


def _depthwise(buf_ref, w_ref, taps, first_off, rows, row0):
    acc = None
    for k in range(taps):
        term = _tap_weight(w_ref, k, rows) * buf_ref[row0 + first_off + k:row0 + first_off + k + rows, :]
        acc = term if acc is None else acc + term
    return acc


def _seq_major(i, hold=None):
    i = _held(hold)(i)
    is_lat = i < N_LAT_TILES
    return (jnp.where(is_lat, i // TILES_PER_SEQ, i - N_LAT_TILES), jnp.where(is_lat, 1 + i % TILES_PER_SEQ, 0), 0)


CONV_ROWS = 32
PAD_A_ROWS = TM + 2 * HALO_A
PAD_B_ROWS = TM + 2 * HALO_S
SHIFT_ROWS = PAD_A_ROWS - 8


def _front0_kernel(x_ref, c_ref, xp_ref, xn_ref, mod_ref, g_ref, w_ref,
                   dww_ref, dwb_ref, lng_ref, lnb_ref, cw_ref, cb_ref, wg_ref, wu_ref, wd_ref,
                   gg_ref, a_ref, v_ref, wg_out, wu_out, wd_out, bufa_ref, bufb_ref, shift_ref):
    _cast_weights((wg_ref, wu_ref, wd_ref), (wg_out, wu_out, wd_out))
    step = pl.program_id(0)

    @pl.when(step == 0)
    def _():
        bufa_ref[...] = jnp.zeros_like(bufa_ref)
        bufb_ref[...] = jnp.zeros_like(bufb_ref)

    tile = jnp.minimum(step, N_TILES - 1)
    first, last = _seq_edges(tile)

    def run(new, old):
        mod = mod_ref[0]
        rows_in = jnp.concatenate([xp_ref[...], _token_tile(x_ref, c_ref, tile), xn_ref[...]], axis=0)
        n = _rms_mod(rows_in, g_ref[...], mod[:, D:2 * D], mod[:, 0:D]).astype(BF16)

        def proj(c, rows=slice(None)):
            return jnp.dot(n[rows], w_ref[:, c * HALF:(c + 1) * HALF], preferred_element_type=F32)

        def conv_chunk(r):
            row0 = r * CONV_ROWS
            u = None
            for k in range(CONV_A):
                off = HALO_A - CONV_A // 2 + k
                base = row0 + off - off % 8
                term = _tap_weight(dww_ref, k, CONV_ROWS) * shift_ref[off % 8, base:base + CONV_ROWS, :]
                u = term if u is None else u + term
            u = u + dwb_ref[...]
            mu = jnp.mean(u, axis=-1, keepdims=True)
            uc = u - mu
            var = jnp.mean(uc * uc, axis=-1, keepdims=True)
            y = uc * lax.rsqrt(var + EPS) * lng_ref[...] + lnb_ref[...]
            a_ref[row0:row0 + CONV_ROWS, :] = _silu(y).astype(a_ref.dtype)
            acc = None
            for k in range(CONV_B):
                lo = row0 + HALO_S - 2 + k
                term = _tap_weight(cw_ref, k, CONV_ROWS) * bufb_ref[old, lo:lo + CONV_ROWS, :]
                acc = term if acc is None else acc + term
            v_ref[0, row0:row0 + CONV_ROWS, :] = acc + cb_ref[...]

        for s in range(8):
            shift_ref[s] = bufa_ref[old, s:s + SHIFT_ROWS, :]
        def conv_quarter(q):
            per_quarter = TM // CONV_ROWS // 4
            for r in range(q * per_quarter, (q + 1) * per_quarter):
                conv_chunk(r)

        value, gate = proj(0), proj(1)
        conv_quarter(0)
        ua = value * (jnp.tanh(gate) + 1.0)
        bufa_ref[new, 0:HALO_A, :] = jnp.where(first, 0.0, ua[0:HALO_A])
        bufa_ref[new, HALO_A:HALO_A + TM, :] = ua[HALO_A:HALO_A + TM]
        bufa_ref[new, HALO_A + TM:PAD_A_ROWS, :] = jnp.where(last, 0.0, ua[HALO_A + TM:PAD_A_ROWS])
        recur = proj(2)
        conv_quarter(1)
        lo = HALO_A - HALO_S
        bufb_ref[new, 0:HALO_S, :] = jnp.where(first, 0.0, recur[lo:HALO_A])
        bufb_ref[new, HALO_S:HALO_S + TM, :] = recur[HALO_A:HALO_A + TM]
        bufb_ref[new, HALO_S + TM:PAD_B_ROWS, :] = jnp.where(last, 0.0, recur[HALO_A + TM:HALO_A + TM + HALO_S])
        gelu_in = proj(3, slice(HALO_A, HALO_A + TM))
        conv_quarter(2)
        gg_ref[...] = _gelu_tanh_of_half(gelu_in).astype(gg_ref.dtype)
        conv_quarter(3)

    @pl.when(step % 2 == 0)
    def _():
        run(0, 1)

    @pl.when(step % 2 == 1)
    def _():
        run(1, 0)


def _front0(x, c, mod, g, w, dw_w, dw_b, ln_g, ln_b, conv_w, conv_b, moe_wg, moe_wu, moe_wd):
    assert CAST_STEPS <= N_TILES + 1
    cast_in, cast_out, cast_shape = _cast_specs(0, lambda i: i)
    hold = N_TILES - 1
    tile = _held(hold)
    lagged = lambda i: jnp.maximum(i - 1, 0)
    per_tile = TM // HALO_A
    last_halo = N_LAT // HALO_A - 1
    prev = pl.BlockSpec((HALO_A, D), lambda i: (jnp.clip(tile(i) * per_tile - 1, 0, last_halo), 0))
    nxt = pl.BlockSpec((HALO_A, D), lambda i: (jnp.clip((tile(i) + 1) * per_tile, 0, last_halo), 0))
    vec = pl.BlockSpec((1, HALF), lambda i: (0, 0))
    return pl.pallas_call(
        _front0_kernel,
        grid=(N_TILES + 1,),
        in_specs=_token_specs(hold) + [prev, nxt,
                  pl.BlockSpec((1, 1, 6 * D), _mod_row(TM, hold)),
                  pl.BlockSpec((1, D), lambda i: (0, 0)),
                  pl.BlockSpec((D, 4 * HALF), lambda i: (0, 0)),
                  pl.BlockSpec((CONV_A, 8, HALF), lambda i: (0, 0, 0)), vec, vec, vec,
                  pl.BlockSpec((CONV_B, 8, HALF), lambda i: (0, 0, 0)), vec] + cast_in,
        out_specs=[pl.BlockSpec((TM, HALF), lambda i: (tile(i), 0)),
                   pl.BlockSpec((TM, HALF), lambda i: (lagged(i), 0)),
                   pl.BlockSpec((1, TM, HALF), lambda i: _seq_major(lagged(i)))] + cast_out,
        out_shape=[jax.ShapeDtypeStruct((N_TOK, HALF), BF16), jax.ShapeDtypeStruct((N_TOK, HALF), BF16),
                   jax.ShapeDtypeStruct((BATCH, SEQ + CTX, HALF), F32)] + cast_shape,
        scratch_shapes=[pltpu.VMEM((2, PAD_A_ROWS, HALF), F32), pltpu.VMEM((2, PAD_B_ROWS, HALF), F32),
                        pltpu.VMEM((8, SHIFT_ROWS, HALF), F32)],
        compiler_params=_params("arbitrary"),
        name="front0",
    )(x, c, x, x, mod, g, w, _sublane_replicated(dw_w), dw_b.reshape(1, HALF), ln_g.reshape(1, HALF),
      ln_b.reshape(1, HALF), _sublane_replicated(conv_w), conv_b.reshape(1, HALF), moe_wg, moe_wu, moe_wd)


SCAN_UNROLL = 8


CAST_STEPS = 2 * N_EXPERTS


def _cast_specs(layer, step_of):
    def half(*idx):
        s = jnp.minimum(step_of(*idx), CAST_STEPS - 1)
        return s // 2, s % 2

    def src(*idx):
        e, h = half(*idx)
        return (layer, e, h, 0)

    def dst(*idx):
        e, h = half(*idx)
        return (e, h, 0)

    in_specs = [pl.BlockSpec((1, 1, D // 2, D_FF), src), pl.BlockSpec((1, 1, D // 2, D_FF), src),
                pl.BlockSpec((1, 1, D_FF // 2, D), src)]
    out_specs = [pl.BlockSpec((1, D // 2, D_FF), dst), pl.BlockSpec((1, D // 2, D_FF), dst),
                 pl.BlockSpec((1, D_FF // 2, D), dst)]
    out_shape = [jax.ShapeDtypeStruct((N_EXPERTS, D, D_FF), BF16), jax.ShapeDtypeStruct((N_EXPERTS, D, D_FF), BF16),
                 jax.ShapeDtypeStruct((N_EXPERTS, D_FF, D), BF16)]
    return in_specs, out_specs, out_shape


def _cast_weights(srcs, dsts):
    for src, dst in zip(srcs, dsts):
        dst[0] = src[0, 0].astype(BF16)


def _scan_kernel(vf_ref, vb_ref, w_ref, gb_ref, lam_ref, yf_ref, yb_ref, h_ref, a_ref, b_ref):
    @pl.when(pl.program_id(0) == 0)
    def _():
        h_ref[...] = jnp.zeros_like(h_ref)

    for d, v_ref in enumerate((vf_ref, vb_ref)):
        v = v_ref[...].reshape(BATCH * TM, HALF)
        vb = v.astype(BF16)
        neg = -lam_ref[d]
        softplus = jnp.maximum(neg, 0.0) + jnp.log(1.0 + jnp.exp(-jnp.abs(neg)))
        rate = (-0.5 * LRU_C * LOG2E) * softplus
        for n in range(N_LRU_BLOCKS):
            sl = slice(n * LRU_BLOCK, (n + 1) * LRU_BLOCK)
            g = jnp.dot(vb[:, sl], w_ref[d, n], preferred_element_type=F32)
            tr = jnp.tanh(g[:, 0:LRU_BLOCK] + gb_ref[d, 0:1, sl])
            ti = jnp.tanh(g[:, LRU_BLOCK:2 * LRU_BLOCK] + gb_ref[d, 1:2, sl])
            a = jnp.exp2(rate[:, sl] * tr + rate[:, sl])
            a_ref[d, :, sl] = a
            b_ref[d, :, sl] = jnp.sqrt(1.0 - a * a) * ((0.5 * ti + 0.5) * v[:, sl])

    def body(s, hs):
        hs = list(hs)
        for u in range(SCAN_UNROLL):
            t = s * SCAN_UNROLL + u
            for d, y_ref in enumerate((yf_ref, yb_ref)):
                row = t if d == 0 else TM - 1 - t
                for bt in range(BATCH):
                    c = d * BATCH + bt
                    src = bt * TM + row
                    hs[c] = a_ref[d, pl.ds(src, 1), :] * hs[c] + b_ref[d, pl.ds(src, 1), :]
                    y_ref[bt, pl.ds(row, 1), :] = hs[c]
        return tuple(hs)

    init = tuple(h_ref[c:c + 1, :] for c in range(2 * BATCH))
    final = init
    for s in range(TM // SCAN_UNROLL):
        final = body(s, final)
    for c in range(2 * BATCH):
        h_ref[c:c + 1, :] = final[c]


def _scan(v, gate_w, gate_b, lam):
    fwd = pl.BlockSpec((BATCH, TM, HALF), lambda j: (0, j, 0))
    bwd = pl.BlockSpec((BATCH, TM, HALF), lambda j: (0, jnp.where(j == 0, 0, TILES_PER_SEQ + 1 - j), 0))
    shp = jax.ShapeDtypeStruct((BATCH, SEQ + CTX, HALF), F32)
    return pl.pallas_call(
        _scan_kernel,
        grid=(TILES_PER_SEQ + 1,),
        in_specs=[fwd, bwd,
                  pl.BlockSpec((2, N_LRU_BLOCKS, LRU_BLOCK, 2 * LRU_BLOCK), lambda j: (0, 0, 0, 0)),
                  pl.BlockSpec((2, 2, HALF), lambda j: (0, 0, 0)),
                  pl.BlockSpec((2, 1, HALF), lambda j: (0, 0, 0))],
        out_specs=[fwd, bwd],
        out_shape=[shp, shp],
        scratch_shapes=[pltpu.VMEM((8, HALF), F32), pltpu.VMEM((2, BATCH * TM, HALF), F32),
                        pltpu.VMEM((2, BATCH * TM, HALF), F32)],
        compiler_params=_params("arbitrary"),
        name="lru_scan",
    )(v, v, gate_w, gate_b, lam)


def _route(score, sel):
    rows = [sel[e:e + 1, :] for e in range(N_EXPERTS)]
    gbest = None
    gidx = None
    for g in range(N_GROUPS):
        top2 = None
        for p in range(PER_GROUP):
            for q in range(p + 1, PER_GROUP):
                s = rows[g * PER_GROUP + p] + rows[g * PER_GROUP + q]
                top2 = s if top2 is None else jnp.maximum(top2, s)
        if g == 0:
            gbest = top2
            gidx = jnp.zeros(top2.shape, jnp.int32)
        else:
            better = top2 > gbest
            gidx = jnp.where(better, g, gidx)
            gbest = jnp.where(better, top2, gbest)
    eint = lax.broadcasted_iota(jnp.int32, sel.shape, 0)
    eidx = eint.astype(F32)
    masked = jnp.where(jnp.right_shift(eint, 2) == gidx, sel, -jnp.inf)
    v1 = jnp.max(masked, axis=0, keepdims=True)
    i1 = jnp.min(jnp.where(masked == v1, eidx, float(N_EXPERTS)), axis=0, keepdims=True)
    masked2 = jnp.where(eidx == i1, -jnp.inf, masked)
    v2 = jnp.max(masked2, axis=0, keepdims=True)
    i2 = jnp.min(jnp.where(masked2 == v2, eidx, float(N_EXPERTS)), axis=0, keepdims=True)
    s1 = jnp.sum(jnp.where(eidx == i1, score, 0.0), axis=0, keepdims=True)
    s2 = jnp.sum(jnp.where(eidx == i2, score, 0.0), axis=0, keepdims=True)
    inv = 1.0 / (s1 + s2)
    return jnp.where(eidx == i1, s1 * inv, 0.0) + jnp.where(eidx == i2, s2 * inv, 0.0), gidx


def _group_ranks(gidx, carry_ref, tile):
    onehot = lax.broadcasted_iota(jnp.int32, (8, TM), 0) == gidx
    oh = jnp.where(onehot, 1.0, 0.0)
    before = lax.broadcasted_iota(jnp.int32, (TM, TM), 0) < lax.broadcasted_iota(jnp.int32, (TM, TM), 1)
    prefix = jnp.dot(oh.astype(BF16), jnp.where(before, 1.0, 0.0).astype(BF16), preferred_element_type=F32)
    carry = jnp.where(tile % (TMOE // TM) == 0, 0.0, carry_ref[...])
    rank = jnp.sum(jnp.where(onehot, prefix + carry[:, 0:1], 0.0), axis=0, keepdims=True)
    carry = carry + jnp.sum(oh, axis=1, keepdims=True)
    carry_ref[...] = carry
    return rank, carry


def _split_bf16(v):
    hi = v.astype(BF16)
    return hi, (v - hi.astype(F32)).astype(BF16)


def _outproj_step(m1, m2, x, mod_ref, wo_ref, g_ref, rw_ref, rb_ref, h_ref, n_ref, route_ref, cnt_ref, tok_ref, carry_ref,
                  nprev_ref):
    step = pl.program_id(0)
    n_hi, n_lo = _split_bf16(nprev_ref[...])
    w_hi, w_lo = _split_bf16(rw_ref[...])

    mod = mod_ref[0]
    mix = (jnp.dot(m1, wo_ref[0:HALF, :], preferred_element_type=F32)
           + jnp.dot(m2, wo_ref[HALF:2 * HALF, :], preferred_element_type=F32))

    logits = _nt_dot(w_hi, n_hi) + _nt_dot(w_hi, n_lo) + _nt_dot(w_lo, n_hi)

    h = x + mod[:, 2 * D:3 * D] * mix
    h_ref[...] = h
    n = _rms_mod(h, g_ref[...], mod[:, 4 * D:5 * D], mod[:, 3 * D:4 * D])
    n_ref[...] = n.astype(n_ref.dtype)
    nprev_ref[...] = n

    score = _sigmoid(logits)
    comb, gidx = _route(score, score + rb_ref[...])
    rank, counts = _group_ranks(gidx, carry_ref, step - 1)
    record = jnp.concatenate([comb, gidx.astype(F32), rank, jnp.zeros((ROUTE_LANES - ROUTE_RANK - 1, TM), F32)], axis=0)
    route_ref[...] = record[0:ROUTE_ROWS]
    tok_ref[...] = record.T
    cnt_ref[0] = counts


def _init_pipeline(carry_ref, nprev_ref):
    @pl.when(pl.program_id(0) == 0)
    def _():
        nprev_ref[...] = jnp.zeros_like(nprev_ref)
        carry_ref[...] = jnp.zeros_like(carry_ref)


def _outproj0_kernel(a_ref, yf_ref, yb_ref, gg_ref, x_ref, c_ref, mod_ref, wo_ref, g_ref, rw_ref, rb_ref,
                     h_ref, n_ref, route_ref, cnt_ref, tok_ref, carry_ref, nprev_ref):
    _init_pipeline(carry_ref, nprev_ref)
    tile = jnp.minimum(pl.program_id(0), N_TILES - 1)
    m2 = ((yf_ref[0] + yb_ref[0]) * gg_ref[...]).astype(BF16)
    _outproj_step(a_ref[...], m2, _token_tile(x_ref, c_ref, tile), mod_ref, wo_ref, g_ref, rw_ref, rb_ref,
                  h_ref, n_ref, route_ref, cnt_ref, tok_ref, carry_ref, nprev_ref)


def _outproj1_kernel(cx_ref, cxp_ref, cxn_ref, bg_ref, att_ref, cw_ref, x_ref, mod_ref, wo_ref, g_ref, rw_ref, rb_ref,
                     h_ref, n_ref, route_ref, cnt_ref, tok_ref, buf_ref, carry_ref, nprev_ref):
    _init_pipeline(carry_ref, nprev_ref)
    tile = jnp.minimum(pl.program_id(0), N_LAT_TILES - 1)
    _fill_padded(buf_ref, cxp_ref, cx_ref, cxn_ref, HALO_S, tile)
    conv = _depthwise(buf_ref, cw_ref, CONV_C, HALO_S - 1, TM, 0)
    m1 = (bg_ref[...] * conv).astype(BF16)
    _outproj_step(m1, att_ref[...], x_ref[...], mod_ref, wo_ref, g_ref, rw_ref, rb_ref,
                  h_ref, n_ref, route_ref, cnt_ref, tok_ref, carry_ref, nprev_ref)


def _outproj_common(n_tiles):
    hold = n_tiles - 1
    tile = _held(hold)
    routed = lambda i: jnp.maximum(i - 1, 0)
    in_specs = [pl.BlockSpec((1, 1, 6 * D), _mod_row(TM, hold)),
                pl.BlockSpec((D, D), lambda i: (0, 0)),
                pl.BlockSpec((1, D), lambda i: (0, 0)),
                pl.BlockSpec((N_EXPERTS, D), lambda i: (0, 0)),
                pl.BlockSpec((N_EXPERTS, 1), lambda i: (0, 0))]
    out_specs = [pl.BlockSpec((TM, D), lambda i: (tile(i), 0)),
                 pl.BlockSpec((TM, D), lambda i: (tile(i), 0)),
                 pl.BlockSpec((ROUTE_ROWS, TM), lambda i: (0, routed(i))),
                 pl.BlockSpec((1, 8, 128), lambda i: (routed(i), 0, 0)),
                 pl.BlockSpec((TM, ROUTE_LANES), lambda i: (routed(i), 0))]
    rows = n_tiles * TM
    out_shape = [jax.ShapeDtypeStruct((rows, D), F32), jax.ShapeDtypeStruct((rows, D), BF16),
                 jax.ShapeDtypeStruct((ROUTE_ROWS, rows), F32), jax.ShapeDtypeStruct((n_tiles, 8, 128), F32),
                 jax.ShapeDtypeStruct((rows, ROUTE_LANES), F32)]
    scratch = [pltpu.VMEM((8, 128), F32), pltpu.VMEM((TM, D), F32)]
    return in_specs, out_specs, out_shape, scratch


def _outproj0(a, yf, yb, gg, x, c, mod, wo, g, rw_t, rb):
    hold = N_TILES - 1
    tok = pl.BlockSpec((TM, HALF), lambda i: (_held(hold)(i), 0))
    scan_tok = pl.BlockSpec((1, TM, HALF), functools.partial(_seq_major, hold=hold))
    common_in, out_specs, out_shape, scratch = _outproj_common(N_TILES)
    return pl.pallas_call(
        _outproj0_kernel,
        grid=(N_TILES + 1,),
        in_specs=[tok, scan_tok, scan_tok, tok] + _token_specs(hold) + common_in,
        out_specs=out_specs,
        out_shape=out_shape,
        scratch_shapes=scratch,
        compiler_params=_params("arbitrary"),
        name="outproj0",
    )(a, yf, yb, gg, x, c, mod, wo, g, rw_t, rb)


def _outproj1(cx, bg, att, conv_w, x, mod, wo, g, rw_t, rb):
    hold = N_LAT_TILES - 1
    tok = pl.BlockSpec((TM, HALF), lambda i: (_held(hold)(i), 0))
    ps, ns = _halo_specs(HALO_S, hold)
    common_in, out_specs, out_shape, scratch = _outproj_common(N_LAT_TILES)
    return pl.pallas_call(
        _outproj1_kernel,
        grid=(N_LAT_TILES + 1,),
        in_specs=[tok, ps, ns, tok, tok, pl.BlockSpec((CONV_C, 8, HALF), lambda i: (0, 0, 0)),
                  pl.BlockSpec((TM, D), lambda i: (_held(hold)(i), 0))] + common_in,
        out_specs=out_specs,
        out_shape=out_shape,
        scratch_shapes=[pltpu.VMEM((TM + 2 * HALO_S, HALF), F32)] + scratch,
        compiler_params=_params("arbitrary"),
        name="outproj1",
    )(cx, cx, cx, bg, att, _sublane_replicated(conv_w), x, mod, wo, g, rw_t, rb)


def _moe_kernel(cnt_ref, n_ref, rt_ref, r_ref, wg_ref, wu_ref, wd_ref, h_ref, mod_ref, fg_ref, o_ref,
                hid_ref, *, final_norm, first_tile, subtiles):
    i = pl.program_id(0)
    g = pl.program_id(1)

    @pl.when(g == 0)
    def _():
        o_ref[...] = jnp.zeros_like(o_ref)

    gf = g.astype(F32)
    slot_row = lax.broadcasted_iota(jnp.int32, (CAP, TMOE), 0).astype(F32)
    slot_col = lax.broadcasted_iota(jnp.int32, (TMOE, CAP), 1).astype(F32)

    counts = [cnt_ref[(first_tile + i * subtiles + s) * N_GROUPS + g] for s in range(subtiles)]
    starts = [sum(counts[:s], jnp.int32(0)) for s in range(subtiles)]
    total = starts[-1] + counts[-1]
    tile_rows = [slice(s * TMOE, (s + 1) * TMOE) for s in range(subtiles)]

    def run_chunk(k, sources):
        base = k * CAP
        sels, shifts = [], []
        xg = None
        for s in sources:
            shift = (starts[s] - base).astype(F32)
            in_group = rt_ref[ROUTE_GROUP:ROUTE_GROUP + 1, tile_rows[s]] == gf
            slot = jnp.where(in_group, rt_ref[ROUTE_RANK:ROUTE_RANK + 1, tile_rows[s]] + shift, -1.0)
            sel = slot_row == slot
            part = jnp.dot(jnp.where(sel, 1.0, 0.0).astype(BF16), n_ref[tile_rows[s], :], preferred_element_type=F32)
            xg = part if xg is None else xg + part
            sels.append(sel)
            shifts.append(shift)
        xg = xg.astype(BF16)
        for j in range(PER_GROUP):
            cw = None
            for s, sel in zip(sources, sels):
                comb_row = rt_ref[pl.ds(g * PER_GROUP + j, 1), tile_rows[s]]
                part = jnp.sum(jnp.where(sel, comb_row, 0.0), axis=1, keepdims=True)
                cw = part if cw is None else cw + part
            hid = (_silu(jnp.dot(xg, wg_ref[j], preferred_element_type=F32))
                   * jnp.dot(xg, wu_ref[j], preferred_element_type=F32) * cw)
            hid_ref[:, j * D_FF:(j + 1) * D_FF] = hid.astype(BF16)
        y = jnp.dot(hid_ref[...], wd_ref[...].reshape(PER_GROUP * D_FF, D),
                    preferred_element_type=F32).astype(BF16)
        for s, shift in zip(sources, shifts):
            in_group = r_ref[tile_rows[s], ROUTE_GROUP:ROUTE_GROUP + 1] == gf
            slot = r_ref[tile_rows[s], ROUTE_RANK:ROUTE_RANK + 1] + shift
            back = jnp.where(jnp.logical_and(in_group, slot_col == slot), 1.0, 0.0).astype(BF16)
            o_ref[tile_rows[s], :] += jnp.dot(back, y, preferred_element_type=F32)

    def loop(lo, hi, sources):
        def body(k, carry):
            run_chunk(k, sources)
            return carry
        lax.fori_loop(lo, hi, body, 0)

    for s in range(subtiles):
        end = starts[s] + counts[s]
        first_inside = (starts[s] + (CAP - 1)) // CAP
        if s + 1 < subtiles:
            loop(first_inside, end // CAP, [s])

            @pl.when(end % CAP != 0)
            def _(s=s, end=end):
                run_chunk(end // CAP, [s, s + 1])
        else:
            loop(first_inside, (end + (CAP - 1)) // CAP, [s])

    @pl.when(g == N_GROUPS - 1)
    def _():
        out = h_ref[...] + mod_ref[0][:, 5 * D:6 * D] * o_ref[...]
        if final_norm:
            out = out * lax.rsqrt(jnp.mean(out * out, axis=-1, keepdims=True) + EPS) * fg_ref[...]
        o_ref[...] = out


def _moe_chunks(counts):
    sub = TMOE // TM
    return counts[sub - 1::sub, :N_GROUPS, 0].astype(jnp.int32).reshape(-1)


def _moe(n_chunks, n, route_t, route, wg, wu, wd, h, mod, final_g, *, first_tile, n_tiles, subtiles, final_norm,
         name):
    assert subtiles in (1, 2) and first_tile % subtiles == 0 and n_tiles % subtiles == 0
    step = subtiles * TMOE
    first = first_tile // subtiles
    mod_row = lambda i, g, nch: (jnp.minimum(((first + i) * step) // SEQ, BATCH), 0, 0)
    tok = pl.BlockSpec((step, D), lambda i, g, nch: (first + i, 0))
    grid_spec = pltpu.PrefetchScalarGridSpec(
        num_scalar_prefetch=1,
        grid=(n_tiles // subtiles, N_GROUPS),
        in_specs=[tok,
                  pl.BlockSpec((ROUTE_ROWS, step), lambda i, g, nch: (0, first + i)),
                  pl.BlockSpec((step, ROUTE_LANES), lambda i, g, nch: (first + i, 0)),
                  pl.BlockSpec((PER_GROUP, D, D_FF), lambda i, g, nch: (g, 0, 0)),
                  pl.BlockSpec((PER_GROUP, D, D_FF), lambda i, g, nch: (g, 0, 0)),
                  pl.BlockSpec((PER_GROUP, D_FF, D), lambda i, g, nch: (g, 0, 0)),
                  tok,
                  pl.BlockSpec((1, 1, 6 * D), mod_row),
                  pl.BlockSpec((1, D), lambda i, g, nch: (0, 0))],
        out_specs=pl.BlockSpec((step, D), lambda i, g, nch: (i, 0)),
        scratch_shapes=[pltpu.VMEM((CAP, PER_GROUP * D_FF), BF16)])
    return pl.pallas_call(
        functools.partial(_moe_kernel, final_norm=final_norm, first_tile=first_tile, subtiles=subtiles),
        grid_spec=grid_spec,
        out_shape=jax.ShapeDtypeStruct((n_tiles * TMOE, D), F32),
        compiler_params=_params("parallel", "arbitrary", vmem=VMEM_LIMIT_MOE),
        name=name,
    )(n_chunks, n, route_t, route, wg, wu, wd, h, mod, final_g)


def _inproj1_kernel(x_ref, c_ref, mod_ref, g_ref, w_ref, wvt_ref, wg_ref, wu_ref, wd_ref,
                    cx_ref, bg_ref, q_ref, k_ref, vt_ref, wg_out, wu_out, wd_out):
    _cast_weights((wg_ref, wu_ref, wd_ref), (wg_out, wu_out, wd_out))
    mod = mod_ref[0]
    n = _rms_mod(_token_tile(x_ref, c_ref, rows=TIN), g_ref[...], mod[:, D:2 * D], mod[:, 0:D]).astype(BF16)

    def proj(c):
        return jnp.dot(n, w_ref[:, c * HALF:(c + 1) * HALF], preferred_element_type=F32)

    conv_in, conv_gate = proj(0), proj(2)
    out_gate = proj(1)
    cx_ref[...] = conv_gate * conv_in
    q = proj(3)
    bg_ref[...] = out_gate
    k = proj(4)
    q_ref[...] = (q * (HEAD_DIM ** -0.5 * LOG2E)).astype(BF16)
    vt = _nt_dot(wvt_ref[...], n)
    k_ref[...] = k.astype(BF16)
    vt_ref[...] = vt.astype(BF16)


def _inproj1(x, c, mod, g, w, moe_wg, moe_wu, moe_wd):
    steps = N_TOK // TIN
    assert CAST_STEPS <= steps
    tok = pl.BlockSpec((TIN, HALF), lambda i: (i, 0))
    f = jax.ShapeDtypeStruct((N_TOK, HALF), F32)
    h = jax.ShapeDtypeStruct((N_TOK, HALF), BF16)
    cast_in, cast_out, cast_shape = _cast_specs(1, lambda i: i)
    return pl.pallas_call(
        _inproj1_kernel,
        grid=(steps,),
        in_specs=_token_specs(rows=TIN) + [
                  pl.BlockSpec((1, 1, 6 * D), _mod_row(TIN)),
                  pl.BlockSpec((1, D), lambda i: (0, 0)),
                  pl.BlockSpec((D, 5 * HALF), lambda i: (0, 0)),
                  pl.BlockSpec((HALF, D), lambda i: (0, 0))] + cast_in,
        out_specs=[tok] * 4 + [pl.BlockSpec((HALF, TIN), lambda i: (0, i))] + cast_out,
        out_shape=[f, f, h, h, jax.ShapeDtypeStruct((HALF, N_TOK), BF16)] + cast_shape,
        compiler_params=_params("arbitrary"),
        name="inproj1",
    )(x, c, mod, g, w[:, :5 * HALF], w[:, 5 * HALF:].T, moe_wg, moe_wu, moe_wd)


def _natten_kernel(q_ref, kp_ref, kc_ref, kn_ref, vp_ref, vc_ref, vn_ref, kx_ref, vx_ref, bias_ref, o_ref):
    pair = 2 * HEAD_DIM
    low = lax.broadcasted_iota(jnp.int32, (TM, pair), 1) < HEAD_DIM
    head_mask = [low.astype(F32).astype(BF16), jnp.logical_not(low).astype(F32).astype(BF16)]
    ones_rows = jnp.ones((HEAD_DIM, 4 * TM), BF16)

    def raw_scores(head):
        g, hh = divmod(head, 2)
        sl = slice(pair * g, pair * (g + 1))
        qm = q_ref[:, sl] * head_mask[hh]
        return [_nt_dot(qm, k_ref[:, sl]) for k_ref in (kp_ref, kc_ref, kn_ref, kx_ref)]

    def biased(head, s):
        s = [s[t] + bias_ref[0, head, :, t * TM:(t + 1) * TM] for t in range(3)] + [s[3]]
        m = jnp.max(jnp.maximum(jnp.maximum(s[0], s[1]), jnp.maximum(s[2], s[3])), axis=-1, keepdims=True)
        return s, m

    def attend(head, s, m):
        g, hh = divmod(head, 2)
        sl = slice(pair * g, pair * (g + 1))
        p = jnp.concatenate([jnp.exp2(st - m).astype(BF16) for st in s], axis=1)
        own = slice(HEAD_DIM * head, HEAD_DIM * (head + 1))
        values_t = jnp.concatenate([vt_ref[own, :] for vt_ref in (vp_ref, vc_ref, vn_ref, vx_ref)], axis=1)
        acc = _nt_dot(jnp.concatenate([values_t, ones_rows], axis=0), p)
        return acc[0:HEAD_DIM] * (1.0 / acc[HEAD_DIM:HEAD_DIM + 1])

    pending = biased(0, raw_scores(0))
    halves = []
    for head in range(N_HEADS):
        current = pending
        if head + 1 < N_HEADS:
            pending = biased(head + 1, raw_scores(head + 1))
        halves.append(attend(head, *current))
        if head % 2 == 1:
            sl = slice(pair * (head // 2), pair * (head // 2 + 1))
            o_ref[:, sl] = jnp.concatenate(halves, axis=0).T.astype(o_ref.dtype)
            halves = []


def _natten_bias(rpb):
    n_rows = SEQ // GRID_W
    n_dr, n_dc = 2 * WIN_H - 1, 2 * WIN_W - 1
    i = np.arange(ROWS_Q)
    j = np.arange(3 * ROWS_Q)
    col = np.arange(GRID_W)
    col_start = np.clip(col - WIN_W // 2, 0, GRID_W - WIN_W)
    col_ok = (col[None, :] >= col_start[:, None]) & (col[None, :] < col_start[:, None] + WIN_W)
    col_idx = col[None, :] - col[:, None] + (WIN_W - 1)
    onehot = ((col_idx[None] == np.arange(n_dc)[:, None, None]) & col_ok[None]).astype(np.float32)
    col_exp = jnp.dot(rpb.reshape(N_HEADS * n_dr, n_dc).astype(F32), onehot.reshape(n_dc, GRID_W * GRID_W),
                      precision=lax.Precision.HIGHEST).reshape(N_HEADS, n_dr, GRID_W, GRID_W)
    col_exp = jnp.where(col_ok[None, None], col_exp * LOG2E, NEG)
    masked = jnp.full((N_HEADS, GRID_W, GRID_W), NEG, F32)
    kinds = []
    for r0 in (0, ROWS_Q, n_rows - ROWS_Q):
        r = r0 + i
        kr = r0 - ROWS_Q + j
        r_start = np.clip(r - WIN_H // 2, 0, n_rows - WIN_H)
        row_ok = ((kr[None, :] >= r_start[:, None]) & (kr[None, :] < r_start[:, None] + WIN_H)
                  & (kr[None, :] >= 0) & (kr[None, :] < n_rows))
        row_idx = kr[None, :] - r[:, None] + (WIN_H - 1)
        rows = [jnp.concatenate([col_exp[:, row_idx[qi, kj]] if row_ok[qi, kj] else masked for kj in range(3 * ROWS_Q)],
                                axis=-1) for qi in range(ROWS_Q)]
        kinds.append(jnp.concatenate(rows, axis=1))
    return jnp.stack(kinds)


def _natten(q, k, vt, bias):
    def tile(b, i, off):
        return b * TILES_PER_SEQ + jnp.clip(i + off, 0, TILES_PER_SEQ - 1)

    def lat(off):
        return pl.BlockSpec((TM, HALF), lambda b, i: (tile(b, i, off), 0))

    def lat_t(off):
        return pl.BlockSpec((HALF, TM), lambda b, i: (0, tile(b, i, off)))

    ctx = pl.BlockSpec((TM, HALF), lambda b, i: (N_LAT_TILES + b, 0))
    ctx_t = pl.BlockSpec((HALF, TM), lambda b, i: (0, N_LAT_TILES + b))

    def kind(b, i):
        return (jnp.where(i == 0, 0, jnp.where(i == TILES_PER_SEQ - 1, 2, 1)), 0, 0, 0)

    return pl.pallas_call(
        _natten_kernel,
        grid=(BATCH, TILES_PER_SEQ),
        in_specs=[lat(0), lat(-1), lat(0), lat(1), lat_t(-1), lat_t(0), lat_t(1), ctx, ctx_t,
                  pl.BlockSpec((1, N_HEADS, TM, 3 * TM), kind)],
        out_specs=lat(0),
        out_shape=jax.ShapeDtypeStruct((N_LAT, HALF), BF16),
        compiler_params=_params("parallel", "arbitrary"),
        name="natten",
    )(q, k, k, k, vt, vt, vt, k, vt, bias)


def kernel(x, c, ctx, c_ctx, ada_w, ada_b, norm_mix_g, norm_ffn_g, w_out, ab_w_in, a_dw_w, a_dw_b, a_ln_g, a_ln_b,
           b_conv_w, b_conv_b, b_gate_w, b_gate_b, b_lambda, cd_w_in, c_conv_w, d_rpb, router_w, router_bias,
           moe_w_gate, moe_w_up, moe_w_down, final_g):
    x_lat = x.reshape(N_LAT, D)
    x_ctx = ctx.reshape(BATCH * CTX, D)
    cond = jnp.concatenate([c, c_ctx[None], jnp.zeros((8 - BATCH - 1, D), F32)], axis=0)
    mod = _modulation(cond, ada_w, ada_b)
    mod0 = mod[0].reshape(8, 1, 6 * D)
    mod1 = mod[1].reshape(8, 1, 6 * D)

    wo = w_out.astype(BF16)
    lat_tiles = N_LAT // TMOE
    ctx_tiles = BATCH * CTX // TMOE
    rw_t = router_w.T
    rb = router_bias.reshape(N_EXPERTS, 1)
    fg = final_g.reshape(1, D)

    col_scale = jnp.concatenate([jnp.full((2 * HALF,), 0.5, F32), jnp.ones((HALF,), F32), jnp.full((HALF,), 0.5, F32)])
    w_in0 = (ab_w_in[0] * col_scale).astype(BF16)
    gg, a_out, v, wg0, wu0, wd0 = _front0(x_lat, x_ctx, mod0, norm_mix_g[0].reshape(1, D), w_in0,
                                          a_dw_w[0], a_dw_b[0], a_ln_g[0], a_ln_b[0], b_conv_w[0], b_conv_b[0],
                                          moe_w_gate, moe_w_up, moe_w_down)
    gw = b_gate_w[0]
    gate_w = (0.5 * jnp.concatenate([gw[:, 0], gw[:, 1]], axis=-1)).astype(BF16)
    yf, yb = _scan(v, gate_w, 0.5 * b_gate_b[0], b_lambda[0].reshape(2, 1, HALF))
    h1, n2, route0, cnt0, tok0 = _outproj0(a_out, yf, yb, gg, x_lat, x_ctx, mod0, wo[0],
                                           norm_ffn_g[0].reshape(1, D), rw_t, rb)
    moe0 = functools.partial(_moe, _moe_chunks(cnt0), n2, route0, tok0, wg0, wu0, wd0, h1, mod0, fg,
                             final_norm=False)
    h2_lat = moe0(first_tile=0, n_tiles=lat_tiles, subtiles=2, name="moe_lat")
    h2_ctx = moe0(first_tile=lat_tiles, n_tiles=ctx_tiles, subtiles=1, name="moe_ctx")

    cx, bg, q, k, vv, wg1, wu1, wd1 = _inproj1(h2_lat, h2_ctx, mod1, norm_mix_g[1].reshape(1, D),
                                               cd_w_in[0].astype(BF16), moe_w_gate, moe_w_up, moe_w_down)
    att = _natten(q, k, vv, _natten_bias(d_rpb[0]))
    h3, n4, route1, cnt1, tok1 = _outproj1(cx, bg, att, c_conv_w[0], h2_lat, mod1, wo[1],
                                           norm_ffn_g[1].reshape(1, D), rw_t, rb)
    out = _moe(_moe_chunks(cnt1), n4, route1, tok1, wg1, wu1, wd1, h3, mod1, fg,
               first_tile=0, n_tiles=lat_tiles, subtiles=2, final_norm=True, name="moe_final")
    return out.reshape(BATCH, SEQ, D)
```

```python
import functools
import math

import jax
import jax.numpy as jnp
import numpy as np
from jax import lax
from jax.experimental import pallas as pl
from jax.experimental.pallas import tpu as pltpu

F32 = jnp.float32
BF16 = jnp.bfloat16

D = 1024
BATCH = 2
SEQ = 8192
CTX = 256
GRID_W = 64
N_LAT = BATCH * SEQ
N_TOK = N_LAT + BATCH * CTX
HALF = 512
CONV_A = 31
CONV_B = 4
CONV_C = 3
LRU_BLOCK = 128
N_LRU_BLOCKS = HALF // LRU_BLOCK
LRU_C = 8.0
HEAD_DIM = 64
N_HEADS = HALF // HEAD_DIM
WIN_H = 8
WIN_W = 16
N_EXPERTS = 16
N_GROUPS = 4
PER_GROUP = N_EXPERTS // N_GROUPS
D_FF = 512
EPS = 1e-6
NEG = -1e30
LOG2E = math.log2(math.e)

TM = 256
TILES_PER_SEQ = SEQ // TM
N_LAT_TILES = N_LAT // TM
N_TILES = N_TOK // TM
TIN = 512
TMOE = 512
CAP = 128
ROUTE_ROWS = 24
ROUTE_LANES = 128
ROUTE_GROUP = 16
ROUTE_RANK = 17
HALO_A = 16
HALO_S = 8
ROWS_Q = TM // GRID_W
VMEM_LIMIT = 48 * 1024 * 1024
VMEM_LIMIT_MOE = 56 * 1024 * 1024


def _params(*sem, vmem=VMEM_LIMIT):
    return pltpu.CompilerParams(dimension_semantics=sem, vmem_limit_bytes=vmem)


def _sigmoid(x):
    return 0.5 * jnp.tanh(0.5 * x) + 0.5


def _silu(x):
    return x * _sigmoid(x)


GELU_C = 0.7978845608028654
GELU_K = 0.044715


def _gelu_tanh_of_half(xh):
    return xh * (1.0 + jnp.tanh(xh * (2.0 * GELU_C + (8.0 * GELU_C * GELU_K) * (xh * xh))))


def _rms_mod(x, g, scale, shift):
    return x * lax.rsqrt(jnp.mean(x * x, axis=-1, keepdims=True) + EPS) * (g * (1.0 + scale)) + shift


def _nt_dot(a, b):
    return lax.dot_general(a, b, (((1,), (1,)), ((), ())), preferred_element_type=F32)


def _held(hold):
    return (lambda i: i) if hold is None else (lambda i: jnp.minimum(i, hold))


def _mod_row(tile_rows, hold=None):
    per_seq = SEQ // tile_rows
    tile = _held(hold)
    return lambda i: (jnp.minimum(tile(i) // per_seq, BATCH), 0, 0)


def _mod_kernel(c_ref, w_ref, b_ref, o_ref):
    c = c_ref[...]
    s = _silu(c).astype(BF16)
    o_ref[0] = jnp.dot(s, w_ref[0].astype(BF16), preferred_element_type=F32) + b_ref[0]


def _modulation(cond, ada_w, ada_b):
    depth = ada_w.shape[0]
    nb = 1536
    return pl.pallas_call(
        _mod_kernel,
        grid=(depth, 6 * D // nb),
        in_specs=[pl.BlockSpec((8, D), lambda l, j: (0, 0)),
                  pl.BlockSpec((1, D, nb), lambda l, j: (l, 0, j)),
                  pl.BlockSpec((1, 1, nb), lambda l, j: (l, 0, j))],
        out_specs=pl.BlockSpec((1, 8, nb), lambda l, j: (l, 0, j)),
        out_shape=jax.ShapeDtypeStruct((depth, 8, 6 * D), F32),
        compiler_params=_params("parallel", "parallel"),
        name="modulation",
    )(cond, ada_w, ada_b.reshape(depth, 1, 6 * D))


def _token_specs(hold=None, rows=TM):
    tile = _held(hold)
    lat_tiles = N_LAT // rows
    lat = pl.BlockSpec((rows, D), lambda i: (jnp.minimum(tile(i), lat_tiles - 1), 0))
    ctx = pl.BlockSpec((rows, D), lambda i: (jnp.maximum(tile(i) - lat_tiles, 0), 0))
    return [lat, ctx]


def _token_tile(lat_ref, ctx_ref, tile=None, rows=TM):
    tile = pl.program_id(0) if tile is None else tile
    return jnp.where(tile < N_LAT // rows, lat_ref[...], ctx_ref[...])


def _halo_specs(halo, hold=None):
    per_tile = TM // halo
    last = N_TOK // halo - 1
    tile = _held(hold)
    prev = pl.BlockSpec((halo, HALF), lambda i: (jnp.maximum(tile(i) * per_tile - 1, 0), 0))
    nxt = pl.BlockSpec((halo, HALF), lambda i: (jnp.minimum((tile(i) + 1) * per_tile, last), 0))
    return prev, nxt


def _seq_edges(i):
    is_ctx = i >= N_LAT_TILES
    first = jnp.logical_or(is_ctx, i % TILES_PER_SEQ == 0)
    last = jnp.logical_or(is_ctx, i % TILES_PER_SEQ == TILES_PER_SEQ - 1)
    return first, last


def _fill_padded(buf_ref, prev_ref, cur_ref, next_ref, halo, tile=None):
    first, last = _seq_edges(pl.program_id(0) if tile is None else tile)
    buf_ref[0:halo, :] = jnp.where(first, 0.0, prev_ref[...])
    buf_ref[halo:halo + TM, :] = cur_ref[...]
    buf_ref[halo + TM:halo + TM + halo, :] = jnp.where(last, 0.0, next_ref[...])


def _tap_weight(w_ref, k, rows):
    return jnp.concatenate([w_ref[k]] * (rows // 8), axis=0)


def _sublane_replicated(w):
    return jnp.broadcast_to(w[:, None, :], (w.shape[0], 8, w.shape[1]))


def _depthwise(buf_ref, w_ref, taps, first_off, rows, row0):
    acc = None
    for k in range(taps):
        term = _tap_weight(w_ref, k, rows) * buf_ref[row0 + first_off + k:row0 + first_off + k + rows, :]
        acc = term if acc is None else acc + term
    return acc


def _seq_major(i, hold=None):
    i = _held(hold)(i)
    is_lat = i < N_LAT_TILES
    return (jnp.where(is_lat, i // TILES_PER_SEQ, i - N_LAT_TILES), jnp.where(is_lat, 1 + i % TILES_PER_SEQ, 0), 0)


CONV_ROWS = 32
PAD_A_ROWS = TM + 2 * HALO_A
PAD_B_ROWS = TM + 2 * HALO_S
SHIFT_ROWS = PAD_A_ROWS - 8


def _front0_kernel(x_ref, c_ref, xp_ref, xn_ref, mod_ref, g_ref, w_ref,
                   dww_ref, dwb_ref, lng_ref, lnb_ref, cw_ref, cb_ref, wg_ref, wu_ref, wd_ref,
                   gg_ref, a_ref, v_ref, wg_out, wu_out, wd_out, bufa_ref, bufb_ref, shift_ref):
    _cast_weights((wg_ref, wu_ref, wd_ref), (wg_out, wu_out, wd_out))
    step = pl.program_id(0)

    @pl.when(step == 0)
    def _():
        bufa_ref[...] = jnp.zeros_like(bufa_ref)
        bufb_ref[...] = jnp.zeros_like(bufb_ref)

    tile = jnp.minimum(step, N_TILES - 1)
    first, last = _seq_edges(tile)

    def run(new, old):
        mod = mod_ref[0]
        rows_in = jnp.concatenate([xp_ref[...], _token_tile(x_ref, c_ref, tile), xn_ref[...]], axis=0)
        n = _rms_mod(rows_in, g_ref[...], mod[:, D:2 * D], mod[:, 0:D]).astype(BF16)

        def proj(c, rows=slice(None)):
            return jnp.dot(n[rows], w_ref[:, c * HALF:(c + 1) * HALF], preferred_element_type=F32)

        def conv_chunk(r):
            row0 = r * CONV_ROWS
            u = None
            for k in range(CONV_A):
                off = HALO_A - CONV_A // 2 + k
                base = row0 + off - off % 8
                term = _tap_weight(dww_ref, k, CONV_ROWS) * shift_ref[off % 8, base:base + CONV_ROWS, :]
                u = term if u is None else u + term
            u = u + dwb_ref[...]
            mu = jnp.mean(u, axis=-1, keepdims=True)
            uc = u - mu
            var = jnp.mean(uc * uc, axis=-1, keepdims=True)
            y = uc * lax.rsqrt(var + EPS) * lng_ref[...] + lnb_ref[...]
            a_ref[row0:row0 + CONV_ROWS, :] = _silu(y).astype(a_ref.dtype)
            acc = None
            for k in range(CONV_B):
                lo = row0 + HALO_S - 2 + k
                term = _tap_weight(cw_ref, k, CONV_ROWS) * bufb_ref[old, lo:lo + CONV_ROWS, :]
                acc = term if acc is None else acc + term
            v_ref[0, row0:row0 + CONV_ROWS, :] = acc + cb_ref[...]

        for s in range(8):
            shift_ref[s] = bufa_ref[old, s:s + SHIFT_ROWS, :]
        def conv_quarter(q):
            per_quarter = TM // CONV_ROWS // 4
            for r in range(q * per_quarter, (q + 1) * per_quarter):
                conv_chunk(r)

        value, gate = proj(0), proj(1)
        conv_quarter(0)
        ua = value * (jnp.tanh(gate) + 1.0)
        bufa_ref[new, 0:HALO_A, :] = jnp.where(first, 0.0, ua[0:HALO_A])
        bufa_ref[new, HALO_A:HALO_A + TM, :] = ua[HALO_A:HALO_A + TM]
        bufa_ref[new, HALO_A + TM:PAD_A_ROWS, :] = jnp.where(last, 0.0, ua[HALO_A + TM:PAD_A_ROWS])
        recur = proj(2)
        conv_quarter(1)
        lo = HALO_A - HALO_S
        bufb_ref[new, 0:HALO_S, :] = jnp.where(first, 0.0, recur[lo:HALO_A])
        bufb_ref[new, HALO_S:HALO_S + TM, :] = recur[HALO_A:HALO_A + TM]
        bufb_ref[new, HALO_S + TM:PAD_B_ROWS, :] = jnp.where(last, 0.0, recur[HALO_A + TM:HALO_A + TM + HALO_S])
        gelu_in = proj(3, slice(HALO_A, HALO_A + TM))
        conv_quarter(2)
        gg_ref[...] = _gelu_tanh_of_half(gelu_in).astype(gg_ref.dtype)
        conv_quarter(3)

    @pl.when(step % 2 == 0)
    def _():
        run(0, 1)

    @pl.when(step % 2 == 1)
    def _():
        run(1, 0)


def _front0(x, c, mod, g, w, dw_w, dw_b, ln_g, ln_b, conv_w, conv_b, moe_wg, moe_wu, moe_wd):
    assert CAST_STEPS <= N_TILES + 1
    cast_in, cast_out, cast_shape = _cast_specs(0, lambda i: i)
    hold = N_TILES - 1
    tile = _held(hold)
    lagged = lambda i: jnp.maximum(i - 1, 0)
    per_tile = TM // HALO_A
    last_halo = N_LAT // HALO_A - 1
    prev = pl.BlockSpec((HALO_A, D), lambda i: (jnp.clip(tile(i) * per_tile - 1, 0, last_halo), 0))
    nxt = pl.BlockSpec((HALO_A, D), lambda i: (jnp.clip((tile(i) + 1) * per_tile, 0, last_halo), 0))
    vec = pl.BlockSpec((1, HALF), lambda i: (0, 0))
    return pl.pallas_call(
        _front0_kernel,
        grid=(N_TILES + 1,),
        in_specs=_token_specs(hold) + [prev, nxt,
                  pl.BlockSpec((1, 1, 6 * D), _mod_row(TM, hold)),
                  pl.BlockSpec((1, D), lambda i: (0, 0)),
                  pl.BlockSpec((D, 4 * HALF), lambda i: (0, 0)),
                  pl.BlockSpec((CONV_A, 8, HALF), lambda i: (0, 0, 0)), vec, vec, vec,
                  pl.BlockSpec((CONV_B, 8, HALF), lambda i: (0, 0, 0)), vec] + cast_in,
        out_specs=[pl.BlockSpec((TM, HALF), lambda i: (tile(i), 0)),
                   pl.BlockSpec((TM, HALF), lambda i: (lagged(i), 0)),
                   pl.BlockSpec((1, TM, HALF), lambda i: _seq_major(lagged(i)))] + cast_out,
        out_shape=[jax.ShapeDtypeStruct((N_TOK, HALF), BF16), jax.ShapeDtypeStruct((N_TOK, HALF), BF16),
                   jax.ShapeDtypeStruct((BATCH, SEQ + CTX, HALF), F32)] + cast_shape,
        scratch_shapes=[pltpu.VMEM((2, PAD_A_ROWS, HALF), F32), pltpu.VMEM((2, PAD_B_ROWS, HALF), F32),
                        pltpu.VMEM((8, SHIFT_ROWS, HALF), F32)],
        compiler_params=_params("arbitrary"),
        name="front0",
    )(x, c, x, x, mod, g, w, _sublane_replicated(dw_w), dw_b.reshape(1, HALF), ln_g.reshape(1, HALF),
      ln_b.reshape(1, HALF), _sublane_replicated(conv_w), conv_b.reshape(1, HALF), moe_wg, moe_wu, moe_wd)


SCAN_UNROLL = 8


CAST_STEPS = 2 * N_EXPERTS


def _cast_specs(layer, step_of):
    def half(*idx):
        s = jnp.minimum(step_of(*idx), CAST_STEPS - 1)
        return s // 2, s % 2

    def src(*idx):
        e, h = half(*idx)
        return (layer, e, h, 0)

    def dst(*idx):
        e, h = half(*idx)
        return (e, h, 0)

    in_specs = [pl.BlockSpec((1, 1, D // 2, D_FF), src), pl.BlockSpec((1, 1, D // 2, D_FF), src),
                pl.BlockSpec((1, 1, D_FF // 2, D), src)]
    out_specs = [pl.BlockSpec((1, D // 2, D_FF), dst), pl.BlockSpec((1, D // 2, D_FF), dst),
                 pl.BlockSpec((1, D_FF // 2, D), dst)]
    out_shape = [jax.ShapeDtypeStruct((N_EXPERTS, D, D_FF), BF16), jax.ShapeDtypeStruct((N_EXPERTS, D, D_FF), BF16),
                 jax.ShapeDtypeStruct((N_EXPERTS, D_FF, D), BF16)]
    return in_specs, out_specs, out_shape


def _cast_weights(srcs, dsts):
    for src, dst in zip(srcs, dsts):
        dst[0] = src[0, 0].astype(BF16)


def _scan_kernel(vf_ref, vb_ref, w_ref, gb_ref, lam_ref, yf_ref, yb_ref, h_ref, a_ref, b_ref):
    @pl.when(pl.program_id(0) == 0)
    def _():
        h_ref[...] = jnp.zeros_like(h_ref)

    for d, v_ref in enumerate((vf_ref, vb_ref)):
        v = v_ref[...].reshape(BATCH * TM, HALF)
        vb = v.astype(BF16)
        neg = -lam_ref[d]
        softplus = jnp.maximum(neg, 0.0) + jnp.log(1.0 + jnp.exp(-jnp.abs(neg)))
        rate = (-0.5 * LRU_C * LOG2E) * softplus
        for n in range(N_LRU_BLOCKS):
            sl = slice(n * LRU_BLOCK, (n + 1) * LRU_BLOCK)
            g = jnp.dot(vb[:, sl], w_ref[d, n], preferred_element_type=F32)
            tr = jnp.tanh(g[:, 0:LRU_BLOCK] + gb_ref[d, 0:1, sl])
            ti = jnp.tanh(g[:, LRU_BLOCK:2 * LRU_BLOCK] + gb_ref[d, 1:2, sl])
            a = jnp.exp2(rate[:, sl] * tr + rate[:, sl])
            a_ref[d, :, sl] = a
            b_ref[d, :, sl] = jnp.sqrt(1.0 - a * a) * ((0.5 * ti + 0.5) * v[:, sl])

    def body(s, hs):
        hs = list(hs)
        for u in range(SCAN_UNROLL):
            t = s * SCAN_UNROLL + u
            for d, y_ref in enumerate((yf_ref, yb_ref)):
                row = t if d == 0 else TM - 1 - t
                for bt in range(BATCH):
                    c = d * BATCH + bt
                    src = bt * TM + row
                    hs[c] = a_ref[d, pl.ds(src, 1), :] * hs[c] + b_ref[d, pl.ds(src, 1), :]
                    y_ref[bt, pl.ds(row, 1), :] = hs[c]
        return tuple(hs)

    init = tuple(h_ref[c:c + 1, :] for c in range(2 * BATCH))
    final = init
    for s in range(TM // SCAN_UNROLL):
        final = body(s, final)
    for c in range(2 * BATCH):
        h_ref[c:c + 1, :] = final[c]


def _scan(v, gate_w, gate_b, lam):
    fwd = pl.BlockSpec((BATCH, TM, HALF), lambda j: (0, j, 0))
    bwd = pl.BlockSpec((BATCH, TM, HALF), lambda j: (0, jnp.where(j == 0, 0, TILES_PER_SEQ + 1 - j), 0))
    shp = jax.ShapeDtypeStruct((BATCH, SEQ + CTX, HALF), F32)
    return pl.pallas_call(
        _scan_kernel,
        grid=(TILES_PER_SEQ + 1,),
        in_specs=[fwd, bwd,
                  pl.BlockSpec((2, N_LRU_BLOCKS, LRU_BLOCK, 2 * LRU_BLOCK), lambda j: (0, 0, 0, 0)),
                  pl.BlockSpec((2, 2, HALF), lambda j: (0, 0, 0)),
                  pl.BlockSpec((2, 1, HALF), lambda j: (0, 0, 0))],
        out_specs=[fwd, bwd],
        out_shape=[shp, shp],
        scratch_shapes=[pltpu.VMEM((8, HALF), F32), pltpu.VMEM((2, BATCH * TM, HALF), F32),
                        pltpu.VMEM((2, BATCH * TM, HALF), F32)],
        compiler_params=_params("arbitrary"),
        name="lru_scan",
    )(v, v, gate_w, gate_b, lam)


def _route(score, sel):
    rows = [sel[e:e + 1, :] for e in range(N_EXPERTS)]
    gbest = None
    gidx = None
    for g in range(N_GROUPS):
        top2 = None
        for p in range(PER_GROUP):
            for q in range(p + 1, PER_GROUP):
                s = rows[g * PER_GROUP + p] + rows[g * PER_GROUP + q]
                top2 = s if top2 is None else jnp.maximum(top2, s)
        if g == 0:
            gbest = top2
            gidx = jnp.zeros(top2.shape, jnp.int32)
        else:
            better = top2 > gbest
            gidx = jnp.where(better, g, gidx)
            gbest = jnp.where(better, top2, gbest)
    eint = lax.broadcasted_iota(jnp.int32, sel.shape, 0)
    eidx = eint.astype(F32)
    masked = jnp.where(jnp.right_shift(eint, 2) == gidx, sel, -jnp.inf)
    v1 = jnp.max(masked, axis=0, keepdims=True)
    i1 = jnp.min(jnp.where(masked == v1, eidx, float(N_EXPERTS)), axis=0, keepdims=True)
    masked2 = jnp.where(eidx == i1, -jnp.inf, masked)
    v2 = jnp.max(masked2, axis=0, keepdims=True)
    i2 = jnp.min(jnp.where(masked2 == v2, eidx, float(N_EXPERTS)), axis=0, keepdims=True)
    s1 = jnp.sum(jnp.where(eidx == i1, score, 0.0), axis=0, keepdims=True)
    s2 = jnp.sum(jnp.where(eidx == i2, score, 0.0), axis=0, keepdims=True)
    inv = 1.0 / (s1 + s2)
    return jnp.where(eidx == i1, s1 * inv, 0.0) + jnp.where(eidx == i2, s2 * inv, 0.0), gidx


def _group_ranks(gidx, carry_ref, tile):
    onehot = lax.broadcasted_iota(jnp.int32, (8, TM), 0) == gidx
    oh = jnp.where(onehot, 1.0, 0.0)
    before = lax.broadcasted_iota(jnp.int32, (TM, TM), 0) < lax.broadcasted_iota(jnp.int32, (TM, TM), 1)
    prefix = jnp.dot(oh.astype(BF16), jnp.where(before, 1.0, 0.0).astype(BF16), preferred_element_type=F32)
    carry = jnp.where(tile % (TMOE // TM) == 0, 0.0, carry_ref[...])
    rank = jnp.sum(jnp.where(onehot, prefix + carry[:, 0:1], 0.0), axis=0, keepdims=True)
    carry = carry + jnp.sum(oh, axis=1, keepdims=True)
    carry_ref[...] = carry
    return rank, carry


def _split_bf16(v):
    hi = v.astype(BF16)
    return hi, (v - hi.astype(F32)).astype(BF16)


def _outproj_step(m1, m2, x, mod_ref, wo_ref, g_ref, rw_ref, rb_ref, h_ref, n_ref, route_ref, cnt_ref, tok_ref, carry_ref,
                  nprev_ref):
    step = pl.program_id(0)
    n_hi, n_lo = _split_bf16(nprev_ref[...])
    w_hi, w_lo = _split_bf16(rw_ref[...])

    mod = mod_ref[0]
    mix = (jnp.dot(m1, wo_ref[0:HALF, :], preferred_element_type=F32)
           + jnp.dot(m2, wo_ref[HALF:2 * HALF, :], preferred_element_type=F32))

    logits = _nt_dot(w_hi, n_hi) + _nt_dot(w_hi, n_lo) + _nt_dot(w_lo, n_hi)

    h = x + mod[:, 2 * D:3 * D] * mix
    h_ref[...] = h
    n = _rms_mod(h, g_ref[...], mod[:, 4 * D:5 * D], mod[:, 3 * D:4 * D])
    n_ref[...] = n.astype(n_ref.dtype)
    nprev_ref[...] = n

    score = _sigmoid(logits)
    comb, gidx = _route(score, score + rb_ref[...])
    rank, counts = _group_ranks(gidx, carry_ref, step - 1)
    record = jnp.concatenate([comb, gidx.astype(F32), rank, jnp.zeros((ROUTE_LANES - ROUTE_RANK - 1, TM), F32)], axis=0)
    route_ref[...] = record[0:ROUTE_ROWS]
    tok_ref[...] = record.T
    cnt_ref[0] = counts


def _init_pipeline(carry_ref, nprev_ref):
    @pl.when(pl.program_id(0) == 0)
    def _():
        nprev_ref[...] = jnp.zeros_like(nprev_ref)
        carry_ref[...] = jnp.zeros_like(carry_ref)


def _outproj0_kernel(a_ref, yf_ref, yb_ref, gg_ref, x_ref, c_ref, mod_ref, wo_ref, g_ref, rw_ref, rb_ref,
                     h_ref, n_ref, route_ref, cnt_ref, tok_ref, carry_ref, nprev_ref):
    _init_pipeline(carry_ref, nprev_ref)
    tile = jnp.minimum(pl.program_id(0), N_TILES - 1)
    m2 = ((yf_ref[0] + yb_ref[0]) * gg_ref[...]).astype(BF16)
    _outproj_step(a_ref[...], m2, _token_tile(x_ref, c_ref, tile), mod_ref, wo_ref, g_ref, rw_ref, rb_ref,
                  h_ref, n_ref, route_ref, cnt_ref, tok_ref, carry_ref, nprev_ref)


def _outproj1_kernel(cx_ref, cxp_ref, cxn_ref, bg_ref, att_ref, cw_ref, x_ref, mod_ref, wo_ref, g_ref, rw_ref, rb_ref,
                     h_ref, n_ref, route_ref, cnt_ref, tok_ref, buf_ref, carry_ref, nprev_ref):
    _init_pipeline(carry_ref, nprev_ref)
    tile = jnp.minimum(pl.program_id(0), N_LAT_TILES - 1)
    _fill_padded(buf_ref, cxp_ref, cx_ref, cxn_ref, HALO_S, tile)
    conv = _depthwise(buf_ref, cw_ref, CONV_C, HALO_S - 1, TM, 0)
    m1 = (bg_ref[...] * conv).astype(BF16)
    _outproj_step(m1, att_ref[...], x_ref[...], mod_ref, wo_ref, g_ref, rw_ref, rb_ref,
                  h_ref, n_ref, route_ref, cnt_ref, tok_ref, carry_ref, nprev_ref)


def _outproj_common(n_tiles):
    hold = n_tiles - 1
    tile = _held(hold)
    routed = lambda i: jnp.maximum(i - 1, 0)
    in_specs = [pl.BlockSpec((1, 1, 6 * D), _mod_row(TM, hold)),
                pl.BlockSpec((D, D), lambda i: (0, 0)),
                pl.BlockSpec((1, D), lambda i: (0, 0)),
                pl.BlockSpec((N_EXPERTS, D), lambda i: (0, 0)),
                pl.BlockSpec((N_EXPERTS, 1), lambda i: (0, 0))]
    out_specs = [pl.BlockSpec((TM, D), lambda i: (tile(i), 0)),
                 pl.BlockSpec((TM, D), lambda i: (tile(i), 0)),
                 pl.BlockSpec((ROUTE_ROWS, TM), lambda i: (0, routed(i))),
                 pl.BlockSpec((1, 8, 128), lambda i: (routed(i), 0, 0)),
                 pl.BlockSpec((TM, ROUTE_LANES), lambda i: (routed(i), 0))]
    rows = n_tiles * TM
    out_shape = [jax.ShapeDtypeStruct((rows, D), F32), jax.ShapeDtypeStruct((rows, D), BF16),
                 jax.ShapeDtypeStruct((ROUTE_ROWS, rows), F32), jax.ShapeDtypeStruct((n_tiles, 8, 128), F32),
                 jax.ShapeDtypeStruct((rows, ROUTE_LANES), F32)]
    scratch = [pltpu.VMEM((8, 128), F32), pltpu.VMEM((TM, D), F32)]
    return in_specs, out_specs, out_shape, scratch


def _outproj0(a, yf, yb, gg, x, c, mod, wo, g, rw_t, rb):
    hold = N_TILES - 1
    tok = pl.BlockSpec((TM, HALF), lambda i: (_held(hold)(i), 0))
    scan_tok = pl.BlockSpec((1, TM, HALF), functools.partial(_seq_major, hold=hold))
    common_in, out_specs, out_shape, scratch = _outproj_common(N_TILES)
    return pl.pallas_call(
        _outproj0_kernel,
        grid=(N_TILES + 1,),
        in_specs=[tok, scan_tok, scan_tok, tok] + _token_specs(hold) + common_in,
        out_specs=out_specs,
        out_shape=out_shape,
        scratch_shapes=scratch,
        compiler_params=_params("arbitrary"),
        name="outproj0",
    )(a, yf, yb, gg, x, c, mod, wo, g, rw_t, rb)


def _outproj1(cx, bg, att, conv_w, x, mod, wo, g, rw_t, rb):
    hold = N_LAT_TILES - 1
    tok = pl.BlockSpec((TM, HALF), lambda i: (_held(hold)(i), 0))
    ps, ns = _halo_specs(HALO_S, hold)
    common_in, out_specs, out_shape, scratch = _outproj_common(N_LAT_TILES)
    return pl.pallas_call(
        _outproj1_kernel,
        grid=(N_LAT_TILES + 1,),
        in_specs=[tok, ps, ns, tok, tok, pl.BlockSpec((CONV_C, 8, HALF), lambda i: (0, 0, 0)),
                  pl.BlockSpec((TM, D), lambda i: (_held(hold)(i), 0))] + common_in,
        out_specs=out_specs,
        out_shape=out_shape,
        scratch_shapes=[pltpu.VMEM((TM + 2 * HALO_S, HALF), F32)] + scratch,
        compiler_params=_params("arbitrary"),
        name="outproj1",
    )(cx, cx, cx, bg, att, _sublane_replicated(conv_w), x, mod, wo, g, rw_t, rb)


def _moe_kernel(cnt_ref, n_ref, rt_ref, r_ref, wg_ref, wu_ref, wd_ref, h_ref, mod_ref, fg_ref, o_ref,
                hid_ref, *, final_norm, first_tile, subtiles):
    i = pl.program_id(0)
    g = pl.program_id(1)

    @pl.when(g == 0)
    def _():
        o_ref[...] = jnp.zeros_like(o_ref)

    gf = g.astype(F32)
    slot_row = lax.broadcasted_iota(jnp.int32, (CAP, TMOE), 0).astype(F32)
    slot_col = lax.broadcasted_iota(jnp.int32, (TMOE, CAP), 1).astype(F32)

    counts = [cnt_ref[(first_tile + i * subtiles + s) * N_GROUPS + g] for s in range(subtiles)]
    starts = [sum(counts[:s], jnp.int32(0)) for s in range(subtiles)]
    total = starts[-1] + counts[-1]
    tile_rows = [slice(s * TMOE, (s + 1) * TMOE) for s in range(subtiles)]

    def run_chunk(k, sources):
        base = k * CAP
        sels, shifts = [], []
        xg = None
        for s in sources:
            shift = (starts[s] - base).astype(F32)
            in_group = rt_ref[ROUTE_GROUP:ROUTE_GROUP + 1, tile_rows[s]] == gf
            slot = jnp.where(in_group, rt_ref[ROUTE_RANK:ROUTE_RANK + 1, tile_rows[s]] + shift, -1.0)
            sel = slot_row == slot
            part = jnp.dot(jnp.where(sel, 1.0, 0.0).astype(BF16), n_ref[tile_rows[s], :], preferred_element_type=F32)
            xg = part if xg is None else xg + part
            sels.append(sel)
            shifts.append(shift)
        xg = xg.astype(BF16)
        for j in range(PER_GROUP):
            cw = None
            for s, sel in zip(sources, sels):
                comb_row = rt_ref[pl.ds(g * PER_GROUP + j, 1), tile_rows[s]]
                part = jnp.sum(jnp.where(sel, comb_row, 0.0), axis=1, keepdims=True)
                cw = part if cw is None else cw + part
            hid = (_silu(jnp.dot(xg, wg_ref[j], preferred_element_type=F32))
                   * jnp.dot(xg, wu_ref[j], preferred_element_type=F32) * cw)
            hid_ref[:, j * D_FF:(j + 1) * D_FF] = hid.astype(BF16)
        y = jnp.dot(hid_ref[...], wd_ref[...].reshape(PER_GROUP * D_FF, D),
                    preferred_element_type=F32).astype(BF16)
        for s, shift in zip(sources, shifts):
            in_group = r_ref[tile_rows[s], ROUTE_GROUP:ROUTE_GROUP + 1] == gf
            slot = r_ref[tile_rows[s], ROUTE_RANK:ROUTE_RANK + 1] + shift
            back = jnp.where(jnp.logical_and(in_group, slot_col == slot), 1.0, 0.0).astype(BF16)
            o_ref[tile_rows[s], :] += jnp.dot(back, y, preferred_element_type=F32)

    def loop(lo, hi, sources):
        def body(k, carry):
            run_chunk(k, sources)
            return carry
        lax.fori_loop(lo, hi, body, 0)

    for s in range(subtiles):
        end = starts[s] + counts[s]
        first_inside = (starts[s] + (CAP - 1)) // CAP
        if s + 1 < subtiles:
            loop(first_inside, end // CAP, [s])

            @pl.when(end % CAP != 0)
            def _(s=s, end=end):
                run_chunk(end // CAP, [s, s + 1])
        else:
            loop(first_inside, (end + (CAP - 1)) // CAP, [s])

    @pl.when(g == N_GROUPS - 1)
    def _():
        out = h_ref[...] + mod_ref[0][:, 5 * D:6 * D] * o_ref[...]
        if final_norm:
            out = out * lax.rsqrt(jnp.mean(out * out, axis=-1, keepdims=True) + EPS) * fg_ref[...]
        o_ref[...] = out


def _moe_chunks(counts):
    sub = TMOE // TM
    return counts[sub - 1::sub, :N_GROUPS, 0].astype(jnp.int32).reshape(-1)


def _moe(n_chunks, n, route_t, route, wg, wu, wd, h, mod, final_g, *, first_tile, n_tiles, subtiles, final_norm,
         name):
    assert subtiles in (1, 2) and first_tile % subtiles == 0 and n_tiles % subtiles == 0
    step = subtiles * TMOE
    first = first_tile // subtiles
    mod_row = lambda i, g, nch: (jnp.minimum(((first + i) * step) // SEQ, BATCH), 0, 0)
    tok = pl.BlockSpec((step, D), lambda i, g, nch: (first + i, 0))
    grid_spec = pltpu.PrefetchScalarGridSpec(
        num_scalar_prefetch=1,
        grid=(n_tiles // subtiles, N_GROUPS),
        in_specs=[tok,
                  pl.BlockSpec((ROUTE_ROWS, step), lambda i, g, nch: (0, first + i)),
                  pl.BlockSpec((step, ROUTE_LANES), lambda i, g, nch: (first + i, 0)),
                  pl.BlockSpec((PER_GROUP, D, D_FF), lambda i, g, nch: (g, 0, 0)),
                  pl.BlockSpec((PER_GROUP, D, D_FF), lambda i, g, nch: (g, 0, 0)),
                  pl.BlockSpec((PER_GROUP, D_FF, D), lambda i, g, nch: (g, 0, 0)),
                  tok,
                  pl.BlockSpec((1, 1, 6 * D), mod_row),
                  pl.BlockSpec((1, D), lambda i, g, nch: (0, 0))],
        out_specs=pl.BlockSpec((step, D), lambda i, g, nch: (i, 0)),
        scratch_shapes=[pltpu.VMEM((CAP, PER_GROUP * D_FF), BF16)])
    return pl.pallas_call(
        functools.partial(_moe_kernel, final_norm=final_norm, first_tile=first_tile, subtiles=subtiles),
        grid_spec=grid_spec,
        out_shape=jax.ShapeDtypeStruct((n_tiles * TMOE, D), F32),
        compiler_params=_params("parallel", "arbitrary", vmem=VMEM_LIMIT_MOE),
        name=name,
    )(n_chunks, n, route_t, route, wg, wu, wd, h, mod, final_g)


def _inproj1_kernel(x_ref, c_ref, mod_ref, g_ref, w_ref, wvt_ref, wg_ref, wu_ref, wd_ref,
                    cx_ref, bg_ref, q_ref, k_ref, vt_ref, wg_out, wu_out, wd_out):
    _cast_weights((wg_ref, wu_ref, wd_ref), (wg_out, wu_out, wd_out))
    mod = mod_ref[0]
    n = _rms_mod(_token_tile(x_ref, c_ref, rows=TIN), g_ref[...], mod[:, D:2 * D], mod[:, 0:D]).astype(BF16)

    def proj(c):
        return jnp.dot(n, w_ref[:, c * HALF:(c + 1) * HALF], preferred_element_type=F32)

    conv_in, conv_gate = proj(0), proj(2)
    out_gate = proj(1)
    cx_ref[...] = conv_gate * conv_in
    q = proj(3)
    bg_ref[...] = out_gate
    k = proj(4)
    q_ref[...] = (q * (HEAD_DIM ** -0.5 * LOG2E)).astype(BF16)
    vt = _nt_dot(wvt_ref[...], n)
    k_ref[...] = k.astype(BF16)
    vt_ref[...] = vt.astype(BF16)


def _inproj1(x, c, mod, g, w, moe_wg, moe_wu, moe_wd):
    steps = N_TOK // TIN
    assert CAST_STEPS <= steps
    tok = pl.BlockSpec((TIN, HALF), lambda i: (i, 0))
    f = jax.ShapeDtypeStruct((N_TOK, HALF), F32)
    h = jax.ShapeDtypeStruct((N_TOK, HALF), BF16)
    cast_in, cast_out, cast_shape = _cast_specs(1, lambda i: i)
    return pl.pallas_call(
        _inproj1_kernel,
        grid=(steps,),
        in_specs=_token_specs(rows=TIN) + [
                  pl.BlockSpec((1, 1, 6 * D), _mod_row(TIN)),
                  pl.BlockSpec((1, D), lambda i: (0, 0)),
                  pl.BlockSpec((D, 5 * HALF), lambda i: (0, 0)),
                  pl.BlockSpec((HALF, D), lambda i: (0, 0))] + cast_in,
        out_specs=[tok] * 4 + [pl.BlockSpec((HALF, TIN), lambda i: (0, i))] + cast_out,
        out_shape=[f, f, h, h, jax.ShapeDtypeStruct((HALF, N_TOK), BF16)] + cast_shape,
        compiler_params=_params("arbitrary"),
        name="inproj1",
    )(x, c, mod, g, w[:, :5 * HALF], w[:, 5 * HALF:].T, moe_wg, moe_wu, moe_wd)


def _natten_kernel(q_ref, kp_ref, kc_ref, kn_ref, vp_ref, vc_ref, vn_ref, kx_ref, vx_ref, bias_ref, o_ref):
    pair = 2 * HEAD_DIM
    low = lax.broadcasted_iota(jnp.int32, (TM, pair), 1) < HEAD_DIM
    head_mask = [low.astype(F32).astype(BF16), jnp.logical_not(low).astype(F32).astype(BF16)]
    ones_rows = jnp.ones((HEAD_DIM, 4 * TM), BF16)

    def raw_scores(head):
        g, hh = divmod(head, 2)
        sl = slice(pair * g, pair * (g + 1))
        qm = q_ref[:, sl] * head_mask[hh]
        return [_nt_dot(qm, k_ref[:, sl]) for k_ref in (kp_ref, kc_ref, kn_ref, kx_ref)]

    def biased(head, s):
        s = [s[t] + bias_ref[head, :, t * TM:(t + 1) * TM] for t in range(3)] + [s[3]]
        m = jnp.max(jnp.maximum(jnp.maximum(s[0], s[1]), jnp.maximum(s[2], s[3])), axis=-1, keepdims=True)
        return s, m

    def attend(head, s, m):
        g, hh = divmod(head, 2)
        sl = slice(pair * g, pair * (g + 1))
        p = jnp.concatenate([jnp.exp2(st - m).astype(BF16) for st in s], axis=1)
        own = slice(HEAD_DIM * head, HEAD_DIM * (head + 1))
        values_t = jnp.concatenate([vt_ref[own, :] for vt_ref in (vp_ref, vc_ref, vn_ref, vx_ref)], axis=1)
        acc = _nt_dot(jnp.concatenate([values_t, ones_rows], axis=0), p)
        return acc[0:HEAD_DIM] * (1.0 / acc[HEAD_DIM:HEAD_DIM + 1])

    pending = biased(0, raw_scores(0))
    halves = []
    for head in range(N_HEADS):
        current = pending
        if head + 1 < N_HEADS:
            pending = biased(head + 1, raw_scores(head + 1))
        halves.append(attend(head, *current))
        if head % 2 == 1:
            sl = slice(pair * (head // 2), pair * (head // 2 + 1))
            o_ref[:, sl] = jnp.concatenate(halves, axis=0).T.astype(o_ref.dtype)
            halves = []


def _natten_bias(rpb):
    n_rows = SEQ // GRID_W
    n_dr, n_dc = 2 * WIN_H - 1, 2 * WIN_W - 1
    i = np.arange(ROWS_Q)
    j = np.arange(3 * ROWS_Q)
    col = np.arange(GRID_W)
    col_start = np.clip(col - WIN_W // 2, 0, GRID_W - WIN_W)
    col_ok = (col[None, :] >= col_start[:, None]) & (col[None, :] < col_start[:, None] + WIN_W)
    col_idx = col[None, :] - col[:, None] + (WIN_W - 1)
    onehot = ((col_idx[None] == np.arange(n_dc)[:, None, None]) & col_ok[None]).astype(np.float32)
    col_exp = jnp.dot(rpb.reshape(N_HEADS * n_dr, n_dc).astype(F32), onehot.reshape(n_dc, GRID_W * GRID_W),
                      precision=lax.Precision.HIGHEST).reshape(N_HEADS, n_dr, GRID_W, GRID_W)
    col_exp = jnp.where(col_ok[None, None], col_exp * LOG2E, NEG)
    masked = jnp.full((N_HEADS, GRID_W, GRID_W), NEG, F32)
    rows = []
    for r0 in (0, ROWS_Q, n_rows - ROWS_Q):
        r = r0 + i
        kr = r0 - ROWS_Q + j
        r_start = np.clip(r - WIN_H // 2, 0, n_rows - WIN_H)
        row_ok = ((kr[None, :] >= r_start[:, None]) & (kr[None, :] < r_start[:, None] + WIN_H)
                  & (kr[None, :] >= 0) & (kr[None, :] < n_rows))
        row_idx = kr[None, :] - r[:, None] + (WIN_H - 1)
        rows += [jnp.concatenate([col_exp[:, row_idx[qi, kj]] if row_ok[qi, kj] else masked for kj in range(3 * ROWS_Q)],
                                 axis=-1) for qi in range(ROWS_Q)]
    return jnp.concatenate(rows, axis=1)


def _natten(q, k, vt, bias):
    def tile(b, i, off):
        return b * TILES_PER_SEQ + jnp.clip(i + off, 0, TILES_PER_SEQ - 1)

    def lat(off):
        return pl.BlockSpec((TM, HALF), lambda b, i: (tile(b, i, off), 0))

    def lat_t(off):
        return pl.BlockSpec((HALF, TM), lambda b, i: (0, tile(b, i, off)))

    ctx = pl.BlockSpec((TM, HALF), lambda b, i: (N_LAT_TILES + b, 0))
    ctx_t = pl.BlockSpec((HALF, TM), lambda b, i: (0, N_LAT_TILES + b))

    def kind(b, i):
        return (0, jnp.where(i == 0, 0, jnp.where(i == TILES_PER_SEQ - 1, 2, 1)), 0)

    return pl.pallas_call(
        _natten_kernel,
        grid=(BATCH, TILES_PER_SEQ),
        in_specs=[lat(0), lat(-1), lat(0), lat(1), lat_t(-1), lat_t(0), lat_t(1), ctx, ctx_t,
                  pl.BlockSpec((N_HEADS, TM, 3 * TM), kind)],
        out_specs=lat(0),
        out_shape=jax.ShapeDtypeStruct((N_LAT, HALF), BF16),
        compiler_params=_params("parallel", "arbitrary"),
        name="natten",
    )(q, k, k, k, vt, vt, vt, k, vt, bias)


def kernel(x, c, ctx, c_ctx, ada_w, ada_b, norm_mix_g, norm_ffn_g, w_out, ab_w_in, a_dw_w, a_dw_b, a_ln_g, a_ln_b,
           b_conv_w, b_conv_b, b_gate_w, b_gate_b, b_lambda, cd_w_in, c_conv_w, d_rpb, router_w, router_bias,
           moe_w_gate, moe_w_up, moe_w_down, final_g):
    x_lat = x.reshape(N_LAT, D)
    x_ctx = ctx.reshape(BATCH * CTX, D)
    cond = jnp.concatenate([c, c_ctx[None], jnp.zeros((8 - BATCH - 1, D), F32)], axis=0)
    mod = _modulation(cond, ada_w, ada_b)
    mod0 = mod[0].reshape(8, 1, 6 * D)
    mod1 = mod[1].reshape(8, 1, 6 * D)

    wo = w_out.astype(BF16)
    lat_tiles = N_LAT // TMOE
    ctx_tiles = BATCH * CTX // TMOE
    rw_t = router_w.T
    rb = router_bias.reshape(N_EXPERTS, 1)
    fg = final_g.reshape(1, D)

    col_scale = jnp.concatenate([jnp.full((2 * HALF,), 0.5, F32), jnp.ones((HALF,), F32), jnp.full((HALF,), 0.5, F32)])
    w_in0 = (ab_w_in[0] * col_scale).astype(BF16)
    gg, a_out, v, wg0, wu0, wd0 = _front0(x_lat, x_ctx, mod0, norm_mix_g[0].reshape(1, D), w_in0,
                                          a_dw_w[0], a_dw_b[0], a_ln_g[0], a_ln_b[0], b_conv_w[0], b_conv_b[0],
                                          moe_w_gate, moe_w_up, moe_w_down)
    gw = b_gate_w[0]
    gate_w = (0.5 * jnp.concatenate([gw[:, 0], gw[:, 1]], axis=-1)).astype(BF16)
    yf, yb = _scan(v, gate_w, 0.5 * b_gate_b[0], b_lambda[0].reshape(2, 1, HALF))
    h1, n2, route0, cnt0, tok0 = _outproj0(a_out, yf, yb, gg, x_lat, x_ctx, mod0, wo[0],
                                           norm_ffn_g[0].reshape(1, D), rw_t, rb)
    moe0 = functools.partial(_moe, _moe_chunks(cnt0), n2, route0, tok0, wg0, wu0, wd0, h1, mod0, fg,
                             final_norm=False)
    h2_lat = moe0(first_tile=0, n_tiles=lat_tiles, subtiles=2, name="moe_lat")
    h2_ctx = moe0(first_tile=lat_tiles, n_tiles=ctx_tiles, subtiles=1, name="moe_ctx")

    cx, bg, q, k, vv, wg1, wu1, wd1 = _inproj1(h2_lat, h2_ctx, mod1, norm_mix_g[1].reshape(1, D),
                                               cd_w_in[0].astype(BF16), moe_w_gate, moe_w_up, moe_w_down)
    att = _natten(q, k, vv, _natten_bias(d_rpb[0]))
    h3, n4, route1, cnt1, tok1 = _outproj1(cx, bg, att, c_conv_w[0], h2_lat, mod1, wo[1],
                                           norm_ffn_g[1].reshape(1, D), rw_t, rb)
    out = _moe(_moe_chunks(cnt1), n4, route1, tok1, wg1, wu1, wd1, h3, mod1, fg,
               first_tile=0, n_tiles=lat_tiles, subtiles=2, final_norm=True, name="moe_final")
    return out.reshape(BATCH, SEQ, D)
```

```python
import functools
import math

import jax
import jax.numpy as jnp
import numpy as np
from jax import lax
from jax.experimental import pallas as pl
from jax.experimental.pallas import tpu as pltpu

F32 = jnp.float32
BF16 = jnp.bfloat16

D = 1024
BATCH = 2
SEQ = 8192
CTX = 256
GRID_W = 64
N_LAT = BATCH * SEQ
N_TOK = N_LAT + BATCH * CTX
HALF = 512
CONV_A = 31
CONV_B = 4
CONV_C = 3
LRU_BLOCK = 128
N_LRU_BLOCKS = HALF // LRU_BLOCK
LRU_C = 8.0
HEAD_DIM = 64
N_HEADS = HALF // HEAD_DIM
WIN_H = 8
WIN_W = 16
N_EXPERTS = 16
N_GROUPS = 4
PER_GROUP = N_EXPERTS // N_GROUPS
D_FF = 512
EPS = 1e-6
NEG = -1e30
LOG2E = math.log2(math.e)

TM = 256
TILES_PER_SEQ = SEQ // TM
N_LAT_TILES = N_LAT // TM
N_TILES = N_TOK // TM
TIN = 512
TMOE = 512
CAP = 128
ROUTE_ROWS = 24
ROUTE_LANES = 128
ROUTE_GROUP = 16
ROUTE_RANK = 17
HALO_A = 16
HALO_S = 8
ROWS_Q = TM // GRID_W
VMEM_LIMIT = 48 * 1024 * 1024
VMEM_LIMIT_MOE = 56 * 1024 * 1024


def _params(*sem, vmem=VMEM_LIMIT):
    return pltpu.CompilerParams(dimension_semantics=sem, vmem_limit_bytes=vmem)


def _sigmoid(x):
    return 0.5 * jnp.tanh(0.5 * x) + 0.5


def _silu(x):
    return x * _sigmoid(x)


GELU_C = 0.7978845608028654
GELU_K = 0.044715


def _gelu_tanh_of_half(xh):
    return xh * (1.0 + jnp.tanh(xh * (2.0 * GELU_C + (8.0 * GELU_C * GELU_K) * (xh * xh))))


def _rms_mod(x, g, scale, shift):
    return x * lax.rsqrt(jnp.mean(x * x, axis=-1, keepdims=True) + EPS) * (g * (1.0 + scale)) + shift


def _nt_dot(a, b):
    return lax.dot_general(a, b, (((1,), (1,)), ((), ())), preferred_element_type=F32)


def _held(hold):
    return (lambda i: i) if hold is None else (lambda i: jnp.minimum(i, hold))


def _mod_row(tile_rows, hold=None):
    per_seq = SEQ // tile_rows
    tile = _held(hold)
    return lambda i: (jnp.minimum(tile(i) // per_seq, BATCH), 0, 0)


def _mod_kernel(c_ref, w_ref, b_ref, o_ref):
    c = c_ref[...]
    s = _silu(c).astype(BF16)
    o_ref[0] = jnp.dot(s, w_ref[0].astype(BF16), preferred_element_type=F32) + b_ref[0]


def _modulation(cond, ada_w, ada_b):
    depth = ada_w.shape[0]
    nb = 1536
    return pl.pallas_call(
        _mod_kernel,
        grid=(depth, 6 * D // nb),
        in_specs=[pl.BlockSpec((8, D), lambda l, j: (0, 0)),
                  pl.BlockSpec((1, D, nb), lambda l, j: (l, 0, j)),
                  pl.BlockSpec((1, 1, nb), lambda l, j: (l, 0, j))],
        out_specs=pl.BlockSpec((1, 8, nb), lambda l, j: (l, 0, j)),
        out_shape=jax.ShapeDtypeStruct((depth, 8, 6 * D), F32),
        compiler_params=_params("parallel", "parallel"),
        name="modulation",
    )(cond, ada_w, ada_b.reshape(depth, 1, 6 * D))


def _token_specs(hold=None, rows=TM):
    tile = _held(hold)
    lat_tiles = N_LAT // rows
    lat = pl.BlockSpec((rows, D), lambda i: (jnp.minimum(tile(i), lat_tiles - 1), 0))
    ctx = pl.BlockSpec((rows, D), lambda i: (jnp.maximum(tile(i) - lat_tiles, 0), 0))
    return [lat, ctx]


def _token_tile(lat_ref, ctx_ref, tile=None, rows=TM):
    tile = pl.program_id(0) if tile is None else tile
    return jnp.where(tile < N_LAT // rows, lat_ref[...], ctx_ref[...])


def _halo_specs(halo, hold=None):
    per_tile = TM // halo
    last = N_TOK // halo - 1
    tile = _held(hold)
    prev = pl.BlockSpec((halo, HALF), lambda i: (jnp.maximum(tile(i) * per_tile - 1, 0), 0))
    nxt = pl.BlockSpec((halo, HALF), lambda i: (jnp.minimum((tile(i) + 1) * per_tile, last), 0))
    return prev, nxt


def _seq_edges(i):
    is_ctx = i >= N_LAT_TILES
    first = jnp.logical_or(is_ctx, i % TILES_PER_SEQ == 0)
    last = jnp.logical_or(is_ctx, i % TILES_PER_SEQ == TILES_PER_SEQ - 1)
    return first, last


def _fill_padded(buf_ref, prev_ref, cur_ref, next_ref, halo, tile=None):
    first, last = _seq_edges(pl.program_id(0) if tile is None else tile)
    buf_ref[0:halo, :] = jnp.where(first, 0.0, prev_ref[...])
    buf_ref[halo:halo + TM, :] = cur_ref[...]
    buf_ref[halo + TM:halo + TM + halo, :] = jnp.where(last, 0.0, next_ref[...])


def _tap_weight(w_ref, k, rows):
    return jnp.concatenate([w_ref[k]] * (rows // 8), axis=0)


def _sublane_replicated(w):
    return jnp.broadcast_to(w[:, None, :], (w.shape[0], 8, w.shape[1]))


def _depthwise(buf_ref, w_ref, taps, first_off, rows, row0):
    acc = None
    for k in range(taps):
        term = _tap_weight(w_ref, k, rows) * buf_ref[row0 + first_off + k:row0 + first_off + k + rows, :]
        acc = term if acc is None else acc + term
    return acc


def _seq_major(i, hold=None):
    i = _held(hold)(i)
    is_lat = i < N_LAT_TILES
    return (jnp.where(is_lat, i // TILES_PER_SEQ, i - N_LAT_TILES), jnp.where(is_lat, 1 + i % TILES_PER_SEQ, 0), 0)


CONV_ROWS = 32
PAD_A_ROWS = TM + 2 * HALO_A
PAD_B_ROWS = TM + 2 * HALO_S
SHIFT_ROWS = PAD_A_ROWS - 8


def _front0_kernel(x_ref, c_ref, xp_ref, xn_ref, mod_ref, g_ref, w_ref,
                   dww_ref, dwb_ref, lng_ref, lnb_ref, cw_ref, cb_ref, wg_ref, wu_ref, wd_ref,
                   gg_ref, a_ref, v_ref, wg_out, wu_out, wd_out, bufa_ref, bufb_ref, shift_ref):
    _cast_weights((wg_ref, wu_ref, wd_ref), (wg_out, wu_out, wd_out))
    step = pl.program_id(0)

    @pl.when(step == 0)
    def _():
        bufa_ref[...] = jnp.zeros_like(bufa_ref)
        bufb_ref[...] = jnp.zeros_like(bufb_ref)

    tile = jnp.minimum(step, N_TILES - 1)
    first, last = _seq_edges(tile)

    def run(new, old):
        mod = mod_ref[0]
        rows_in = jnp.concatenate([xp_ref[...], _token_tile(x_ref, c_ref, tile), xn_ref[...]], axis=0)
        n = _rms_mod(rows_in, g_ref[...], mod[:, D:2 * D], mod[:, 0:D]).astype(BF16)

        def proj(c, rows=slice(None)):
            return jnp.dot(n[rows], w_ref[:, c * HALF:(c + 1) * HALF], preferred_element_type=F32)

        def conv_chunk(r):
            row0 = r * CONV_ROWS
            u = None
            for k in range(CONV_A):
                off = HALO_A - CONV_A // 2 + k
                base = row0 + off - off % 8
                rows = (bufa_ref[old, base:base + CONV_ROWS, :] if off % 8 == 0
                        else shift_ref[off % 8 - 1, base:base + CONV_ROWS, :])
                term = _tap_weight(dww_ref, k, CONV_ROWS) * rows
                u = term if u is None else u + term
            u = u + dwb_ref[...]
            mu = jnp.mean(u, axis=-1, keepdims=True)
            uc = u - mu
            var = jnp.mean(uc * uc, axis=-1, keepdims=True)
            y = uc * lax.rsqrt(var + EPS) * lng_ref[...] + lnb_ref[...]
            a_ref[row0:row0 + CONV_ROWS, :] = _silu(y).astype(a_ref.dtype)
            acc = None
            for k in range(CONV_B):
                lo = row0 + HALO_S - 2 + k
                term = _tap_weight(cw_ref, k, CONV_ROWS) * bufb_ref[old, lo:lo + CONV_ROWS, :]
                acc = term if acc is None else acc + term
            v_ref[0, row0:row0 + CONV_ROWS, :] = acc + cb_ref[...]

        for s in range(1, 8):
            shift_ref[s - 1] = bufa_ref[old, s:s + SHIFT_ROWS, :]
        def conv_quarter(q):
            per_quarter = TM // CONV_ROWS // 4
            for r in range(q * per_quarter, (q + 1) * per_quarter):
                conv_chunk(r)

        value, gate = proj(0), proj(1)
        conv_quarter(0)
        ua = value * (jnp.tanh(gate) + 1.0)
        bufa_ref[new, 0:HALO_A, :] = jnp.where(first, 0.0, ua[0:HALO_A])
        bufa_ref[new, HALO_A:HALO_A + TM, :] = ua[HALO_A:HALO_A + TM]
        bufa_ref[new, HALO_A + TM:PAD_A_ROWS, :] = jnp.where(last, 0.0, ua[HALO_A + TM:PAD_A_ROWS])
        recur = proj(2)
        conv_quarter(1)
        lo = HALO_A - HALO_S
        bufb_ref[new, 0:HALO_S, :] = jnp.where(first, 0.0, recur[lo:HALO_A])
        bufb_ref[new, HALO_S:HALO_S + TM, :] = recur[HALO_A:HALO_A + TM]
        bufb_ref[new, HALO_S + TM:PAD_B_ROWS, :] = jnp.where(last, 0.0, recur[HALO_A + TM:HALO_A + TM + HALO_S])
        gelu_in = proj(3, slice(HALO_A, HALO_A + TM))
        conv_quarter(2)
        gg_ref[...] = _gelu_tanh_of_half(gelu_in).astype(gg_ref.dtype)
        conv_quarter(3)

    @pl.when(step % 2 == 0)
    def _():
        run(0, 1)

    @pl.when(step % 2 == 1)
    def _():
        run(1, 0)


def _front0(x, c, mod, g, w, dw_w, dw_b, ln_g, ln_b, conv_w, conv_b, moe_wg, moe_wu, moe_wd):
    assert CAST_STEPS <= N_TILES + 1
    cast_in, cast_out, cast_shape = _cast_specs(0, lambda i: i)
    hold = N_TILES - 1
    tile = _held(hold)
    lagged = lambda i: jnp.maximum(i - 1, 0)
    per_tile = TM // HALO_A
    last_halo = N_LAT // HALO_A - 1
    prev = pl.BlockSpec((HALO_A, D), lambda i: (jnp.clip(tile(i) * per_tile - 1, 0, last_halo), 0))
    nxt = pl.BlockSpec((HALO_A, D), lambda i: (jnp.clip((tile(i) + 1) * per_tile, 0, last_halo), 0))
    vec = pl.BlockSpec((1, HALF), lambda i: (0, 0))
    return pl.pallas_call(
        _front0_kernel,
        grid=(N_TILES + 1,),
        in_specs=_token_specs(hold) + [prev, nxt,
                  pl.BlockSpec((1, 1, 6 * D), _mod_row(TM, hold)),
                  pl.BlockSpec((1, D), lambda i: (0, 0)),
                  pl.BlockSpec((D, 4 * HALF), lambda i: (0, 0)),
                  pl.BlockSpec((CONV_A, 8, HALF), lambda i: (0, 0, 0)), vec, vec, vec,
                  pl.BlockSpec((CONV_B, 8, HALF), lambda i: (0, 0, 0)), vec] + cast_in,
        out_specs=[pl.BlockSpec((TM, HALF), lambda i: (tile(i), 0)),
                   pl.BlockSpec((TM, HALF), lambda i: (lagged(i), 0)),
                   pl.BlockSpec((1, TM, HALF), lambda i: _seq_major(lagged(i)))] + cast_out,
        out_shape=[jax.ShapeDtypeStruct((N_TOK, HALF), BF16), jax.ShapeDtypeStruct((N_TOK, HALF), BF16),
                   jax.ShapeDtypeStruct((BATCH, SEQ + CTX, HALF), F32)] + cast_shape,
        scratch_shapes=[pltpu.VMEM((2, PAD_A_ROWS, HALF), F32), pltpu.VMEM((2, PAD_B_ROWS, HALF), F32),
                        pltpu.VMEM((7, SHIFT_ROWS, HALF), F32)],
        compiler_params=_params("arbitrary"),
        name="front0",
    )(x, c, x, x, mod, g, w, _sublane_replicated(dw_w), dw_b.reshape(1, HALF), ln_g.reshape(1, HALF),
      ln_b.reshape(1, HALF), _sublane_replicated(conv_w), conv_b.reshape(1, HALF), moe_wg, moe_wu, moe_wd)


SCAN_UNROLL = 8


CAST_STEPS = 2 * N_EXPERTS


def _cast_specs(layer, step_of):
    def half(*idx):
        s = jnp.minimum(step_of(*idx), CAST_STEPS - 1)
        return s // 2, s % 2

    def src(*idx):
        e, h = half(*idx)
        return (layer, e, h, 0)

    def dst(*idx):
        e, h = half(*idx)
        return (e, h, 0)

    in_specs = [pl.BlockSpec((1, 1, D // 2, D_FF), src), pl.BlockSpec((1, 1, D // 2, D_FF), src),
                pl.BlockSpec((1, 1, D_FF // 2, D), src)]
    out_specs = [pl.BlockSpec((1, D // 2, D_FF), dst), pl.BlockSpec((1, D // 2, D_FF), dst),
                 pl.BlockSpec((1, D_FF // 2, D), dst)]
    out_shape = [jax.ShapeDtypeStruct((N_EXPERTS, D, D_FF), BF16), jax.ShapeDtypeStruct((N_EXPERTS, D, D_FF), BF16),
                 jax.ShapeDtypeStruct((N_EXPERTS, D_FF, D), BF16)]
    return in_specs, out_specs, out_shape


def _cast_weights(srcs, dsts):
    for src, dst in zip(srcs, dsts):
        dst[0] = src[0, 0].astype(BF16)


def _scan_kernel(vf_ref, vb_ref, w_ref, gb_ref, lam_ref, yf_ref, yb_ref, h_ref, a_ref, b_ref):
    @pl.when(pl.program_id(0) == 0)
    def _():
        h_ref[...] = jnp.zeros_like(h_ref)

    for d, v_ref in enumerate((vf_ref, vb_ref)):
        v = v_ref[...].reshape(BATCH * TM, HALF)
        vb = v.astype(BF16)
        neg = -lam_ref[d]
        softplus = jnp.maximum(neg, 0.0) + jnp.log(1.0 + jnp.exp(-jnp.abs(neg)))
        rate = (-0.5 * LRU_C * LOG2E) * softplus
        for n in range(N_LRU_BLOCKS):
            sl = slice(n * LRU_BLOCK, (n + 1) * LRU_BLOCK)
            g = jnp.dot(vb[:, sl], w_ref[d, n], preferred_element_type=F32)
            tr = jnp.tanh(g[:, 0:LRU_BLOCK] + gb_ref[d, 0:1, sl])
            ti = jnp.tanh(g[:, LRU_BLOCK:2 * LRU_BLOCK] + gb_ref[d, 1:2, sl])
            a = jnp.exp2(rate[:, sl] * tr + rate[:, sl])
            a_ref[d, :, sl] = a
            b_ref[d, :, sl] = jnp.sqrt(1.0 - a * a) * ((0.5 * ti + 0.5) * v[:, sl])

    def body(s, hs):
        hs = list(hs)
        for u in range(SCAN_UNROLL):
            t = s * SCAN_UNROLL + u
            for d, y_ref in enumerate((yf_ref, yb_ref)):
                row = t if d == 0 else TM - 1 - t
                for bt in range(BATCH):
                    c = d * BATCH + bt
                    src = bt * TM + row
                    hs[c] = a_ref[d, pl.ds(src, 1), :] * hs[c] + b_ref[d, pl.ds(src, 1), :]
                    y_ref[bt, pl.ds(row, 1), :] = hs[c]
        return tuple(hs)

    init = tuple(h_ref[c:c + 1, :] for c in range(2 * BATCH))
    final = init
    for s in range(TM // SCAN_UNROLL):
        final = body(s, final)
    for c in range(2 * BATCH):
        h_ref[c:c + 1, :] = final[c]


def _scan(v, gate_w, gate_b, lam):
    fwd = pl.BlockSpec((BATCH, TM, HALF), lambda j: (0, j, 0))
    bwd = pl.BlockSpec((BATCH, TM, HALF), lambda j: (0, jnp.where(j == 0, 0, TILES_PER_SEQ + 1 - j), 0))
    shp = jax.ShapeDtypeStruct((BATCH, SEQ + CTX, HALF), F32)
    return pl.pallas_call(
        _scan_kernel,
        grid=(TILES_PER_SEQ + 1,),
        in_specs=[fwd, bwd,
                  pl.BlockSpec((2, N_LRU_BLOCKS, LRU_BLOCK, 2 * LRU_BLOCK), lambda j: (0, 0, 0, 0)),
                  pl.BlockSpec((2, 2, HALF), lambda j: (0, 0, 0)),
                  pl.BlockSpec((2, 1, HALF), lambda j: (0, 0, 0))],
        out_specs=[fwd, bwd],
        out_shape=[shp, shp],
        scratch_shapes=[pltpu.VMEM((8, HALF), F32), pltpu.VMEM((2, BATCH * TM, HALF), F32),
                        pltpu.VMEM((2, BATCH * TM, HALF), F32)],
        compiler_params=_params("arbitrary"),
        name="lru_scan",
    )(v, v, gate_w, gate_b, lam)


def _route(score, sel):
    rows = [sel[e:e + 1, :] for e in range(N_EXPERTS)]
    gbest = None
    gidx = None
    for g in range(N_GROUPS):
        top2 = None
        for p in range(PER_GROUP):
            for q in range(p + 1, PER_GROUP):
                s = rows[g * PER_GROUP + p] + rows[g * PER_GROUP + q]
                top2 = s if top2 is None else jnp.maximum(top2, s)
        if g == 0:
            gbest = top2
            gidx = jnp.zeros(top2.shape, jnp.int32)
        else:
            better = top2 > gbest
            gidx = jnp.where(better, g, gidx)
            gbest = jnp.where(better, top2, gbest)
    eint = lax.broadcasted_iota(jnp.int32, sel.shape, 0)
    eidx = eint.astype(F32)
    masked = jnp.where(jnp.right_shift(eint, 2) == gidx, sel, -jnp.inf)
    v1 = jnp.max(masked, axis=0, keepdims=True)
    i1 = jnp.min(jnp.where(masked == v1, eidx, float(N_EXPERTS)), axis=0, keepdims=True)
    masked2 = jnp.where(eidx == i1, -jnp.inf, masked)
    v2 = jnp.max(masked2, axis=0, keepdims=True)
    i2 = jnp.min(jnp.where(masked2 == v2, eidx, float(N_EXPERTS)), axis=0, keepdims=True)
    s1 = jnp.sum(jnp.where(eidx == i1, score, 0.0), axis=0, keepdims=True)
    s2 = jnp.sum(jnp.where(eidx == i2, score, 0.0), axis=0, keepdims=True)
    inv = 1.0 / (s1 + s2)
    return jnp.where(eidx == i1, s1 * inv, 0.0) + jnp.where(eidx == i2, s2 * inv, 0.0), gidx


def _group_ranks(gidx, carry_ref, tile):
    onehot = lax.broadcasted_iota(jnp.int32, (8, TM), 0) == gidx
    oh = jnp.where(onehot, 1.0, 0.0)
    before = lax.broadcasted_iota(jnp.int32, (TM, TM), 0) < lax.broadcasted_iota(jnp.int32, (TM, TM), 1)
    prefix = jnp.dot(oh.astype(BF16), jnp.where(before, 1.0, 0.0).astype(BF16), preferred_element_type=F32)
    carry = jnp.where(tile % (TMOE // TM) == 0, 0.0, carry_ref[...])
    rank = jnp.sum(jnp.where(onehot, prefix + carry[:, 0:1], 0.0), axis=0, keepdims=True)
    carry = carry + jnp.sum(oh, axis=1, keepdims=True)
    carry_ref[...] = carry
    return rank, carry


def _split_bf16(v):
    hi = v.astype(BF16)
    return hi, (v - hi.astype(F32)).astype(BF16)


def _outproj_step(m1, m2, x, mod_ref, wo_ref, g_ref, rw_ref, rb_ref, h_ref, n_ref, route_ref, cnt_ref, tok_ref, carry_ref,
                  nprev_ref):
    step = pl.program_id(0)
    n_hi, n_lo = _split_bf16(nprev_ref[...])
    w_hi, w_lo = _split_bf16(rw_ref[...])

    mod = mod_ref[0]
    mix = (jnp.dot(m1, wo_ref[0:HALF, :], preferred_element_type=F32)
           + jnp.dot(m2, wo_ref[HALF:2 * HALF, :], preferred_element_type=F32))

    logits = _nt_dot(w_hi, n_hi) + _nt_dot(w_hi, n_lo) + _nt_dot(w_lo, n_hi)

    h = x + mod[:, 2 * D:3 * D] * mix
    h_ref[...] = h
    n = _rms_mod(h, g_ref[...], mod[:, 4 * D:5 * D], mod[:, 3 * D:4 * D])
    n_ref[...] = n.astype(n_ref.dtype)
    nprev_ref[...] = n

    score = _sigmoid(logits)
    comb, gidx = _route(score, score + rb_ref[...])
    rank, counts = _group_ranks(gidx, carry_ref, step - 1)
    record = jnp.concatenate([comb, gidx.astype(F32), rank, jnp.zeros((ROUTE_LANES - ROUTE_RANK - 1, TM), F32)], axis=0)
    route_ref[...] = record[0:ROUTE_ROWS]
    tok_ref[...] = record.T
    cnt_ref[0] = counts


def _init_pipeline(carry_ref, nprev_ref):
    @pl.when(pl.program_id(0) == 0)
    def _():
        nprev_ref[...] = jnp.zeros_like(nprev_ref)
        carry_ref[...] = jnp.zeros_like(carry_ref)


def _outproj0_kernel(a_ref, yf_ref, yb_ref, gg_ref, x_ref, c_ref, mod_ref, wo_ref, g_ref, rw_ref, rb_ref,
                     h_ref, n_ref, route_ref, cnt_ref, tok_ref, carry_ref, nprev_ref):
    _init_pipeline(carry_ref, nprev_ref)
    tile = jnp.minimum(pl.program_id(0), N_TILES - 1)
    m2 = ((yf_ref[0] + yb_ref[0]) * gg_ref[...]).astype(BF16)
    _outproj_step(a_ref[...], m2, _token_tile(x_ref, c_ref, tile), mod_ref, wo_ref, g_ref, rw_ref, rb_ref,
                  h_ref, n_ref, route_ref, cnt_ref, tok_ref, carry_ref, nprev_ref)


def _outproj1_kernel(cx_ref, cxp_ref, cxn_ref, bg_ref, att_ref, cw_ref, x_ref, mod_ref, wo_ref, g_ref, rw_ref, rb_ref,
                     h_ref, n_ref, route_ref, cnt_ref, tok_ref, buf_ref, carry_ref, nprev_ref):
    _init_pipeline(carry_ref, nprev_ref)
    tile = jnp.minimum(pl.program_id(0), N_LAT_TILES - 1)
    _fill_padded(buf_ref, cxp_ref, cx_ref, cxn_ref, HALO_S, tile)
    conv = _depthwise(buf_ref, cw_ref, CONV_C, HALO_S - 1, TM, 0)
    m1 = (bg_ref[...] * conv).astype(BF16)
    _outproj_step(m1, att_ref[...], x_ref[...], mod_ref, wo_ref, g_ref, rw_ref, rb_ref,
                  h_ref, n_ref, route_ref, cnt_ref, tok_ref, carry_ref, nprev_ref)


def _outproj_common(n_tiles):
    hold = n_tiles - 1
    tile = _held(hold)
    routed = lambda i: jnp.maximum(i - 1, 0)
    in_specs = [pl.BlockSpec((1, 1, 6 * D), _mod_row(TM, hold)),
                pl.BlockSpec((D, D), lambda i: (0, 0)),
                pl.BlockSpec((1, D), lambda i: (0, 0)),
                pl.BlockSpec((N_EXPERTS, D), lambda i: (0, 0)),
                pl.BlockSpec((N_EXPERTS, 1), lambda i: (0, 0))]
    out_specs = [pl.BlockSpec((TM, D), lambda i: (tile(i), 0)),
                 pl.BlockSpec((TM, D), lambda i: (tile(i), 0)),
                 pl.BlockSpec((ROUTE_ROWS, TM), lambda i: (0, routed(i))),
                 pl.BlockSpec((1, 8, 128), lambda i: (routed(i), 0, 0)),
                 pl.BlockSpec((TM, ROUTE_LANES), lambda i: (routed(i), 0))]
    rows = n_tiles * TM
    out_shape = [jax.ShapeDtypeStruct((rows, D), F32), jax.ShapeDtypeStruct((rows, D), BF16),
                 jax.ShapeDtypeStruct((ROUTE_ROWS, rows), F32), jax.ShapeDtypeStruct((n_tiles, 8, 128), F32),
                 jax.ShapeDtypeStruct((rows, ROUTE_LANES), F32)]
    scratch = [pltpu.VMEM((8, 128), F32), pltpu.VMEM((TM, D), F32)]
    return in_specs, out_specs, out_shape, scratch


def _outproj0(a, yf, yb, gg, x, c, mod, wo, g, rw_t, rb):
    hold = N_TILES - 1
    tok = pl.BlockSpec((TM, HALF), lambda i: (_held(hold)(i), 0))
    scan_tok = pl.BlockSpec((1, TM, HALF), functools.partial(_seq_major, hold=hold))
    common_in, out_specs, out_shape, scratch = _outproj_common(N_TILES)
    return pl.pallas_call(
        _outproj0_kernel,
        grid=(N_TILES + 1,),
        in_specs=[tok, scan_tok, scan_tok, tok] + _token_specs(hold) + common_in,
        out_specs=out_specs,
        out_shape=out_shape,
        scratch_shapes=scratch,
        compiler_params=_params("arbitrary"),
        name="outproj0",
    )(a, yf, yb, gg, x, c, mod, wo, g, rw_t, rb)


def _outproj1(cx, bg, att, conv_w, x, mod, wo, g, rw_t, rb):
    hold = N_LAT_TILES - 1
    tok = pl.BlockSpec((TM, HALF), lambda i: (_held(hold)(i), 0))
    ps, ns = _halo_specs(HALO_S, hold)
    common_in, out_specs, out_shape, scratch = _outproj_common(N_LAT_TILES)
    return pl.pallas_call(
        _outproj1_kernel,
        grid=(N_LAT_TILES + 1,),
        in_specs=[tok, ps, ns, tok, tok, pl.BlockSpec((CONV_C, 8, HALF), lambda i: (0, 0, 0)),
                  pl.BlockSpec((TM, D), lambda i: (_held(hold)(i), 0))] + common_in,
        out_specs=out_specs,
        out_shape=out_shape,
        scratch_shapes=[pltpu.VMEM((TM + 2 * HALO_S, HALF), F32)] + scratch,
        compiler_params=_params("arbitrary"),
        name="outproj1",
    )(cx, cx, cx, bg, att, _sublane_replicated(conv_w), x, mod, wo, g, rw_t, rb)


def _moe_kernel(cnt_ref, n_ref, rt_ref, r_ref, wg_ref, wu_ref, wd_ref, h_ref, mod_ref, fg_ref, o_ref,
                hid_ref, *, final_norm, first_tile, subtiles):
    i = pl.program_id(0)
    g = pl.program_id(1)

    @pl.when(g == 0)
    def _():
        o_ref[...] = jnp.zeros_like(o_ref)

    gf = g.astype(F32)
    slot_row = lax.broadcasted_iota(jnp.int32, (CAP, TMOE), 0).astype(F32)
    slot_col = lax.broadcasted_iota(jnp.int32, (TMOE, CAP), 1).astype(F32)

    counts = [cnt_ref[(first_tile + i * subtiles + s) * N_GROUPS + g] for s in range(subtiles)]
    starts = [sum(counts[:s], jnp.int32(0)) for s in range(subtiles)]
    total = starts[-1] + counts[-1]
    tile_rows = [slice(s * TMOE, (s + 1) * TMOE) for s in range(subtiles)]

    def run_chunk(k, sources):
        base = k * CAP
        sels, shifts = [], []
        xg = None
        for s in sources:
            shift = (starts[s] - base).astype(F32)
            in_group = rt_ref[ROUTE_GROUP:ROUTE_GROUP + 1, tile_rows[s]] == gf
            slot = jnp.where(in_group, rt_ref[ROUTE_RANK:ROUTE_RANK + 1, tile_rows[s]] + shift, -1.0)
            sel = slot_row == slot
            part = jnp.dot(jnp.where(sel, 1.0, 0.0).astype(BF16), n_ref[tile_rows[s], :], preferred_element_type=F32)
            xg = part if xg is None else xg + part
            sels.append(sel)
            shifts.append(shift)
        xg = xg.astype(BF16)
        for j in range(PER_GROUP):
            cw = None
            for s, sel in zip(sources, sels):
                comb_row = rt_ref[pl.ds(g * PER_GROUP + j, 1), tile_rows[s]]
                part = jnp.sum(jnp.where(sel, comb_row, 0.0), axis=1, keepdims=True)
                cw = part if cw is None else cw + part
            hid = (_silu(jnp.dot(xg, wg_ref[j], preferred_element_type=F32))
                   * jnp.dot(xg, wu_ref[j], preferred_element_type=F32) * cw)
            hid_ref[:, j * D_FF:(j + 1) * D_FF] = hid.astype(BF16)
        y = jnp.dot(hid_ref[...], wd_ref[...].reshape(PER_GROUP * D_FF, D),
                    preferred_element_type=F32).astype(BF16)
        for s, shift in zip(sources, shifts):
            in_group = r_ref[tile_rows[s], ROUTE_GROUP:ROUTE_GROUP + 1] == gf
            slot = r_ref[tile_rows[s], ROUTE_RANK:ROUTE_RANK + 1] + shift
            back = jnp.where(jnp.logical_and(in_group, slot_col == slot), 1.0, 0.0).astype(BF16)
            o_ref[tile_rows[s], :] += jnp.dot(back, y, preferred_element_type=F32)

    def loop(lo, hi, sources):
        def body(k, carry):
            run_chunk(k, sources)
            return carry
        lax.fori_loop(lo, hi, body, 0)

    for s in range(subtiles):
        end = starts[s] + counts[s]
        first_inside = (starts[s] + (CAP - 1)) // CAP
        if s + 1 < subtiles:
            loop(first_inside, end // CAP, [s])

            @pl.when(end % CAP != 0)
            def _(s=s, end=end):
                run_chunk(end // CAP, [s, s + 1])
        else:
            loop(first_inside, (end + (CAP - 1)) // CAP, [s])

    @pl.when(g == N_GROUPS - 1)
    def _():
        out = h_ref[...] + mod_ref[0][:, 5 * D:6 * D] * o_ref[...]
        if final_norm:
            out = out * lax.rsqrt(jnp.mean(out * out, axis=-1, keepdims=True) + EPS) * fg_ref[...]
        o_ref[...] = out


def _moe_chunks(counts):
    sub = TMOE // TM
    return counts[sub - 1::sub, :N_GROUPS, 0].astype(jnp.int32).reshape(-1)


def _moe(n_chunks, n, route_t, route, wg, wu, wd, h, mod, final_g, *, first_tile, n_tiles, subtiles, final_norm,
         name):
    assert subtiles in (1, 2) and first_tile % subtiles == 0 and n_tiles % subtiles == 0
    step = subtiles * TMOE
    first = first_tile // subtiles
    mod_row = lambda i, g, nch: (jnp.minimum(((first + i) * step) // SEQ, BATCH), 0, 0)
    tok = pl.BlockSpec((step, D), lambda i, g, nch: (first + i, 0))
    grid_spec = pltpu.PrefetchScalarGridSpec(
        num_scalar_prefetch=1,
        grid=(n_tiles // subtiles, N_GROUPS),
        in_specs=[tok,
                  pl.BlockSpec((ROUTE_ROWS, step), lambda i, g, nch: (0, first + i)),
                  pl.BlockSpec((step, ROUTE_LANES), lambda i, g, nch: (first + i, 0)),
                  pl.BlockSpec((PER_GROUP, D, D_FF), lambda i, g, nch: (g, 0, 0)),
                  pl.BlockSpec((PER_GROUP, D, D_FF), lambda i, g, nch: (g, 0, 0)),
                  pl.BlockSpec((PER_GROUP, D_FF, D), lambda i, g, nch: (g, 0, 0)),
                  tok,
                  pl.BlockSpec((1, 1, 6 * D), mod_row),
                  pl.BlockSpec((1, D), lambda i, g, nch: (0, 0))],
        out_specs=pl.BlockSpec((step, D), lambda i, g, nch: (i, 0)),
        scratch_shapes=[pltpu.VMEM((CAP, PER_GROUP * D_FF), BF16)])
    return pl.pallas_call(
        functools.partial(_moe_kernel, final_norm=final_norm, first_tile=first_tile, subtiles=subtiles),
        grid_spec=grid_spec,
        out_shape=jax.ShapeDtypeStruct((n_tiles * TMOE, D), F32),
        compiler_params=_params("parallel", "arbitrary", vmem=VMEM_LIMIT_MOE),
        name=name,
    )(n_chunks, n, route_t, route, wg, wu, wd, h, mod, final_g)


def _inproj1_kernel(x_ref, c_ref, mod_ref, g_ref, w_ref, wvt_ref, wg_ref, wu_ref, wd_ref,
                    cx_ref, bg_ref, q_ref, k_ref, vt_ref, wg_out, wu_out, wd_out):
    _cast_weights((wg_ref, wu_ref, wd_ref), (wg_out, wu_out, wd_out))
    mod = mod_ref[0]
    n = _rms_mod(_token_tile(x_ref, c_ref, rows=TIN), g_ref[...], mod[:, D:2 * D], mod[:, 0:D]).astype(BF16)

    def proj(c):
        return jnp.dot(n, w_ref[:, c * HALF:(c + 1) * HALF], preferred_element_type=F32)

    conv_in, conv_gate = proj(0), proj(2)
    out_gate = proj(1)
    cx_ref[...] = conv_gate * conv_in
    q = proj(3)
    bg_ref[...] = out_gate
    k = proj(4)
    q_ref[...] = (q * (HEAD_DIM ** -0.5 * LOG2E)).astype(BF16)
    vt = _nt_dot(wvt_ref[...], n)
    k_ref[...] = k.astype(BF16)
    vt_ref[...] = vt.astype(BF16)


def _inproj1(x, c, mod, g, w, moe_wg, moe_wu, moe_wd):
    steps = N_TOK // TIN
    assert CAST_STEPS <= steps
    tok = pl.BlockSpec((TIN, HALF), lambda i: (i, 0))
    f = jax.ShapeDtypeStruct((N_TOK, HALF), F32)
    h = jax.ShapeDtypeStruct((N_TOK, HALF), BF16)
    cast_in, cast_out, cast_shape = _cast_specs(1, lambda i: i)
    return pl.pallas_call(
        _inproj1_kernel,
        grid=(steps,),
        in_specs=_token_specs(rows=TIN) + [
                  pl.BlockSpec((1, 1, 6 * D), _mod_row(TIN)),
                  pl.BlockSpec((1, D), lambda i: (0, 0)),
                  pl.BlockSpec((D, 5 * HALF), lambda i: (0, 0)),
                  pl.BlockSpec((HALF, D), lambda i: (0, 0))] + cast_in,
        out_specs=[tok] * 4 + [pl.BlockSpec((HALF, TIN), lambda i: (0, i))] + cast_out,
        out_shape=[f, f, h, h, jax.ShapeDtypeStruct((HALF, N_TOK), BF16)] + cast_shape,
        compiler_params=_params("arbitrary"),
        name="inproj1",
    )(x, c, mod, g, w[:, :5 * HALF], w[:, 5 * HALF:].T, moe_wg, moe_wu, moe_wd)


def _natten_kernel(q_ref, kp_ref, kc_ref, kn_ref, vp_ref, vc_ref, vn_ref, kx_ref, vx_ref, bias_ref, o_ref):
    pair = 2 * HEAD_DIM
    low = lax.broadcasted_iota(jnp.int32, (TM, pair), 1) < HEAD_DIM
    head_mask = [low.astype(F32).astype(BF16), jnp.logical_not(low).astype(F32).astype(BF16)]
    ones_rows = jnp.ones((HEAD_DIM, 4 * TM), BF16)

    def raw_scores(head):
        g, hh = divmod(head, 2)
        sl = slice(pair * g, pair * (g + 1))
        qm = q_ref[:, sl] * head_mask[hh]
        return [_nt_dot(qm, k_ref[:, sl]) for k_ref in (kp_ref, kc_ref, kn_ref, kx_ref)]

    def biased(head, s):
        s = [s[t] + bias_ref[head, :, t * TM:(t + 1) * TM] for t in range(3)] + [s[3]]
        m = jnp.max(jnp.maximum(jnp.maximum(s[0], s[1]), jnp.maximum(s[2], s[3])), axis=-1, keepdims=True)
        return s, m

    def attend(head, s, m):
        g, hh = divmod(head, 2)
        sl = slice(pair * g, pair * (g + 1))
        p = jnp.concatenate([jnp.exp2(st - m).astype(BF16) for st in s], axis=1)
        own = slice(HEAD_DIM * head, HEAD_DIM * (head + 1))
        values_t = jnp.concatenate([vt_ref[own, :] for vt_ref in (vp_ref, vc_ref, vn_ref, vx_ref)], axis=1)
        acc = _nt_dot(jnp.concatenate([values_t, ones_rows], axis=0), p)
        return acc[0:HEAD_DIM] * (1.0 / acc[HEAD_DIM:HEAD_DIM + 1])

    pending = biased(0, raw_scores(0))
    halves = []
    for head in range(N_HEADS):
        current = pending
        if head + 1 < N_HEADS:
            pending = biased(head + 1, raw_scores(head + 1))
        halves.append(attend(head, *current))
        if head % 2 == 1:
            sl = slice(pair * (head // 2), pair * (head // 2 + 1))
            o_ref[:, sl] = jnp.concatenate(halves, axis=0).T.astype(o_ref.dtype)
            halves = []


def _natten_bias(rpb):
    n_rows = SEQ // GRID_W
    n_dr, n_dc = 2 * WIN_H - 1, 2 * WIN_W - 1
    i = np.arange(ROWS_Q)
    j = np.arange(3 * ROWS_Q)
    col = np.arange(GRID_W)
    col_start = np.clip(col - WIN_W // 2, 0, GRID_W - WIN_W)
    col_ok = (col[None, :] >= col_start[:, None]) & (col[None, :] < col_start[:, None] + WIN_W)
    col_idx = col[None, :] - col[:, None] + (WIN_W - 1)
    onehot = ((col_idx[None] == np.arange(n_dc)[:, None, None]) & col_ok[None]).astype(np.float32)
    col_exp = jnp.dot(rpb.reshape(N_HEADS * n_dr, n_dc).astype(F32), onehot.reshape(n_dc, GRID_W * GRID_W),
                      precision=lax.Precision.HIGHEST).reshape(N_HEADS, n_dr, GRID_W, GRID_W)
    col_exp = jnp.where(col_ok[None, None], col_exp * LOG2E, NEG)
    masked = jnp.full((N_HEADS, GRID_W, GRID_W), NEG, F32)
    rows = []
    for r0 in (0, ROWS_Q, n_rows - ROWS_Q):
        r = r0 + i
        kr = r0 - ROWS_Q + j
        r_start = np.clip(r - WIN_H // 2, 0, n_rows - WIN_H)
        row_ok = ((kr[None, :] >= r_start[:, None]) & (kr[None, :] < r_start[:, None] + WIN_H)
                  & (kr[None, :] >= 0) & (kr[None, :] < n_rows))
        row_idx = kr[None, :] - r[:, None] + (WIN_H - 1)
        rows += [jnp.concatenate([col_exp[:, row_idx[qi, kj]] if row_ok[qi, kj] else masked for kj in range(3 * ROWS_Q)],
                                 axis=-1) for qi in range(ROWS_Q)]
    return jnp.concatenate(rows, axis=1)


def _natten(q, k, vt, bias):
    def tile(b, i, off):
        return b * TILES_PER_SEQ + jnp.clip(i + off, 0, TILES_PER_SEQ - 1)

    def lat(off):
        return pl.BlockSpec((TM, HALF), lambda b, i: (tile(b, i, off), 0))

    def lat_t(off):
        return pl.BlockSpec((HALF, TM), lambda b, i: (0, tile(b, i, off)))

    ctx = pl.BlockSpec((TM, HALF), lambda b, i: (N_LAT_TILES + b, 0))
    ctx_t = pl.BlockSpec((HALF, TM), lambda b, i: (0, N_LAT_TILES + b))

    def kind(b, i):
        return (0, jnp.where(i == 0, 0, jnp.where(i == TILES_PER_SEQ - 1, 2, 1)), 0)

    return pl.pallas_call(
        _natten_kernel,
        grid=(BATCH, TILES_PER_SEQ),
        in_specs=[lat(0), lat(-1), lat(0), lat(1), lat_t(-1), lat_t(0), lat_t(1), ctx, ctx_t,
                  pl.BlockSpec((N_HEADS, TM, 3 * TM), kind)],
        out_specs=lat(0),
        out_shape=jax.ShapeDtypeStruct((N_LAT, HALF), BF16),
        compiler_params=_params("parallel", "arbitrary"),
        name="natten",
    )(q, k, k, k, vt, vt, vt, k, vt, bias)


def kernel(x, c, ctx, c_ctx, ada_w, ada_b, norm_mix_g, norm_ffn_g, w_out, ab_w_in, a_dw_w, a_dw_b, a_ln_g, a_ln_b,
           b_conv_w, b_conv_b, b_gate_w, b_gate_b, b_lambda, cd_w_in, c_conv_w, d_rpb, router_w, router_bias,
           moe_w_gate, moe_w_up, moe_w_down, final_g):
    x_lat = x.reshape(N_LAT, D)
    x_ctx = ctx.reshape(BATCH * CTX, D)
    cond = jnp.concatenate([c, c_ctx[None], jnp.zeros((8 - BATCH - 1, D), F32)], axis=0)
    mod = _modulation(cond, ada_w, ada_b)
    mod0 = mod[0].reshape(8, 1, 6 * D)
    mod1 = mod[1].reshape(8, 1, 6 * D)

    wo = w_out.astype(BF16)
    lat_tiles = N_LAT // TMOE
    ctx_tiles = BATCH * CTX // TMOE
    rw_t = router_w.T
    rb = router_bias.reshape(N_EXPERTS, 1)
    fg = final_g.reshape(1, D)

    col_scale = jnp.concatenate([jnp.full((2 * HALF,), 0.5, F32), jnp.ones((HALF,), F32), jnp.full((HALF,), 0.5, F32)])
    w_in0 = (ab_w_in[0] * col_scale).astype(BF16)
    gg, a_out, v, wg0, wu0, wd0 = _front0(x_lat, x_ctx, mod0, norm_mix_g[0].reshape(1, D), w_in0,
                                          a_dw_w[0], a_dw_b[0], a_ln_g[0], a_ln_b[0], b_conv_w[0], b_conv_b[0],
                                          moe_w_gate, moe_w_up, moe_w_down)
    gw = b_gate_w[0]
    gate_w = (0.5 * jnp.concatenate([gw[:, 0], gw[:, 1]], axis=-1)).astype(BF16)
    yf, yb = _scan(v, gate_w, 0.5 * b_gate_b[0], b_lambda[0].reshape(2, 1, HALF))
    h1, n2, route0, cnt0, tok0 = _outproj0(a_out, yf, yb, gg, x_lat, x_ctx, mod0, wo[0],
                                           norm_ffn_g[0].reshape(1, D), rw_t, rb)
    moe0 = functools.partial(_moe, _moe_chunks(cnt0), n2, route0, tok0, wg0, wu0, wd0, h1, mod0, fg,
                             final_norm=False)
    h2_lat = moe0(first_tile=0, n_tiles=lat_tiles, subtiles=2, name="moe_lat")
    h2_ctx = moe0(first_tile=lat_tiles, n_tiles=ctx_tiles, subtiles=1, name="moe_ctx")

    cx, bg, q, k, vv, wg1, wu1, wd1 = _inproj1(h2_lat, h2_ctx, mod1, norm_mix_g[1].reshape(1, D),
                                               cd_w_in[0].astype(BF16), moe_w_gate, moe_w_up, moe_w_down)
    att = _natten(q, k, vv, _natten_bias(d_rpb[0]))
    h3, n4, route1, cnt1, tok1 = _outproj1(cx, bg, att, c_conv_w[0], h2_lat, mod1, wo[1],
                                           norm_ffn_g[1].reshape(1, D), rw_t, rb)
    out = _moe(_moe_chunks(cnt1), n4, route1, tok1, wg1, wu1, wd1, h3, mod1, fg,
               first_tile=0, n_tiles=lat_tiles, subtiles=2, final_norm=True, name="moe_final")
    return out.reshape(BATCH, SEQ, D)
```

```python
import functools
import math

import jax
import jax.numpy as jnp
import numpy as np
from jax import lax
from jax.experimental import pallas as pl
from jax.experimental.pallas import tpu as pltpu

F32 = jnp.float32
BF16 = jnp.bfloat16

D = 1024
BATCH = 2
SEQ = 8192
CTX = 256
GRID_W = 64
N_LAT = BATCH * SEQ
N_TOK = N_LAT + BATCH * CTX
HALF = 512
CONV_A = 31
CONV_B = 4
CONV_C = 3
LRU_BLOCK = 128
N_LRU_BLOCKS = HALF // LRU_BLOCK
LRU_C = 8.0
HEAD_DIM = 64
N_HEADS = HALF // HEAD_DIM
WIN_H = 8
WIN_W = 16
N_EXPERTS = 16
N_GROUPS = 4
PER_GROUP = N_EXPERTS // N_GROUPS
D_FF = 512
EPS = 1e-6
NEG = -1e30
LOG2E = math.log2(math.e)

TM = 256
TILES_PER_SEQ = SEQ // TM
N_LAT_TILES = N_LAT // TM
N_TILES = N_TOK // TM
TIN = 512
TMOE = 512
CAP = 128
ROUTE_ROWS = 24
ROUTE_LANES = 128
ROUTE_GROUP = 16
ROUTE_RANK = 17
HALO_A = 16
HALO_S = 8
ROWS_Q = TM // GRID_W
VMEM_LIMIT = 48 * 1024 * 1024
VMEM_LIMIT_MOE = 56 * 1024 * 1024


def _params(*sem, vmem=VMEM_LIMIT):
    return pltpu.CompilerParams(dimension_semantics=sem, vmem_limit_bytes=vmem)


def _sigmoid(x):
    return 0.5 * jnp.tanh(0.5 * x) + 0.5


def _silu(x):
    return x * _sigmoid(x)


GELU_C = 0.7978845608028654
GELU_K = 0.044715


def _gelu_tanh_of_half(xh):
    return xh * (1.0 + jnp.tanh(xh * (2.0 * GELU_C + (8.0 * GELU_C * GELU_K) * (xh * xh))))


def _rms_mod(x, g, scale, shift):
    return x * lax.rsqrt(jnp.mean(x * x, axis=-1, keepdims=True) + EPS) * (g * (1.0 + scale)) + shift


def _nt_dot(a, b):
    return lax.dot_general(a, b, (((1,), (1,)), ((), ())), preferred_element_type=F32)


def _held(hold):
    return (lambda i: i) if hold is None else (lambda i: jnp.minimum(i, hold))


def _mod_row(tile_rows, hold=None):
    per_seq = SEQ // tile_rows
    tile = _held(hold)
    return lambda i: (jnp.minimum(tile(i) // per_seq, BATCH), 0, 0)


def _mod_kernel(c_ref, w_ref, b_ref, o_ref):
    c = c_ref[...]
    s = _silu(c).astype(BF16)
    o_ref[0] = jnp.dot(s, w_ref[0].astype(BF16), preferred_element_type=F32) + b_ref[0]


def _modulation(cond, ada_w, ada_b):
    depth = ada_w.shape[0]
    nb = 1536
    return pl.pallas_call(
        _mod_kernel,
        grid=(depth, 6 * D // nb),
        in_specs=[pl.BlockSpec((8, D), lambda l, j: (0, 0)),
                  pl.BlockSpec((1, D, nb), lambda l, j: (l, 0, j)),
                  pl.BlockSpec((1, 1, nb), lambda l, j: (l, 0, j))],
        out_specs=pl.BlockSpec((1, 8, nb), lambda l, j: (l, 0, j)),
        out_shape=jax.ShapeDtypeStruct((depth, 8, 6 * D), F32),
        compiler_params=_params("parallel", "parallel"),
        name="modulation",
    )(cond, ada_w, ada_b.reshape(depth, 1, 6 * D))


def _token_specs(hold=None, rows=TM):
    tile = _held(hold)
    lat_tiles = N_LAT // rows
    lat = pl.BlockSpec((rows, D), lambda i: (jnp.minimum(tile(i), lat_tiles - 1), 0))
    ctx = pl.BlockSpec((rows, D), lambda i: (jnp.maximum(tile(i) - lat_tiles, 0), 0))
    return [lat, ctx]


def _token_tile(lat_ref, ctx_ref, tile=None, rows=TM):
    tile = pl.program_id(0) if tile is None else tile
    return jnp.where(tile < N_LAT // rows, lat_ref[...], ctx_ref[...])


def _halo_specs(halo, hold=None):
    per_tile = TM // halo
    last = N_TOK // halo - 1
    tile = _held(hold)
    prev = pl.BlockSpec((halo, HALF), lambda i: (jnp.maximum(tile(i) * per_tile - 1, 0), 0))
    nxt = pl.BlockSpec((halo, HALF), lambda i: (jnp.minimum((tile(i) + 1) * per_tile, last), 0))
    return prev, nxt


def _seq_edges(i):
    is_ctx = i >= N_LAT_TILES
    first = jnp.logical_or(is_ctx, i % TILES_PER_SEQ == 0)
    last = jnp.logical_or(is_ctx, i % TILES_PER_SEQ == TILES_PER_SEQ - 1)
    return first, last


def _fill_padded(buf_ref, prev_ref, cur_ref, next_ref, halo, tile=None):
    first, last = _seq_edges(pl.program_id(0) if tile is None else tile)
    buf_ref[0:halo, :] = jnp.where(first, 0.0, prev_ref[...])
    buf_ref[halo:halo + TM, :] = cur_ref[...]
    buf_ref[halo + TM:halo + TM + halo, :] = jnp.where(last, 0.0, next_ref[...])


def _tap_weight(w_ref, k, rows):
    return jnp.concatenate([w_ref[k]] * (rows // 8), axis=0)


def _sublane_replicated(w):
    return jnp.broadcast_to(w[:, None, :], (w.shape[0], 8, w.shape[1]))


def _depthwise(buf_ref, w_ref, taps, first_off, rows, row0):
    acc = None
    for k in range(taps):
        term = _tap_weight(w_ref, k, rows) * buf_ref[row0 + first_off + k:row0 + first_off + k + rows, :]
        acc = term if acc is None else acc + term
    return acc


def _seq_major(i, hold=None):
    i = _held(hold)(i)
    is_lat = i < N_LAT_TILES
    return (jnp.where(is_lat, i // TILES_PER_SEQ, i - N_LAT_TILES), jnp.where(is_lat, 1 + i % TILES_PER_SEQ, 0), 0)


CONV_ROWS = 32
PAD_A_ROWS = TM + 2 * HALO_A
PAD_B_ROWS = TM + 2 * HALO_S
SHIFT_ROWS = PAD_A_ROWS - 8


def _front0_kernel(x_ref, c_ref, xp_ref, xn_ref, mod_ref, g_ref, w_ref,
                   dww_ref, dwb_ref, lng_ref, lnb_ref, cw_ref, cb_ref, wg_ref, wu_ref, wd_ref,
                   gg_ref, a_ref, v_ref, wg_out, wu_out, wd_out, bufa_ref, bufb_ref, shift_ref):
    _cast_weights((wg_ref, wu_ref, wd_ref), (wg_out, wu_out, wd_out))
    step = pl.program_id(0)

    @pl.when(step == 0)
    def _():
        bufa_ref[...] = jnp.zeros_like(bufa_ref)
        bufb_ref[...] = jnp.zeros_like(bufb_ref)

    tile = jnp.minimum(step, N_TILES - 1)
    first, last = _seq_edges(tile)

    def run(new, old):
        mod = mod_ref[0]
        rows_in = jnp.concatenate([xp_ref[...], _token_tile(x_ref, c_ref, tile), xn_ref[...]], axis=0)
        n = _rms_mod(rows_in, g_ref[...], mod[:, D:2 * D], mod[:, 0:D]).astype(BF16)

        def proj(c, rows=slice(None)):
            return jnp.dot(n[rows], w_ref[:, c * HALF:(c + 1) * HALF], preferred_element_type=F32)

        def conv_chunk(r):
            row0 = r * CONV_ROWS
            u = None
            for k in range(CONV_A):
                off = HALO_A - CONV_A // 2 + k
                base = row0 + off - off % 8
                term = _tap_weight(dww_ref, k, CONV_ROWS) * shift_ref[off % 8, base:base + CONV_ROWS, :]
                u = term if u is None else u + term
            u = u + dwb_ref[...]
            mu = jnp.mean(u, axis=-1, keepdims=True)
            uc = u - mu
            var = jnp.mean(uc * uc, axis=-1, keepdims=True)
            y = uc * lax.rsqrt(var + EPS) * lng_ref[...] + lnb_ref[...]
            a_ref[row0:row0 + CONV_ROWS, :] = _silu(y).astype(a_ref.dtype)
            acc = None
            for k in range(CONV_B):
                lo = row0 + HALO_S - 2 + k
                term = _tap_weight(cw_ref, k, CONV_ROWS) * bufb_ref[old, lo:lo + CONV_ROWS, :]
                acc = term if acc is None else acc + term
            v_ref[0, row0:row0 + CONV_ROWS, :] = acc + cb_ref[...]

        for s in range(8):
            shift_ref[s] = bufa_ref[old, s:s + SHIFT_ROWS, :]
        def conv_quarter(q):
            per_quarter = TM // CONV_ROWS // 4
            for r in range(q * per_quarter, (q + 1) * per_quarter):
                conv_chunk(r)

        value, gate = proj(0), proj(1)
        conv_quarter(0)
        ua = value * (jnp.tanh(gate) + 1.0)
        bufa_ref[new, 0:HALO_A, :] = jnp.where(first, 0.0, ua[0:HALO_A])
        bufa_ref[new, HALO_A:HALO_A + TM, :] = ua[HALO_A:HALO_A + TM]
        bufa_ref[new, HALO_A + TM:PAD_A_ROWS, :] = jnp.where(last, 0.0, ua[HALO_A + TM:PAD_A_ROWS])
        recur = proj(2)
        conv_quarter(1)
        lo = HALO_A - HALO_S
        bufb_ref[new, 0:HALO_S, :] = jnp.where(first, 0.0, recur[lo:HALO_A])
        bufb_ref[new, HALO_S:HALO_S + TM, :] = recur[HALO_A:HALO_A + TM]
        bufb_ref[new, HALO_S + TM:PAD_B_ROWS, :] = jnp.where(last, 0.0, recur[HALO_A + TM:HALO_A + TM + HALO_S])
        gelu_in = proj(3, slice(HALO_A, HALO_A + TM))
        conv_quarter(2)
        gg_ref[...] = _gelu_tanh_of_half(gelu_in).astype(gg_ref.dtype)
        conv_quarter(3)

    @pl.when(step % 2 == 0)
    def _():
        run(0, 1)

    @pl.when(step % 2 == 1)
    def _():
        run(1, 0)


def _front0(x, c, mod, g, w, dw_w, dw_b, ln_g, ln_b, conv_w, conv_b, moe_wg, moe_wu, moe_wd):
    assert CAST_STEPS <= N_TILES + 1
    cast_in, cast_out, cast_shape = _cast_specs(0, lambda i: i)
    hold = N_TILES - 1
    tile = _held(hold)
    lagged = lambda i: jnp.maximum(i - 1, 0)
    per_tile = TM // HALO_A
    last_halo = N_LAT // HALO_A - 1
    prev = pl.BlockSpec((HALO_A, D), lambda i: (jnp.clip(tile(i) * per_tile - 1, 0, last_halo), 0))
    nxt = pl.BlockSpec((HALO_A, D), lambda i: (jnp.clip((tile(i) + 1) * per_tile, 0, last_halo), 0))
    vec = pl.BlockSpec((1, HALF), lambda i: (0, 0))
    return pl.pallas_call(
        _front0_kernel,
        grid=(N_TILES + 1,),
        in_specs=_token_specs(hold) + [prev, nxt,
                  pl.BlockSpec((1, 1, 6 * D), _mod_row(TM, hold)),
                  pl.BlockSpec((1, D), lambda i: (0, 0)),
                  pl.BlockSpec((D, 4 * HALF), lambda i: (0, 0)),
                  pl.BlockSpec((CONV_A, 8, HALF), lambda i: (0, 0, 0)), vec, vec, vec,
                  pl.BlockSpec((CONV_B, 8, HALF), lambda i: (0, 0, 0)), vec] + cast_in,
        out_specs=[pl.BlockSpec((TM, HALF), lambda i: (tile(i), 0)),
                   pl.BlockSpec((TM, HALF), lambda i: (lagged(i), 0)),
                   pl.BlockSpec((1, TM, HALF), lambda i: _seq_major(lagged(i)))] + cast_out,
        out_shape=[jax.ShapeDtypeStruct((N_TOK, HALF), BF16), jax.ShapeDtypeStruct((N_TOK, HALF), BF16),
                   jax.ShapeDtypeStruct((BATCH, SEQ + CTX, HALF), F32)] + cast_shape,
        scratch_shapes=[pltpu.VMEM((2, PAD_A_ROWS, HALF), F32), pltpu.VMEM((2, PAD_B_ROWS, HALF), F32),
                        pltpu.VMEM((8, SHIFT_ROWS, HALF), F32)],
        compiler_params=_params("arbitrary"),
        name="front0",
    )(x, c, x, x, mod, g, w, _sublane_replicated(dw_w), dw_b.reshape(1, HALF), ln_g.reshape(1, HALF),
      ln_b.reshape(1, HALF), _sublane_replicated(conv_w), conv_b.reshape(1, HALF), moe_wg, moe_wu, moe_wd)


SCAN_UNROLL = 8


CAST_STEPS = 2 * N_EXPERTS


def _cast_specs(layer, step_of):
    def half(*idx):
        s = jnp.minimum(step_of(*idx), CAST_STEPS - 1)
        return s // 2, s % 2

    def src(*idx):
        e, h = half(*idx)
        return (layer, e, h, 0)

    def dst(*idx):
        e, h = half(*idx)
        return (e, h, 0)

    in_specs = [pl.BlockSpec((1, 1, D // 2, D_FF), src), pl.BlockSpec((1, 1, D // 2, D_FF), src),
                pl.BlockSpec((1, 1, D_FF // 2, D), src)]
    out_specs = [pl.BlockSpec((1, D // 2, D_FF), dst), pl.BlockSpec((1, D // 2, D_FF), dst),
                 pl.BlockSpec((1, D_FF // 2, D), dst)]
    out_shape = [jax.ShapeDtypeStruct((N_EXPERTS, D, D_FF), BF16), jax.ShapeDtypeStruct((N_EXPERTS, D, D_FF), BF16),
                 jax.ShapeDtypeStruct((N_EXPERTS, D_FF, D), BF16)]
    return in_specs, out_specs, out_shape


def _cast_weights(srcs, dsts):
    for src, dst in zip(srcs, dsts):
        dst[0] = src[0, 0].astype(BF16)


def _scan_kernel(vf_ref, vb_ref, w_ref, gb_ref, lam_ref, yf_ref, yb_ref, h_ref, a_ref, b_ref):
    @pl.when(pl.program_id(0) == 0)
    def _():
        h_ref[...] = jnp.zeros_like(h_ref)

    for d, v_ref in enumerate((vf_ref, vb_ref)):
        v = v_ref[...].reshape(BATCH * TM, HALF)
        vb = v.astype(BF16)
        neg = -lam_ref[d]
        softplus = jnp.maximum(neg, 0.0) + jnp.log(1.0 + jnp.exp(-jnp.abs(neg)))
        rate = (-0.5 * LRU_C * LOG2E) * softplus
        for n in range(N_LRU_BLOCKS):
            sl = slice(n * LRU_BLOCK, (n + 1) * LRU_BLOCK)
            g = jnp.dot(vb[:, sl], w_ref[d, n], preferred_element_type=F32)
            tr = jnp.tanh(g[:, 0:LRU_BLOCK] + gb_ref[d, 0:1, sl])
            ti = jnp.tanh(g[:, LRU_BLOCK:2 * LRU_BLOCK] + gb_ref[d, 1:2, sl])
            a = jnp.exp2(rate[:, sl] * tr + rate[:, sl])
            a_ref[d, :, sl] = a
            b_ref[d, :, sl] = jnp.sqrt(1.0 - a * a) * ((0.5 * ti + 0.5) * v[:, sl])

    def body(s, hs):
        hs = list(hs)
        for u in range(SCAN_UNROLL):
            t = s * SCAN_UNROLL + u
            for d, y_ref in enumerate((yf_ref, yb_ref)):
                row = t if d == 0 else TM - 1 - t
                for bt in range(BATCH):
                    c = d * BATCH + bt
                    src = bt * TM + row
                    hs[c] = a_ref[d, pl.ds(src, 1), :] * hs[c] + b_ref[d, pl.ds(src, 1), :]
                    y_ref[bt, pl.ds(row, 1), :] = hs[c]
        return tuple(hs)

    init = tuple(h_ref[c:c + 1, :] for c in range(2 * BATCH))
    final = init
    for s in range(TM // SCAN_UNROLL):
        final = body(s, final)
    for c in range(2 * BATCH):
        h_ref[c:c + 1, :] = final[c]


def _scan(v, gate_w, gate_b, lam):
    fwd = pl.BlockSpec((BATCH, TM, HALF), lambda j: (0, j, 0))
    bwd = pl.BlockSpec((BATCH, TM, HALF), lambda j: (0, jnp.where(j == 0, 0, TILES_PER_SEQ + 1 - j), 0))
    shp = jax.ShapeDtypeStruct((BATCH, SEQ + CTX, HALF), F32)
    return pl.pallas_call(
        _scan_kernel,
        grid=(TILES_PER_SEQ + 1,),
        in_specs=[fwd, bwd,
                  pl.BlockSpec((2, N_LRU_BLOCKS, LRU_BLOCK, 2 * LRU_BLOCK), lambda j: (0, 0, 0, 0)),
                  pl.BlockSpec((2, 2, HALF), lambda j: (0, 0, 0)),
                  pl.BlockSpec((2, 1, HALF), lambda j: (0, 0, 0))],
        out_specs=[fwd, bwd],
        out_shape=[shp, shp],
        scratch_shapes=[pltpu.VMEM((8, HALF), F32), pltpu.VMEM((2, BATCH * TM, HALF), F32),
                        pltpu.VMEM((2, BATCH * TM, HALF), F32)],
        compiler_params=_params("arbitrary"),
        name="lru_scan",
    )(v, v, gate_w, gate_b, lam)


def _route(score, sel):
    rows = [sel[e:e + 1, :] for e in range(N_EXPERTS)]
    gbest = None
    gidx = None
    for g in range(N_GROUPS):
        top2 = None
        for p in range(PER_GROUP):
            for q in range(p + 1, PER_GROUP):
                s = rows[g * PER_GROUP + p] + rows[g * PER_GROUP + q]
                top2 = s if top2 is None else jnp.maximum(top2, s)
        if g == 0:
            gbest = top2
            gidx = jnp.zeros(top2.shape, jnp.int32)
        else:
            better = top2 > gbest
            gidx = jnp.where(better, g, gidx)
            gbest = jnp.where(better, top2, gbest)
    eint = lax.broadcasted_iota(jnp.int32, sel.shape, 0)
    eidx = eint.astype(F32)
    masked = jnp.where(jnp.right_shift(eint, 2) == gidx, sel, -jnp.inf)
    v1 = jnp.max(masked, axis=0, keepdims=True)
    i1 = jnp.min(jnp.where(masked == v1, eidx, float(N_EXPERTS)), axis=0, keepdims=True)
    masked2 = jnp.where(eidx == i1, -jnp.inf, masked)
    v2 = jnp.max(masked2, axis=0, keepdims=True)
    i2 = jnp.min(jnp.where(masked2 == v2, eidx, float(N_EXPERTS)), axis=0, keepdims=True)
    s1 = jnp.sum(jnp.where(eidx == i1, score, 0.0), axis=0, keepdims=True)
    s2 = jnp.sum(jnp.where(eidx == i2, score, 0.0), axis=0, keepdims=True)
    inv = 1.0 / (s1 + s2)
    return jnp.where(eidx == i1, s1 * inv, 0.0) + jnp.where(eidx == i2, s2 * inv, 0.0), gidx


def _group_ranks(gidx, carry_ref, tile):
    onehot = lax.broadcasted_iota(jnp.int32, (8, TM), 0) == gidx
    oh = jnp.where(onehot, 1.0, 0.0)
    before = lax.broadcasted_iota(jnp.int32, (TM, TM), 0) < lax.broadcasted_iota(jnp.int32, (TM, TM), 1)
    prefix = jnp.dot(oh.astype(BF16), jnp.where(before, 1.0, 0.0).astype(BF16), preferred_element_type=F32)
    carry = jnp.where(tile % (TMOE // TM) == 0, 0.0, carry_ref[...])
    rank = jnp.sum(jnp.where(onehot, prefix + carry[:, 0:1], 0.0), axis=0, keepdims=True)
    carry = carry + jnp.sum(oh, axis=1, keepdims=True)
    carry_ref[...] = carry
    return rank, carry


def _split_bf16(v):
    hi = v.astype(BF16)
    return hi, (v - hi.astype(F32)).astype(BF16)


def _outproj_step(m1, m2, x, mod_ref, wo_ref, g_ref, rw_ref, rb_ref, h_ref, n_ref, route_ref, cnt_ref, tok_ref, carry_ref,
                  nprev_ref):
    step = pl.program_id(0)
    n_hi, n_lo = _split_bf16(nprev_ref[...])
    w_hi, w_lo = _split_bf16(rw_ref[...])

    mod = mod_ref[0]
    mix = (jnp.dot(m1, wo_ref[0:HALF, :], preferred_element_type=F32)
           + jnp.dot(m2, wo_ref[HALF:2 * HALF, :], preferred_element_type=F32))

    logits = _nt_dot(w_hi, n_hi) + _nt_dot(w_hi, n_lo) + _nt_dot(w_lo, n_hi)

    h = x + mod[:, 2 * D:3 * D] * mix
    h_ref[...] = h
    n = _rms_mod(h, g_ref[...], mod[:, 4 * D:5 * D], mod[:, 3 * D:4 * D])
    n_ref[...] = n.astype(n_ref.dtype)
    nprev_ref[...] = n

    score = _sigmoid(logits)
    comb, gidx = _route(score, score + rb_ref[...])
    rank, counts = _group_ranks(gidx, carry_ref, step - 1)
    record = jnp.concatenate([comb, gidx.astype(F32), rank, jnp.zeros((ROUTE_LANES - ROUTE_RANK - 1, TM), F32)], axis=0)
    route_ref[...] = record[0:ROUTE_ROWS]
    tok_ref[...] = record.T
    cnt_ref[0] = counts


def _init_pipeline(carry_ref, nprev_ref):
    @pl.when(pl.program_id(0) == 0)
    def _():
        nprev_ref[...] = jnp.zeros_like(nprev_ref)
        carry_ref[...] = jnp.zeros_like(carry_ref)


def _outproj0_kernel(a_ref, yf_ref, yb_ref, gg_ref, x_ref, c_ref, mod_ref, wo_ref, g_ref, rw_ref, rb_ref,
                     h_ref, n_ref, route_ref, cnt_ref, tok_ref, carry_ref, nprev_ref):
    _init_pipeline(carry_ref, nprev_ref)
    tile = jnp.minimum(pl.program_id(0), N_TILES - 1)
    m2 = ((yf_ref[0] + yb_ref[0]) * gg_ref[...]).astype(BF16)
    _outproj_step(a_ref[...], m2, _token_tile(x_ref, c_ref, tile), mod_ref, wo_ref, g_ref, rw_ref, rb_ref,
                  h_ref, n_ref, route_ref, cnt_ref, tok_ref, carry_ref, nprev_ref)


def _outproj1_kernel(cx_ref, cxp_ref, cxn_ref, bg_ref, att_ref, cw_ref, x_ref, mod_ref, wo_ref, g_ref, rw_ref, rb_ref,
                     h_ref, n_ref, route_ref, cnt_ref, tok_ref, buf_ref, carry_ref, nprev_ref):
    _init_pipeline(carry_ref, nprev_ref)
    tile = jnp.minimum(pl.program_id(0), N_LAT_TILES - 1)
    _fill_padded(buf_ref, cxp_ref, cx_ref, cxn_ref, HALO_S, tile)
    conv = _depthwise(buf_ref, cw_ref, CONV_C, HALO_S - 1, TM, 0)
    m1 = (bg_ref[...] * conv).astype(BF16)
    _outproj_step(m1, att_ref[...], x_ref[...], mod_ref, wo_ref, g_ref, rw_ref, rb_ref,
                  h_ref, n_ref, route_ref, cnt_ref, tok_ref, carry_ref, nprev_ref)


def _outproj_common(n_tiles):
    hold = n_tiles - 1
    tile = _held(hold)
    routed = lambda i: jnp.maximum(i - 1, 0)
    in_specs = [pl.BlockSpec((1, 1, 6 * D), _mod_row(TM, hold)),
                pl.BlockSpec((D, D), lambda i: (0, 0)),
                pl.BlockSpec((1, D), lambda i: (0, 0)),
                pl.BlockSpec((N_EXPERTS, D), lambda i: (0, 0)),
                pl.BlockSpec((N_EXPERTS, 1), lambda i: (0, 0))]
    out_specs = [pl.BlockSpec((TM, D), lambda i: (tile(i), 0)),
                 pl.BlockSpec((TM, D), lambda i: (tile(i), 0)),
                 pl.BlockSpec((ROUTE_ROWS, TM), lambda i: (0, routed(i))),
                 pl.BlockSpec((1, 8, 128), lambda i: (routed(i), 0, 0)),
                 pl.BlockSpec((TM, ROUTE_LANES), lambda i: (routed(i), 0))]
    rows = n_tiles * TM
    out_shape = [jax.ShapeDtypeStruct((rows, D), F32), jax.ShapeDtypeStruct((rows, D), BF16),
                 jax.ShapeDtypeStruct((ROUTE_ROWS, rows), F32), jax.ShapeDtypeStruct((n_tiles, 8, 128), F32),
                 jax.ShapeDtypeStruct((rows, ROUTE_LANES), F32)]
    scratch = [pltpu.VMEM((8, 128), F32), pltpu.VMEM((TM, D), F32)]
    return in_specs, out_specs, out_shape, scratch


def _outproj0(a, yf, yb, gg, x, c, mod, wo, g, rw_t, rb):
    hold = N_TILES - 1
    tok = pl.BlockSpec((TM, HALF), lambda i: (_held(hold)(i), 0))
    scan_tok = pl.BlockSpec((1, TM, HALF), functools.partial(_seq_major, hold=hold))
    common_in, out_specs, out_shape, scratch = _outproj_common(N_TILES)
    return pl.pallas_call(
        _outproj0_kernel,
        grid=(N_TILES + 1,),
        in_specs=[tok, scan_tok, scan_tok, tok] + _token_specs(hold) + common_in,
        out_specs=out_specs,
        out_shape=out_shape,
        scratch_shapes=scratch,
        compiler_params=_params("arbitrary"),
        name="outproj0",
    )(a, yf, yb, gg, x, c, mod, wo, g, rw_t, rb)


def _outproj1(cx, bg, att, conv_w, x, mod, wo, g, rw_t, rb):
    hold = N_LAT_TILES - 1
    tok = pl.BlockSpec((TM, HALF), lambda i: (_held(hold)(i), 0))
    ps, ns = _halo_specs(HALO_S, hold)
    common_in, out_specs, out_shape, scratch = _outproj_common(N_LAT_TILES)
    return pl.pallas_call(
        _outproj1_kernel,
        grid=(N_LAT_TILES + 1,),
        in_specs=[tok, ps, ns, tok, tok, pl.BlockSpec((CONV_C, 8, HALF), lambda i: (0, 0, 0)),
                  pl.BlockSpec((TM, D), lambda i: (_held(hold)(i), 0))] + common_in,
        out_specs=out_specs,
        out_shape=out_shape,
        scratch_shapes=[pltpu.VMEM((TM + 2 * HALO_S, HALF), F32)] + scratch,
        compiler_params=_params("arbitrary"),
        name="outproj1",
    )(cx, cx, cx, bg, att, _sublane_replicated(conv_w), x, mod, wo, g, rw_t, rb)


def _moe_kernel(cnt_ref, n_ref, rt_ref, r_ref, wg_ref, wu_ref, wd_ref, h_ref, mod_ref, fg_ref, o_ref,
                hid_ref, *, final_norm, first_tile, subtiles):
    i = pl.program_id(0)
    g = pl.program_id(1)

    @pl.when(g == 0)
    def _():
        o_ref[...] = jnp.zeros_like(o_ref)

    gf = g.astype(F32)
    slot_row = lax.broadcasted_iota(jnp.int32, (CAP, TMOE), 0).astype(F32)
    slot_col = lax.broadcasted_iota(jnp.int32, (TMOE, CAP), 1).astype(F32)

    counts = [cnt_ref[(first_tile + i * subtiles + s) * N_GROUPS + g] for s in range(subtiles)]
    starts = [sum(counts[:s], jnp.int32(0)) for s in range(subtiles)]
    total = starts[-1] + counts[-1]
    tile_rows = [slice(s * TMOE, (s + 1) * TMOE) for s in range(subtiles)]

    def run_chunk(k, sources):
        base = k * CAP
        sels, shifts = [], []
        xg = None
        for s in sources:
            shift = (starts[s] - base).astype(F32)
            in_group = rt_ref[ROUTE_GROUP:ROUTE_GROUP + 1, tile_rows[s]] == gf
            slot = jnp.where(in_group, rt_ref[ROUTE_RANK:ROUTE_RANK + 1, tile_rows[s]] + shift, -1.0)
            sel = slot_row == slot
            part = jnp.dot(jnp.where(sel, 1.0, 0.0).astype(BF16), n_ref[tile_rows[s], :], preferred_element_type=F32)
            xg = part if xg is None else xg + part
            sels.append(sel)
            shifts.append(shift)
        xg = xg.astype(BF16)
        for j in range(PER_GROUP):
            cw = None
            for s, sel in zip(sources, sels):
                comb_row = rt_ref[pl.ds(g * PER_GROUP + j, 1), tile_rows[s]]
                part = jnp.sum(jnp.where(sel, comb_row, 0.0), axis=1, keepdims=True)
                cw = part if cw is None else cw + part
            hid = (_silu(jnp.dot(xg, wg_ref[j], preferred_element_type=F32))
                   * jnp.dot(xg, wu_ref[j], preferred_element_type=F32) * cw)
            hid_ref[:, j * D_FF:(j + 1) * D_FF] = hid.astype(BF16)
        y = jnp.dot(hid_ref[...], wd_ref[...].reshape(PER_GROUP * D_FF, D),
                    preferred_element_type=F32).astype(BF16)
        for s, shift in zip(sources, shifts):
            in_group = r_ref[tile_rows[s], ROUTE_GROUP:ROUTE_GROUP + 1] == gf
            slot = r_ref[tile_rows[s], ROUTE_RANK:ROUTE_RANK + 1] + shift
            back = jnp.where(jnp.logical_and(in_group, slot_col == slot), 1.0, 0.0).astype(BF16)
            o_ref[tile_rows[s], :] += jnp.dot(back, y, preferred_element_type=F32)

    def loop(lo, hi, sources):
        def body(k, carry):
            run_chunk(k, sources)
            return carry
        lax.fori_loop(lo, hi, body, 0)

    for s in range(subtiles):
        end = starts[s] + counts[s]
        first_inside = (starts[s] + (CAP - 1)) // CAP
        if s + 1 < subtiles:
            loop(first_inside, end // CAP, [s])

            @pl.when(end % CAP != 0)
            def _(s=s, end=end):
                run_chunk(end // CAP, [s, s + 1])
        else:
            loop(first_inside, (end + (CAP - 1)) // CAP, [s])

    @pl.when(g == N_GROUPS - 1)
    def _():
        out = h_ref[...] + mod_ref[0][:, 5 * D:6 * D] * o_ref[...]
        if final_norm:
            out = out * lax.rsqrt(jnp.mean(out * out, axis=-1, keepdims=True) + EPS) * fg_ref[...]
        o_ref[...] = out


def _moe_chunks(counts):
    sub = TMOE // TM
    return counts[sub - 1::sub, :N_GROUPS, 0].astype(jnp.int32).reshape(-1)


def _moe(n_chunks, n, route_t, route, wg, wu, wd, h, mod, final_g, *, first_tile, n_tiles, subtiles, final_norm,
         name):
    assert subtiles in (1, 2) and first_tile % subtiles == 0 and n_tiles % subtiles == 0
    step = subtiles * TMOE
    first = first_tile // subtiles
    mod_row = lambda i, g, nch: (jnp.minimum(((first + i) * step) // SEQ, BATCH), 0, 0)
    tok = pl.BlockSpec((step, D), lambda i, g, nch: (first + i, 0))
    grid_spec = pltpu.PrefetchScalarGridSpec(
        num_scalar_prefetch=1,
        grid=(n_tiles // subtiles, N_GROUPS),
        in_specs=[tok,
                  pl.BlockSpec((ROUTE_ROWS, step), lambda i, g, nch: (0, first + i)),
                  pl.BlockSpec((step, ROUTE_LANES), lambda i, g, nch: (first + i, 0)),
                  pl.BlockSpec((PER_GROUP, D, D_FF), lambda i, g, nch: (g, 0, 0)),
                  pl.BlockSpec((PER_GROUP, D, D_FF), lambda i, g, nch: (g, 0, 0)),
                  pl.BlockSpec((PER_GROUP, D_FF, D), lambda i, g, nch: (g, 0, 0)),
                  tok,
                  pl.BlockSpec((1, 1, 6 * D), mod_row),
                  pl.BlockSpec((1, D), lambda i, g, nch: (0, 0))],
        out_specs=pl.BlockSpec((step, D), lambda i, g, nch: (i, 0)),
        scratch_shapes=[pltpu.VMEM((CAP, PER_GROUP * D_FF), BF16)])
    return pl.pallas_call(
        functools.partial(_moe_kernel, final_norm=final_norm, first_tile=first_tile, subtiles=subtiles),
        grid_spec=grid_spec,
        out_shape=jax.ShapeDtypeStruct((n_tiles * TMOE, D), F32),
        compiler_params=_params("parallel", "arbitrary", vmem=VMEM_LIMIT_MOE),
        name=name,
    )(n_chunks, n, route_t, route, wg, wu, wd, h, mod, final_g)


def _inproj1_kernel(x_ref, c_ref, mod_ref, g_ref, w_ref, wvt_ref, wg_ref, wu_ref, wd_ref,
                    cx_ref, bg_ref, q_ref, k_ref, vt_ref, wg_out, wu_out, wd_out):
    _cast_weights((wg_ref, wu_ref, wd_ref), (wg_out, wu_out, wd_out))
    mod = mod_ref[0]
    n = _rms_mod(_token_tile(x_ref, c_ref, rows=TIN), g_ref[...], mod[:, D:2 * D], mod[:, 0:D]).astype(BF16)

    def proj(c):
        return jnp.dot(n, w_ref[:, c * HALF:(c + 1) * HALF], preferred_element_type=F32)

    conv_in, conv_gate = proj(0), proj(2)
    out_gate = proj(1)
    cx_ref[...] = conv_gate * conv_in
    q = proj(3)
    bg_ref[...] = out_gate
    k = proj(4)
    q_ref[...] = (q * (HEAD_DIM ** -0.5 * LOG2E)).astype(BF16)
    vt = _nt_dot(wvt_ref[...], n)
    k_ref[...] = k.astype(BF16)
    vt_ref[...] = vt.astype(BF16)


def _inproj1(x, c, mod, g, w, moe_wg, moe_wu, moe_wd):
    steps = N_TOK // TIN
    assert CAST_STEPS <= steps
    tok = pl.BlockSpec((TIN, HALF), lambda i: (i, 0))
    f = jax.ShapeDtypeStruct((N_TOK, HALF), F32)
    h = jax.ShapeDtypeStruct((N_TOK, HALF), BF16)
    cast_in, cast_out, cast_shape = _cast_specs(1, lambda i: i)
    return pl.pallas_call(
        _inproj1_kernel,
        grid=(steps,),
        in_specs=_token_specs(rows=TIN) + [
                  pl.BlockSpec((1, 1, 6 * D), _mod_row(TIN)),
                  pl.BlockSpec((1, D), lambda i: (0, 0)),
                  pl.BlockSpec((D, 5 * HALF), lambda i: (0, 0)),
                  pl.BlockSpec((HALF, D), lambda i: (0, 0))] + cast_in,
        out_specs=[tok] * 4 + [pl.BlockSpec((HALF, TIN), lambda i: (0, i))] + cast_out,
        out_shape=[f, f, h, h, jax.ShapeDtypeStruct((HALF, N_TOK), BF16)] + cast_shape,
        compiler_params=_params("arbitrary"),
        name="inproj1",
    )(x, c, mod, g, w[:, :5 * HALF], w[:, 5 * HALF:].T, moe_wg, moe_wu, moe_wd)


def _natten_kernel(q_ref, kp_ref, kc_ref, kn_ref, vp_ref, vc_ref, vn_ref, kx_ref, vx_ref, bias_ref, o_ref):
    pair = 2 * HEAD_DIM
    q_half = TM // 2
    low = lax.broadcasted_iota(jnp.int32, (q_half, pair), 1) < HEAD_DIM
    head_mask = [low.astype(F32).astype(BF16), jnp.logical_not(low).astype(F32).astype(BF16)]
    ones_rows = jnp.ones((HEAD_DIM, 4 * TM), BF16)

    def raw_scores(unit):
        half, head = divmod(unit, N_HEADS)
        g, hh = divmod(head, 2)
        sl = slice(pair * g, pair * (g + 1))
        qm = q_ref[half * q_half:(half + 1) * q_half, sl] * head_mask[hh]
        return [_nt_dot(qm, k_ref[:, sl]) for k_ref in (kp_ref, kc_ref, kn_ref, kx_ref)]

    def biased(unit, s):
        half, head = divmod(unit, N_HEADS)
        rows = slice(half * q_half, (half + 1) * q_half)
        s = [s[t] + bias_ref[0, head, rows, t * TM:(t + 1) * TM] for t in range(3)] + [s[3]]
        m = jnp.max(jnp.maximum(jnp.maximum(s[0], s[1]), jnp.maximum(s[2], s[3])), axis=-1, keepdims=True)
        return s, m

    def attend(unit, s, m):
        head = unit % N_HEADS
        p = jnp.concatenate([jnp.exp2(st - m).astype(BF16) for st in s], axis=1)
        own = slice(HEAD_DIM * head, HEAD_DIM * (head + 1))
        values_t = jnp.concatenate([vt_ref[own, :] for vt_ref in (vp_ref, vc_ref, vn_ref, vx_ref)], axis=1)
        acc = _nt_dot(jnp.concatenate([values_t, ones_rows], axis=0), p)
        return acc[0:HEAD_DIM] * (1.0 / acc[HEAD_DIM:HEAD_DIM + 1])

    n_units = 2 * N_HEADS
    pending = biased(0, raw_scores(0))
    halves = []
    for unit in range(n_units):
        current = pending
        if unit + 1 < n_units:
            pending = biased(unit + 1, raw_scores(unit + 1))
        halves.append(attend(unit, *current))
        if unit % 2 == 1:
            half, head = divmod(unit, N_HEADS)
            sl = slice(pair * (head // 2), pair * (head // 2 + 1))
            o_ref[half * q_half:(half + 1) * q_half, sl] = jnp.concatenate(halves, axis=0).T.astype(o_ref.dtype)
            halves = []


def _natten_bias(rpb):
    n_rows = SEQ // GRID_W
    n_dr, n_dc = 2 * WIN_H - 1, 2 * WIN_W - 1
    i = np.arange(ROWS_Q)
    j = np.arange(3 * ROWS_Q)
    col = np.arange(GRID_W)
    col_start = np.clip(col - WIN_W // 2, 0, GRID_W - WIN_W)
    col_ok = (col[None, :] >= col_start[:, None]) & (col[None, :] < col_start[:, None] + WIN_W)
    col_idx = col[None, :] - col[:, None] + (WIN_W - 1)
    onehot = ((col_idx[None] == np.arange(n_dc)[:, None, None]) & col_ok[None]).astype(np.float32)
    col_exp = jnp.dot(rpb.reshape(N_HEADS * n_dr, n_dc).astype(F32), onehot.reshape(n_dc, GRID_W * GRID_W),
                      precision=lax.Precision.HIGHEST).reshape(N_HEADS, n_dr, GRID_W, GRID_W)
    col_exp = jnp.where(col_ok[None, None], col_exp * LOG2E, NEG)
    masked = jnp.full((N_HEADS, GRID_W, GRID_W), NEG, F32)
    kinds = []
    for r0 in (0, ROWS_Q, n_rows - ROWS_Q):
        r = r0 + i
        kr = r0 - ROWS_Q + j
        r_start = np.clip(r - WIN_H // 2, 0, n_rows - WIN_H)
        row_ok = ((kr[None, :] >= r_start[:, None]) & (kr[None, :] < r_start[:, None] + WIN_H)
                  & (kr[None, :] >= 0) & (kr[None, :] < n_rows))
        row_idx = kr[None, :] - r[:, None] + (WIN_H - 1)
        rows = [jnp.concatenate([col_exp[:, row_idx[qi, kj]] if row_ok[qi, kj] else masked for kj in range(3 * ROWS_Q)],
                                axis=-1) for qi in range(ROWS_Q)]
        kinds.append(jnp.concatenate(rows, axis=1))
    return jnp.stack(kinds)


def _natten(q, k, vt, bias):
    def tile(b, i, off):
        return b * TILES_PER_SEQ + jnp.clip(i + off, 0, TILES_PER_SEQ - 1)

    def lat(off):
        return pl.BlockSpec((TM, HALF), lambda b, i: (tile(b, i, off), 0))

    def lat_t(off):
        return pl.BlockSpec((HALF, TM), lambda b, i: (0, tile(b, i, off)))

    ctx = pl.BlockSpec((TM, HALF), lambda b, i: (N_LAT_TILES + b, 0))
    ctx_t = pl.BlockSpec((HALF, TM), lambda b, i: (0, N_LAT_TILES + b))

    def kind(b, i):
        return (jnp.where(i == 0, 0, jnp.where(i == TILES_PER_SEQ - 1, 2, 1)), 0, 0, 0)

    return pl.pallas_call(
        _natten_kernel,
        grid=(BATCH, TILES_PER_SEQ),
        in_specs=[lat(0), lat(-1), lat(0), lat(1), lat_t(-1), lat_t(0), lat_t(1), ctx, ctx_t,
                  pl.BlockSpec((1, N_HEADS, TM, 3 * TM), kind)],
        out_specs=lat(0),
        out_shape=jax.ShapeDtypeStruct((N_LAT, HALF), BF16),
        compiler_params=_params("parallel", "arbitrary"),
        name="natten",
    )(q, k, k, k, vt, vt, vt, k, vt, bias)


def kernel(x, c, ctx, c_ctx, ada_w, ada_b, norm_mix_g, norm_ffn_g, w_out, ab_w_in, a_dw_w, a_dw_b, a_ln_g, a_ln_b,
           b_conv_w, b_conv_b, b_gate_w, b_gate_b, b_lambda, cd_w_in, c_conv_w, d_rpb, router_w, router_bias,
           moe_w_gate, moe_w_up, moe_w_down, final_g):
    x_lat = x.reshape(N_LAT, D)
    x_ctx = ctx.reshape(BATCH * CTX, D)
    cond = jnp.concatenate([c, c_ctx[None], jnp.zeros((8 - BATCH - 1, D), F32)], axis=0)
    mod = _modulation(cond, ada_w, ada_b)
    mod0 = mod[0].reshape(8, 1, 6 * D)
    mod1 = mod[1].reshape(8, 1, 6 * D)

    wo = w_out.astype(BF16)
    lat_tiles = N_LAT // TMOE
    ctx_tiles = BATCH * CTX // TMOE
    rw_t = router_w.T
    rb = router_bias.reshape(N_EXPERTS, 1)
    fg = final_g.reshape(1, D)

    col_scale = jnp.concatenate([jnp.full((2 * HALF,), 0.5, F32), jnp.ones((HALF,), F32), jnp.full((HALF,), 0.5, F32)])
    w_in0 = (ab_w_in[0] * col_scale).astype(BF16)
    gg, a_out, v, wg0, wu0, wd0 = _front0(x_lat, x_ctx, mod0, norm_mix_g[0].reshape(1, D), w_in0,
                                          a_dw_w[0], a_dw_b[0], a_ln_g[0], a_ln_b[0], b_conv_w[0], b_conv_b[0],
                                          moe_w_gate, moe_w_up, moe_w_down)
    gw = b_gate_w[0]
    gate_w = (0.5 * jnp.concatenate([gw[:, 0], gw[:, 1]], axis=-1)).astype(BF16)
    yf, yb = _scan(v, gate_w, 0.5 * b_gate_b[0], b_lambda[0].reshape(2, 1, HALF))
    h1, n2, route0, cnt0, tok0 = _outproj0(a_out, yf, yb, gg, x_lat, x_ctx, mod0, wo[0],
                                           norm_ffn_g[0].reshape(1, D), rw_t, rb)
    moe0 = functools.partial(_moe, _moe_chunks(cnt0), n2, route0, tok0, wg0, wu0, wd0, h1, mod0, fg,
                             final_norm=False)
    h2_lat = moe0(first_tile=0, n_tiles=lat_tiles, subtiles=2, name="moe_lat")
    h2_ctx = moe0(first_tile=lat_tiles, n_tiles=ctx_tiles, subtiles=1, name="moe_ctx")

    cx, bg, q, k, vv, wg1, wu1, wd1 = _inproj1(h2_lat, h2_ctx, mod1, norm_mix_g[1].reshape(1, D),
                                               cd_w_in[0].astype(BF16), moe_w_gate, moe_w_up, moe_w_down)
    att = _natten(q, k, vv, _natten_bias(d_rpb[0]))
    h3, n4, route1, cnt1, tok1 = _outproj1(cx, bg, att, c_conv_w[0], h2_lat, mod1, wo[1],
                                           norm_ffn_g[1].reshape(1, D), rw_t, rb)
    out = _moe(_moe_chunks(cnt1), n4, route1, tok1, wg1, wu1, wd1, h3, mod1, fg,
               first_tile=0, n_tiles=lat_tiles, subtiles=2, final_norm=True, name="moe_final")
    return out.reshape(BATCH, SEQ, D)
```
